```python
import math
import jax, jax.numpy as jnp
from jax import lax
import numpy as np

D_MODEL = 1024
BATCH = 4
SEQ = 4096
DEPTH = 1
DEC_BATCH = 8
DEC_SEQ = 32
PAST_LEN = 2048

CHUNK = 64
WINDOW = 128
WINDOW_CHUNKS = WINDOW // CHUNK

ATTN_HEAD_DIM = 64
ATTN_WIDTH = D_MODEL // 2
ATTN_HEADS = ATTN_WIDTH // ATTN_HEAD_DIM
ATTN_KV_HEADS = 2
ATTN_GROUP = ATTN_HEADS // ATTN_KV_HEADS
KV_WIDTH = ATTN_KV_HEADS * ATTN_HEAD_DIM

RWKV_HEAD_DIM = 64
RWKV_WIDTH = D_MODEL - ATTN_WIDTH
RWKV_HEADS = RWKV_WIDTH // RWKV_HEAD_DIM
DECAY_LORA = 64
AAA_LORA = 64
GATE_LORA = 128
RWKV_PROJ = 3 * RWKV_WIDTH + DECAY_LORA + AAA_LORA + GATE_LORA
RWKV_SPLITS = (RWKV_WIDTH, 2 * RWKV_WIDTH, 3 * RWKV_WIDTH,
               3 * RWKV_WIDTH + DECAY_LORA, 3 * RWKV_WIDTH + DECAY_LORA + AAA_LORA)

MIX_WIDTH = ATTN_WIDTH + RWKV_WIDTH
IN_PROJ = ATTN_WIDTH + 2 * KV_WIDTH + RWKV_PROJ

N_MEM = 256
MEM_HEADS = 4
MEM_HEAD_DIM = 128
MEM_WIDTH = MEM_HEADS * MEM_HEAD_DIM

D_FF = 4 * D_MODEL
REL_BUCKETS = 32
REL_MAX_DIST = 128
NORM_EPS = 1e-6
GN_EPS = 64e-5

kernel_name = 'hymba_swa_sink_rwkv7_stream_step'


def rmsnorm(x, g):
    x32 = x.astype(jnp.float32)
    y = x32 * lax.rsqrt(jnp.mean(x32 * x32, axis=-1, keepdims=True) + NORM_EPS)
    return (y * g.astype(jnp.float32)).astype(x.dtype)


def t5_bucket(rel):
    half = REL_BUCKETS // 2
    max_exact = half // 2
    n = jnp.abs(rel)
    large = max_exact + (jnp.log(jnp.maximum(n, 1).astype(jnp.float32) / max_exact)
                         / math.log(REL_MAX_DIST / max_exact) * (half - max_exact)).astype(jnp.int32)
    large = jnp.minimum(large, half - 1)
    return jnp.where(rel > 0, half, 0) + jnp.where(n < max_exact, n, large)


def rel_bias(rel, table):
    b = table[t5_bucket(rel)]
    return jnp.transpose(b, (2, 0, 1)).reshape(ATTN_KV_HEADS, ATTN_GROUP, rel.shape[0], rel.shape[1])


def sink_attention(q, k, v, bias, valid, sink):
    s = jnp.einsum('bcqhgd,bckhd->bchgqk', q, k).astype(jnp.float32) * (ATTN_HEAD_DIM ** -0.5)
    s = s + bias.astype(jnp.float32)
    s = jnp.where(valid[None, :, None, None, None, :], s, -jnp.inf)
    sk = sink.astype(jnp.float32).reshape(1, 1, ATTN_KV_HEADS, ATTN_GROUP, 1, 1)
    m = jnp.maximum(jnp.max(s, axis=-1, keepdims=True), sk)
    p = jnp.exp(s - m)
    den = jnp.sum(p, axis=-1, keepdims=True) + jnp.exp(sk - m)
    return jnp.einsum('bchgqk,bckhd->bcqhgd', (p / den).astype(v.dtype), v)


def window_attention(q, k, v, k_past, v_past, sink, table):
    B, T = q.shape[0], q.shape[1]
    if k_past is None:
        nc = T // CHUNK
        qb = q.reshape(B, nc, CHUNK, ATTN_KV_HEADS, ATTN_GROUP, ATTN_HEAD_DIM)
        pad = ((0, 0), (WINDOW, 0), (0, 0), (0, 0))
        kp = jnp.pad(k, pad).reshape(B, nc + WINDOW_CHUNKS, CHUNK, ATTN_KV_HEADS, ATTN_HEAD_DIM)
        vp = jnp.pad(v, pad).reshape(B, nc + WINDOW_CHUNKS, CHUNK, ATTN_KV_HEADS, ATTN_HEAD_DIM)
        kb = jnp.concatenate([kp[:, i:i + nc] for i in range(WINDOW_CHUNKS + 1)], axis=2)
        vb = jnp.concatenate([vp[:, i:i + nc] for i in range(WINDOW_CHUNKS + 1)], axis=2)
        n_q, n_k = CHUNK, WINDOW + CHUNK
        valid = (jnp.arange(nc)[:, None] * CHUNK - WINDOW + jnp.arange(n_k)[None, :]) >= 0
        k_buf, v_buf = k[:, -WINDOW:], v[:, -WINDOW:]
    else:
        kc = jnp.concatenate([k_past.astype(k.dtype), k], axis=1)
        vc = jnp.concatenate([v_past.astype(v.dtype), v], axis=1)
        qb = q.reshape(B, 1, T, ATTN_KV_HEADS, ATTN_GROUP, ATTN_HEAD_DIM)
        kb, vb = kc[:, None], vc[:, None]
        n_q, n_k = T, WINDOW + T
        valid = jnp.ones((1, n_k), dtype=bool)
        k_buf, v_buf = kc[:, -WINDOW:], vc[:, -WINDOW:]
    rel = jnp.arange(n_k)[None, :] - WINDOW - jnp.arange(n_q)[:, None]
    o = sink_attention(qb, kb, vb, rel_bias(rel, table), valid, sink)
    return o.reshape(B, T, ATTN_WIDTH), k_buf, v_buf


def wkv7_scan(r, w, k, v, a, b, state0):
    def step(S, inp):
        r_t, w_t, k_t, v_t, a_t, b_t = inp
        sa = jnp.einsum('bhij,bhj->bhi', S, a_t)
        S = S * w_t[:, :, None, :] + sa[..., None] * b_t[:, :, None, :] + v_t[..., None] * k_t[:, :, None, :]
        return S, jnp.einsum('bhij,bhj->bhi', S, r_t)
    xs = tuple(jnp.moveaxis(t.astype(jnp.float32), 1, 0) for t in (r, w, k, v, a, b))
    S, ys = lax.scan(step, state0.astype(jnp.float32), xs)
    return jnp.moveaxis(ys, 0, 1), S


def rwkv7_mixer(zr, shift_prev, wkv0, lw):
    B, T = zr.shape[0], zr.shape[1]
    f32 = jnp.float32
    z_prev = jnp.concatenate([shift_prev.astype(zr.dtype), zr[:, :-1]], axis=1)
    zs = zr + (z_prev - zr) * lw['rwkv_mu']
    r, k, v, wd, ad, gd = jnp.split(zs, RWKV_SPLITS, axis=-1)
    w_log = -jax.nn.softplus(-(lw['rwkv_w0'] + jnp.tanh(wd) @ lw['rwkv_w2'])) - 0.5
    decay = jnp.exp(-jnp.exp(w_log.astype(f32)))
    a = jax.nn.sigmoid(lw['rwkv_a0'] + ad @ lw['rwkv_a2'])
    g = jax.nn.sigmoid(gd) @ lw['rwkv_g2']
    heads = lambda t: t.reshape(B, T, RWKV_HEADS, RWKV_HEAD_DIM)
    kk = heads(k * lw['rwkv_k_k']).astype(f32)
    kk = kk / jnp.maximum(jnp.sqrt(jnp.sum(kk * kk, axis=-1, keepdims=True)), 1e-12)
    k = k * (1.0 + (a - 1.0) * lw['rwkv_k_a'])
    r_h, k_h, v_h, a_h = heads(r).astype(f32), heads(k).astype(f32), heads(v).astype(f32), heads(a).astype(f32)
    y, S = wkv7_scan(r_h, heads(decay), k_h, v_h, -kk, kk * a_h, wkv0)
    mu = jnp.mean(y, axis=-1, keepdims=True)
    var = jnp.mean(jnp.square(y - mu), axis=-1, keepdims=True)
    y = ((y - mu) * lax.rsqrt(var + GN_EPS)).reshape(B, T, RWKV_WIDTH)
    y = y * lw['rwkv_ln_w'].astype(f32) + lw['rwkv_ln_b'].astype(f32)
    bonus = jnp.sum(r_h * k_h * lw['rwkv_r_k'].astype(f32), axis=-1, keepdims=True) * v_h
    y = (y + bonus.reshape(B, T, RWKV_WIDTH)).astype(zr.dtype)
    return y * g, zr[:, -1:], S


def memory_kv(mem, lw):
    B = mem.shape[0]
    mn = rmsnorm(mem, lw['norm_mem_g'])
    mk = (mn @ lw['w_mk']).reshape(B, N_MEM, MEM_HEADS, MEM_HEAD_DIM)
    mv = (mn @ lw['w_mv']).reshape(B, N_MEM, MEM_HEADS, MEM_HEAD_DIM)
    return mk, mv


def memory_cross_attention(h, mem_k, mem_v, lw):
    B, T = h.shape[0], h.shape[1]
    q = (h @ lw['w_cq']).reshape(B, T, MEM_HEADS, MEM_HEAD_DIM)
    s = jnp.einsum('bthd,bmhd->bhtm', q, mem_k.astype(h.dtype)).astype(jnp.float32) * (MEM_HEAD_DIM ** -0.5)
    p = jax.nn.softmax(s, axis=-1).astype(h.dtype)
    o = jnp.einsum('bhtm,bmhd->bthd', p, mem_v.astype(h.dtype)).reshape(B, T, MEM_WIDTH)
    return o @ lw['w_co']


def trunk_layer(x, mem_k, mem_v, k_past, v_past, shift_prev, wkv0, lw, table):
    B, T = x.shape[0], x.shape[1]
    z = rmsnorm(x, lw['norm_mix_g']) @ lw['w_in']
    q, k, v, zr = jnp.split(z, (ATTN_WIDTH, ATTN_WIDTH + KV_WIDTH, ATTN_WIDTH + 2 * KV_WIDTH), axis=-1)
    q = q.reshape(B, T, ATTN_HEADS, ATTN_HEAD_DIM)
    k = k.reshape(B, T, ATTN_KV_HEADS, ATTN_HEAD_DIM)
    v = v.reshape(B, T, ATTN_KV_HEADS, ATTN_HEAD_DIM)
    a_out, k_buf, v_buf = window_attention(q, k, v, k_past, v_past, lw['attn_sink'], table)
    r_out, shift_new, S = rwkv7_mixer(zr, shift_prev, wkv0, lw)
    x = x + jnp.concatenate([a_out, r_out], axis=-1) @ lw['w_out']
    x = x + memory_cross_attention(rmsnorm(x, lw['norm_cross_g']), mem_k, mem_v, lw)
    hm = rmsnorm(x, lw['norm_mlp_g'])
    x = x + jnp.square(jax.nn.relu(hm @ lw['w_up'])) @ lw['w_down']
    return x, k_buf, v_buf, shift_new, S


def setup_inputs(seed: int = 0) -> dict:
    key = jax.random.key(seed)
    ks = iter(jax.random.split(key, 48))
    f32 = jnp.float32

    def nrm(shape, scale):
        return scale * jax.random.normal(next(ks), shape, f32)

    def unif(shape, lo, hi):
        return jax.random.uniform(next(ks), shape, f32, minval=lo, maxval=hi)

    L = DEPTH
    return {
        'x_prompt': nrm((BATCH, SEQ, D_MODEL), 1.0),
        'x_sample': nrm((DEC_BATCH, DEC_SEQ, D_MODEL), 1.0),
        'mem_prompt': nrm((BATCH, N_MEM, D_MODEL), 1.0),
        'cache_attn_k': nrm((L, DEC_BATCH, WINDOW, ATTN_KV_HEADS, ATTN_HEAD_DIM), 1.0),
        'cache_attn_v': nrm((L, DEC_BATCH, WINDOW, ATTN_KV_HEADS, ATTN_HEAD_DIM), 1.0),
        'cache_mem_k': nrm((L, DEC_BATCH, N_MEM, MEM_HEADS, MEM_HEAD_DIM), 1.0),
        'cache_mem_v': nrm((L, DEC_BATCH, N_MEM, MEM_HEADS, MEM_HEAD_DIM), 1.0),
        'state_shift': nrm((L, DEC_BATCH, 1, RWKV_PROJ), 1.0),
        'state_wkv': nrm((L, DEC_BATCH, RWKV_HEADS, RWKV_HEAD_DIM, RWKV_HEAD_DIM), 0.3),
        'norm_mix_g': 1.0 + nrm((L, D_MODEL), 0.02),
        'w_in': nrm((L, D_MODEL, IN_PROJ), D_MODEL ** -0.5),
        'attn_sink': nrm((L, ATTN_HEADS), 1.0),
        'rel_bias_table': nrm((REL_BUCKETS, ATTN_HEADS), 0.5),
        'rwkv_mu': unif((L, RWKV_PROJ), 0.0, 1.0),
        'rwkv_w0': unif((L, RWKV_WIDTH), -6.0, -0.5),
        'rwkv_w2': nrm((L, DECAY_LORA, RWKV_WIDTH), 0.1 * DECAY_LORA ** -0.5),
        'rwkv_a0': nrm((L, RWKV_WIDTH), 0.5),
        'rwkv_a2': nrm((L, AAA_LORA, RWKV_WIDTH), 0.1 * AAA_LORA ** -0.5),
        'rwkv_g2': nrm((L, GATE_LORA, RWKV_WIDTH), GATE_LORA ** -0.5),
        'rwkv_k_k': 0.85 + nrm((L, RWKV_WIDTH), 0.05),
        'rwkv_k_a': 1.0 + nrm((L, RWKV_WIDTH), 0.05),
        'rwkv_r_k': nrm((L, RWKV_HEADS, RWKV_HEAD_DIM), 0.1),
        'rwkv_ln_w': 1.0 + nrm((L, RWKV_WIDTH), 0.02),
        'rwkv_ln_b': nrm((L, RWKV_WIDTH), 0.02),
        'w_out': nrm((L, MIX_WIDTH, D_MODEL), MIX_WIDTH ** -0.5),
        'norm_cross_g': 1.0 + nrm((L, D_MODEL), 0.02),
        'norm_mem_g': 1.0 + nrm((L, D_MODEL), 0.02),
        'w_cq': nrm((L, D_MODEL, MEM_WIDTH), D_MODEL ** -0.5),
        'w_mk': nrm((L, D_MODEL, MEM_WIDTH), D_MODEL ** -0.5),
        'w_mv': nrm((L, D_MODEL, MEM_WIDTH), D_MODEL ** -0.5),
        'w_co': nrm((L, MEM_WIDTH, D_MODEL), MEM_WIDTH ** -0.5),
        'norm_mlp_g': 1.0 + nrm((L, D_MODEL), 0.02),
        'w_up': nrm((L, D_MODEL, D_FF), D_MODEL ** -0.5),
        'w_down': nrm((L, D_FF, D_MODEL), D_FF ** -0.5),
        'norm_final_g': 1.0 + nrm((D_MODEL,), 0.02),
    }


def reference(x_prompt, x_sample, mem_prompt, cache_attn_k, cache_attn_v, cache_mem_k, cache_mem_v,
              state_shift, state_wkv, norm_mix_g, w_in, attn_sink, rel_bias_table, rwkv_mu, rwkv_w0,
              rwkv_w2, rwkv_a0, rwkv_a2, rwkv_g2, rwkv_k_k, rwkv_k_a, rwkv_r_k, rwkv_ln_w, rwkv_ln_b,
              w_out, norm_cross_g, norm_mem_g, w_cq, w_mk, w_mv, w_co, norm_mlp_g, w_up, w_down,
              norm_final_g):
    Bp = x_prompt.shape[0]
    xp, xs = x_prompt, x_sample
    p_k, p_v, p_mk, p_mv, p_sh, p_S = [], [], [], [], [], []
    s_k, s_v, s_sh, s_S = [], [], [], []
    for l in range(DEPTH):
        lw = {
            'norm_mix_g': norm_mix_g[l], 'w_in': w_in[l], 'attn_sink': attn_sink[l],
            'rwkv_mu': rwkv_mu[l], 'rwkv_w0': rwkv_w0[l], 'rwkv_w2': rwkv_w2[l],
            'rwkv_a0': rwkv_a0[l], 'rwkv_a2': rwkv_a2[l], 'rwkv_g2': rwkv_g2[l],
            'rwkv_k_k': rwkv_k_k[l], 'rwkv_k_a': rwkv_k_a[l], 'rwkv_r_k': rwkv_r_k[l],
            'rwkv_ln_w': rwkv_ln_w[l], 'rwkv_ln_b': rwkv_ln_b[l], 'w_out': w_out[l],
            'norm_cross_g': norm_cross_g[l], 'norm_mem_g': norm_mem_g[l],
            'w_cq': w_cq[l], 'w_mk': w_mk[l], 'w_mv': w_mv[l], 'w_co': w_co[l],
            'norm_mlp_g': norm_mlp_g[l], 'w_up': w_up[l], 'w_down': w_down[l],
        }
        mk, mv = memory_kv(mem_prompt, lw)
        shift0 = jnp.zeros((Bp, 1, RWKV_PROJ), xp.dtype)
        wkv_zero = jnp.zeros((Bp, RWKV_HEADS, RWKV_HEAD_DIM, RWKV_HEAD_DIM), jnp.float32)
        xp, kb, vb, sh, S = trunk_layer(xp, mk, mv, None, None, shift0, wkv_zero, lw, rel_bias_table)
        p_k.append(kb); p_v.append(vb); p_mk.append(mk); p_mv.append(mv); p_sh.append(sh); p_S.append(S)
        xs, kb2, vb2, sh2, S2 = trunk_layer(xs, cache_mem_k[l], cache_mem_v[l], cache_attn_k[l], cache_attn_v[l],
                                            state_shift[l], state_wkv[l], lw, rel_bias_table)
        s_k.append(kb2); s_v.append(vb2); s_sh.append(sh2); s_S.append(S2)
    y_prompt = rmsnorm(xp, norm_final_g)
    y_sample = rmsnorm(xs, norm_final_g)
    return (y_prompt, y_sample,
            jnp.stack(p_k), jnp.stack(p_v), jnp.stack(p_mk), jnp.stack(p_mv), jnp.stack(p_sh), jnp.stack(p_S),
            jnp.stack(s_k), jnp.stack(s_v), jnp.stack(s_sh), jnp.stack(s_S))
```

```python
import functools
import math

import numpy as np
import jax
import jax.numpy as jnp
from jax import lax
from jax.experimental import pallas as pl
from jax.experimental.pallas import tpu as pltpu

F32 = jnp.float32
BF16 = jnp.bfloat16

D_MODEL = 1024
CHUNK = 64
WINDOW = 128
HEAD_DIM = 64
ATTN_WIDTH = 512
ATTN_HEADS = 8
KV_HEADS = 2
GROUP = 4
KV_WIDTH = 128
RWKV_WIDTH = 512
RWKV_HEADS = 8
DECAY_LORA = 64
AAA_LORA = 64
GATE_LORA = 128
RWKV_PROJ = 1792
IN_PROJ = 2560
N_MEM = 256
MEM_HEADS = 4
MEM_HEAD_DIM = 128
MEM_WIDTH = 512
D_FF = 4096
REL_BUCKETS = 32
REL_MAX_DIST = 128
NORM_EPS = 1e-6
GN_EPS = 64e-5

V7X_VMEM_LIMIT_BYTES = 52 * 1024 * 1024
ROW_TILE = 256


def _params(*sem):
    return pltpu.CompilerParams(dimension_semantics=sem, vmem_limit_bytes=V7X_VMEM_LIMIT_BYTES)


def _const_spec(shape):
    nd = len(shape)
    return pl.BlockSpec(shape, lambda *_: (0,) * nd)


def _dot(a, b):
    return jnp.dot(a.astype(BF16), b.astype(BF16), preferred_element_type=F32)


def _dot_nt(a, b):
    return lax.dot_general(a.astype(BF16), b.astype(BF16), (((1,), (1,)), ((), ())),
                           preferred_element_type=F32)


def _rms(x, g):
    return x * lax.rsqrt(jnp.mean(x * x, axis=-1, keepdims=True) + NORM_EPS) * g


def _inproj_kernel(x_ref, g_ref, w_ref, q_ref, k_ref, v_ref, zr_ref):
    h = _rms(x_ref[...], g_ref[...]).astype(BF16)
    q_ref[...] = jnp.dot(h, w_ref[:, :ATTN_WIDTH], preferred_element_type=F32)
    k_ref[...] = jnp.dot(h, w_ref[:, ATTN_WIDTH:ATTN_WIDTH + KV_WIDTH], preferred_element_type=F32)
    v_ref[...] = jnp.dot(h, w_ref[:, ATTN_WIDTH + KV_WIDTH:ATTN_WIDTH + 2 * KV_WIDTH],
                         preferred_element_type=F32)
    zr_ref[...] = jnp.dot(h, w_ref[:, ATTN_WIDTH + 2 * KV_WIDTH:], preferred_element_type=F32)


def _in_proj(x2d, g, w_bf16):
    n = x2d.shape[0]
    tm = min(ROW_TILE, n)
    row = lambda w: pl.BlockSpec((tm, w), lambda i: (i, 0))
    return pl.pallas_call(
        _inproj_kernel,
        grid=(n // tm,),
        in_specs=[row(D_MODEL), _const_spec((1, D_MODEL)), _const_spec((D_MODEL, IN_PROJ))],
        out_specs=[row(ATTN_WIDTH), row(KV_WIDTH), row(KV_WIDTH), row(RWKV_PROJ)],
        out_shape=[jax.ShapeDtypeStruct((n, w), F32) for w in (ATTN_WIDTH, KV_WIDTH, KV_WIDTH, RWKV_PROJ)],
        compiler_params=_params("parallel"),
        name="in_proj",
    )(x2d, g, w_bf16)


def _t5_bucket(rel):
    half = REL_BUCKETS // 2
    max_exact = half // 2
    n = jnp.abs(rel)
    large = max_exact + (jnp.log(jnp.maximum(n, 1).astype(jnp.float32) / max_exact)
                         / math.log(REL_MAX_DIST / max_exact) * (half - max_exact)).astype(jnp.int32)
    large = jnp.minimum(large, half - 1)
    return jnp.where(rel > 0, half, 0) + jnp.where(n < max_exact, n, large)


def _bias_kernel(table_ref, bucket_ref, out_ref):
    bucket = bucket_ref[...]
    hits = [bucket == b for b in range(REL_BUCKETS)]
    for h in range(ATTN_HEADS):
        acc = jnp.zeros(bucket.shape, F32)
        for b in range(REL_BUCKETS):
            acc = jnp.where(hits[b], table_ref[b, h], acc)
        out_ref[h] = acc


def _rel_bias(table, n_q, n_k):
    rel = jnp.arange(n_k)[None, :] - WINDOW - jnp.arange(n_q)[:, None]
    bucket = _t5_bucket(rel).astype(jnp.int32)
    bias = pl.pallas_call(
        _bias_kernel,
        in_specs=[pl.BlockSpec(memory_space=pltpu.SMEM), pl.BlockSpec(memory_space=pltpu.VMEM)],
        out_specs=pl.BlockSpec(memory_space=pltpu.VMEM),
        out_shape=jax.ShapeDtypeStruct((ATTN_HEADS, n_q, n_k), F32),
        name="rel_bias",
    )(table, bucket)
    return bias.reshape(KV_HEADS, GROUP * n_q, n_k)


def _attn_kernel(n_kblocks, mask_history, sink_ref, q_ref, *rest):
    k_refs = rest[:n_kblocks]
    v_refs = rest[n_kblocks:2 * n_kblocks]
    bias_ref = rest[2 * n_kblocks]
    o_ref = rest[2 * n_kblocks + 1]
    n_q = q_ref.shape[0]
    q = q_ref[...]
    k_all = jnp.concatenate([r[...] for r in k_refs], axis=0)
    v_all = jnp.concatenate([r[...] for r in v_refs], axis=0)
    n_k = k_all.shape[0]
    row_group = lax.broadcasted_iota(jnp.int32, (GROUP * n_q, 1), 0) // n_q
    if mask_history:
        first_valid = WINDOW - pl.program_id(1) * CHUNK
        valid = lax.broadcasted_iota(jnp.int32, (1, n_k), 1) >= first_valid
    for kvh in range(KV_HEADS):
        lanes = slice(kvh * HEAD_DIM, (kvh + 1) * HEAD_DIM)
        qh = jnp.concatenate(
            [q[:, (kvh * GROUP + g) * HEAD_DIM:(kvh * GROUP + g + 1) * HEAD_DIM] for g in range(GROUP)], axis=0)
        s = _dot_nt(qh, k_all[:, lanes]) * (HEAD_DIM ** -0.5) + bias_ref[kvh]
        if mask_history:
            s = jnp.where(valid, s, -jnp.inf)
        sink = jnp.zeros((GROUP * n_q, 1), F32)
        for g in range(GROUP):
            sink = jnp.where(row_group == g, sink_ref[kvh * GROUP + g], sink)
        m = jnp.maximum(jnp.max(s, axis=-1, keepdims=True), sink)
        p = jnp.exp(s - m)
        den = jnp.sum(p, axis=-1, keepdims=True) + jnp.exp(sink - m)
        o = _dot(p / den, v_all[:, lanes])
        for g in range(GROUP):
            head = kvh * GROUP + g
            o_ref[:, head * HEAD_DIM:(head + 1) * HEAD_DIM] = o[g * n_q:(g + 1) * n_q]


def _prompt_attention(q, k, v, sink, bias, batch, seq):
    nc = seq // CHUNK
    q_spec = pl.BlockSpec((CHUNK, ATTN_WIDTH), lambda b, c: (b * nc + c, 0))

    def kv_spec(back):
        return pl.BlockSpec((CHUNK, KV_WIDTH), lambda b, c: (b * nc + jnp.maximum(c - back, 0), 0))

    kv_specs = [kv_spec(2), kv_spec(1), kv_spec(0)]
    return pl.pallas_call(
        functools.partial(_attn_kernel, 3, True),
        grid=(batch, nc),
        in_specs=[pl.BlockSpec(memory_space=pltpu.SMEM), q_spec] + kv_specs + kv_specs
                 + [_const_spec(bias.shape)],
        out_specs=q_spec,
        out_shape=jax.ShapeDtypeStruct(q.shape, F32),
        compiler_params=_params("parallel", "parallel"),
        name="prompt_attention",
    )(sink, q, k, k, k, v, v, v, bias)


def _sample_attention(q, k, v, k_past, v_past, sink, bias, batch, seq):
    q_spec = pl.BlockSpec((seq, ATTN_WIDTH), lambda b: (b, 0))
    past_spec = pl.BlockSpec((WINDOW, KV_WIDTH), lambda b: (b, 0))
    new_spec = pl.BlockSpec((seq, KV_WIDTH), lambda b: (b, 0))
    return pl.pallas_call(
        functools.partial(_attn_kernel, 2, False),
        grid=(batch,),
        in_specs=[pl.BlockSpec(memory_space=pltpu.SMEM), q_spec, past_spec, new_spec, past_spec, new_spec,
                  _const_spec(bias.shape)],
        out_specs=q_spec,
        out_shape=jax.ShapeDtypeStruct(q.shape, F32),
        compiler_params=_params("parallel"),
        name="sample_attention",
    )(sink, q, k_past, k, v_past, v, bias)


def _split2(x):
    hi = x.astype(BF16)
    lo = (x - hi.astype(F32)).astype(BF16)
    return hi, lo


def _mm3(a, b, dims):
    dn = (dims, ((), ()))
    d = lambda x, y: lax.dot_general(x, y, dn, preferred_element_type=F32)
    return d(a[0], b[0]) + d(a[0], b[1]) + d(a[1], b[0])


_NN = ((1,), (0,))
_NT = ((1,), (1,))
_TN = ((0,), (0,))


def _softplus(x):
    return jnp.maximum(x, 0.0) + jnp.log(1.0 + jnp.exp(-jnp.abs(x)))


def _sigmoid(x):
    return 1.0 / (1.0 + jnp.exp(-x))


def _rwkv_kernel(zr_ref, shift_ref, s0_ref, mu_ref, w0_ref, w2_ref, a0_ref, a2_ref, g2_ref, kk_ref, ka_ref,
                 rk_ref, lnw_ref, lnb_ref, seg_ref, tri_ref,
                 out_ref, s_ref, carry_ref, y_ref):
    c = pl.program_id(1)
    C = zr_ref.shape[0]

    @pl.when(c == 0)
    def _():
        carry_ref[0:1, :] = shift_ref[0]
        s_ref[...] = s0_ref[...]

    zr = zr_ref[...]
    first_row = lax.broadcasted_iota(jnp.int32, (C, 1), 0) == 0
    z_prev = jnp.where(first_row, carry_ref[0:1, :], pltpu.roll(zr, 1, axis=0))
    carry_ref[0:1, :] = zr[C - 1:C, :]
    zs = zr + (z_prev - zr) * mu_ref[...]

    W = RWKV_WIDTH
    r = zs[:, :W]
    k = zs[:, W:2 * W]
    v = zs[:, 2 * W:3 * W]
    wd = zs[:, 3 * W:3 * W + DECAY_LORA]
    ad = zs[:, 3 * W + DECAY_LORA:3 * W + DECAY_LORA + AAA_LORA]
    gd = zs[:, 3 * W + DECAY_LORA + AAA_LORA:]

    seg = seg_ref[...]

    def head_sum(x):
        hi, lo = _split2(x)
        return jnp.dot(hi, seg, preferred_element_type=F32) + jnp.dot(lo, seg, preferred_element_type=F32)

    w_log = -_softplus(-(w0_ref[...] + _dot(jnp.tanh(wd), w2_ref[...]))) - 0.5
    lw = -jnp.exp(w_log)
    a = _sigmoid(a0_ref[...] + _dot(ad, a2_ref[...]))
    gate = _dot(_sigmoid(gd), g2_ref[...])
    kk = k * kk_ref[...]
    kk = kk / jnp.maximum(jnp.sqrt(head_sum(kk * kk)), 1e-12)
    k2 = k * (1.0 + (a - 1.0) * ka_ref[...])
    bvec = kk * a

    tri = tri_ref[...]
    l1 = lw.astype(BF16)
    rem = lw - l1.astype(F32)
    l2 = rem.astype(BF16)
    l3 = (rem - l2.astype(F32)).astype(BF16)
    li = (jnp.dot(tri, l1, preferred_element_type=F32) + jnp.dot(tri, l2, preferred_element_type=F32)
          + jnp.dot(tri, l3, preferred_element_type=F32))
    ltot = li[C - 1:C, :]
    inv_p = jnp.exp(-li)
    to_end = jnp.exp(ltot - li)
    p_end = jnp.exp(ltot)
    at = _split2(-kk * jnp.exp(li - lw))
    rt = _split2(r * jnp.exp(li))
    bt = _split2(bvec * inv_p)
    kt = _split2(k2 * inv_p)
    bh = _split2(bvec * to_end)
    kh = _split2(k2 * to_end)
    vs = _split2(v)

    ti = lax.broadcasted_iota(jnp.int32, (C, C), 0)
    si = lax.broadcasted_iota(jnp.int32, (C, C), 1)
    strict = si < ti
    incl = si <= ti
    eye = jnp.where(si == ti, 1.0, 0.0).astype(F32)

    for h in range(RWKV_HEADS):
        sl = slice(h * HEAD_DIM, (h + 1) * HEAD_DIM)
        cut = lambda pair: (pair[0][:, sl], pair[1][:, sl])
        at_h, rt_h, bt_h, kt_h, bh_h, kh_h, v_h = map(cut, (at, rt, bt, kt, bh, kh, vs))
        left = tuple(jnp.concatenate([x, y], axis=0) for x, y in zip(at_h, rt_h))
        right = tuple(jnp.concatenate([x, y], axis=0) for x, y in zip(bt_h, kt_h))
        aa = _mm3(left, right, _NT)
        a_ab = jnp.where(strict, aa[:C, :C], 0.0)
        a_ak = jnp.where(strict, aa[:C, C:], 0.0)
        a_rb = jnp.where(incl, aa[C:, :C], 0.0)
        a_rk = jnp.where(incl, aa[C:, C:], 0.0)

        inv = eye + a_ab
        power = a_ab
        span = 1
        while span * 2 < C:
            ps = _split2(power)
            power = _mm3(ps, ps, _NN)
            inv = inv + _mm3(_split2(power), _split2(inv), _NN)
            span *= 2

        s_prev = s_ref[0, h]
        s_pair = _split2(s_prev)
        rhs = _mm3(at_h, s_pair, _NT) + _mm3(_split2(a_ak), v_h, _NN)
        u = _mm3(_split2(inv), _split2(rhs), _NN)
        u_pair = _split2(u)
        y = _mm3(rt_h, s_pair, _NT) + _mm3(_split2(a_rb), u_pair, _NN) + _mm3(_split2(a_rk), v_h, _NN)
        s_ref[0, h] = s_prev * p_end[:, sl] + _mm3(u_pair, bh_h, _TN) + _mm3(v_h, kh_h, _TN)
        y_ref[:, sl] = y

    y = y_ref[...]
    mean = head_sum(y) * (1.0 / HEAD_DIM)
    d = y - mean
    var = head_sum(d * d) * (1.0 / HEAD_DIM)
    yn = d * lax.rsqrt(var + GN_EPS) * lnw_ref[...] + lnb_ref[...]
    bonus = head_sum(r * k2 * rk_ref[...]) * v
    out_ref[...] = (yn + bonus) * gate


def _rwkv_mixer(zr, shift_prev, state0, lw, batch, seq, chunk):
    nc = seq // chunk
    seg = jnp.asarray(np.kron(np.eye(RWKV_HEADS), np.ones((HEAD_DIM, HEAD_DIM))), BF16)
    tri = jnp.asarray(np.tril(np.ones((chunk, chunk))), BF16)
    row = lambda name: lw[name].reshape(1, -1)
    params = [row('rwkv_mu'), row('rwkv_w0'), lw['rwkv_w2'].astype(BF16), row('rwkv_a0'),
              lw['rwkv_a2'].astype(BF16), lw['rwkv_g2'].astype(BF16), row('rwkv_k_k'), row('rwkv_k_a'),
              row('rwkv_r_k'), row('rwkv_ln_w'), row('rwkv_ln_b'), seg, tri]
    state_spec = pl.BlockSpec((1, RWKV_HEADS, HEAD_DIM, HEAD_DIM), lambda b, c: (b, 0, 0, 0))
    return pl.pallas_call(
        _rwkv_kernel,
        grid=(batch, nc),
        in_specs=[pl.BlockSpec((chunk, RWKV_PROJ), lambda b, c: (b * nc + c, 0)),
                  pl.BlockSpec((1, 1, RWKV_PROJ), lambda b, c: (b, 0, 0)),
                  state_spec] + [_const_spec(p.shape) for p in params],
        out_specs=[pl.BlockSpec((chunk, RWKV_WIDTH), lambda b, c: (b * nc + c, 0)), state_spec],
        out_shape=[jax.ShapeDtypeStruct((batch * seq, RWKV_WIDTH), F32),
                   jax.ShapeDtypeStruct(state0.shape, F32)],
        scratch_shapes=[pltpu.VMEM((8, RWKV_PROJ), F32), pltpu.VMEM((chunk, RWKV_WIDTH), F32)],
        compiler_params=_params("parallel", "arbitrary"),
        name="rwkv_mixer",
    )(zr, shift_prev, state0, *params)


def _outproj_kernel(x_ref, a_ref, r_ref, wo_ref, g_ref, wq_ref, x1_ref, qc_ref):
    x1 = (x_ref[...] + _dot(a_ref[...], wo_ref[:ATTN_WIDTH, :]) + _dot(r_ref[...], wo_ref[ATTN_WIDTH:, :]))
    x1_ref[...] = x1
    qc_ref[...] = _dot(_rms(x1, g_ref[...]), wq_ref[...])


def _out_proj(x2d, a_out, r_out, w_out, g_cross, w_cq):
    n = x2d.shape[0]
    tm = min(ROW_TILE, n)
    row = lambda w: pl.BlockSpec((tm, w), lambda i: (i, 0))
    return pl.pallas_call(
        _outproj_kernel,
        grid=(n // tm,),
        in_specs=[row(D_MODEL), row(ATTN_WIDTH), row(RWKV_WIDTH), _const_spec(w_out.shape),
                  _const_spec((1, D_MODEL)), _const_spec(w_cq.shape)],
        out_specs=[row(D_MODEL), row(MEM_WIDTH)],
        out_shape=[jax.ShapeDtypeStruct((n, D_MODEL), F32), jax.ShapeDtypeStruct((n, MEM_WIDTH), F32)],
        compiler_params=_params("parallel"),
        name="out_proj",
    )(x2d, a_out, r_out, w_out, g_cross, w_cq)


def _memkv_kernel(m_ref, g_ref, wk_ref, wv_ref, k_ref, v_ref):
    mn = _rms(m_ref[...], g_ref[...]).astype(BF16)
    k_ref[...] = jnp.dot(mn, wk_ref[...], preferred_element_type=F32)
    v_ref[...] = jnp.dot(mn, wv_ref[...], preferred_element_type=F32)


def _memory_kv(mem2d, g, w_mk, w_mv):
    n = mem2d.shape[0]
    tm = min(ROW_TILE, n)
    row = lambda w: pl.BlockSpec((tm, w), lambda i: (i, 0))
    return pl.pallas_call(
        _memkv_kernel,
        grid=(n // tm,),
        in_specs=[row(D_MODEL), _const_spec((1, D_MODEL)), _const_spec(w_mk.shape), _const_spec(w_mv.shape)],
        out_specs=[row(MEM_WIDTH), row(MEM_WIDTH)],
        out_shape=[jax.ShapeDtypeStruct((n, MEM_WIDTH), F32)] * 2,
        compiler_params=_params("parallel"),
        name="memory_kv",
    )(mem2d, g, w_mk, w_mv)


def _cross_kernel(x1_ref, q_ref, mk_ref, mv_ref, wco_ref, x2_ref):
    q = q_ref[...]
    mk = mk_ref[0]
    mv = mv_ref[0]
    outs = []
    for h in range(MEM_HEADS):
        sl = slice(h * MEM_HEAD_DIM, (h + 1) * MEM_HEAD_DIM)
        s = _dot_nt(q[:, sl], mk[:, sl]) * (MEM_HEAD_DIM ** -0.5)
        m = jnp.max(s, axis=-1, keepdims=True)
        p = jnp.exp(s - m)
        p = p / jnp.sum(p, axis=-1, keepdims=True)
        outs.append(_dot(p, mv[:, sl]))
    o = jnp.concatenate(outs, axis=-1)
    x2_ref[...] = x1_ref[...] + _dot(o, wco_ref[...])


def _cross_attention(x1, qc, mk, mv, w_co, batch, seq):
    tq = min(ROW_TILE, seq)
    nt = seq // tq
    row = lambda w: pl.BlockSpec((tq, w), lambda b, t: (b * nt + t, 0))
    mem_spec = pl.BlockSpec((1, N_MEM, MEM_WIDTH), lambda b, t: (b, 0, 0))
    return pl.pallas_call(
        _cross_kernel,
        grid=(batch, nt),
        in_specs=[row(D_MODEL), row(MEM_WIDTH), mem_spec, mem_spec, _const_spec(w_co.shape)],
        out_specs=row(D_MODEL),
        out_shape=jax.ShapeDtypeStruct(x1.shape, F32),
        compiler_params=_params("parallel", "parallel"),
        name="cross_attention",
    )(x1, qc, mk, mv, w_co)


def _mlp_kernel(x_ref, g_ref, wu_ref, wd_ref, gf_ref, y_ref):
    x = x_ref[...]
    up = _dot(_rms(x, g_ref[...]), wu_ref[...])
    act = jnp.square(jnp.maximum(up, 0.0))
    y_ref[...] = _rms(x + _dot(act, wd_ref[...]), gf_ref[...])


def _mlp(x2, g_mlp, w_up, w_down, g_final):
    n = x2.shape[0]
    tm = min(ROW_TILE, n)
    row = pl.BlockSpec((tm, D_MODEL), lambda i: (i, 0))
    return pl.pallas_call(
        _mlp_kernel,
        grid=(n // tm,),
        in_specs=[row, _const_spec((1, D_MODEL)), _const_spec(w_up.shape), _const_spec(w_down.shape),
                  _const_spec((1, D_MODEL))],
        out_specs=row,
        out_shape=jax.ShapeDtypeStruct(x2.shape, F32),
        compiler_params=_params("parallel"),
        name="mlp",
    )(x2, g_mlp, w_up, w_down, g_final)


def _trunk(x, mk, mv, k_past, v_past, shift_prev, state0, lw, table, chunk):
    batch, seq = x.shape[0], x.shape[1]
    x2d = x.reshape(batch * seq, D_MODEL)
    q, k, v, zr = _in_proj(x2d, lw['norm_mix_g'], lw['w_in'])
    if k_past is None:
        bias = _rel_bias(table, CHUNK, WINDOW + CHUNK)
        a_out = _prompt_attention(q, k, v, lw['attn_sink'], bias, batch, seq)
        k3 = k.reshape(batch, seq, KV_WIDTH)
        v3 = v.reshape(batch, seq, KV_WIDTH)
        k_buf, v_buf = k3[:, -WINDOW:], v3[:, -WINDOW:]
    else:
        bias = _rel_bias(table, seq, WINDOW + seq)
        a_out = _sample_attention(q, k, v, k_past.reshape(batch * WINDOW, KV_WIDTH),
                                  v_past.reshape(batch * WINDOW, KV_WIDTH), lw['attn_sink'], bias, batch, seq)
        k_buf = jnp.concatenate([k_past, k.reshape(batch, seq, KV_WIDTH)], axis=1)[:, -WINDOW:]
        v_buf = jnp.concatenate([v_past, v.reshape(batch, seq, KV_WIDTH)], axis=1)[:, -WINDOW:]
    r_out, state = _rwkv_mixer(zr, shift_prev, state0, lw, batch, seq, chunk)
    shift_new = zr.reshape(batch, seq, RWKV_PROJ)[:, -1:]
    x1, qc = _out_proj(x2d, a_out, r_out, lw['w_out'], lw['norm_cross_g'], lw['w_cq'])
    x2 = _cross_attention(x1, qc, mk, mv, lw['w_co'], batch, seq)
    y = _mlp(x2, lw['norm_mlp_g'], lw['w_up'], lw['w_down'], lw['norm_final_g'])
    kv_shape = (batch, WINDOW, KV_HEADS, HEAD_DIM)
    return y.reshape(x.shape), k_buf.reshape(kv_shape), v_buf.reshape(kv_shape), shift_new, state


def kernel(x_prompt, x_sample, mem_prompt, cache_attn_k, cache_attn_v, cache_mem_k, cache_mem_v, state_shift,
           state_wkv, norm_mix_g, w_in, attn_sink, rel_bias_table, rwkv_mu, rwkv_w0, rwkv_w2, rwkv_a0, rwkv_a2,
           rwkv_g2, rwkv_k_k, rwkv_k_a, rwkv_r_k, rwkv_ln_w, rwkv_ln_b, w_out, norm_cross_g, norm_mem_g, w_cq,
           w_mk, w_mv, w_co, norm_mlp_g, w_up, w_down, norm_final_g):
    assert norm_mix_g.shape[0] == 1, "single-layer trunk"
    bp, dec_b = x_prompt.shape[0], x_sample.shape[0]
    vec = lambda p: p[0].reshape(1, -1)
    lw = {
        'norm_mix_g': vec(norm_mix_g), 'w_in': w_in[0].astype(BF16), 'attn_sink': attn_sink[0],
        'rwkv_mu': rwkv_mu[0], 'rwkv_w0': rwkv_w0[0], 'rwkv_w2': rwkv_w2[0], 'rwkv_a0': rwkv_a0[0],
        'rwkv_a2': rwkv_a2[0], 'rwkv_g2': rwkv_g2[0], 'rwkv_k_k': rwkv_k_k[0], 'rwkv_k_a': rwkv_k_a[0],
        'rwkv_r_k': rwkv_r_k[0], 'rwkv_ln_w': rwkv_ln_w[0], 'rwkv_ln_b': rwkv_ln_b[0],
        'w_out': w_out[0].astype(BF16), 'norm_cross_g': vec(norm_cross_g), 'w_cq': w_cq[0].astype(BF16),
        'w_co': w_co[0].astype(BF16), 'norm_mlp_g': vec(norm_mlp_g), 'w_up': w_up[0].astype(BF16),
        'w_down': w_down[0].astype(BF16), 'norm_final_g': norm_final_g.reshape(1, -1),
    }
    mk, mv = _memory_kv(mem_prompt.reshape(bp * N_MEM, D_MODEL), vec(norm_mem_g),
                        w_mk[0].astype(BF16), w_mv[0].astype(BF16))
    mk = mk.reshape(bp, N_MEM, MEM_WIDTH)
    mv = mv.reshape(bp, N_MEM, MEM_WIDTH)
    shift0 = jnp.zeros((bp, 1, RWKV_PROJ), F32)
    wkv0 = jnp.zeros((bp, RWKV_HEADS, HEAD_DIM, HEAD_DIM), F32)
    yp, pk, pv, psh, pS = _trunk(x_prompt, mk, mv, None, None, shift0, wkv0, lw, rel_bias_table, CHUNK)
    ys, sk, sv, ssh, sS = _trunk(
        x_sample, cache_mem_k[0].reshape(dec_b, N_MEM, MEM_WIDTH), cache_mem_v[0].reshape(dec_b, N_MEM, MEM_WIDTH),
        cache_attn_k[0].reshape(dec_b, WINDOW, KV_WIDTH), cache_attn_v[0].reshape(dec_b, WINDOW, KV_WIDTH),
        state_shift[0], state_wkv[0], lw, rel_bias_table, x_sample.shape[1])
    mem_shape = (1, bp, N_MEM, MEM_HEADS, MEM_HEAD_DIM)
    return (yp, ys, pk[None], pv[None], mk.reshape(mem_shape), mv.reshape(mem_shape), psh[None], pS[None],
            sk[None], sv[None], ssh[None], sS[None])
```

```python
import functools
import math

import numpy as np
import jax
import jax.numpy as jnp
from jax import lax
from jax.experimental import pallas as pl
from jax.experimental.pallas import tpu as pltpu

F32 = jnp.float32
BF16 = jnp.bfloat16

D_MODEL = 1024
CHUNK = 64
WINDOW = 128
HEAD_DIM = 64
ATTN_WIDTH = 512
ATTN_HEADS = 8
KV_HEADS = 2
GROUP = 4
KV_WIDTH = 128
RWKV_WIDTH = 512
RWKV_HEADS = 8
DECAY_LORA = 64
AAA_LORA = 64
GATE_LORA = 128
RWKV_PROJ = 1792
IN_PROJ = 2560
N_MEM = 256
MEM_HEADS = 4
MEM_HEAD_DIM = 128
MEM_WIDTH = 512
D_FF = 4096
REL_BUCKETS = 32
REL_MAX_DIST = 128
NORM_EPS = 1e-6
GN_EPS = 64e-5

V7X_VMEM_LIMIT_BYTES = 52 * 1024 * 1024
ROW_TILE = 256
RWKV_SUB_CHUNKS = 2


def _params(*sem):
    return pltpu.CompilerParams(dimension_semantics=sem, vmem_limit_bytes=V7X_VMEM_LIMIT_BYTES)


def _const_spec(shape):
    nd = len(shape)
    return pl.BlockSpec(shape, lambda *_: (0,) * nd)


def _dot(a, b):
    return jnp.dot(a.astype(BF16), b.astype(BF16), preferred_element_type=F32)


def _dot_nt(a, b):
    return lax.dot_general(a.astype(BF16), b.astype(BF16), (((1,), (1,)), ((), ())),
                           preferred_element_type=F32)


def _rms(x, g):
    return x * lax.rsqrt(jnp.mean(x * x, axis=-1, keepdims=True) + NORM_EPS) * g


def _inproj_kernel(x_ref, g_ref, w_ref, q_ref, k_ref, v_ref, zr_ref):
    h = _rms(x_ref[...], g_ref[...]).astype(BF16)
    q_ref[...] = jnp.dot(h, w_ref[:, :ATTN_WIDTH], preferred_element_type=F32)
    k_ref[...] = jnp.dot(h, w_ref[:, ATTN_WIDTH:ATTN_WIDTH + KV_WIDTH], preferred_element_type=F32)
    v_ref[...] = jnp.dot(h, w_ref[:, ATTN_WIDTH + KV_WIDTH:ATTN_WIDTH + 2 * KV_WIDTH],
                         preferred_element_type=F32)
    zr_ref[...] = jnp.dot(h, w_ref[:, ATTN_WIDTH + 2 * KV_WIDTH:], preferred_element_type=F32)


def _in_proj(x2d, g, w_bf16):
    n = x2d.shape[0]
    tm = min(ROW_TILE, n)
    row = lambda w: pl.BlockSpec((tm, w), lambda i: (i, 0))
    return pl.pallas_call(
        _inproj_kernel,
        grid=(n // tm,),
        in_specs=[row(D_MODEL), _const_spec((1, D_MODEL)), _const_spec((D_MODEL, IN_PROJ))],
        out_specs=[row(ATTN_WIDTH), row(KV_WIDTH), row(KV_WIDTH), row(RWKV_PROJ)],
        out_shape=[jax.ShapeDtypeStruct((n, w), F32) for w in (ATTN_WIDTH, KV_WIDTH, KV_WIDTH, RWKV_PROJ)],
        compiler_params=_params("parallel"),
        name="in_proj",
    )(x2d, g, w_bf16)


def _t5_bucket(rel):
    half = REL_BUCKETS // 2
    max_exact = half // 2
    n = jnp.abs(rel)
    large = max_exact + (jnp.log(jnp.maximum(n, 1).astype(jnp.float32) / max_exact)
                         / math.log(REL_MAX_DIST / max_exact) * (half - max_exact)).astype(jnp.int32)
    large = jnp.minimum(large, half - 1)
    return jnp.where(rel > 0, half, 0) + jnp.where(n < max_exact, n, large)


def _bias_kernel(table_ref, bucket_ref, out_ref):
    bucket = bucket_ref[...]
    hits = [bucket == b for b in range(REL_BUCKETS)]
    for h in range(ATTN_HEADS):
        acc = jnp.zeros(bucket.shape, F32)
        for b in range(REL_BUCKETS):
            acc = jnp.where(hits[b], table_ref[b, h], acc)
        out_ref[h] = acc


def _rel_bias(table, n_q, n_k):
    rel = jnp.arange(n_k)[None, :] - WINDOW - jnp.arange(n_q)[:, None]
    bucket = _t5_bucket(rel).astype(jnp.int32)
    bias = pl.pallas_call(
        _bias_kernel,
        in_specs=[pl.BlockSpec(memory_space=pltpu.SMEM), pl.BlockSpec(memory_space=pltpu.VMEM)],
        out_specs=pl.BlockSpec(memory_space=pltpu.VMEM),
        out_shape=jax.ShapeDtypeStruct((ATTN_HEADS, n_q, n_k), F32),
        name="rel_bias",
    )(table, bucket)
    return bias.reshape(KV_HEADS, GROUP * n_q, n_k)


def _attn_kernel(n_kblocks, mask_history, sink_ref, q_ref, *rest):
    k_refs = rest[:n_kblocks]
    v_refs = rest[n_kblocks:2 * n_kblocks]
    bias_ref = rest[2 * n_kblocks]
    o_ref = rest[2 * n_kblocks + 1]
    n_q = q_ref.shape[0]
    q = q_ref[...]
    k_all = jnp.concatenate([r[...] for r in k_refs], axis=0)
    v_all = jnp.concatenate([r[...] for r in v_refs], axis=0)
    n_k = k_all.shape[0]
    row_group = lax.broadcasted_iota(jnp.int32, (GROUP * n_q, 1), 0) // n_q
    if mask_history:
        first_valid = WINDOW - pl.program_id(1) * CHUNK
        valid = lax.broadcasted_iota(jnp.int32, (1, n_k), 1) >= first_valid
    for kvh in range(KV_HEADS):
        lanes = slice(kvh * HEAD_DIM, (kvh + 1) * HEAD_DIM)
        qh = jnp.concatenate(
            [q[:, (kvh * GROUP + g) * HEAD_DIM:(kvh * GROUP + g + 1) * HEAD_DIM] for g in range(GROUP)], axis=0)
        s = _dot_nt(qh, k_all[:, lanes]) * (HEAD_DIM ** -0.5) + bias_ref[kvh]
        if mask_history:
            s = jnp.where(valid, s, -jnp.inf)
        sink = jnp.zeros((GROUP * n_q, 1), F32)
        for g in range(GROUP):
            sink = jnp.where(row_group == g, sink_ref[kvh * GROUP + g], sink)
        m = jnp.maximum(jnp.max(s, axis=-1, keepdims=True), sink)
        p = jnp.exp(s - m)
        den = jnp.sum(p, axis=-1, keepdims=True) + jnp.exp(sink - m)
        o = _dot(p / den, v_all[:, lanes])
        for g in range(GROUP):
            head = kvh * GROUP + g
            o_ref[:, head * HEAD_DIM:(head + 1) * HEAD_DIM] = o[g * n_q:(g + 1) * n_q]


def _prompt_attention(q, k, v, sink, bias, batch, seq):
    nc = seq // CHUNK
    q_spec = pl.BlockSpec((CHUNK, ATTN_WIDTH), lambda b, c: (b * nc + c, 0))

    def kv_spec(back):
        return pl.BlockSpec((CHUNK, KV_WIDTH), lambda b, c: (b * nc + jnp.maximum(c - back, 0), 0))

    kv_specs = [kv_spec(2), kv_spec(1), kv_spec(0)]
    return pl.pallas_call(
        functools.partial(_attn_kernel, 3, True),
        grid=(batch, nc),
        in_specs=[pl.BlockSpec(memory_space=pltpu.SMEM), q_spec] + kv_specs + kv_specs
                 + [_const_spec(bias.shape)],
        out_specs=q_spec,
        out_shape=jax.ShapeDtypeStruct(q.shape, F32),
        compiler_params=_params("parallel", "parallel"),
        name="prompt_attention",
    )(sink, q, k, k, k, v, v, v, bias)


def _sample_attention(q, k, v, k_past, v_past, sink, bias, batch, seq):
    q_spec = pl.BlockSpec((seq, ATTN_WIDTH), lambda b: (b, 0))
    past_spec = pl.BlockSpec((WINDOW, KV_WIDTH), lambda b: (b, 0))
    new_spec = pl.BlockSpec((seq, KV_WIDTH), lambda b: (b, 0))
    return pl.pallas_call(
        functools.partial(_attn_kernel, 2, False),
        grid=(batch,),
        in_specs=[pl.BlockSpec(memory_space=pltpu.SMEM), q_spec, past_spec, new_spec, past_spec, new_spec,
                  _const_spec(bias.shape)],
        out_specs=q_spec,
        out_shape=jax.ShapeDtypeStruct(q.shape, F32),
        compiler_params=_params("parallel"),
        name="sample_attention",
    )(sink, q, k_past, k, v_past, v, bias)


def _split2(x):
    hi = x.astype(BF16)
    lo = (x - hi.astype(F32)).astype(BF16)
    return hi, lo


def _mm3(a, b, dims):
    dn = (dims, ((), ()))
    d = lambda x, y: lax.dot_general(x, y, dn, preferred_element_type=F32)
    return d(a[0], b[0]) + d(a[0], b[1]) + d(a[1], b[0])


_NN = ((1,), (0,))
_NT = ((1,), (1,))
_TN = ((0,), (0,))


def _softplus(x):
    return jnp.maximum(x, 0.0) + jnp.log(1.0 + jnp.exp(-jnp.abs(x)))


def _sigmoid(x):
    return 1.0 / (1.0 + jnp.exp(-x))


def _run_interleaved(chains):
    active = list(chains)
    while active:
        still = []
        for ch in active:
            try:
                next(ch)
                still.append(ch)
            except StopIteration:
                pass
        active = still


def _rwkv_kernel(C, zr_ref, shift_ref, s0_ref, mu_ref, w0_ref, w2_ref, a0_ref, a2_ref, g2_ref, kk_ref, ka_ref,
                 rk_ref, lnw_ref, lnb_ref, seg_ref, tri_ref, upper_ref,
                 out_ref, s_ref, carry_ref, y_ref):
    c = pl.program_id(1)
    R = zr_ref.shape[0]
    n_sub = R // C

    @pl.when(c == 0)
    def _():
        carry_ref[0:1, :] = shift_ref[0]
        s_ref[...] = s0_ref[...]

    zr = zr_ref[...]
    first_row = lax.broadcasted_iota(jnp.int32, (R, 1), 0) == 0
    z_prev = jnp.where(first_row, carry_ref[0:1, :], pltpu.roll(zr, 1, axis=0))
    carry_ref[0:1, :] = zr[R - 1:R, :]
    zs = zr + (z_prev - zr) * mu_ref[...]

    W = RWKV_WIDTH
    r = zs[:, :W]
    k = zs[:, W:2 * W]
    v = zs[:, 2 * W:3 * W]
    wd = zs[:, 3 * W:3 * W + DECAY_LORA]
    ad = zs[:, 3 * W + DECAY_LORA:3 * W + DECAY_LORA + AAA_LORA]
    gd = zs[:, 3 * W + DECAY_LORA + AAA_LORA:]

    seg = seg_ref[...]

    def head_sum(x):
        hi, lo = _split2(x)
        return jnp.dot(hi, seg, preferred_element_type=F32) + jnp.dot(lo, seg, preferred_element_type=F32)

    w_log = -_softplus(-(w0_ref[...] + _dot(jnp.tanh(wd), w2_ref[...]))) - 0.5
    lw = -jnp.exp(w_log)
    a = _sigmoid(a0_ref[...] + _dot(ad, a2_ref[...]))
    gate = _dot(_sigmoid(gd), g2_ref[...])
    kk = k * kk_ref[...]
    kk = kk / jnp.maximum(jnp.sqrt(head_sum(kk * kk)), 1e-12)
    k2 = k * (1.0 + (a - 1.0) * ka_ref[...])
    bvec = kk * a

    l1 = lw.astype(BF16)
    rem = lw - l1.astype(F32)
    l2 = rem.astype(BF16)
    l3 = (rem - l2.astype(F32)).astype(BF16)

    def time_sum(m):
        return (jnp.dot(m, l1, preferred_element_type=F32) + jnp.dot(m, l2, preferred_element_type=F32)
                + jnp.dot(m, l3, preferred_element_type=F32))

    li = time_sum(tri_ref[...])
    lrev = time_sum(upper_ref[...])
    inv_p = jnp.exp(-li)
    to_end = jnp.exp(lrev)
    at = _split2(-kk * jnp.exp(li - lw))
    rt = _split2(r * jnp.exp(li))
    bt = _split2(bvec * inv_p)
    kt = _split2(k2 * inv_p)
    bh = _split2(bvec * to_end)
    kh = _split2(k2 * to_end)
    vs = _split2(v)

    ti = lax.broadcasted_iota(jnp.int32, (C, C), 0)
    si = lax.broadcasted_iota(jnp.int32, (C, C), 1)
    strict = si < ti
    incl = si <= ti
    eye = jnp.where(si == ti, 1.0, 0.0).astype(F32)

    ready = {}

    def local_chain(j, h):
        rows = slice(j * C, (j + 1) * C)
        sl = slice(h * HEAD_DIM, (h + 1) * HEAD_DIM)
        cut = lambda pair: (pair[0][rows, sl], pair[1][rows, sl])
        at_h, rt_h, bt_h, kt_h, bh_h, kh_h, v_h = map(cut, (at, rt, bt, kt, bh, kh, vs))
        left = tuple(jnp.concatenate([x, y], axis=0) for x, y in zip(at_h, rt_h))
        right = tuple(jnp.concatenate([x, y], axis=0) for x, y in zip(bt_h, kt_h))
        aa = _mm3(left, right, _NT)
        yield
        a_ab = jnp.where(strict, aa[:C, :C], 0.0)
        a_ak = jnp.where(strict, aa[:C, C:], 0.0)
        a_rb = jnp.where(incl, aa[C:, :C], 0.0)
        a_rk = jnp.where(incl, aa[C:, C:], 0.0)
        inv = eye + a_ab
        ps = _split2(a_ab)
        span = 1
        while span * 2 < C:
            power = _mm3(ps, ps, _NN)
            ps = _split2(power)
            yield
            inv = inv + _mm3(ps, _split2(inv), _NN)
            span *= 2
        yield
        ready[(j, h)] = dict(inv=_split2(inv), ak=_split2(a_ak), rb=_split2(a_rb), rk=_split2(a_rk),
                             at=at_h, rt=rt_h, bh=bh_h, kh=kh_h, v=v_h)

    state = [s_ref[0, h] for h in range(RWKV_HEADS)]

    def state_chain(j, h):
        d = ready.pop((j, h))
        rows = slice(j * C, (j + 1) * C)
        sl = slice(h * HEAD_DIM, (h + 1) * HEAD_DIM)
        s_prev = state[h]
        s_pair = _split2(s_prev)
        rhs = _mm3(d['at'], s_pair, _NT) + _mm3(d['ak'], d['v'], _NN)
        y0 = _mm3(d['rt'], s_pair, _NT) + _mm3(d['rk'], d['v'], _NN)
        yield
        u_pair = _split2(_mm3(d['inv'], _split2(rhs), _NN))
        yield
        p_end = jnp.exp(li[(j + 1) * C - 1:(j + 1) * C, sl])
        state[h] = s_prev * p_end + _mm3(u_pair, d['bh'], _TN) + _mm3(d['v'], d['kh'], _TN)
        y_ref[rows, sl] = y0 + _mm3(d['rb'], u_pair, _NN)
        yield

    heads = range(RWKV_HEADS)
    _run_interleaved([local_chain(0, h) for h in heads])
    for j in range(n_sub):
        chains = [state_chain(j, h) for h in heads]
        if j + 1 < n_sub:
            chains += [local_chain(j + 1, h) for h in heads]
        _run_interleaved(chains)
    for h in heads:
        s_ref[0, h] = state[h]

    y = y_ref[...]
    mean = head_sum(y) * (1.0 / HEAD_DIM)
    d = y - mean
    var = head_sum(d * d) * (1.0 / HEAD_DIM)
    yn = d * lax.rsqrt(var + GN_EPS) * lnw_ref[...] + lnb_ref[...]
    bonus = head_sum(r * k2 * rk_ref[...]) * v
    out_ref[...] = (yn + bonus) * gate


def _rwkv_mixer(zr, shift_prev, state0, lw, batch, seq, chunk, n_sub):
    rows = chunk * n_sub
    steps = seq // rows
    seg = jnp.asarray(np.kron(np.eye(RWKV_HEADS), np.ones((HEAD_DIM, HEAD_DIM))), BF16)
    blocks = np.eye(n_sub)
    tri = jnp.asarray(np.kron(blocks, np.tril(np.ones((chunk, chunk)))), BF16)
    upper = jnp.asarray(np.kron(blocks, np.triu(np.ones((chunk, chunk)), 1)), BF16)
    row = lambda name: lw[name].reshape(1, -1)
    params = [row('rwkv_mu'), row('rwkv_w0'), lw['rwkv_w2'].astype(BF16), row('rwkv_a0'),
              lw['rwkv_a2'].astype(BF16), lw['rwkv_g2'].astype(BF16), row('rwkv_k_k'), row('rwkv_k_a'),
              row('rwkv_r_k'), row('rwkv_ln_w'), row('rwkv_ln_b'), seg, tri, upper]
    state_spec = pl.BlockSpec((1, RWKV_HEADS, HEAD_DIM, HEAD_DIM), lambda b, c: (b, 0, 0, 0))
    return pl.pallas_call(
        functools.partial(_rwkv_kernel, chunk),
        grid=(batch, steps),
        in_specs=[pl.BlockSpec((rows, RWKV_PROJ), lambda b, c: (b * steps + c, 0)),
                  pl.BlockSpec((1, 1, RWKV_PROJ), lambda b, c: (b, 0, 0)),
                  state_spec] + [_const_spec(p.shape) for p in params],
        out_specs=[pl.BlockSpec((rows, RWKV_WIDTH), lambda b, c: (b * steps + c, 0)), state_spec],
        out_shape=[jax.ShapeDtypeStruct((batch * seq, RWKV_WIDTH), F32),
                   jax.ShapeDtypeStruct(state0.shape, F32)],
        scratch_shapes=[pltpu.VMEM((8, RWKV_PROJ), F32), pltpu.VMEM((rows, RWKV_WIDTH), F32)],
        compiler_params=_params("parallel", "arbitrary"),
        name="rwkv_mixer",
    )(zr, shift_prev, state0, *params)


def _outproj_kernel(x_ref, a_ref, r_ref, wo_ref, g_ref, wq_ref, x1_ref, qc_ref):
    x1 = (x_ref[...] + _dot(a_ref[...], wo_ref[:ATTN_WIDTH, :]) + _dot(r_ref[...], wo_ref[ATTN_WIDTH:, :]))
    x1_ref[...] = x1
    qc_ref[...] = _dot(_rms(x1, g_ref[...]), wq_ref[...])


def _out_proj(x2d, a_out, r_out, w_out, g_cross, w_cq):
    n = x2d.shape[0]
    tm = min(ROW_TILE, n)
    row = lambda w: pl.BlockSpec((tm, w), lambda i: (i, 0))
    return pl.pallas_call(
        _outproj_kernel,
        grid=(n // tm,),
        in_specs=[row(D_MODEL), row(ATTN_WIDTH), row(RWKV_WIDTH), _const_spec(w_out.shape),
                  _const_spec((1, D_MODEL)), _const_spec(w_cq.shape)],
        out_specs=[row(D_MODEL), row(MEM_WIDTH)],
        out_shape=[jax.ShapeDtypeStruct((n, D_MODEL), F32), jax.ShapeDtypeStruct((n, MEM_WIDTH), F32)],
        compiler_params=_params("parallel"),
        name="out_proj",
    )(x2d, a_out, r_out, w_out, g_cross, w_cq)


def _memkv_kernel(m_ref, g_ref, wk_ref, wv_ref, k_ref, v_ref):
    mn = _rms(m_ref[...], g_ref[...]).astype(BF16)
    k_ref[...] = jnp.dot(mn, wk_ref[...], preferred_element_type=F32)
    v_ref[...] = jnp.dot(mn, wv_ref[...], preferred_element_type=F32)


def _memory_kv(mem2d, g, w_mk, w_mv):
    n = mem2d.shape[0]
    tm = min(ROW_TILE, n)
    row = lambda w: pl.BlockSpec((tm, w), lambda i: (i, 0))
    return pl.pallas_call(
        _memkv_kernel,
        grid=(n // tm,),
        in_specs=[row(D_MODEL), _const_spec((1, D_MODEL)), _const_spec(w_mk.shape), _const_spec(w_mv.shape)],
        out_specs=[row(MEM_WIDTH), row(MEM_WIDTH)],
        out_shape=[jax.ShapeDtypeStruct((n, MEM_WIDTH), F32)] * 2,
        compiler_params=_params("parallel"),
        name="memory_kv",
    )(mem2d, g, w_mk, w_mv)


def _cross_kernel(x1_ref, q_ref, mk_ref, mv_ref, wco_ref, x2_ref):
    q = q_ref[...]
    mk = mk_ref[0]
    mv = mv_ref[0]
    outs = []
    for h in range(MEM_HEADS):
        sl = slice(h * MEM_HEAD_DIM, (h + 1) * MEM_HEAD_DIM)
        s = _dot_nt(q[:, sl], mk[:, sl]) * (MEM_HEAD_DIM ** -0.5)
        m = jnp.max(s, axis=-1, keepdims=True)
        p = jnp.exp(s - m)
        p = p / jnp.sum(p, axis=-1, keepdims=True)
        outs.append(_dot(p, mv[:, sl]))
    o = jnp.concatenate(outs, axis=-1)
    x2_ref[...] = x1_ref[...] + _dot(o, wco_ref[...])


def _cross_attention(x1, qc, mk, mv, w_co, batch, seq):
    tq = min(ROW_TILE, seq)
    nt = seq // tq
    row = lambda w: pl.BlockSpec((tq, w), lambda b, t: (b * nt + t, 0))
    mem_spec = pl.BlockSpec((1, N_MEM, MEM_WIDTH), lambda b, t: (b, 0, 0))
    return pl.pallas_call(
        _cross_kernel,
        grid=(batch, nt),
        in_specs=[row(D_MODEL), row(MEM_WIDTH), mem_spec, mem_spec, _const_spec(w_co.shape)],
        out_specs=row(D_MODEL),
        out_shape=jax.ShapeDtypeStruct(x1.shape, F32),
        compiler_params=_params("parallel", "parallel"),
        name="cross_attention",
    )(x1, qc, mk, mv, w_co)


def _mlp_kernel(x_ref, g_ref, wu_ref, wd_ref, gf_ref, y_ref):
    x = x_ref[...]
    up = _dot(_rms(x, g_ref[...]), wu_ref[...])
    act = jnp.square(jnp.maximum(up, 0.0))
    y_ref[...] = _rms(x + _dot(act, wd_ref[...]), gf_ref[...])


def _mlp(x2, g_mlp, w_up, w_down, g_final):
    n = x2.shape[0]
    tm = min(ROW_TILE, n)
    row = pl.BlockSpec((tm, D_MODEL), lambda i: (i, 0))
    return pl.pallas_call(
        _mlp_kernel,
        grid=(n // tm,),
        in_specs=[row, _const_spec((1, D_MODEL)), _const_spec(w_up.shape), _const_spec(w_down.shape),
                  _const_spec((1, D_MODEL))],
        out_specs=row,
        out_shape=jax.ShapeDtypeStruct(x2.shape, F32),
        compiler_params=_params("parallel"),
        name="mlp",
    )(x2, g_mlp, w_up, w_down, g_final)


def _trunk(x, mk, mv, k_past, v_past, shift_prev, state0, lw, table, chunk, n_sub):
    batch, seq = x.shape[0], x.shape[1]
    x2d = x.reshape(batch * seq, D_MODEL)
    q, k, v, zr = _in_proj(x2d, lw['norm_mix_g'], lw['w_in'])
    if k_past is None:
        bias = _rel_bias(table, CHUNK, WINDOW + CHUNK)
        a_out = _prompt_attention(q, k, v, lw['attn_sink'], bias, batch, seq)
        k3 = k.reshape(batch, seq, KV_WIDTH)
        v3 = v.reshape(batch, seq, KV_WIDTH)
        k_buf, v_buf = k3[:, -WINDOW:], v3[:, -WINDOW:]
    else:
        bias = _rel_bias(table, seq, WINDOW + seq)
        a_out = _sample_attention(q, k, v, k_past.reshape(batch * WINDOW, KV_WIDTH),
                                  v_past.reshape(batch * WINDOW, KV_WIDTH), lw['attn_sink'], bias, batch, seq)
        k_buf = jnp.concatenate([k_past, k.reshape(batch, seq, KV_WIDTH)], axis=1)[:, -WINDOW:]
        v_buf = jnp.concatenate([v_past, v.reshape(batch, seq, KV_WIDTH)], axis=1)[:, -WINDOW:]
    r_out, state = _rwkv_mixer(zr, shift_prev, state0, lw, batch, seq, chunk, n_sub)
    shift_new = zr.reshape(batch, seq, RWKV_PROJ)[:, -1:]
    x1, qc = _out_proj(x2d, a_out, r_out, lw['w_out'], lw['norm_cross_g'], lw['w_cq'])
    x2 = _cross_attention(x1, qc, mk, mv, lw['w_co'], batch, seq)
    y = _mlp(x2, lw['norm_mlp_g'], lw['w_up'], lw['w_down'], lw['norm_final_g'])
    kv_shape = (batch, WINDOW, KV_HEADS, HEAD_DIM)
    return y.reshape(x.shape), k_buf.reshape(kv_shape), v_buf.reshape(kv_shape), shift_new, state


def kernel(x_prompt, x_sample, mem_prompt, cache_attn_k, cache_attn_v, cache_mem_k, cache_mem_v, state_shift,
           state_wkv, norm_mix_g, w_in, attn_sink, rel_bias_table, rwkv_mu, rwkv_w0, rwkv_w2, rwkv_a0, rwkv_a2,
           rwkv_g2, rwkv_k_k, rwkv_k_a, rwkv_r_k, rwkv_ln_w, rwkv_ln_b, w_out, norm_cross_g, norm_mem_g, w_cq,
           w_mk, w_mv, w_co, norm_mlp_g, w_up, w_down, norm_final_g):
    assert norm_mix_g.shape[0] == 1, "single-layer trunk"
    bp, dec_b = x_prompt.shape[0], x_sample.shape[0]
    vec = lambda p: p[0].reshape(1, -1)
    lw = {
        'norm_mix_g': vec(norm_mix_g), 'w_in': w_in[0].astype(BF16), 'attn_sink': attn_sink[0],
        'rwkv_mu': rwkv_mu[0], 'rwkv_w0': rwkv_w0[0], 'rwkv_w2': rwkv_w2[0], 'rwkv_a0': rwkv_a0[0],
        'rwkv_a2': rwkv_a2[0], 'rwkv_g2': rwkv_g2[0], 'rwkv_k_k': rwkv_k_k[0], 'rwkv_k_a': rwkv_k_a[0],
        'rwkv_r_k': rwkv_r_k[0], 'rwkv_ln_w': rwkv_ln_w[0], 'rwkv_ln_b': rwkv_ln_b[0],
        'w_out': w_out[0].astype(BF16), 'norm_cross_g': vec(norm_cross_g), 'w_cq': w_cq[0].astype(BF16),
        'w_co': w_co[0].astype(BF16), 'norm_mlp_g': vec(norm_mlp_g), 'w_up': w_up[0].astype(BF16),
        'w_down': w_down[0].astype(BF16), 'norm_final_g': norm_final_g.reshape(1, -1),
    }
    mk, mv = _memory_kv(mem_prompt.reshape(bp * N_MEM, D_MODEL), vec(norm_mem_g),
                        w_mk[0].astype(BF16), w_mv[0].astype(BF16))
    mk = mk.reshape(bp, N_MEM, MEM_WIDTH)
    mv = mv.reshape(bp, N_MEM, MEM_WIDTH)
    shift0 = jnp.zeros((bp, 1, RWKV_PROJ), F32)
    wkv0 = jnp.zeros((bp, RWKV_HEADS, HEAD_DIM, HEAD_DIM), F32)
    yp, pk, pv, psh, pS = _trunk(x_prompt, mk, mv, None, None, shift0, wkv0, lw, rel_bias_table, CHUNK,
                                 RWKV_SUB_CHUNKS)
    ys, sk, sv, ssh, sS = _trunk(
        x_sample, cache_mem_k[0].reshape(dec_b, N_MEM, MEM_WIDTH), cache_mem_v[0].reshape(dec_b, N_MEM, MEM_WIDTH),
        cache_attn_k[0].reshape(dec_b, WINDOW, KV_WIDTH), cache_attn_v[0].reshape(dec_b, WINDOW, KV_WIDTH),
        state_shift[0], state_wkv[0], lw, rel_bias_table, x_sample.shape[1], 1)
    mem_shape = (1, bp, N_MEM, MEM_HEADS, MEM_HEAD_DIM)
    return (yp, ys, pk[None], pv[None], mk.reshape(mem_shape), mv.reshape(mem_shape), psh[None], pS[None],
            sk[None], sv[None], ssh[None], sS[None])
```

```python
import functools
import math

import numpy as np
import jax
import jax.numpy as jnp
from jax import lax
from jax.experimental import pallas as pl
from jax.experimental.pallas import tpu as pltpu

F32 = jnp.float32
BF16 = jnp.bfloat16

D_MODEL = 1024
CHUNK = 64
WINDOW = 128
HEAD_DIM = 64
ATTN_WIDTH = 512
ATTN_HEADS = 8
KV_HEADS = 2
GROUP = 4
KV_WIDTH = 128
RWKV_WIDTH = 512
RWKV_HEADS = 8
DECAY_LORA = 64
AAA_LORA = 64
GATE_LORA = 128
RWKV_PROJ = 1792
IN_PROJ = 2560
N_MEM = 256
MEM_HEADS = 4
MEM_HEAD_DIM = 128
MEM_WIDTH = 512
D_FF = 4096
REL_BUCKETS = 32
REL_MAX_DIST = 128
NORM_EPS = 1e-6
GN_EPS = 64e-5

V7X_VMEM_LIMIT_BYTES = 52 * 1024 * 1024
ROW_TILE = 256
RWKV_SUB_CHUNKS = 2


def _params(*sem):
    return pltpu.CompilerParams(dimension_semantics=sem, vmem_limit_bytes=V7X_VMEM_LIMIT_BYTES)


def _const_spec(shape):
    nd = len(shape)
    return pl.BlockSpec(shape, lambda *_: (0,) * nd)


def _dot(a, b):
    return jnp.dot(a.astype(BF16), b.astype(BF16), preferred_element_type=F32)


def _dot_nt(a, b):
    return lax.dot_general(a.astype(BF16), b.astype(BF16), (((1,), (1,)), ((), ())),
                           preferred_element_type=F32)


def _rms(x, g):
    return x * lax.rsqrt(jnp.mean(x * x, axis=-1, keepdims=True) + NORM_EPS) * g


def _inproj_kernel(x_ref, g_ref, w_ref, q_ref, k_ref, v_ref, zr_ref):
    h = _rms(x_ref[...], g_ref[...]).astype(BF16)
    q_ref[...] = jnp.dot(h, w_ref[:, :ATTN_WIDTH], preferred_element_type=F32)
    k_ref[...] = jnp.dot(h, w_ref[:, ATTN_WIDTH:ATTN_WIDTH + KV_WIDTH], preferred_element_type=F32)
    v_ref[...] = jnp.dot(h, w_ref[:, ATTN_WIDTH + KV_WIDTH:ATTN_WIDTH + 2 * KV_WIDTH],
                         preferred_element_type=F32)
    zr_ref[...] = jnp.dot(h, w_ref[:, ATTN_WIDTH + 2 * KV_WIDTH:], preferred_element_type=F32)


def _in_proj(x2d, g, w_bf16):
    n = x2d.shape[0]
    tm = min(ROW_TILE, n)
    row = lambda w: pl.BlockSpec((tm, w), lambda i: (i, 0))
    return pl.pallas_call(
        _inproj_kernel,
        grid=(n // tm,),
        in_specs=[row(D_MODEL), _const_spec((1, D_MODEL)), _const_spec((D_MODEL, IN_PROJ))],
        out_specs=[row(ATTN_WIDTH), row(KV_WIDTH), row(KV_WIDTH), row(RWKV_PROJ)],
        out_shape=[jax.ShapeDtypeStruct((n, w), F32) for w in (ATTN_WIDTH, KV_WIDTH, KV_WIDTH, RWKV_PROJ)],
        compiler_params=_params("parallel"),
        name="in_proj",
    )(x2d, g, w_bf16)


def _t5_bucket(rel):
    half = REL_BUCKETS // 2
    max_exact = half // 2
    n = jnp.abs(rel)
    large = max_exact + (jnp.log(jnp.maximum(n, 1).astype(jnp.float32) / max_exact)
                         / math.log(REL_MAX_DIST / max_exact) * (half - max_exact)).astype(jnp.int32)
    large = jnp.minimum(large, half - 1)
    return jnp.where(rel > 0, half, 0) + jnp.where(n < max_exact, n, large)


def _bias_kernel(table_ref, bucket_ref, out_ref):
    bucket = bucket_ref[...]
    hits = [bucket == b for b in range(REL_BUCKETS)]
    for h in range(ATTN_HEADS):
        acc = jnp.zeros(bucket.shape, F32)
        for b in range(REL_BUCKETS):
            acc = jnp.where(hits[b], table_ref[b, h], acc)
        out_ref[h] = acc


def _rel_bias(table, n_q, n_k):
    rel = jnp.arange(n_k)[None, :] - WINDOW - jnp.arange(n_q)[:, None]
    bucket = _t5_bucket(rel).astype(jnp.int32)
    bias = pl.pallas_call(
        _bias_kernel,
        in_specs=[pl.BlockSpec(memory_space=pltpu.SMEM), pl.BlockSpec(memory_space=pltpu.VMEM)],
        out_specs=pl.BlockSpec(memory_space=pltpu.VMEM),
        out_shape=jax.ShapeDtypeStruct((ATTN_HEADS, n_q, n_k), F32),
        name="rel_bias",
    )(table, bucket)
    return bias.reshape(KV_HEADS, GROUP * n_q, n_k)


def _attn_kernel(n_kblocks, mask_history, sink_ref, q_ref, *rest):
    k_refs = rest[:n_kblocks]
    v_refs = rest[n_kblocks:2 * n_kblocks]
    bias_ref = rest[2 * n_kblocks]
    o_ref = rest[2 * n_kblocks + 1]
    n_q = q_ref.shape[0]
    q = q_ref[...]
    k_all = jnp.concatenate([r[...] for r in k_refs], axis=0)
    v_all = jnp.concatenate([r[...] for r in v_refs], axis=0)
    n_k = k_all.shape[0]
    row_group = lax.broadcasted_iota(jnp.int32, (GROUP * n_q, 1), 0) // n_q
    if mask_history:
        first_valid = WINDOW - pl.program_id(1) * CHUNK
        valid = lax.broadcasted_iota(jnp.int32, (1, n_k), 1) >= first_valid
    for kvh in range(KV_HEADS):
        lanes = slice(kvh * HEAD_DIM, (kvh + 1) * HEAD_DIM)
        qh = jnp.concatenate(
            [q[:, (kvh * GROUP + g) * HEAD_DIM:(kvh * GROUP + g + 1) * HEAD_DIM] for g in range(GROUP)], axis=0)
        s = _dot_nt(qh, k_all[:, lanes]) * (HEAD_DIM ** -0.5) + bias_ref[kvh]
        if mask_history:
            s = jnp.where(valid, s, -jnp.inf)
        sink = jnp.zeros((GROUP * n_q, 1), F32)
        for g in range(GROUP):
            sink = jnp.where(row_group == g, sink_ref[kvh * GROUP + g], sink)
        m = jnp.maximum(jnp.max(s, axis=-1, keepdims=True), sink)
        p = jnp.exp(s - m)
        den = jnp.sum(p, axis=-1, keepdims=True) + jnp.exp(sink - m)
        o = _dot(p / den, v_all[:, lanes])
        for g in range(GROUP):
            head = kvh * GROUP + g
            o_ref[:, head * HEAD_DIM:(head + 1) * HEAD_DIM] = o[g * n_q:(g + 1) * n_q]


def _prompt_attention(q, k, v, sink, bias, batch, seq):
    nc = seq // CHUNK
    q_spec = pl.BlockSpec((CHUNK, ATTN_WIDTH), lambda b, c: (b * nc + c, 0))

    def kv_spec(back):
        return pl.BlockSpec((CHUNK, KV_WIDTH), lambda b, c: (b * nc + jnp.maximum(c - back, 0), 0))

    kv_specs = [kv_spec(2), kv_spec(1), kv_spec(0)]
    return pl.pallas_call(
        functools.partial(_attn_kernel, 3, True),
        grid=(batch, nc),
        in_specs=[pl.BlockSpec(memory_space=pltpu.SMEM), q_spec] + kv_specs + kv_specs
                 + [_const_spec(bias.shape)],
        out_specs=q_spec,
        out_shape=jax.ShapeDtypeStruct(q.shape, F32),
        compiler_params=_params("parallel", "parallel"),
        name="prompt_attention",
    )(sink, q, k, k, k, v, v, v, bias)


def _sample_attention(q, k, v, k_past, v_past, sink, bias, batch, seq):
    q_spec = pl.BlockSpec((seq, ATTN_WIDTH), lambda b: (b, 0))
    past_spec = pl.BlockSpec((WINDOW, KV_WIDTH), lambda b: (b, 0))
    new_spec = pl.BlockSpec((seq, KV_WIDTH), lambda b: (b, 0))
    return pl.pallas_call(
        functools.partial(_attn_kernel, 2, False),
        grid=(batch,),
        in_specs=[pl.BlockSpec(memory_space=pltpu.SMEM), q_spec, past_spec, new_spec, past_spec, new_spec,
                  _const_spec(bias.shape)],
        out_specs=q_spec,
        out_shape=jax.ShapeDtypeStruct(q.shape, F32),
        compiler_params=_params("parallel"),
        name="sample_attention",
    )(sink, q, k_past, k, v_past, v, bias)


def _split2(x):
    hi = x.astype(BF16)
    lo = (x - hi.astype(F32)).astype(BF16)
    return hi, lo


def _mm3(a, b, dims):
    dn = (dims, ((), ()))
    d = lambda x, y: lax.dot_general(x, y, dn, preferred_element_type=F32)
    return d(a[0], b[0]) + d(a[0], b[1]) + d(a[1], b[0])


_NN = ((1,), (0,))
_NT = ((1,), (1,))
_TN = ((0,), (0,))


def _softplus(x):
    return jnp.maximum(x, 0.0) + jnp.log(1.0 + jnp.exp(-jnp.abs(x)))


def _sigmoid(x):
    return 1.0 / (1.0 + jnp.exp(-x))


def _run_interleaved(chains):
    active = list(chains)
    while active:
        still = []
        for ch in active:
            try:
                next(ch)
                still.append(ch)
            except StopIteration:
                pass
        active = still


def _cat_rows(*xs):
    return jnp.concatenate(xs, axis=0)


def _cat_lanes(*xs):
    return jnp.concatenate(xs, axis=1)


def _dg(a, b, dims):
    return lax.dot_general(a, b, (dims, ((), ())), preferred_element_type=F32)


def _rwkv_kernel(valid_rows, zr_ref, shift_ref, s0_ref, mu_ref, w0_ref, w2_ref, a0_ref, a2_ref, g2_ref, kk_ref,
                 ka_ref, rk_ref, lnw_ref, lnb_ref, seg_ref, tri_ref, upper_ref,
                 out_ref, s_ref, carry_ref, y_ref, sbd_ref):
    c = pl.program_id(1)
    C = CHUNK
    R = zr_ref.shape[0]
    n_sub = R // C
    pairs = range(RWKV_HEADS // 2)
    PAIR = 2 * HEAD_DIM

    @pl.when(c == 0)
    def _():
        carry_ref[0:1, :] = shift_ref[0]
        zero = jnp.zeros((HEAD_DIM, HEAD_DIM), F32)
        for p in pairs:
            sbd_ref[p] = _cat_rows(_cat_lanes(s0_ref[0, 2 * p], zero), _cat_lanes(zero, s0_ref[0, 2 * p + 1]))

    zr = zr_ref[...]
    first_row = lax.broadcasted_iota(jnp.int32, (R, 1), 0) == 0
    z_prev = jnp.where(first_row, carry_ref[0:1, :], pltpu.roll(zr, 1, axis=0))
    carry_ref[0:1, :] = zr[R - 1:R, :]
    zs = zr + (z_prev - zr) * mu_ref[...]

    W = RWKV_WIDTH
    r = zs[:, :W]
    k = zs[:, W:2 * W]
    v = zs[:, 2 * W:3 * W]
    wd = zs[:, 3 * W:3 * W + DECAY_LORA]
    ad = zs[:, 3 * W + DECAY_LORA:3 * W + DECAY_LORA + AAA_LORA]
    gd = zs[:, 3 * W + DECAY_LORA + AAA_LORA:]

    seg = seg_ref[...]

    def head_sum(x):
        hi, lo = _split2(x)
        return jnp.dot(hi, seg, preferred_element_type=F32) + jnp.dot(lo, seg, preferred_element_type=F32)

    w_log = -_softplus(-(w0_ref[...] + _dot(jnp.tanh(wd), w2_ref[...]))) - 0.5
    lw = -jnp.exp(w_log)
    a = _sigmoid(a0_ref[...] + _dot(ad, a2_ref[...]))
    gate = _dot(_sigmoid(gd), g2_ref[...])
    kk = k * kk_ref[...]
    kk = kk / jnp.maximum(jnp.sqrt(head_sum(kk * kk)), 1e-12)
    k2 = k * (1.0 + (a - 1.0) * ka_ref[...])
    if valid_rows < R:
        live = lax.broadcasted_iota(jnp.int32, (R, 1), 0) < valid_rows
        lw = jnp.where(live, lw, 0.0)
        kk = jnp.where(live, kk, 0.0)
        k2 = jnp.where(live, k2, 0.0)
    bvec = kk * a

    l1 = lw.astype(BF16)
    rem = lw - l1.astype(F32)
    l2 = rem.astype(BF16)
    l3 = (rem - l2.astype(F32)).astype(BF16)

    def time_sum(m):
        return (jnp.dot(m, l1, preferred_element_type=F32) + jnp.dot(m, l2, preferred_element_type=F32)
                + jnp.dot(m, l3, preferred_element_type=F32))

    li = time_sum(tri_ref[...])
    lrev = time_sum(upper_ref[...])
    inv_p = jnp.exp(-li)
    to_end = jnp.exp(lrev)
    at = _split2(-kk * jnp.exp(li - lw))
    rt = _split2(r * jnp.exp(li))
    bt = _split2(bvec * inv_p)
    kt = _split2(k2 * inv_p)
    bh = _split2(bvec * to_end)
    kh = _split2(k2 * to_end)
    vs = _split2(v)

    lane = lax.broadcasted_iota(jnp.int32, (C, PAIR), 1)
    trow = lax.broadcasted_iota(jnp.int32, (C, PAIR), 0)
    even = lane < HEAD_DIM
    tcol = jnp.where(even, lane, lane - HEAD_DIM)
    strict = tcol < trow
    incl = tcol <= trow
    eye = jnp.where(tcol == trow, 1.0, 0.0).astype(F32)
    brow = lax.broadcasted_iota(jnp.int32, (PAIR, PAIR), 0) < HEAD_DIM
    bcol = lax.broadcasted_iota(jnp.int32, (PAIR, PAIR), 1) < HEAD_DIM
    on_diag = brow == bcol

    def bd(x):
        zero = jnp.zeros_like(x)
        return _cat_rows(jnp.where(even, x, zero), jnp.where(even, zero, x))

    def bd2(pair):
        return bd(pair[0]), bd(pair[1])

    def mm(a_pair, w_pair, dims=_NN):
        if dims == _NN:
            first = _dg(_cat_lanes(a_pair[0], a_pair[1]), _cat_rows(w_pair[0], w_pair[0]), dims)
        else:
            first = _dg(_cat_lanes(a_pair[0], a_pair[1]), _cat_lanes(w_pair[0], w_pair[0]), dims)
        return first + _dg(a_pair[0], w_pair[1], dims)

    ready = {}

    def local_chain(j, p):
        rows = slice(j * C, (j + 1) * C)
        lanes = slice(p * PAIR, (p + 1) * PAIR)
        cut = lambda pair: (pair[0][rows, lanes], pair[1][rows, lanes])
        at_p, rt_p, bt_p, kt_p, bh_p, kh_p, v_p = map(cut, (at, rt, bt, kt, bh, kh, vs))
        left = (_cat_rows(at_p[0], rt_p[0]), _cat_rows(at_p[1], rt_p[1]))
        right = (_cat_rows(bd(bt_p[0]), bd(kt_p[0])), _cat_rows(bd(bt_p[1]), bd(kt_p[1])))
        aa = mm(left, right, _NT)
        yield
        a_ab = jnp.where(strict, aa[:C, :PAIR], 0.0)
        a_ak = jnp.where(strict, aa[:C, PAIR:], 0.0)
        a_rb = jnp.where(incl, aa[C:, :PAIR], 0.0)
        a_rk = jnp.where(incl, aa[C:, PAIR:], 0.0)
        inv = eye + a_ab
        ps = _split2(a_ab)
        power = mm(ps, bd2(ps))
        span = 2
        yield
        while span < C:
            ps = _split2(power)
            iw = bd2(_split2(inv))
            if span * 2 < C:
                pw = bd2(ps)
                both = mm(ps, (_cat_lanes(iw[0], pw[0]), _cat_lanes(iw[1], pw[1])))
                inv = inv + both[:, :PAIR]
                power = both[:, PAIR:]
            else:
                inv = inv + mm(ps, iw)
            span *= 2
            yield
        ready[(j, p)] = dict(inv=_split2(inv), akrk=_split2(_cat_rows(a_ak, a_rk)), rb=_split2(a_rb), left=left,
                             bhkh=(_cat_rows(bh_p[0], kh_p[0]), _cat_rows(bh_p[1], kh_p[1])), v=v_p)

    state = [sbd_ref[p] for p in pairs]

    def state_chain(j, p):
        d = ready.pop((j, p))
        rows = slice(j * C, (j + 1) * C)
        lanes = slice(p * PAIR, (p + 1) * PAIR)
        s_prev = state[p]
        v_hi, v_lo = d['v']
        both = mm(d['left'], _split2(s_prev), _NT) + mm(d['akrk'], (bd(v_hi), bd(v_lo)))
        rhs = both[:C]
        y0 = both[C:]
        yield
        u_pair = _split2(mm(d['inv'], bd2(_split2(rhs))))
        yield
        y_ref[rows, lanes] = y0 + mm(d['rb'], bd2(u_pair))
        t_hi = _cat_rows(u_pair[0], v_hi)
        t_lo = _cat_rows(u_pair[1], v_lo)
        w_hi, w_lo = d['bhkh']
        upd = _dg(_cat_rows(t_hi, t_lo), _cat_rows(w_hi, w_hi), _TN) + _dg(t_hi, w_lo, _TN)
        p_end = jnp.exp(li[(j + 1) * C - 1:(j + 1) * C, lanes])
        state[p] = s_prev * p_end + jnp.where(on_diag, upd, 0.0)
        yield

    _run_interleaved([local_chain(0, p) for p in pairs])
    for j in range(n_sub):
        chains = [state_chain(j, p) for p in pairs]
        if j + 1 < n_sub:
            chains += [local_chain(j + 1, p) for p in pairs]
        _run_interleaved(chains)
    for p in pairs:
        sbd_ref[p] = state[p]

    @pl.when(c == pl.num_programs(1) - 1)
    def _():
        for p in pairs:
            s_ref[0, 2 * p] = state[p][:HEAD_DIM, :HEAD_DIM]
            s_ref[0, 2 * p + 1] = state[p][HEAD_DIM:, HEAD_DIM:]

    y = y_ref[...]
    mean = head_sum(y) * (1.0 / HEAD_DIM)
    d = y - mean
    var = head_sum(d * d) * (1.0 / HEAD_DIM)
    yn = d * lax.rsqrt(var + GN_EPS) * lnw_ref[...] + lnb_ref[...]
    bonus = head_sum(r * k2 * rk_ref[...]) * v
    out_ref[...] = (yn + bonus) * gate


def _rwkv_mixer(zr, shift_prev, state0, lw, batch, seq, n_sub):
    valid = seq
    if seq < CHUNK:
        zr = jnp.pad(zr.reshape(batch, seq, RWKV_PROJ), ((0, 0), (0, CHUNK - seq), (0, 0))).reshape(-1, RWKV_PROJ)
        seq, n_sub = CHUNK, 1
    rows = CHUNK * n_sub
    steps = seq // rows
    seg = jnp.asarray(np.kron(np.eye(RWKV_HEADS), np.ones((HEAD_DIM, HEAD_DIM))), BF16)
    blocks = np.eye(n_sub)
    tri = jnp.asarray(np.kron(blocks, np.tril(np.ones((CHUNK, CHUNK)))), BF16)
    upper = jnp.asarray(np.kron(blocks, np.triu(np.ones((CHUNK, CHUNK)), 1)), BF16)
    row = lambda name: lw[name].reshape(1, -1)
    params = [row('rwkv_mu'), row('rwkv_w0'), lw['rwkv_w2'].astype(BF16), row('rwkv_a0'),
              lw['rwkv_a2'].astype(BF16), lw['rwkv_g2'].astype(BF16), row('rwkv_k_k'), row('rwkv_k_a'),
              row('rwkv_r_k'), row('rwkv_ln_w'), row('rwkv_ln_b'), seg, tri, upper]
    state_spec = pl.BlockSpec((1, RWKV_HEADS, HEAD_DIM, HEAD_DIM), lambda b, c: (b, 0, 0, 0))
    out, state = pl.pallas_call(
        functools.partial(_rwkv_kernel, min(valid, rows)),
        grid=(batch, steps),
        in_specs=[pl.BlockSpec((rows, RWKV_PROJ), lambda b, c: (b * steps + c, 0)),
                  pl.BlockSpec((1, 1, RWKV_PROJ), lambda b, c: (b, 0, 0)),
                  state_spec] + [_const_spec(p.shape) for p in params],
        out_specs=[pl.BlockSpec((rows, RWKV_WIDTH), lambda b, c: (b * steps + c, 0)), state_spec],
        out_shape=[jax.ShapeDtypeStruct((batch * seq, RWKV_WIDTH), F32),
                   jax.ShapeDtypeStruct(state0.shape, F32)],
        scratch_shapes=[pltpu.VMEM((8, RWKV_PROJ), F32), pltpu.VMEM((rows, RWKV_WIDTH), F32),
                        pltpu.VMEM((RWKV_HEADS // 2, 2 * HEAD_DIM, 2 * HEAD_DIM), F32)],
        compiler_params=_params("parallel", "arbitrary"),
        name="rwkv_mixer",
    )(zr, shift_prev, state0, *params)
    if valid < seq:
        out = out.reshape(batch, seq, RWKV_WIDTH)[:, :valid].reshape(batch * valid, RWKV_WIDTH)
    return out, state


def _outproj_kernel(x_ref, a_ref, r_ref, wo_ref, g_ref, wq_ref, x1_ref, qc_ref):
    x1 = (x_ref[...] + _dot(a_ref[...], wo_ref[:ATTN_WIDTH, :]) + _dot(r_ref[...], wo_ref[ATTN_WIDTH:, :]))
    x1_ref[...] = x1
    qc_ref[...] = _dot(_rms(x1, g_ref[...]), wq_ref[...])


def _out_proj(x2d, a_out, r_out, w_out, g_cross, w_cq):
    n = x2d.shape[0]
    tm = min(ROW_TILE, n)
    row = lambda w: pl.BlockSpec((tm, w), lambda i: (i, 0))
    return pl.pallas_call(
        _outproj_kernel,
        grid=(n // tm,),
        in_specs=[row(D_MODEL), row(ATTN_WIDTH), row(RWKV_WIDTH), _const_spec(w_out.shape),
                  _const_spec((1, D_MODEL)), _const_spec(w_cq.shape)],
        out_specs=[row(D_MODEL), row(MEM_WIDTH)],
        out_shape=[jax.ShapeDtypeStruct((n, D_MODEL), F32), jax.ShapeDtypeStruct((n, MEM_WIDTH), F32)],
        compiler_params=_params("parallel"),
        name="out_proj",
    )(x2d, a_out, r_out, w_out, g_cross, w_cq)


def _memkv_kernel(m_ref, g_ref, wk_ref, wv_ref, k_ref, v_ref):
    mn = _rms(m_ref[...], g_ref[...]).astype(BF16)
    k_ref[...] = jnp.dot(mn, wk_ref[...], preferred_element_type=F32)
    v_ref[...] = jnp.dot(mn, wv_ref[...], preferred_element_type=F32)


def _memory_kv(mem2d, g, w_mk, w_mv):
    n = mem2d.shape[0]
    tm = min(ROW_TILE, n)
    row = lambda w: pl.BlockSpec((tm, w), lambda i: (i, 0))
    return pl.pallas_call(
        _memkv_kernel,
        grid=(n // tm,),
        in_specs=[row(D_MODEL), _const_spec((1, D_MODEL)), _const_spec(w_mk.shape), _const_spec(w_mv.shape)],
        out_specs=[row(MEM_WIDTH), row(MEM_WIDTH)],
        out_shape=[jax.ShapeDtypeStruct((n, MEM_WIDTH), F32)] * 2,
        compiler_params=_params("parallel"),
        name="memory_kv",
    )(mem2d, g, w_mk, w_mv)


def _cross_kernel(x1_ref, q_ref, mk_ref, mv_ref, wco_ref, x2_ref):
    q = q_ref[...]
    mk = mk_ref[0]
    mv = mv_ref[0]
    outs = []
    for h in range(MEM_HEADS):
        sl = slice(h * MEM_HEAD_DIM, (h + 1) * MEM_HEAD_DIM)
        s = _dot_nt(q[:, sl], mk[:, sl]) * (MEM_HEAD_DIM ** -0.5)
        m = jnp.max(s, axis=-1, keepdims=True)
        p = jnp.exp(s - m)
        p = p / jnp.sum(p, axis=-1, keepdims=True)
        outs.append(_dot(p, mv[:, sl]))
    o = jnp.concatenate(outs, axis=-1)
    x2_ref[...] = x1_ref[...] + _dot(o, wco_ref[...])


def _cross_attention(x1, qc, mk, mv, w_co, batch, seq):
    tq = min(ROW_TILE, seq)
    nt = seq // tq
    row = lambda w: pl.BlockSpec((tq, w), lambda b, t: (b * nt + t, 0))
    mem_spec = pl.BlockSpec((1, N_MEM, MEM_WIDTH), lambda b, t: (b, 0, 0))
    return pl.pallas_call(
        _cross_kernel,
        grid=(batch, nt),
        in_specs=[row(D_MODEL), row(MEM_WIDTH), mem_spec, mem_spec, _const_spec(w_co.shape)],
        out_specs=row(D_MODEL),
        out_shape=jax.ShapeDtypeStruct(x1.shape, F32),
        compiler_params=_params("parallel", "parallel"),
        name="cross_attention",
    )(x1, qc, mk, mv, w_co)


def _mlp_kernel(x_ref, g_ref, wu_ref, wd_ref, gf_ref, y_ref):
    x = x_ref[...]
    up = _dot(_rms(x, g_ref[...]), wu_ref[...])
    act = jnp.square(jnp.maximum(up, 0.0))
    y_ref[...] = _rms(x + _dot(act, wd_ref[...]), gf_ref[...])


def _mlp(x2, g_mlp, w_up, w_down, g_final):
    n = x2.shape[0]
    tm = min(ROW_TILE, n)
    row = pl.BlockSpec((tm, D_MODEL), lambda i: (i, 0))
    return pl.pallas_call(
        _mlp_kernel,
        grid=(n // tm,),
        in_specs=[row, _const_spec((1, D_MODEL)), _const_spec(w_up.shape), _const_spec(w_down.shape),
                  _const_spec((1, D_MODEL))],
        out_specs=row,
        out_shape=jax.ShapeDtypeStruct(x2.shape, F32),
        compiler_params=_params("parallel"),
        name="mlp",
    )(x2, g_mlp, w_up, w_down, g_final)


def _trunk(x, mk, mv, k_past, v_past, shift_prev, state0, lw, table, n_sub):
    batch, seq = x.shape[0], x.shape[1]
    x2d = x.reshape(batch * seq, D_MODEL)
    q, k, v, zr = _in_proj(x2d, lw['norm_mix_g'], lw['w_in'])
    if k_past is None:
        bias = _rel_bias(table, CHUNK, WINDOW + CHUNK)
        a_out = _prompt_attention(q, k, v, lw['attn_sink'], bias, batch, seq)
        k3 = k.reshape(batch, seq, KV_WIDTH)
        v3 = v.reshape(batch, seq, KV_WIDTH)
        k_buf, v_buf = k3[:, -WINDOW:], v3[:, -WINDOW:]
    else:
        bias = _rel_bias(table, seq, WINDOW + seq)
        a_out = _sample_attention(q, k, v, k_past.reshape(batch * WINDOW, KV_WIDTH),
                                  v_past.reshape(batch * WINDOW, KV_WIDTH), lw['attn_sink'], bias, batch, seq)
        k_buf = jnp.concatenate([k_past, k.reshape(batch, seq, KV_WIDTH)], axis=1)[:, -WINDOW:]
        v_buf = jnp.concatenate([v_past, v.reshape(batch, seq, KV_WIDTH)], axis=1)[:, -WINDOW:]
    r_out, state = _rwkv_mixer(zr, shift_prev, state0, lw, batch, seq, n_sub)
    shift_new = zr.reshape(batch, seq, RWKV_PROJ)[:, -1:]
    x1, qc = _out_proj(x2d, a_out, r_out, lw['w_out'], lw['norm_cross_g'], lw['w_cq'])
    x2 = _cross_attention(x1, qc, mk, mv, lw['w_co'], batch, seq)
    y = _mlp(x2, lw['norm_mlp_g'], lw['w_up'], lw['w_down'], lw['norm_final_g'])
    kv_shape = (batch, WINDOW, KV_HEADS, HEAD_DIM)
    return y.reshape(x.shape), k_buf.reshape(kv_shape), v_buf.reshape(kv_shape), shift_new, state


def kernel(x_prompt, x_sample, mem_prompt, cache_attn_k, cache_attn_v, cache_mem_k, cache_mem_v, state_shift,
           state_wkv, norm_mix_g, w_in, attn_sink, rel_bias_table, rwkv_mu, rwkv_w0, rwkv_w2, rwkv_a0, rwkv_a2,
           rwkv_g2, rwkv_k_k, rwkv_k_a, rwkv_r_k, rwkv_ln_w, rwkv_ln_b, w_out, norm_cross_g, norm_mem_g, w_cq,
           w_mk, w_mv, w_co, norm_mlp_g, w_up, w_down, norm_final_g):
    assert norm_mix_g.shape[0] == 1, "single-layer trunk"
    bp, dec_b = x_prompt.shape[0], x_sample.shape[0]
    vec = lambda p: p[0].reshape(1, -1)
    lw = {
        'norm_mix_g': vec(norm_mix_g), 'w_in': w_in[0].astype(BF16), 'attn_sink': attn_sink[0],
        'rwkv_mu': rwkv_mu[0], 'rwkv_w0': rwkv_w0[0], 'rwkv_w2': rwkv_w2[0], 'rwkv_a0': rwkv_a0[0],
        'rwkv_a2': rwkv_a2[0], 'rwkv_g2': rwkv_g2[0], 'rwkv_k_k': rwkv_k_k[0], 'rwkv_k_a': rwkv_k_a[0],
        'rwkv_r_k': rwkv_r_k[0], 'rwkv_ln_w': rwkv_ln_w[0], 'rwkv_ln_b': rwkv_ln_b[0],
        'w_out': w_out[0].astype(BF16), 'norm_cross_g': vec(norm_cross_g), 'w_cq': w_cq[0].astype(BF16),
        'w_co': w_co[0].astype(BF16), 'norm_mlp_g': vec(norm_mlp_g), 'w_up': w_up[0].astype(BF16),
        'w_down': w_down[0].astype(BF16), 'norm_final_g': norm_final_g.reshape(1, -1),
    }
    mk, mv = _memory_kv(mem_prompt.reshape(bp * N_MEM, D_MODEL), vec(norm_mem_g),
                        w_mk[0].astype(BF16), w_mv[0].astype(BF16))
    mk = mk.reshape(bp, N_MEM, MEM_WIDTH)
    mv = mv.reshape(bp, N_MEM, MEM_WIDTH)
    shift0 = jnp.zeros((bp, 1, RWKV_PROJ), F32)
    wkv0 = jnp.zeros((bp, RWKV_HEADS, HEAD_DIM, HEAD_DIM), F32)
    yp, pk, pv, psh, pS = _trunk(x_prompt, mk, mv, None, None, shift0, wkv0, lw, rel_bias_table, RWKV_SUB_CHUNKS)
    ys, sk, sv, ssh, sS = _trunk(
        x_sample, cache_mem_k[0].reshape(dec_b, N_MEM, MEM_WIDTH), cache_mem_v[0].reshape(dec_b, N_MEM, MEM_WIDTH),
        cache_attn_k[0].reshape(dec_b, WINDOW, KV_WIDTH), cache_attn_v[0].reshape(dec_b, WINDOW, KV_WIDTH),
        state_shift[0], state_wkv[0], lw, rel_bias_table, 1)
    mem_shape = (1, bp, N_MEM, MEM_HEADS, MEM_HEAD_DIM)
    return (yp, ys, pk[None], pv[None], mk.reshape(mem_shape), mv.reshape(mem_shape), psh[None], pS[None],
            sk[None], sv[None], ssh[None], sS[None])
```

```python
import functools
import math

import numpy as np
import jax
import jax.numpy as jnp
from jax import lax
from jax.experimental import pallas as pl
from jax.experimental.pallas import tpu as pltpu

F32 = jnp.float32
BF16 = jnp.bfloat16

D_MODEL = 1024
CHUNK = 64
WINDOW = 128
HEAD_DIM = 64
ATTN_WIDTH = 512
ATTN_HEADS = 8
KV_HEADS = 2
GROUP = 4
KV_WIDTH = 128
RWKV_WIDTH = 512
RWKV_HEADS = 8
DECAY_LORA = 64
AAA_LORA = 64
GATE_LORA = 128
RWKV_PROJ = 1792
IN_PROJ = 2560
N_MEM = 256
MEM_HEADS = 4
MEM_HEAD_DIM = 128
MEM_WIDTH = 512
D_FF = 4096
REL_BUCKETS = 32
REL_MAX_DIST = 128
NORM_EPS = 1e-6
GN_EPS = 64e-5

V7X_VMEM_LIMIT_BYTES = 52 * 1024 * 1024
ROW_TILE = 256
ATTN_CHUNKS_PER_STEP = 4
RWKV_SUB_CHUNKS = 4


def _params(*sem):
    return pltpu.CompilerParams(dimension_semantics=sem, vmem_limit_bytes=V7X_VMEM_LIMIT_BYTES)


def _const_spec(shape):
    nd = len(shape)
    return pl.BlockSpec(shape, lambda *_: (0,) * nd)


_NN = ((1,), (0,))
_NT = ((1,), (1,))
_TN = ((0,), (0,))


def _dg(a, b, dims):
    return lax.dot_general(a, b, (dims, ((), ())), preferred_element_type=F32)


def _dot(a, b):
    return _dg(a.astype(BF16), b.astype(BF16), _NN)


def _dot_nt(a, b):
    return _dg(a.astype(BF16), b.astype(BF16), _NT)


def _cat_rows(*xs):
    return jnp.concatenate(xs, axis=0)


def _cat_lanes(*xs):
    return jnp.concatenate(xs, axis=1)


def _run_interleaved(chains):
    active = list(chains)
    while active:
        still = []
        for ch in active:
            try:
                next(ch)
                still.append(ch)
            except StopIteration:
                pass
        active = still


def _rms(x, g):
    return x * lax.rsqrt(jnp.mean(x * x, axis=-1, keepdims=True) + NORM_EPS) * g


def _inproj_kernel(x_ref, g_ref, w_ref, q_ref, k_ref, v_ref, zr_ref):
    h = _rms(x_ref[...], g_ref[...]).astype(BF16)
    q_ref[...] = jnp.dot(h, w_ref[:, :ATTN_WIDTH], preferred_element_type=F32)
    k_ref[...] = jnp.dot(h, w_ref[:, ATTN_WIDTH:ATTN_WIDTH + KV_WIDTH], preferred_element_type=F32)
    v_ref[...] = jnp.dot(h, w_ref[:, ATTN_WIDTH + KV_WIDTH:ATTN_WIDTH + 2 * KV_WIDTH],
                         preferred_element_type=F32)
    zr_ref[...] = jnp.dot(h, w_ref[:, ATTN_WIDTH + 2 * KV_WIDTH:], preferred_element_type=F32)


def _in_proj(x2d, g, w_bf16):
    n = x2d.shape[0]
    tm = min(ROW_TILE, n)
    row = lambda w: pl.BlockSpec((tm, w), lambda i: (i, 0))
    return pl.pallas_call(
        _inproj_kernel,
        grid=(n // tm,),
        in_specs=[row(D_MODEL), _const_spec((1, D_MODEL)), _const_spec((D_MODEL, IN_PROJ))],
        out_specs=[row(ATTN_WIDTH), row(KV_WIDTH), row(KV_WIDTH), row(RWKV_PROJ)],
        out_shape=[jax.ShapeDtypeStruct((n, w), F32) for w in (ATTN_WIDTH, KV_WIDTH, KV_WIDTH, RWKV_PROJ)],
        compiler_params=_params("parallel"),
        name="in_proj",
    )(x2d, g, w_bf16)


def _t5_bucket(rel):
    half = REL_BUCKETS // 2
    max_exact = half // 2
    assert REL_MAX_DIST == max_exact * 2 ** 4 and half - max_exact == 2 * 4
    n = np.abs(rel)
    large = max_exact + sum((n * n >= max_exact * max_exact * 2 ** t).astype(np.int64)
                            for t in range(1, half - max_exact))
    return (np.where(rel > 0, half, 0) + np.where(n < max_exact, n, large)).astype(np.int32)


def _bias_kernel(table_ref, bucket_ref, out_ref):
    bucket = bucket_ref[...]
    hits = [bucket == b for b in range(REL_BUCKETS)]
    for h in range(ATTN_HEADS):
        acc = jnp.zeros(bucket.shape, F32)
        for b in range(REL_BUCKETS):
            acc = jnp.where(hits[b], table_ref[b, h], acc)
        out_ref[h] = acc


def _rel_bias(table, n_q, n_k):
    rel = np.arange(n_k)[None, :] - WINDOW - np.arange(n_q)[:, None]
    bucket = jnp.asarray(_t5_bucket(rel))
    bias = pl.pallas_call(
        _bias_kernel,
        in_specs=[pl.BlockSpec(memory_space=pltpu.SMEM), pl.BlockSpec(memory_space=pltpu.VMEM)],
        out_specs=pl.BlockSpec(memory_space=pltpu.VMEM),
        out_shape=jax.ShapeDtypeStruct((ATTN_HEADS, n_q, n_k), F32),
        name="rel_bias",
    )(table, bucket)
    return bias.reshape(KV_HEADS, GROUP * n_q, n_k)


def _group_sinks(sink_ref, n_q):
    row_group = lax.broadcasted_iota(jnp.int32, (GROUP * n_q, 1), 0) // n_q
    sinks = []
    for kvh in range(KV_HEADS):
        sink = jnp.zeros((GROUP * n_q, 1), F32)
        for g in range(GROUP):
            sink = jnp.where(row_group == g, sink_ref[kvh * GROUP + g], sink)
        sinks.append(sink)
    return sinks


def _attn_chain(q, keys, vals, bias, sink, valid, o_ref, rows, kvh):
    n_q = q.shape[0]
    qh = _cat_rows(*[q[:, (kvh * GROUP + g) * HEAD_DIM:(kvh * GROUP + g + 1) * HEAD_DIM]
                     for g in range(GROUP)]).astype(BF16)
    s = _dg(qh, keys, _NT) * (HEAD_DIM ** -0.5) + bias
    if valid is not None:
        s = jnp.where(valid, s, -jnp.inf)
    yield
    m = jnp.maximum(jnp.max(s, axis=-1, keepdims=True), sink)
    p = jnp.exp(s - m)
    den = jnp.sum(p, axis=-1, keepdims=True) + jnp.exp(sink - m)
    yield
    o = _dg(p.astype(BF16), vals, _NN) * (1.0 / den)
    for g in range(GROUP):
        head = kvh * GROUP + g
        o_ref[rows, head * HEAD_DIM:(head + 1) * HEAD_DIM] = o[g * n_q:(g + 1) * n_q]
    yield


def _prompt_attn_kernel(sink_ref, q_ref, kp_ref, kc_ref, vp_ref, vc_ref, bias_ref, o_ref):
    n_chunks = q_ref.shape[0] // CHUNK
    n_k = WINDOW + CHUNK
    k_all = _cat_rows(kp_ref[...], kc_ref[...]).astype(BF16)
    v_all = _cat_rows(vp_ref[...], vc_ref[...]).astype(BF16)
    first_valid = jnp.where(pl.program_id(1) == 0, WINDOW, 0)
    kcol = lax.broadcasted_iota(jnp.int32, (1, n_k), 1)
    sinks = _group_sinks(sink_ref, CHUNK)
    chains = []
    for jj in range(n_chunks):
        rows = slice(jj * CHUNK, (jj + 1) * CHUNK)
        keys = slice(jj * CHUNK, jj * CHUNK + n_k)
        valid = kcol + jj * CHUNK >= first_valid
        for kvh in range(KV_HEADS):
            lanes = slice(kvh * HEAD_DIM, (kvh + 1) * HEAD_DIM)
            chains.append(_attn_chain(q_ref[rows, :], k_all[keys, lanes], v_all[keys, lanes], bias_ref[kvh],
                                      sinks[kvh], valid, o_ref, rows, kvh))
    _run_interleaved(chains)


def _prompt_attention(q, k, v, sink, bias, batch, seq):
    rows = ATTN_CHUNKS_PER_STEP * CHUNK
    steps = seq // rows
    per_window = rows // WINDOW
    q_spec = pl.BlockSpec((rows, ATTN_WIDTH), lambda b, i: (b * steps + i, 0))
    cur_spec = pl.BlockSpec((rows, KV_WIDTH), lambda b, i: (b * steps + i, 0))
    prev_spec = pl.BlockSpec((WINDOW, KV_WIDTH),
                             lambda b, i: (jnp.maximum((b * steps + i) * per_window - 1, 0), 0))
    return pl.pallas_call(
        _prompt_attn_kernel,
        grid=(batch, steps),
        in_specs=[pl.BlockSpec(memory_space=pltpu.SMEM), q_spec, prev_spec, cur_spec, prev_spec, cur_spec,
                  _const_spec(bias.shape)],
        out_specs=q_spec,
        out_shape=jax.ShapeDtypeStruct(q.shape, F32),
        compiler_params=_params("parallel", "parallel"),
        name="prompt_attention",
    )(sink, q, k, k, v, v, bias)


def _sample_attn_kernel(seq, sink_ref, q_ref, kp_ref, kn_ref, vp_ref, vn_ref, bias_ref, o_ref):
    batch = q_ref.shape[0] // seq
    sinks = _group_sinks(sink_ref, seq)
    chains = []
    for b in range(batch):
        rows = slice(b * seq, (b + 1) * seq)
        past = slice(b * WINDOW, (b + 1) * WINDOW)
        k_all = _cat_rows(kp_ref[past, :], kn_ref[rows, :]).astype(BF16)
        v_all = _cat_rows(vp_ref[past, :], vn_ref[rows, :]).astype(BF16)
        for kvh in range(KV_HEADS):
            lanes = slice(kvh * HEAD_DIM, (kvh + 1) * HEAD_DIM)
            chains.append(_attn_chain(q_ref[rows, :], k_all[:, lanes], v_all[:, lanes], bias_ref[kvh], sinks[kvh],
                                      None, o_ref, rows, kvh))
    _run_interleaved(chains)


def _sample_attention(q, k, v, k_past, v_past, sink, bias, seq):
    vmem = pl.BlockSpec(memory_space=pltpu.VMEM)
    return pl.pallas_call(
        functools.partial(_sample_attn_kernel, seq),
        in_specs=[pl.BlockSpec(memory_space=pltpu.SMEM)] + [vmem] * 6,
        out_specs=vmem,
        out_shape=jax.ShapeDtypeStruct(q.shape, F32),
        name="sample_attention",
    )(sink, q, k_past, k, v_past, v, bias)


def _split2(x):
    hi = x.astype(BF16)
    lo = (x - hi.astype(F32)).astype(BF16)
    return hi, lo


def _softplus(x):
    return jnp.maximum(x, 0.0) + jnp.log(1.0 + jnp.exp(-jnp.abs(x)))


def _sigmoid(x):
    return 1.0 / (1.0 + jnp.exp(-x))


def _rwkv_kernel(valid_rows, zr_ref, shift_ref, s0_ref, mu_ref, w0_ref, w2_ref, a0_ref, a2_ref, g2_ref, kk_ref,
                 ka_ref, rk_ref, lnw_ref, lnb_ref, seg_ref, tri_ref,
                 out_ref, s_ref, carry_ref, y_ref, sbd_ref):
    c = pl.program_id(1)
    C = CHUNK
    R = zr_ref.shape[0]
    n_sub = R // C
    pairs = range(RWKV_HEADS // 2)
    PAIR = 2 * HEAD_DIM
    W = RWKV_WIDTH

    @pl.when(c == 0)
    def _():
        carry_ref[0:1, :] = shift_ref[0]
        zero = jnp.zeros((HEAD_DIM, HEAD_DIM), F32)
        for p in pairs:
            sbd_ref[p] = _cat_rows(_cat_lanes(s0_ref[0, 2 * p], zero), _cat_lanes(zero, s0_ref[0, 2 * p + 1]))

    seg = seg_ref[...]
    seg2 = _cat_rows(seg, seg)

    def head_sum(x):
        hi, lo = _split2(x)
        tiles = [_dg(_cat_lanes(hi[:, t * PAIR:(t + 1) * PAIR], lo[:, t * PAIR:(t + 1) * PAIR]), seg2, _NN)
                 for t in range(W // PAIR)]
        return _cat_lanes(*tiles)

    first_row = lax.broadcasted_iota(jnp.int32, (C, 1), 0) == 0
    tri3 = tri_ref[...]

    lane = lax.broadcasted_iota(jnp.int32, (C, PAIR), 1)
    trow = lax.broadcasted_iota(jnp.int32, (C, PAIR), 0)
    even = lane < HEAD_DIM
    tcol = jnp.where(even, lane, lane - HEAD_DIM)
    strict = tcol < trow
    incl = tcol <= trow
    eye = jnp.where(tcol == trow, 1.0, 0.0).astype(F32)
    brow = lax.broadcasted_iota(jnp.int32, (PAIR, PAIR), 0) < HEAD_DIM
    bcol = lax.broadcasted_iota(jnp.int32, (PAIR, PAIR), 1) < HEAD_DIM
    on_diag = brow == bcol

    def bd(x):
        zero = jnp.zeros_like(x)
        return _cat_rows(jnp.where(even, x, zero), jnp.where(even, zero, x))

    def bd2(pair):
        return bd(pair[0]), bd(pair[1])

    def mm(a_pair, w_pair, dims=_NN):
        if dims == _NN:
            first = _dg(_cat_lanes(a_pair[0], a_pair[1]), _cat_rows(w_pair[0], w_pair[0]), dims)
        else:
            first = _dg(_cat_lanes(a_pair[0], a_pair[1]), _cat_lanes(w_pair[0], w_pair[0]), dims)
        return first + _dg(a_pair[0], w_pair[1], dims)

    prepped = {}
    ready = {}
    state = [sbd_ref[p] for p in pairs]

    def prep_chain(j):
        rows = slice(j * C, (j + 1) * C)
        zr = zr_ref[rows, :]
        before = carry_ref[0:1, :] if j == 0 else zr_ref[j * C - 1:j * C, :]
        z_prev = jnp.where(first_row, before, pltpu.roll(zr, 1, axis=0))
        zs = zr + (z_prev - zr) * mu_ref[...]
        r = zs[:, :W]
        k = zs[:, W:2 * W]
        v = zs[:, 2 * W:3 * W]
        wd = zs[:, 3 * W:3 * W + DECAY_LORA]
        ad = zs[:, 3 * W + DECAY_LORA:3 * W + DECAY_LORA + AAA_LORA]
        gd = zs[:, 3 * W + DECAY_LORA + AAA_LORA:]
        w_log = -_softplus(-(w0_ref[...] + _dot(jnp.tanh(wd), w2_ref[...]))) - 0.5
        lw = -jnp.exp(w_log)
        a = _sigmoid(a0_ref[...] + _dot(ad, a2_ref[...]))
        gate = _dot(_sigmoid(gd), g2_ref[...])
        kk = k * kk_ref[...]
        kk = kk / jnp.maximum(jnp.sqrt(head_sum(kk * kk)), 1e-12)
        k2 = k * (1.0 + (a - 1.0) * ka_ref[...])
        if (j + 1) * C > valid_rows:
            live = lax.broadcasted_iota(jnp.int32, (C, 1), 0) < valid_rows - j * C
            lw = jnp.where(live, lw, 0.0)
            kk = jnp.where(live, kk, 0.0)
            k2 = jnp.where(live, k2, 0.0)
        bvec = kk * a
        yield
        l1 = lw.astype(BF16)
        rem = lw - l1.astype(F32)
        l2 = rem.astype(BF16)
        l3 = (rem - l2.astype(F32)).astype(BF16)
        sums = _dg(tri3, _cat_rows(l1, l2, l3), _NN)
        li = sums[:C]
        lrev = sums[C:]
        yield
        inv_p = jnp.exp(-li)
        to_end = jnp.exp(lrev)
        prepped[j] = dict(
            at=_split2(-kk * jnp.exp(li - lw)), rt=_split2(r * jnp.exp(li)), bt=_split2(bvec * inv_p),
            kt=_split2(k2 * inv_p), bh=_split2(bvec * to_end), kh=_split2(k2 * to_end), v=_split2(v),
            p_end=jnp.exp(li[C - 1:C, :]), bonus=head_sum(r * k2 * rk_ref[...]) * v, gate=gate)
        yield

    def local_chain(j, p):
        d = prepped[j]
        lanes = slice(p * PAIR, (p + 1) * PAIR)
        cut = lambda pair: (pair[0][:, lanes], pair[1][:, lanes])
        at_p, rt_p, bt_p, kt_p, bh_p, kh_p, v_p = map(cut, (d['at'], d['rt'], d['bt'], d['kt'], d['bh'], d['kh'],
                                                            d['v']))
        left = (_cat_rows(at_p[0], rt_p[0]), _cat_rows(at_p[1], rt_p[1]))
        right = (_cat_rows(bd(bt_p[0]), bd(kt_p[0])), _cat_rows(bd(bt_p[1]), bd(kt_p[1])))
        aa = mm(left, right, _NT)
        yield
        a_ab = jnp.where(strict, aa[:C, :PAIR], 0.0)
        a_ak = jnp.where(strict, aa[:C, PAIR:], 0.0)
        a_rb = jnp.where(incl, aa[C:, :PAIR], 0.0)
        a_rk = jnp.where(incl, aa[C:, PAIR:], 0.0)
        inv = eye + a_ab
        ps = _split2(a_ab)
        power = mm(ps, bd2(ps))
        span = 2
        yield
        while span < C:
            ps = _split2(power)
            pw = bd2(ps)
            ih = _split2(inv)
            if span * 2 < C:
                both = mm((_cat_rows(ih[0], ps[0]), _cat_rows(ih[1], ps[1])), pw)
                inv = inv + both[:C]
                power = both[C:]
            else:
                inv = inv + mm(ih, pw)
            span *= 2
            yield
        ready[(j, p)] = dict(inv=_split2(inv), akrk=_split2(_cat_rows(a_ak, a_rk)), rb=_split2(a_rb), left=left,
                             bhkh=(_cat_rows(bh_p[0], kh_p[0]), _cat_rows(bh_p[1], kh_p[1])), v=v_p,
                             p_end=d['p_end'][:, lanes])

    def state_chain(j, p):
        d = ready.pop((j, p))
        s_prev = state[p]
        v_hi, v_lo = d['v']
        both = mm(d['left'], _split2(s_prev), _NT) + mm(d['akrk'], (bd(v_hi), bd(v_lo)))
        rhs = both[:C]
        y0 = both[C:]
        yield
        u_pair = _split2(mm(d['inv'], bd2(_split2(rhs))))
        yield
        y_ref[j * C:(j + 1) * C, p * PAIR:(p + 1) * PAIR] = y0 + mm(d['rb'], bd2(u_pair))
        t_hi = _cat_rows(u_pair[0], v_hi)
        t_lo = _cat_rows(u_pair[1], v_lo)
        w_hi, w_lo = d['bhkh']
        upd = _dg(_cat_rows(t_hi, t_lo), _cat_rows(w_hi, w_hi), _TN) + _dg(t_hi, w_lo, _TN)
        state[p] = s_prev * d['p_end'] + jnp.where(on_diag, upd, 0.0)
        yield

    def post_chain(j):
        rows = slice(j * C, (j + 1) * C)
        d = prepped.pop(j)
        y = y_ref[rows, :]
        mean = head_sum(y) * (1.0 / HEAD_DIM)
        dev = y - mean
        yield
        var = head_sum(dev * dev) * (1.0 / HEAD_DIM)
        yn = dev * lax.rsqrt(var + GN_EPS) * lnw_ref[...] + lnb_ref[...]
        out_ref[rows, :] = (yn + d['bonus']) * d['gate']
        yield

    for t in range(n_sub + 3):
        chains = []
        if 0 <= t - 2 < n_sub:
            chains += [state_chain(t - 2, p) for p in pairs]
        if 0 <= t - 1 < n_sub:
            chains += [local_chain(t - 1, p) for p in pairs]
        if t < n_sub:
            chains.append(prep_chain(t))
        if 0 <= t - 3 < n_sub:
            chains.append(post_chain(t - 3))
        _run_interleaved(chains)

    carry_ref[0:1, :] = zr_ref[R - 1:R, :]
    for p in pairs:
        sbd_ref[p] = state[p]

    @pl.when(c == pl.num_programs(1) - 1)
    def _():
        for p in pairs:
            s_ref[0, 2 * p] = state[p][:HEAD_DIM, :HEAD_DIM]
            s_ref[0, 2 * p + 1] = state[p][HEAD_DIM:, HEAD_DIM:]


def _rwkv_mixer(zr, shift_prev, state0, lw, batch, seq, n_sub):
    valid = seq
    if seq < CHUNK:
        zr = jnp.pad(zr.reshape(batch, seq, RWKV_PROJ), ((0, 0), (0, CHUNK - seq), (0, 0))).reshape(-1, RWKV_PROJ)
        seq, n_sub = CHUNK, 1
    rows = CHUNK * n_sub
    steps = seq // rows
    seg = jnp.asarray(np.kron(np.eye(2), np.ones((HEAD_DIM, HEAD_DIM))), BF16)
    ones = np.ones((CHUNK, CHUNK))
    tri3 = jnp.asarray(np.concatenate([np.tile(np.tril(ones), (1, 3)), np.tile(np.triu(ones, 1), (1, 3))]), BF16)
    row = lambda name: lw[name].reshape(1, -1)
    params = [row('rwkv_mu'), row('rwkv_w0'), lw['rwkv_w2'].astype(BF16), row('rwkv_a0'),
              lw['rwkv_a2'].astype(BF16), lw['rwkv_g2'].astype(BF16), row('rwkv_k_k'), row('rwkv_k_a'),
              row('rwkv_r_k'), row('rwkv_ln_w'), row('rwkv_ln_b'), seg, tri3]
    state_spec = pl.BlockSpec((1, RWKV_HEADS, HEAD_DIM, HEAD_DIM), lambda b, c: (b, 0, 0, 0))
    out, state = pl.pallas_call(
        functools.partial(_rwkv_kernel, min(valid, rows)),
        grid=(batch, steps),
        in_specs=[pl.BlockSpec((rows, RWKV_PROJ), lambda b, c: (b * steps + c, 0)),
                  pl.BlockSpec((1, 1, RWKV_PROJ), lambda b, c: (b, 0, 0)),
                  state_spec] + [_const_spec(p.shape) for p in params],
        out_specs=[pl.BlockSpec((rows, RWKV_WIDTH), lambda b, c: (b * steps + c, 0)), state_spec],
        out_shape=[jax.ShapeDtypeStruct((batch * seq, RWKV_WIDTH), F32),
                   jax.ShapeDtypeStruct(state0.shape, F32)],
        scratch_shapes=[pltpu.VMEM((8, RWKV_PROJ), F32), pltpu.VMEM((rows, RWKV_WIDTH), F32),
                        pltpu.VMEM((RWKV_HEADS // 2, 2 * HEAD_DIM, 2 * HEAD_DIM), F32)],
        compiler_params=_params("parallel", "arbitrary"),
        name="rwkv_mixer",
    )(zr, shift_prev, state0, *params)
    if valid < seq:
        out = out.reshape(batch, seq, RWKV_WIDTH)[:, :valid].reshape(batch * valid, RWKV_WIDTH)
    return out, state


def _outproj_kernel(x_ref, a_ref, r_ref, wo_ref, g_ref, wq_ref, x1_ref, qc_ref):
    x1 = (x_ref[...] + _dot(a_ref[...], wo_ref[:ATTN_WIDTH, :]) + _dot(r_ref[...], wo_ref[ATTN_WIDTH:, :]))
    x1_ref[...] = x1
    qc_ref[...] = _dot(_rms(x1, g_ref[...]), wq_ref[...])


def _out_proj(x2d, a_out, r_out, w_out, g_cross, w_cq):
    n = x2d.shape[0]
    tm = min(ROW_TILE, n)
    row = lambda w: pl.BlockSpec((tm, w), lambda i: (i, 0))
    return pl.pallas_call(
        _outproj_kernel,
        grid=(n // tm,),
        in_specs=[row(D_MODEL), row(ATTN_WIDTH), row(RWKV_WIDTH), _const_spec(w_out.shape),
                  _const_spec((1, D_MODEL)), _const_spec(w_cq.shape)],
        out_specs=[row(D_MODEL), row(MEM_WIDTH)],
        out_shape=[jax.ShapeDtypeStruct((n, D_MODEL), F32), jax.ShapeDtypeStruct((n, MEM_WIDTH), F32)],
        compiler_params=_params("parallel"),
        name="out_proj",
    )(x2d, a_out, r_out, w_out, g_cross, w_cq)


def _memkv_kernel(m_ref, g_ref, wk_ref, wv_ref, k_ref, v_ref):
    mn = _rms(m_ref[...], g_ref[...]).astype(BF16)
    k_ref[...] = jnp.dot(mn, wk_ref[...], preferred_element_type=F32)
    v_ref[...] = jnp.dot(mn, wv_ref[...], preferred_element_type=F32)


def _memory_kv(mem2d, g, w_mk, w_mv):
    n = mem2d.shape[0]
    tm = min(ROW_TILE, n)
    row = lambda w: pl.BlockSpec((tm, w), lambda i: (i, 0))
    return pl.pallas_call(
        _memkv_kernel,
        grid=(n // tm,),
        in_specs=[row(D_MODEL), _const_spec((1, D_MODEL)), _const_spec(w_mk.shape), _const_spec(w_mv.shape)],
        out_specs=[row(MEM_WIDTH), row(MEM_WIDTH)],
        out_shape=[jax.ShapeDtypeStruct((n, MEM_WIDTH), F32)] * 2,
        compiler_params=_params("parallel"),
        name="memory_kv",
    )(mem2d, g, w_mk, w_mv)


def _cross_kernel(x1_ref, q_ref, mk_ref, mv_ref, wco_ref, x2_ref):
    q = q_ref[...]
    mk = mk_ref[0]
    mv = mv_ref[0]
    outs = []
    for h in range(MEM_HEADS):
        sl = slice(h * MEM_HEAD_DIM, (h + 1) * MEM_HEAD_DIM)
        s = _dot_nt(q[:, sl], mk[:, sl]) * (MEM_HEAD_DIM ** -0.5)
        m = jnp.max(s, axis=-1, keepdims=True)
        p = jnp.exp(s - m)
        p = p / jnp.sum(p, axis=-1, keepdims=True)
        outs.append(_dot(p, mv[:, sl]))
    o = jnp.concatenate(outs, axis=-1)
    x2_ref[...] = x1_ref[...] + _dot(o, wco_ref[...])


def _cross_attention(x1, qc, mk, mv, w_co, batch, seq):
    tq = min(ROW_TILE, seq)
    nt = seq // tq
    row = lambda w: pl.BlockSpec((tq, w), lambda b, t: (b * nt + t, 0))
    mem_spec = pl.BlockSpec((1, N_MEM, MEM_WIDTH), lambda b, t: (b, 0, 0))
    return pl.pallas_call(
        _cross_kernel,
        grid=(batch, nt),
        in_specs=[row(D_MODEL), row(MEM_WIDTH), mem_spec, mem_spec, _const_spec(w_co.shape)],
        out_specs=row(D_MODEL),
        out_shape=jax.ShapeDtypeStruct(x1.shape, F32),
        compiler_params=_params("parallel", "parallel"),
        name="cross_attention",
    )(x1, qc, mk, mv, w_co)


def _mlp_kernel(x_ref, g_ref, wu_ref, wd_ref, gf_ref, y_ref):
    x = x_ref[...]
    up = _dot(_rms(x, g_ref[...]), wu_ref[...])
    act = jnp.square(jnp.maximum(up, 0.0))
    y_ref[...] = _rms(x + _dot(act, wd_ref[...]), gf_ref[...])


def _mlp(x2, g_mlp, w_up, w_down, g_final):
    n = x2.shape[0]
    tm = min(ROW_TILE, n)
    row = pl.BlockSpec((tm, D_MODEL), lambda i: (i, 0))
    return pl.pallas_call(
        _mlp_kernel,
        grid=(n // tm,),
        in_specs=[row, _const_spec((1, D_MODEL)), _const_spec(w_up.shape), _const_spec(w_down.shape),
                  _const_spec((1, D_MODEL))],
        out_specs=row,
        out_shape=jax.ShapeDtypeStruct(x2.shape, F32),
        compiler_params=_params("parallel"),
        name="mlp",
    )(x2, g_mlp, w_up, w_down, g_final)


def _trunk(x, mk, mv, k_past, v_past, shift_prev, state0, lw, table, n_sub):
    batch, seq = x.shape[0], x.shape[1]
    x2d = x.reshape(batch * seq, D_MODEL)
    q, k, v, zr = _in_proj(x2d, lw['norm_mix_g'], lw['w_in'])
    if k_past is None:
        bias = _rel_bias(table, CHUNK, WINDOW + CHUNK)
        a_out = _prompt_attention(q, k, v, lw['attn_sink'], bias, batch, seq)
        k3 = k.reshape(batch, seq, KV_WIDTH)
        v3 = v.reshape(batch, seq, KV_WIDTH)
        k_buf, v_buf = k3[:, -WINDOW:], v3[:, -WINDOW:]
    else:
        bias = _rel_bias(table, seq, WINDOW + seq)
        a_out = _sample_attention(q, k, v, k_past.reshape(batch * WINDOW, KV_WIDTH),
                                  v_past.reshape(batch * WINDOW, KV_WIDTH), lw['attn_sink'], bias, seq)
        k_buf = jnp.concatenate([k_past, k.reshape(batch, seq, KV_WIDTH)], axis=1)[:, -WINDOW:]
        v_buf = jnp.concatenate([v_past, v.reshape(batch, seq, KV_WIDTH)], axis=1)[:, -WINDOW:]
    r_out, state = _rwkv_mixer(zr, shift_prev, state0, lw, batch, seq, n_sub)
    shift_new = zr.reshape(batch, seq, RWKV_PROJ)[:, -1:]
    x1, qc = _out_proj(x2d, a_out, r_out, lw['w_out'], lw['norm_cross_g'], lw['w_cq'])
    x2 = _cross_attention(x1, qc, mk, mv, lw['w_co'], batch, seq)
    y = _mlp(x2, lw['norm_mlp_g'], lw['w_up'], lw['w_down'], lw['norm_final_g'])
    kv_shape = (batch, WINDOW, KV_HEADS, HEAD_DIM)
    return y.reshape(x.shape), k_buf.reshape(kv_shape), v_buf.reshape(kv_shape), shift_new, state


def kernel(x_prompt, x_sample, mem_prompt, cache_attn_k, cache_attn_v, cache_mem_k, cache_mem_v, state_shift,
           state_wkv, norm_mix_g, w_in, attn_sink, rel_bias_table, rwkv_mu, rwkv_w0, rwkv_w2, rwkv_a0, rwkv_a2,
           rwkv_g2, rwkv_k_k, rwkv_k_a, rwkv_r_k, rwkv_ln_w, rwkv_ln_b, w_out, norm_cross_g, norm_mem_g, w_cq,
           w_mk, w_mv, w_co, norm_mlp_g, w_up, w_down, norm_final_g):
    assert norm_mix_g.shape[0] == 1, "single-layer trunk"
    bp, dec_b = x_prompt.shape[0], x_sample.shape[0]
    vec = lambda p: p[0].reshape(1, -1)
    lw = {
        'norm_mix_g': vec(norm_mix_g), 'w_in': w_in[0].astype(BF16), 'attn_sink': attn_sink[0],
        'rwkv_mu': rwkv_mu[0], 'rwkv_w0': rwkv_w0[0], 'rwkv_w2': rwkv_w2[0], 'rwkv_a0': rwkv_a0[0],
        'rwkv_a2': rwkv_a2[0], 'rwkv_g2': rwkv_g2[0], 'rwkv_k_k': rwkv_k_k[0], 'rwkv_k_a': rwkv_k_a[0],
        'rwkv_r_k': rwkv_r_k[0], 'rwkv_ln_w': rwkv_ln_w[0], 'rwkv_ln_b': rwkv_ln_b[0],
        'w_out': w_out[0].astype(BF16), 'norm_cross_g': vec(norm_cross_g), 'w_cq': w_cq[0].astype(BF16),
        'w_co': w_co[0].astype(BF16), 'norm_mlp_g': vec(norm_mlp_g), 'w_up': w_up[0].astype(BF16),
        'w_down': w_down[0].astype(BF16), 'norm_final_g': norm_final_g.reshape(1, -1),
    }
    mk, mv = _memory_kv(mem_prompt.reshape(bp * N_MEM, D_MODEL), vec(norm_mem_g),
                        w_mk[0].astype(BF16), w_mv[0].astype(BF16))
    mk = mk.reshape(bp, N_MEM, MEM_WIDTH)
    mv = mv.reshape(bp, N_MEM, MEM_WIDTH)
    shift0 = jnp.zeros((bp, 1, RWKV_PROJ), F32)
    wkv0 = jnp.zeros((bp, RWKV_HEADS, HEAD_DIM, HEAD_DIM), F32)
    yp, pk, pv, psh, pS = _trunk(x_prompt, mk, mv, None, None, shift0, wkv0, lw, rel_bias_table, RWKV_SUB_CHUNKS)
    ys, sk, sv, ssh, sS = _trunk(
        x_sample, cache_mem_k[0].reshape(dec_b, N_MEM, MEM_WIDTH), cache_mem_v[0].reshape(dec_b, N_MEM, MEM_WIDTH),
        cache_attn_k[0].reshape(dec_b, WINDOW, KV_WIDTH), cache_attn_v[0].reshape(dec_b, WINDOW, KV_WIDTH),
        state_shift[0], state_wkv[0], lw, rel_bias_table, 1)
    mem_shape = (1, bp, N_MEM, MEM_HEADS, MEM_HEAD_DIM)
    return (yp, ys, pk[None], pv[None], mk.reshape(mem_shape), mv.reshape(mem_shape), psh[None], pS[None],
            sk[None], sv[None], ssh[None], sS[None])
```

```python
import functools
import math

import numpy as np
import jax
import jax.numpy as jnp
from jax import lax
from jax.experimental import pallas as pl
from jax.experimental.pallas import tpu as pltpu

F32 = jnp.float32
BF16 = jnp.bfloat16

D_MODEL = 1024
CHUNK = 64
WINDOW = 128
HEAD_DIM = 64
ATTN_WIDTH = 512
ATTN_HEADS = 8
KV_HEADS = 2
GROUP = 4
KV_WIDTH = 128
RWKV_WIDTH = 512
RWKV_HEADS = 8
DECAY_LORA = 64
AAA_LORA = 64
GATE_LORA = 128
RWKV_PROJ = 1792
IN_PROJ = 2560
N_MEM = 256
MEM_HEADS = 4
MEM_HEAD_DIM = 128
MEM_WIDTH = 512
D_FF = 4096
REL_BUCKETS = 32
REL_MAX_DIST = 128
NORM_EPS = 1e-6
GN_EPS = 64e-5

V7X_VMEM_LIMIT_BYTES = 52 * 1024 * 1024
ROW_TILE = 256
IN_PROJ_ROW_TILE = 512
ATTN_CHUNKS_PER_STEP = 4
RWKV_SUB_CHUNKS = 4


def _params(*sem):
    return pltpu.CompilerParams(dimension_semantics=sem, vmem_limit_bytes=V7X_VMEM_LIMIT_BYTES)


def _const_spec(shape):
    nd = len(shape)
    return pl.BlockSpec(shape, lambda *_: (0,) * nd)


_NN = ((1,), (0,))
_NT = ((1,), (1,))
_TN = ((0,), (0,))


def _dg(a, b, dims):
    return lax.dot_general(a, b, (dims, ((), ())), preferred_element_type=F32)


def _dot(a, b):
    return _dg(a.astype(BF16), b.astype(BF16), _NN)


def _dot_nt(a, b):
    return _dg(a.astype(BF16), b.astype(BF16), _NT)


def _cat_rows(*xs):
    return jnp.concatenate(xs, axis=0)


def _cat_lanes(*xs):
    return jnp.concatenate(xs, axis=1)


def _run_interleaved(chains):
    active = list(chains)
    while active:
        still = []
        for ch in active:
            try:
                next(ch)
                still.append(ch)
            except StopIteration:
                pass
        active = still


def _rms(x, g):
    return x * lax.rsqrt(jnp.mean(x * x, axis=-1, keepdims=True) + NORM_EPS) * g


def _inproj_kernel(x_ref, g_ref, w_ref, q_ref, k_ref, v_ref, zr_ref):
    h = _rms(x_ref[...], g_ref[...]).astype(BF16)
    q_ref[...] = jnp.dot(h, w_ref[:, :ATTN_WIDTH], preferred_element_type=F32)
    k_ref[...] = jnp.dot(h, w_ref[:, ATTN_WIDTH:ATTN_WIDTH + KV_WIDTH], preferred_element_type=F32)
    v_ref[...] = jnp.dot(h, w_ref[:, ATTN_WIDTH + KV_WIDTH:ATTN_WIDTH + 2 * KV_WIDTH],
                         preferred_element_type=F32)
    zr_ref[...] = jnp.dot(h, w_ref[:, ATTN_WIDTH + 2 * KV_WIDTH:], preferred_element_type=F32)


def _in_proj(x2d, g, w_bf16):
    n = x2d.shape[0]
    tm = min(IN_PROJ_ROW_TILE, n)
    row = lambda w: pl.BlockSpec((tm, w), lambda i: (i, 0))
    return pl.pallas_call(
        _inproj_kernel,
        grid=(n // tm,),
        in_specs=[row(D_MODEL), _const_spec((1, D_MODEL)), _const_spec((D_MODEL, IN_PROJ))],
        out_specs=[row(ATTN_WIDTH), row(KV_WIDTH), row(KV_WIDTH), row(RWKV_PROJ)],
        out_shape=[jax.ShapeDtypeStruct((n, w), F32) for w in (ATTN_WIDTH, KV_WIDTH, KV_WIDTH, RWKV_PROJ)],
        compiler_params=_params("parallel"),
        name="in_proj",
    )(x2d, g, w_bf16)


def _t5_bucket(rel):
    half = REL_BUCKETS // 2
    max_exact = half // 2
    assert REL_MAX_DIST == max_exact * 2 ** 4 and half - max_exact == 2 * 4
    n = np.abs(rel)
    large = max_exact + sum((n * n >= max_exact * max_exact * 2 ** t).astype(np.int64)
                            for t in range(1, half - max_exact))
    return (np.where(rel > 0, half, 0) + np.where(n < max_exact, n, large)).astype(np.int32)


def _bias_kernel(table_ref, bucket_ref, out_ref):
    bucket = bucket_ref[...]
    hits = [bucket == b for b in range(REL_BUCKETS)]
    for h in range(ATTN_HEADS):
        acc = jnp.zeros(bucket.shape, F32)
        for b in range(REL_BUCKETS):
            acc = jnp.where(hits[b], table_ref[b, h], acc)
        out_ref[h] = acc


def _rel_bias(table, n_q, n_k):
    rel = np.arange(n_k)[None, :] - WINDOW - np.arange(n_q)[:, None]
    bucket = jnp.asarray(_t5_bucket(rel))
    bias = pl.pallas_call(
        _bias_kernel,
        in_specs=[pl.BlockSpec(memory_space=pltpu.SMEM), pl.BlockSpec(memory_space=pltpu.VMEM)],
        out_specs=pl.BlockSpec(memory_space=pltpu.VMEM),
        out_shape=jax.ShapeDtypeStruct((ATTN_HEADS, n_q, n_k), F32),
        name="rel_bias",
    )(table, bucket)
    return bias.reshape(KV_HEADS, GROUP * n_q, n_k)


def _group_sinks(sink_ref, n_q):
    row_group = lax.broadcasted_iota(jnp.int32, (GROUP * n_q, 1), 0) // n_q
    sinks = []
    for kvh in range(KV_HEADS):
        sink = jnp.zeros((GROUP * n_q, 1), F32)
        for g in range(GROUP):
            sink = jnp.where(row_group == g, sink_ref[kvh * GROUP + g], sink)
        sinks.append(sink)
    return sinks


def _attn_chain(q, keys, vals, bias, sink, valid, o_ref, rows, kvh):
    n_q = q.shape[0]
    qh = _cat_rows(*[q[:, (kvh * GROUP + g) * HEAD_DIM:(kvh * GROUP + g + 1) * HEAD_DIM]
                     for g in range(GROUP)]).astype(BF16)
    s = _dg(qh, keys, _NT) * (HEAD_DIM ** -0.5) + bias
    if valid is not None:
        s = jnp.where(valid, s, -jnp.inf)
    yield
    m = jnp.maximum(jnp.max(s, axis=-1, keepdims=True), sink)
    p = jnp.exp(s - m)
    den = jnp.sum(p, axis=-1, keepdims=True) + jnp.exp(sink - m)
    yield
    o = _dg(p.astype(BF16), vals, _NN) * (1.0 / den)
    for g in range(GROUP):
        head = kvh * GROUP + g
        o_ref[rows, head * HEAD_DIM:(head + 1) * HEAD_DIM] = o[g * n_q:(g + 1) * n_q]
    yield


def _prompt_attn_kernel(sink_ref, q_ref, kp_ref, kc_ref, vp_ref, vc_ref, bias_ref, o_ref):
    n_chunks = q_ref.shape[0] // CHUNK
    n_k = WINDOW + CHUNK
    k_all = _cat_rows(kp_ref[...], kc_ref[...]).astype(BF16)
    v_all = _cat_rows(vp_ref[...], vc_ref[...]).astype(BF16)
    first_valid = jnp.where(pl.program_id(1) == 0, WINDOW, 0)
    kcol = lax.broadcasted_iota(jnp.int32, (1, n_k), 1)
    sinks = _group_sinks(sink_ref, CHUNK)
    chains = []
    for jj in range(n_chunks):
        rows = slice(jj * CHUNK, (jj + 1) * CHUNK)
        keys = slice(jj * CHUNK, jj * CHUNK + n_k)
        valid = kcol + jj * CHUNK >= first_valid
        for kvh in range(KV_HEADS):
            lanes = slice(kvh * HEAD_DIM, (kvh + 1) * HEAD_DIM)
            chains.append(_attn_chain(q_ref[rows, :], k_all[keys, lanes], v_all[keys, lanes], bias_ref[kvh],
                                      sinks[kvh], valid, o_ref, rows, kvh))
    _run_interleaved(chains)


def _prompt_attention(q, k, v, sink, bias, batch, seq):
    rows = ATTN_CHUNKS_PER_STEP * CHUNK
    steps = seq // rows
    per_window = rows // WINDOW
    q_spec = pl.BlockSpec((rows, ATTN_WIDTH), lambda b, i: (b * steps + i, 0))
    cur_spec = pl.BlockSpec((rows, KV_WIDTH), lambda b, i: (b * steps + i, 0))
    prev_spec = pl.BlockSpec((WINDOW, KV_WIDTH),
                             lambda b, i: (jnp.maximum((b * steps + i) * per_window - 1, 0), 0))
    return pl.pallas_call(
        _prompt_attn_kernel,
        grid=(batch, steps),
        in_specs=[pl.BlockSpec(memory_space=pltpu.SMEM), q_spec, prev_spec, cur_spec, prev_spec, cur_spec,
                  _const_spec(bias.shape)],
        out_specs=q_spec,
        out_shape=jax.ShapeDtypeStruct(q.shape, F32),
        compiler_params=_params("parallel", "parallel"),
        name="prompt_attention",
    )(sink, q, k, k, v, v, bias)


def _sample_attn_kernel(seq, sink_ref, q_ref, kp_ref, kn_ref, vp_ref, vn_ref, bias_ref, o_ref):
    batch = q_ref.shape[0] // seq
    sinks = _group_sinks(sink_ref, seq)
    chains = []
    for b in range(batch):
        rows = slice(b * seq, (b + 1) * seq)
        past = slice(b * WINDOW, (b + 1) * WINDOW)
        k_all = _cat_rows(kp_ref[past, :], kn_ref[rows, :]).astype(BF16)
        v_all = _cat_rows(vp_ref[past, :], vn_ref[rows, :]).astype(BF16)
        for kvh in range(KV_HEADS):
            lanes = slice(kvh * HEAD_DIM, (kvh + 1) * HEAD_DIM)
            chains.append(_attn_chain(q_ref[rows, :], k_all[:, lanes], v_all[:, lanes], bias_ref[kvh], sinks[kvh],
                                      None, o_ref, rows, kvh))
    _run_interleaved(chains)


def _sample_attention(q, k, v, k_past, v_past, sink, bias, seq):
    vmem = pl.BlockSpec(memory_space=pltpu.VMEM)
    return pl.pallas_call(
        functools.partial(_sample_attn_kernel, seq),
        in_specs=[pl.BlockSpec(memory_space=pltpu.SMEM)] + [vmem] * 6,
        out_specs=vmem,
        out_shape=jax.ShapeDtypeStruct(q.shape, F32),
        name="sample_attention",
    )(sink, q, k_past, k, v_past, v, bias)


def _split2(x):
    hi = x.astype(BF16)
    lo = (x - hi.astype(F32)).astype(BF16)
    return hi, lo


def _softplus(x):
    return jnp.maximum(x, 0.0) + jnp.log(1.0 + jnp.exp(-jnp.abs(x)))


def _sigmoid(x):
    return 1.0 / (1.0 + jnp.exp(-x))


def _rwkv_kernel(valid_rows, zr_ref, shift_ref, s0_ref, mu_ref, w0_ref, w2_ref, a0_ref, a2_ref, g2_ref, kk_ref,
                 ka_ref, rk_ref, lnw_ref, lnb_ref, seg_ref, tri_ref,
                 out_ref, s_ref, carry_ref, y_ref, sbd_ref):
    c = pl.program_id(1)
    C = CHUNK
    R = zr_ref.shape[0]
    n_sub = R // C
    pairs = range(RWKV_HEADS // 2)
    PAIR = 2 * HEAD_DIM
    W = RWKV_WIDTH

    @pl.when(c == 0)
    def _():
        carry_ref[0:1, :] = shift_ref[0]
        zero = jnp.zeros((HEAD_DIM, HEAD_DIM), F32)
        for p in pairs:
            sbd_ref[p] = _cat_rows(_cat_lanes(s0_ref[0, 2 * p], zero), _cat_lanes(zero, s0_ref[0, 2 * p + 1]))

    seg = seg_ref[...]
    seg2 = _cat_rows(seg, seg)

    def head_sum(x):
        hi, lo = _split2(x)
        tiles = [_dg(_cat_lanes(hi[:, t * PAIR:(t + 1) * PAIR], lo[:, t * PAIR:(t + 1) * PAIR]), seg2, _NN)
                 for t in range(W // PAIR)]
        return _cat_lanes(*tiles)

    first_row = lax.broadcasted_iota(jnp.int32, (C, 1), 0) == 0
    tri3 = tri_ref[...]

    lane = lax.broadcasted_iota(jnp.int32, (C, PAIR), 1)
    trow = lax.broadcasted_iota(jnp.int32, (C, PAIR), 0)
    even = lane < HEAD_DIM
    tcol = jnp.where(even, lane, lane - HEAD_DIM)
    strict = tcol < trow
    incl = tcol <= trow
    eye = jnp.where(tcol == trow, 1.0, 0.0).astype(F32)
    brow = lax.broadcasted_iota(jnp.int32, (PAIR, PAIR), 0) < HEAD_DIM
    bcol = lax.broadcasted_iota(jnp.int32, (PAIR, PAIR), 1) < HEAD_DIM
    on_diag = brow == bcol

    def bd(x):
        zero = jnp.zeros_like(x)
        return _cat_rows(jnp.where(even, x, zero), jnp.where(even, zero, x))

    def bd2(pair):
        return bd(pair[0]), bd(pair[1])

    def mm(a_pair, w_pair, dims=_NN):
        if dims == _NN:
            first = _dg(_cat_lanes(a_pair[0], a_pair[1]), _cat_rows(w_pair[0], w_pair[0]), dims)
        else:
            first = _dg(_cat_lanes(a_pair[0], a_pair[1]), _cat_lanes(w_pair[0], w_pair[0]), dims)
        return first + _dg(a_pair[0], w_pair[1], dims)

    prepped = {}
    ready = {}
    state = [sbd_ref[p] for p in pairs]

    def prep_chain(j):
        rows = slice(j * C, (j + 1) * C)
        zr = zr_ref[rows, :]
        before = carry_ref[0:1, :] if j == 0 else zr_ref[j * C - 1:j * C, :]
        z_prev = jnp.where(first_row, before, pltpu.roll(zr, 1, axis=0))
        zs = zr + (z_prev - zr) * mu_ref[...]
        r = zs[:, :W]
        k = zs[:, W:2 * W]
        v = zs[:, 2 * W:3 * W]
        wd = zs[:, 3 * W:3 * W + DECAY_LORA]
        ad = zs[:, 3 * W + DECAY_LORA:3 * W + DECAY_LORA + AAA_LORA]
        gd = zs[:, 3 * W + DECAY_LORA + AAA_LORA:]
        w_log = -_softplus(-(w0_ref[...] + _dot(jnp.tanh(wd), w2_ref[...]))) - 0.5
        lw = -jnp.exp(w_log)
        a = _sigmoid(a0_ref[...] + _dot(ad, a2_ref[...]))
        gate = _dot(_sigmoid(gd), g2_ref[...])
        kk = k * kk_ref[...]
        kk = kk / jnp.maximum(jnp.sqrt(head_sum(kk * kk)), 1e-12)
        k2 = k * (1.0 + (a - 1.0) * ka_ref[...])
        if (j + 1) * C > valid_rows:
            live = lax.broadcasted_iota(jnp.int32, (C, 1), 0) < valid_rows - j * C
            lw = jnp.where(live, lw, 0.0)
            kk = jnp.where(live, kk, 0.0)
            k2 = jnp.where(live, k2, 0.0)
        bvec = kk * a
        yield
        l1 = lw.astype(BF16)
        rem = lw - l1.astype(F32)
        l2 = rem.astype(BF16)
        l3 = (rem - l2.astype(F32)).astype(BF16)
        sums = _dg(tri3, _cat_rows(l1, l2, l3), _NN)
        li = sums[:C]
        lrev = sums[C:]
        yield
        inv_p = jnp.exp(-li)
        to_end = jnp.exp(lrev)
        prepped[j] = dict(
            at=_split2(-kk * jnp.exp(li - lw)), rt=_split2(r * jnp.exp(li)), bt=_split2(bvec * inv_p),
            kt=_split2(k2 * inv_p), bh=_split2(bvec * to_end), kh=_split2(k2 * to_end), v=_split2(v),
            p_end=jnp.exp(li[C - 1:C, :]), bonus=head_sum(r * k2 * rk_ref[...]) * v, gate=gate)
        yield

    def local_chain(j, p):
        d = prepped[j]
        lanes = slice(p * PAIR, (p + 1) * PAIR)
        cut = lambda pair: (pair[0][:, lanes], pair[1][:, lanes])
        at_p, rt_p, bt_p, kt_p, bh_p, kh_p, v_p = map(cut, (d['at'], d['rt'], d['bt'], d['kt'], d['bh'], d['kh'],
                                                            d['v']))
        left = (_cat_rows(at_p[0], rt_p[0]), _cat_rows(at_p[1], rt_p[1]))
        right = (_cat_rows(bd(bt_p[0]), bd(kt_p[0])), _cat_rows(bd(bt_p[1]), bd(kt_p[1])))
        aa = mm(left, right, _NT)
        yield
        a_ab = jnp.where(strict, aa[:C, :PAIR], 0.0)
        a_ak = jnp.where(strict, aa[:C, PAIR:], 0.0)
        a_rb = jnp.where(incl, aa[C:, :PAIR], 0.0)
        a_rk = jnp.where(incl, aa[C:, PAIR:], 0.0)
        inv = eye + a_ab
        ps = _split2(a_ab)
        power = mm(ps, bd2(ps))
        span = 2
        yield
        while span < C:
            ps = _split2(power)
            pw = bd2(ps)
            ih = _split2(inv)
            if span * 2 < C:
                both = mm((_cat_rows(ih[0], ps[0]), _cat_rows(ih[1], ps[1])), pw)
                inv = inv + both[:C]
                power = both[C:]
            else:
                inv = inv + mm(ih, pw)
            span *= 2
            yield
        ready[(j, p)] = dict(inv=_split2(inv), akrk=_split2(_cat_rows(a_ak, a_rk)), rb=_split2(a_rb), left=left,
                             bhkh=(_cat_rows(bh_p[0], kh_p[0]), _cat_rows(bh_p[1], kh_p[1])), v=v_p,
                             p_end=d['p_end'][:, lanes])

    def state_chain(j, p):
        d = ready.pop((j, p))
        s_prev = state[p]
        v_hi, v_lo = d['v']
        both = mm(d['left'], _split2(s_prev), _NT) + mm(d['akrk'], (bd(v_hi), bd(v_lo)))
        rhs = both[:C]
        y0 = both[C:]
        yield
        u_pair = _split2(mm(d['inv'], bd2(_split2(rhs))))
        yield
        y_ref[j * C:(j + 1) * C, p * PAIR:(p + 1) * PAIR] = y0 + mm(d['rb'], bd2(u_pair))
        t_hi = _cat_rows(u_pair[0], v_hi)
        t_lo = _cat_rows(u_pair[1], v_lo)
        w_hi, w_lo = d['bhkh']
        upd = _dg(_cat_rows(t_hi, t_lo), _cat_rows(w_hi, w_hi), _TN) + _dg(t_hi, w_lo, _TN)
        state[p] = s_prev * d['p_end'] + jnp.where(on_diag, upd, 0.0)
        yield

    def post_chain(j):
        rows = slice(j * C, (j + 1) * C)
        d = prepped.pop(j)
        y = y_ref[rows, :]
        mean = head_sum(y) * (1.0 / HEAD_DIM)
        dev = y - mean
        yield
        var = head_sum(dev * dev) * (1.0 / HEAD_DIM)
        yn = dev * lax.rsqrt(var + GN_EPS) * lnw_ref[...] + lnb_ref[...]
        out_ref[rows, :] = (yn + d['bonus']) * d['gate']
        yield

    for t in range(n_sub + 3):
        chains = []
        if 0 <= t - 2 < n_sub:
            chains += [state_chain(t - 2, p) for p in pairs]
        if 0 <= t - 1 < n_sub:
            chains += [local_chain(t - 1, p) for p in pairs]
        if t < n_sub:
            chains.append(prep_chain(t))
        if 0 <= t - 3 < n_sub:
            chains.append(post_chain(t - 3))
        _run_interleaved(chains)

    carry_ref[0:1, :] = zr_ref[R - 1:R, :]
    for p in pairs:
        sbd_ref[p] = state[p]

    @pl.when(c == pl.num_programs(1) - 1)
    def _():
        for p in pairs:
            s_ref[0, 2 * p] = state[p][:HEAD_DIM, :HEAD_DIM]
            s_ref[0, 2 * p + 1] = state[p][HEAD_DIM:, HEAD_DIM:]


def _rwkv_mixer(zr, shift_prev, state0, lw, batch, seq, n_sub):
    valid = seq
    if seq < CHUNK:
        zr = jnp.pad(zr.reshape(batch, seq, RWKV_PROJ), ((0, 0), (0, CHUNK - seq), (0, 0))).reshape(-1, RWKV_PROJ)
        seq, n_sub = CHUNK, 1
    rows = CHUNK * n_sub
    steps = seq // rows
    seg = jnp.asarray(np.kron(np.eye(2), np.ones((HEAD_DIM, HEAD_DIM))), BF16)
    ones = np.ones((CHUNK, CHUNK))
    tri3 = jnp.asarray(np.concatenate([np.tile(np.tril(ones), (1, 3)), np.tile(np.triu(ones, 1), (1, 3))]), BF16)
    row = lambda name: lw[name].reshape(1, -1)
    params = [row('rwkv_mu'), row('rwkv_w0'), lw['rwkv_w2'].astype(BF16), row('rwkv_a0'),
              lw['rwkv_a2'].astype(BF16), lw['rwkv_g2'].astype(BF16), row('rwkv_k_k'), row('rwkv_k_a'),
              row('rwkv_r_k'), row('rwkv_ln_w'), row('rwkv_ln_b'), seg, tri3]
    state_spec = pl.BlockSpec((1, RWKV_HEADS, HEAD_DIM, HEAD_DIM), lambda b, c: (b, 0, 0, 0))
    out, state = pl.pallas_call(
        functools.partial(_rwkv_kernel, min(valid, rows)),
        grid=(batch, steps),
        in_specs=[pl.BlockSpec((rows, RWKV_PROJ), lambda b, c: (b * steps + c, 0)),
                  pl.BlockSpec((1, 1, RWKV_PROJ), lambda b, c: (b, 0, 0)),
                  state_spec] + [_const_spec(p.shape) for p in params],
        out_specs=[pl.BlockSpec((rows, RWKV_WIDTH), lambda b, c: (b * steps + c, 0)), state_spec],
        out_shape=[jax.ShapeDtypeStruct((batch * seq, RWKV_WIDTH), F32),
                   jax.ShapeDtypeStruct(state0.shape, F32)],
        scratch_shapes=[pltpu.VMEM((8, RWKV_PROJ), F32), pltpu.VMEM((rows, RWKV_WIDTH), F32),
                        pltpu.VMEM((RWKV_HEADS // 2, 2 * HEAD_DIM, 2 * HEAD_DIM), F32)],
        compiler_params=_params("parallel", "arbitrary"),
        name="rwkv_mixer",
    )(zr, shift_prev, state0, *params)
    if valid < seq:
        out = out.reshape(batch, seq, RWKV_WIDTH)[:, :valid].reshape(batch * valid, RWKV_WIDTH)
    return out, state


def _memkv_kernel(m_ref, g_ref, wk_ref, wv_ref, k_ref, v_ref):
    mn = _rms(m_ref[...], g_ref[...]).astype(BF16)
    k_ref[...] = jnp.dot(mn, wk_ref[...], preferred_element_type=F32)
    v_ref[...] = jnp.dot(mn, wv_ref[...], preferred_element_type=F32)


def _memory_kv(mem2d, g, w_mk, w_mv):
    n = mem2d.shape[0]
    tm = min(ROW_TILE, n)
    row = lambda w: pl.BlockSpec((tm, w), lambda i: (i, 0))
    return pl.pallas_call(
        _memkv_kernel,
        grid=(n // tm,),
        in_specs=[row(D_MODEL), _const_spec((1, D_MODEL)), _const_spec(w_mk.shape), _const_spec(w_mv.shape)],
        out_specs=[row(MEM_WIDTH), row(MEM_WIDTH)],
        out_shape=[jax.ShapeDtypeStruct((n, MEM_WIDTH), F32)] * 2,
        compiler_params=_params("parallel"),
        name="memory_kv",
    )(mem2d, g, w_mk, w_mv)


def _tail_kernel(x_ref, a_ref, r_ref, mk_ref, mv_ref, wo_ref, gc_ref, wq_ref, wco_ref, gm_ref, wu_ref, wd_ref, gf_ref,
                 y_ref):
    x1 = x_ref[...] + _dot(a_ref[...], wo_ref[:ATTN_WIDTH, :]) + _dot(r_ref[...], wo_ref[ATTN_WIDTH:, :])
    q = _dot(_rms(x1, gc_ref[...]), wq_ref[...]).astype(BF16)
    mk = mk_ref[0].astype(BF16)
    mv = mv_ref[0].astype(BF16)
    outs = []
    for h in range(MEM_HEADS):
        sl = slice(h * MEM_HEAD_DIM, (h + 1) * MEM_HEAD_DIM)
        s = _dg(q[:, sl], mk[:, sl], _NT) * (MEM_HEAD_DIM ** -0.5)
        p = jnp.exp(s - jnp.max(s, axis=-1, keepdims=True))
        o = _dg(p.astype(BF16), mv[:, sl], _NN) * (1.0 / jnp.sum(p, axis=-1, keepdims=True))
        outs.append(o)
    x2 = x1 + _dot(_cat_lanes(*outs), wco_ref[...])
    up = _dot(_rms(x2, gm_ref[...]), wu_ref[...])
    act = jnp.square(jnp.maximum(up, 0.0))
    y_ref[...] = _rms(x2 + _dot(act, wd_ref[...]), gf_ref[...])


def _tail(x2d, a_out, r_out, mk, mv, lw, batch, seq):
    tq = min(ROW_TILE, seq)
    nt = seq // tq
    row = lambda w: pl.BlockSpec((tq, w), lambda b, t: (b * nt + t, 0))
    mem_spec = pl.BlockSpec((1, N_MEM, MEM_WIDTH), lambda b, t: (b, 0, 0))

    def resident(shape):
        nd = len(shape)
        return pl.BlockSpec(shape, lambda *_: (0,) * nd, pipeline_mode=pl.Buffered(1))

    weights = [lw['w_out'], lw['norm_cross_g'], lw['w_cq'], lw['w_co'], lw['norm_mlp_g'], lw['w_up'], lw['w_down'],
               lw['norm_final_g']]
    return pl.pallas_call(
        _tail_kernel,
        grid=(batch, nt),
        in_specs=[row(D_MODEL), row(ATTN_WIDTH), row(RWKV_WIDTH), mem_spec, mem_spec]
                 + [resident(w.shape) for w in weights],
        out_specs=row(D_MODEL),
        out_shape=jax.ShapeDtypeStruct(x2d.shape, F32),
        compiler_params=_params("parallel", "parallel"),
        name="tail",
    )(x2d, a_out, r_out, mk, mv, *weights)


def _trunk(x, mk, mv, k_past, v_past, shift_prev, state0, lw, table, n_sub):
    batch, seq = x.shape[0], x.shape[1]
    x2d = x.reshape(batch * seq, D_MODEL)
    q, k, v, zr = _in_proj(x2d, lw['norm_mix_g'], lw['w_in'])
    if k_past is None:
        bias = _rel_bias(table, CHUNK, WINDOW + CHUNK)
        a_out = _prompt_attention(q, k, v, lw['attn_sink'], bias, batch, seq)
        k3 = k.reshape(batch, seq, KV_WIDTH)
        v3 = v.reshape(batch, seq, KV_WIDTH)
        k_buf, v_buf = k3[:, -WINDOW:], v3[:, -WINDOW:]
    else:
        bias = _rel_bias(table, seq, WINDOW + seq)
        a_out = _sample_attention(q, k, v, k_past.reshape(batch * WINDOW, KV_WIDTH),
                                  v_past.reshape(batch * WINDOW, KV_WIDTH), lw['attn_sink'], bias, seq)
        k_buf = jnp.concatenate([k_past, k.reshape(batch, seq, KV_WIDTH)], axis=1)[:, -WINDOW:]
        v_buf = jnp.concatenate([v_past, v.reshape(batch, seq, KV_WIDTH)], axis=1)[:, -WINDOW:]
    r_out, state = _rwkv_mixer(zr, shift_prev, state0, lw, batch, seq, n_sub)
    shift_new = zr.reshape(batch, seq, RWKV_PROJ)[:, -1:]
    y = _tail(x2d, a_out, r_out, mk, mv, lw, batch, seq)
    kv_shape = (batch, WINDOW, KV_HEADS, HEAD_DIM)
    return y.reshape(x.shape), k_buf.reshape(kv_shape), v_buf.reshape(kv_shape), shift_new, state


def kernel(x_prompt, x_sample, mem_prompt, cache_attn_k, cache_attn_v, cache_mem_k, cache_mem_v, state_shift,
           state_wkv, norm_mix_g, w_in, attn_sink, rel_bias_table, rwkv_mu, rwkv_w0, rwkv_w2, rwkv_a0, rwkv_a2,
           rwkv_g2, rwkv_k_k, rwkv_k_a, rwkv_r_k, rwkv_ln_w, rwkv_ln_b, w_out, norm_cross_g, norm_mem_g, w_cq,
           w_mk, w_mv, w_co, norm_mlp_g, w_up, w_down, norm_final_g):
    assert norm_mix_g.shape[0] == 1, "single-layer trunk"
    bp, dec_b = x_prompt.shape[0], x_sample.shape[0]
    vec = lambda p: p[0].reshape(1, -1)
    lw = {
        'norm_mix_g': vec(norm_mix_g), 'w_in': w_in[0].astype(BF16), 'attn_sink': attn_sink[0],
        'rwkv_mu': rwkv_mu[0], 'rwkv_w0': rwkv_w0[0], 'rwkv_w2': rwkv_w2[0], 'rwkv_a0': rwkv_a0[0],
        'rwkv_a2': rwkv_a2[0], 'rwkv_g2': rwkv_g2[0], 'rwkv_k_k': rwkv_k_k[0], 'rwkv_k_a': rwkv_k_a[0],
        'rwkv_r_k': rwkv_r_k[0], 'rwkv_ln_w': rwkv_ln_w[0], 'rwkv_ln_b': rwkv_ln_b[0],
        'w_out': w_out[0].astype(BF16), 'norm_cross_g': vec(norm_cross_g), 'w_cq': w_cq[0].astype(BF16),
        'w_co': w_co[0].astype(BF16), 'norm_mlp_g': vec(norm_mlp_g), 'w_up': w_up[0].astype(BF16),
        'w_down': w_down[0].astype(BF16), 'norm_final_g': norm_final_g.reshape(1, -1),
    }
    mk, mv = _memory_kv(mem_prompt.reshape(bp * N_MEM, D_MODEL), vec(norm_mem_g),
                        w_mk[0].astype(BF16), w_mv[0].astype(BF16))
    mk = mk.reshape(bp, N_MEM, MEM_WIDTH)
    mv = mv.reshape(bp, N_MEM, MEM_WIDTH)
    shift0 = jnp.zeros((bp, 1, RWKV_PROJ), F32)
    wkv0 = jnp.zeros((bp, RWKV_HEADS, HEAD_DIM, HEAD_DIM), F32)
    yp, pk, pv, psh, pS = _trunk(x_prompt, mk, mv, None, None, shift0, wkv0, lw, rel_bias_table, RWKV_SUB_CHUNKS)
    ys, sk, sv, ssh, sS = _trunk(
        x_sample, cache_mem_k[0].reshape(dec_b, N_MEM, MEM_WIDTH), cache_mem_v[0].reshape(dec_b, N_MEM, MEM_WIDTH),
        cache_attn_k[0].reshape(dec_b, WINDOW, KV_WIDTH), cache_attn_v[0].reshape(dec_b, WINDOW, KV_WIDTH),
        state_shift[0], state_wkv[0], lw, rel_bias_table, 1)
    mem_shape = (1, bp, N_MEM, MEM_HEADS, MEM_HEAD_DIM)
    return (yp, ys, pk[None], pv[None], mk.reshape(mem_shape), mv.reshape(mem_shape), psh[None], pS[None],
            sk[None], sv[None], ssh[None], sS[None])
```

```python
import functools
import math

import numpy as np
import jax
import jax.numpy as jnp
from jax import lax
from jax.experimental import pallas as pl
from jax.experimental.pallas import tpu as pltpu

F32 = jnp.float32
BF16 = jnp.bfloat16

D_MODEL = 1024
CHUNK = 64
WINDOW = 128
HEAD_DIM = 64
ATTN_WIDTH = 512
ATTN_HEADS = 8
KV_HEADS = 2
GROUP = 4
KV_WIDTH = 128
RWKV_WIDTH = 512
RWKV_HEADS = 8
DECAY_LORA = 64
AAA_LORA = 64
GATE_LORA = 128
RWKV_PROJ = 1792
IN_PROJ = 2560
N_MEM = 256
MEM_HEADS = 4
MEM_HEAD_DIM = 128
MEM_WIDTH = 512
D_FF = 4096
REL_BUCKETS = 32
REL_MAX_DIST = 128
NORM_EPS = 1e-6
GN_EPS = 64e-5

V7X_VMEM_LIMIT_BYTES = 52 * 1024 * 1024
ROW_TILE = 256
IN_PROJ_ROW_TILE = 512
TAIL_ROW_TILE = 512
ATTN_CHUNKS_PER_STEP = 4
RWKV_SUB_CHUNKS = 8


def _params(*sem):
    return pltpu.CompilerParams(dimension_semantics=sem, vmem_limit_bytes=V7X_VMEM_LIMIT_BYTES)


def _const_spec(shape):
    nd = len(shape)
    return pl.BlockSpec(shape, lambda *_: (0,) * nd)


_NN = ((1,), (0,))
_NT = ((1,), (1,))
_TN = ((0,), (0,))


def _dg(a, b, dims):
    return lax.dot_general(a, b, (dims, ((), ())), preferred_element_type=F32)


def _dot(a, b):
    return _dg(a.astype(BF16), b.astype(BF16), _NN)


def _dot_nt(a, b):
    return _dg(a.astype(BF16), b.astype(BF16), _NT)


def _cat_rows(*xs):
    return jnp.concatenate(xs, axis=0)


def _cat_lanes(*xs):
    return jnp.concatenate(xs, axis=1)


def _run_interleaved(chains):
    active = list(chains)
    while active:
        still = []
        for ch in active:
            try:
                next(ch)
                still.append(ch)
            except StopIteration:
                pass
        active = still


def _rms(x, g):
    return x * lax.rsqrt(jnp.mean(x * x, axis=-1, keepdims=True) + NORM_EPS) * g


def _inproj_kernel(x_ref, g_ref, w_ref, q_ref, k_ref, v_ref, zr_ref):
    h = _rms(x_ref[...], g_ref[...]).astype(BF16)
    q_ref[...] = jnp.dot(h, w_ref[:, :ATTN_WIDTH], preferred_element_type=F32)
    k_ref[...] = jnp.dot(h, w_ref[:, ATTN_WIDTH:ATTN_WIDTH + KV_WIDTH], preferred_element_type=F32)
    v_ref[...] = jnp.dot(h, w_ref[:, ATTN_WIDTH + KV_WIDTH:ATTN_WIDTH + 2 * KV_WIDTH],
                         preferred_element_type=F32)
    zr_ref[...] = jnp.dot(h, w_ref[:, ATTN_WIDTH + 2 * KV_WIDTH:], preferred_element_type=F32)


def _in_proj(x2d, g, w_bf16):
    n = x2d.shape[0]
    tm = min(IN_PROJ_ROW_TILE, n)
    row = lambda w: pl.BlockSpec((tm, w), lambda i: (i, 0))
    return pl.pallas_call(
        _inproj_kernel,
        grid=(n // tm,),
        in_specs=[row(D_MODEL), _const_spec((1, D_MODEL)), _const_spec((D_MODEL, IN_PROJ))],
        out_specs=[row(ATTN_WIDTH), row(KV_WIDTH), row(KV_WIDTH), row(RWKV_PROJ)],
        out_shape=[jax.ShapeDtypeStruct((n, w), F32) for w in (ATTN_WIDTH, KV_WIDTH, KV_WIDTH, RWKV_PROJ)],
        compiler_params=_params("parallel"),
        name="in_proj",
    )(x2d, g, w_bf16)


def _t5_bucket(rel):
    half = REL_BUCKETS // 2
    max_exact = half // 2
    assert REL_MAX_DIST == max_exact * 2 ** 4 and half - max_exact == 2 * 4
    n = np.abs(rel)
    large = max_exact + sum((n * n >= max_exact * max_exact * 2 ** t).astype(np.int64)
                            for t in range(1, half - max_exact))
    return (np.where(rel > 0, half, 0) + np.where(n < max_exact, n, large)).astype(np.int32)


def _bias_kernel(table_ref, bucket_ref, out_ref):
    bucket = bucket_ref[...]
    hits = [bucket == b for b in range(REL_BUCKETS)]
    for h in range(ATTN_HEADS):
        acc = jnp.zeros(bucket.shape, F32)
        for b in range(REL_BUCKETS):
            acc = jnp.where(hits[b], table_ref[b, h], acc)
        out_ref[h] = acc


def _rel_bias(table, n_q, n_k):
    rel = np.arange(n_k)[None, :] - WINDOW - np.arange(n_q)[:, None]
    bucket = jnp.asarray(_t5_bucket(rel))
    bias = pl.pallas_call(
        _bias_kernel,
        in_specs=[pl.BlockSpec(memory_space=pltpu.SMEM), pl.BlockSpec(memory_space=pltpu.VMEM)],
        out_specs=pl.BlockSpec(memory_space=pltpu.VMEM),
        out_shape=jax.ShapeDtypeStruct((ATTN_HEADS, n_q, n_k), F32),
        name="rel_bias",
    )(table, bucket)
    return bias.reshape(KV_HEADS, GROUP * n_q, n_k)


def _group_sinks(sink_ref, n_q):
    row_group = lax.broadcasted_iota(jnp.int32, (GROUP * n_q, 1), 0) // n_q
    sinks = []
    for kvh in range(KV_HEADS):
        sink = jnp.zeros((GROUP * n_q, 1), F32)
        for g in range(GROUP):
            sink = jnp.where(row_group == g, sink_ref[kvh * GROUP + g], sink)
        sinks.append(sink)
    return sinks


def _attn_chain(q, keys, vals, bias, sink, valid, o_ref, rows, kvh):
    n_q = q.shape[0]
    qh = _cat_rows(*[q[:, (kvh * GROUP + g) * HEAD_DIM:(kvh * GROUP + g + 1) * HEAD_DIM]
                     for g in range(GROUP)]).astype(BF16)
    s = _dg(qh, keys, _NT) * (HEAD_DIM ** -0.5) + bias
    if valid is not None:
        s = jnp.where(valid, s, -jnp.inf)
    yield
    m = jnp.maximum(jnp.max(s, axis=-1, keepdims=True), sink)
    p = jnp.exp(s - m)
    den = jnp.sum(p, axis=-1, keepdims=True) + jnp.exp(sink - m)
    yield
    o = _dg(p.astype(BF16), vals, _NN) * (1.0 / den)
    for g in range(GROUP):
        head = kvh * GROUP + g
        o_ref[rows, head * HEAD_DIM:(head + 1) * HEAD_DIM] = o[g * n_q:(g + 1) * n_q]
    yield


def _prompt_attn_kernel(sink_ref, q_ref, kp_ref, kc_ref, vp_ref, vc_ref, bias_ref, o_ref):
    n_chunks = q_ref.shape[0] // CHUNK
    n_k = WINDOW + CHUNK
    k_all = _cat_rows(kp_ref[...], kc_ref[...]).astype(BF16)
    v_all = _cat_rows(vp_ref[...], vc_ref[...]).astype(BF16)
    first_valid = jnp.where(pl.program_id(1) == 0, WINDOW, 0)
    kcol = lax.broadcasted_iota(jnp.int32, (1, n_k), 1)
    sinks = _group_sinks(sink_ref, CHUNK)
    chains = []
    for jj in range(n_chunks):
        rows = slice(jj * CHUNK, (jj + 1) * CHUNK)
        keys = slice(jj * CHUNK, jj * CHUNK + n_k)
        valid = kcol + jj * CHUNK >= first_valid
        for kvh in range(KV_HEADS):
            lanes = slice(kvh * HEAD_DIM, (kvh + 1) * HEAD_DIM)
            chains.append(_attn_chain(q_ref[rows, :], k_all[keys, lanes], v_all[keys, lanes], bias_ref[kvh],
                                      sinks[kvh], valid, o_ref, rows, kvh))
    _run_interleaved(chains)


def _prompt_attention(q, k, v, sink, bias, batch, seq):
    rows = ATTN_CHUNKS_PER_STEP * CHUNK
    steps = seq // rows
    per_window = rows // WINDOW
    q_spec = pl.BlockSpec((rows, ATTN_WIDTH), lambda b, i: (b * steps + i, 0))
    cur_spec = pl.BlockSpec((rows, KV_WIDTH), lambda b, i: (b * steps + i, 0))
    prev_spec = pl.BlockSpec((WINDOW, KV_WIDTH),
                             lambda b, i: (jnp.maximum((b * steps + i) * per_window - 1, 0), 0))
    return pl.pallas_call(
        _prompt_attn_kernel,
        grid=(batch, steps),
        in_specs=[pl.BlockSpec(memory_space=pltpu.SMEM), q_spec, prev_spec, cur_spec, prev_spec, cur_spec,
                  _const_spec(bias.shape)],
        out_specs=q_spec,
        out_shape=jax.ShapeDtypeStruct(q.shape, F32),
        compiler_params=_params("parallel", "parallel"),
        name="prompt_attention",
    )(sink, q, k, k, v, v, bias)


def _sample_attn_kernel(seq, sink_ref, q_ref, kp_ref, kn_ref, vp_ref, vn_ref, bias_ref, o_ref):
    batch = q_ref.shape[0] // seq
    sinks = _group_sinks(sink_ref, seq)
    chains = []
    for b in range(batch):
        rows = slice(b * seq, (b + 1) * seq)
        past = slice(b * WINDOW, (b + 1) * WINDOW)
        k_all = _cat_rows(kp_ref[past, :], kn_ref[rows, :]).astype(BF16)
        v_all = _cat_rows(vp_ref[past, :], vn_ref[rows, :]).astype(BF16)
        for kvh in range(KV_HEADS):
            lanes = slice(kvh * HEAD_DIM, (kvh + 1) * HEAD_DIM)
            chains.append(_attn_chain(q_ref[rows, :], k_all[:, lanes], v_all[:, lanes], bias_ref[kvh], sinks[kvh],
                                      None, o_ref, rows, kvh))
    _run_interleaved(chains)


def _sample_attention(q, k, v, k_past, v_past, sink, bias, seq):
    vmem = pl.BlockSpec(memory_space=pltpu.VMEM)
    return pl.pallas_call(
        functools.partial(_sample_attn_kernel, seq),
        in_specs=[pl.BlockSpec(memory_space=pltpu.SMEM)] + [vmem] * 6,
        out_specs=vmem,
        out_shape=jax.ShapeDtypeStruct(q.shape, F32),
        name="sample_attention",
    )(sink, q, k_past, k, v_past, v, bias)


def _split2(x):
    hi = x.astype(BF16)
    lo = (x - hi.astype(F32)).astype(BF16)
    return hi, lo


def _softplus(x):
    return jnp.maximum(x, 0.0) + jnp.log(1.0 + jnp.exp(-jnp.abs(x)))


def _sigmoid(x):
    return 1.0 / (1.0 + jnp.exp(-x))


def _rwkv_kernel(valid_rows, zr_ref, shift_ref, s0_ref, mu_ref, w0_ref, w2_ref, a0_ref, a2_ref, g2_ref, kk_ref,
                 ka_ref, rk_ref, lnw_ref, lnb_ref, seg_ref, tri_ref,
                 out_ref, s_ref, carry_ref, y_ref, sbd_ref):
    c = pl.program_id(1)
    C = CHUNK
    R = zr_ref.shape[0]
    n_sub = R // C
    pairs = range(RWKV_HEADS // 2)
    PAIR = 2 * HEAD_DIM
    W = RWKV_WIDTH

    @pl.when(c == 0)
    def _():
        carry_ref[0:1, :] = shift_ref[0]
        zero = jnp.zeros((HEAD_DIM, HEAD_DIM), F32)
        for p in pairs:
            sbd_ref[p] = _cat_rows(_cat_lanes(s0_ref[0, 2 * p], zero), _cat_lanes(zero, s0_ref[0, 2 * p + 1]))

    seg = seg_ref[...]
    seg2 = _cat_rows(seg, seg)

    def head_sum(x):
        hi, lo = _split2(x)
        tiles = [_dg(_cat_lanes(hi[:, t * PAIR:(t + 1) * PAIR], lo[:, t * PAIR:(t + 1) * PAIR]), seg2, _NN)
                 for t in range(W // PAIR)]
        return _cat_lanes(*tiles)

    first_row = lax.broadcasted_iota(jnp.int32, (C, 1), 0) == 0
    tri3 = tri_ref[...]

    lane = lax.broadcasted_iota(jnp.int32, (C, PAIR), 1)
    trow = lax.broadcasted_iota(jnp.int32, (C, PAIR), 0)
    even = lane < HEAD_DIM
    tcol = jnp.where(even, lane, lane - HEAD_DIM)
    strict = tcol < trow
    incl = tcol <= trow
    eye = jnp.where(tcol == trow, 1.0, 0.0).astype(F32)
    brow = lax.broadcasted_iota(jnp.int32, (PAIR, PAIR), 0) < HEAD_DIM
    bcol = lax.broadcasted_iota(jnp.int32, (PAIR, PAIR), 1) < HEAD_DIM
    on_diag = brow == bcol

    def bd(x):
        zero = jnp.zeros_like(x)
        return _cat_rows(jnp.where(even, x, zero), jnp.where(even, zero, x))

    def bd2(pair):
        return bd(pair[0]), bd(pair[1])

    def mm(a_pair, w_pair, dims=_NN):
        if dims == _NN:
            first = _dg(_cat_lanes(a_pair[0], a_pair[1]), _cat_rows(w_pair[0], w_pair[0]), dims)
        else:
            first = _dg(_cat_lanes(a_pair[0], a_pair[1]), _cat_lanes(w_pair[0], w_pair[0]), dims)
        return first + _dg(a_pair[0], w_pair[1], dims)

    prepped = {}
    ready = {}
    state = [sbd_ref[p] for p in pairs]

    def prep_chain(j):
        rows = slice(j * C, (j + 1) * C)
        zr = zr_ref[rows, :]
        before = carry_ref[0:1, :] if j == 0 else zr_ref[j * C - 1:j * C, :]
        z_prev = jnp.where(first_row, before, pltpu.roll(zr, 1, axis=0))
        zs = zr + (z_prev - zr) * mu_ref[...]
        r = zs[:, :W]
        k = zs[:, W:2 * W]
        v = zs[:, 2 * W:3 * W]
        wd = zs[:, 3 * W:3 * W + DECAY_LORA]
        ad = zs[:, 3 * W + DECAY_LORA:3 * W + DECAY_LORA + AAA_LORA]
        gd = zs[:, 3 * W + DECAY_LORA + AAA_LORA:]
        w_log = -_softplus(-(w0_ref[...] + _dot(jnp.tanh(wd), w2_ref[...]))) - 0.5
        lw = -jnp.exp(w_log)
        a = _sigmoid(a0_ref[...] + _dot(ad, a2_ref[...]))
        gate = _dot(_sigmoid(gd), g2_ref[...])
        kk = k * kk_ref[...]
        kk = kk / jnp.maximum(jnp.sqrt(head_sum(kk * kk)), 1e-12)
        k2 = k * (1.0 + (a - 1.0) * ka_ref[...])
        if (j + 1) * C > valid_rows:
            live = lax.broadcasted_iota(jnp.int32, (C, 1), 0) < valid_rows - j * C
            lw = jnp.where(live, lw, 0.0)
            kk = jnp.where(live, kk, 0.0)
            k2 = jnp.where(live, k2, 0.0)
        bvec = kk * a
        yield
        l1 = lw.astype(BF16)
        rem = lw - l1.astype(F32)
        l2 = rem.astype(BF16)
        l3 = (rem - l2.astype(F32)).astype(BF16)
        sums = _dg(tri3, _cat_rows(l1, l2, l3), _NN)
        li = sums[:C]
        lrev = sums[C:]
        yield
        inv_p = jnp.exp(-li)
        to_end = jnp.exp(lrev)
        prepped[j] = dict(
            at=_split2(-kk * jnp.exp(li - lw)), rt=_split2(r * jnp.exp(li)), bt=_split2(bvec * inv_p),
            kt=_split2(k2 * inv_p), bh=_split2(bvec * to_end), kh=_split2(k2 * to_end), v=_split2(v),
            p_end=jnp.exp(li[C - 1:C, :]), bonus=head_sum(r * k2 * rk_ref[...]) * v, gate=gate)
        yield

    def local_chain(j, p):
        d = prepped[j]
        lanes = slice(p * PAIR, (p + 1) * PAIR)
        cut = lambda pair: (pair[0][:, lanes], pair[1][:, lanes])
        at_p, rt_p, bt_p, kt_p, bh_p, kh_p, v_p = map(cut, (d['at'], d['rt'], d['bt'], d['kt'], d['bh'], d['kh'],
                                                            d['v']))
        left = (_cat_rows(at_p[0], rt_p[0]), _cat_rows(at_p[1], rt_p[1]))
        right = (_cat_rows(bd(bt_p[0]), bd(kt_p[0])), _cat_rows(bd(bt_p[1]), bd(kt_p[1])))
        aa = mm(left, right, _NT)
        yield
        a_ab = jnp.where(strict, aa[:C, :PAIR], 0.0)
        a_ak = jnp.where(strict, aa[:C, PAIR:], 0.0)
        a_rb = jnp.where(incl, aa[C:, :PAIR], 0.0)
        a_rk = jnp.where(incl, aa[C:, PAIR:], 0.0)
        inv = eye + a_ab
        ps = _split2(a_ab)
        power = mm(ps, bd2(ps))
        span = 2
        yield
        while span < C:
            ps = _split2(power)
            pw = bd2(ps)
            ih = _split2(inv)
            if span * 2 < C:
                both = mm((_cat_rows(ih[0], ps[0]), _cat_rows(ih[1], ps[1])), pw)
                inv = inv + both[:C]
                power = both[C:]
            else:
                inv = inv + mm(ih, pw)
            span *= 2
            yield
        ready[(j, p)] = dict(inv=_split2(inv), akrk=_split2(_cat_rows(a_ak, a_rk)), rb=_split2(a_rb), left=left,
                             bhkh=(_cat_rows(bh_p[0], kh_p[0]), _cat_rows(bh_p[1], kh_p[1])), v=v_p,
                             p_end=d['p_end'][:, lanes])

    def state_chain(j, p):
        d = ready.pop((j, p))
        s_prev = state[p]
        v_hi, v_lo = d['v']
        both = mm(d['left'], _split2(s_prev), _NT) + mm(d['akrk'], (bd(v_hi), bd(v_lo)))
        rhs = both[:C]
        y0 = both[C:]
        yield
        u_pair = _split2(mm(d['inv'], bd2(_split2(rhs))))
        yield
        y_ref[j * C:(j + 1) * C, p * PAIR:(p + 1) * PAIR] = y0 + mm(d['rb'], bd2(u_pair))
        t_hi = _cat_rows(u_pair[0], v_hi)
        t_lo = _cat_rows(u_pair[1], v_lo)
        w_hi, w_lo = d['bhkh']
        upd = _dg(_cat_rows(t_hi, t_lo), _cat_rows(w_hi, w_hi), _TN) + _dg(t_hi, w_lo, _TN)
        state[p] = s_prev * d['p_end'] + jnp.where(on_diag, upd, 0.0)
        yield

    def post_chain(j):
        rows = slice(j * C, (j + 1) * C)
        d = prepped.pop(j)
        y = y_ref[rows, :]
        mean = head_sum(y) * (1.0 / HEAD_DIM)
        dev = y - mean
        yield
        var = head_sum(dev * dev) * (1.0 / HEAD_DIM)
        yn = dev * lax.rsqrt(var + GN_EPS) * lnw_ref[...] + lnb_ref[...]
        out_ref[rows, :] = (yn + d['bonus']) * d['gate']
        yield

    for t in range(n_sub + 3):
        chains = []
        if 0 <= t - 2 < n_sub:
            chains += [state_chain(t - 2, p) for p in pairs]
        if 0 <= t - 1 < n_sub:
            chains += [local_chain(t - 1, p) for p in pairs]
        if t < n_sub:
            chains.append(prep_chain(t))
        if 0 <= t - 3 < n_sub:
            chains.append(post_chain(t - 3))
        _run_interleaved(chains)

    carry_ref[0:1, :] = zr_ref[R - 1:R, :]
    for p in pairs:
        sbd_ref[p] = state[p]

    @pl.when(c == pl.num_programs(1) - 1)
    def _():
        for p in pairs:
            s_ref[0, 2 * p] = state[p][:HEAD_DIM, :HEAD_DIM]
            s_ref[0, 2 * p + 1] = state[p][HEAD_DIM:, HEAD_DIM:]


def _rwkv_mixer(zr, shift_prev, state0, lw, batch, seq, n_sub):
    valid = seq
    if seq < CHUNK:
        zr = jnp.pad(zr.reshape(batch, seq, RWKV_PROJ), ((0, 0), (0, CHUNK - seq), (0, 0))).reshape(-1, RWKV_PROJ)
        seq, n_sub = CHUNK, 1
    rows = CHUNK * n_sub
    steps = seq // rows
    seg = jnp.asarray(np.kron(np.eye(2), np.ones((HEAD_DIM, HEAD_DIM))), BF16)
    ones = np.ones((CHUNK, CHUNK))
    tri3 = jnp.asarray(np.concatenate([np.tile(np.tril(ones), (1, 3)), np.tile(np.triu(ones, 1), (1, 3))]), BF16)
    row = lambda name: lw[name].reshape(1, -1)
    params = [row('rwkv_mu'), row('rwkv_w0'), lw['rwkv_w2'].astype(BF16), row('rwkv_a0'),
              lw['rwkv_a2'].astype(BF16), lw['rwkv_g2'].astype(BF16), row('rwkv_k_k'), row('rwkv_k_a'),
              row('rwkv_r_k'), row('rwkv_ln_w'), row('rwkv_ln_b'), seg, tri3]
    state_spec = pl.BlockSpec((1, RWKV_HEADS, HEAD_DIM, HEAD_DIM), lambda b, c: (b, 0, 0, 0))
    out, state = pl.pallas_call(
        functools.partial(_rwkv_kernel, min(valid, rows)),
        grid=(batch, steps),
        in_specs=[pl.BlockSpec((rows, RWKV_PROJ), lambda b, c: (b * steps + c, 0)),
                  pl.BlockSpec((1, 1, RWKV_PROJ), lambda b, c: (b, 0, 0)),
                  state_spec] + [_const_spec(p.shape) for p in params],
        out_specs=[pl.BlockSpec((rows, RWKV_WIDTH), lambda b, c: (b * steps + c, 0)), state_spec],
        out_shape=[jax.ShapeDtypeStruct((batch * seq, RWKV_WIDTH), F32),
                   jax.ShapeDtypeStruct(state0.shape, F32)],
        scratch_shapes=[pltpu.VMEM((8, RWKV_PROJ), F32), pltpu.VMEM((rows, RWKV_WIDTH), F32),
                        pltpu.VMEM((RWKV_HEADS // 2, 2 * HEAD_DIM, 2 * HEAD_DIM), F32)],
        compiler_params=_params("parallel", "arbitrary"),
        name="rwkv_mixer",
    )(zr, shift_prev, state0, *params)
    if valid < seq:
        out = out.reshape(batch, seq, RWKV_WIDTH)[:, :valid].reshape(batch * valid, RWKV_WIDTH)
    return out, state


def _memkv_kernel(m_ref, g_ref, wk_ref, wv_ref, k_ref, v_ref):
    mn = _rms(m_ref[...], g_ref[...]).astype(BF16)
    k_ref[...] = jnp.dot(mn, wk_ref[...], preferred_element_type=F32)
    v_ref[...] = jnp.dot(mn, wv_ref[...], preferred_element_type=F32)


def _memory_kv(mem2d, g, w_mk, w_mv):
    n = mem2d.shape[0]
    tm = min(ROW_TILE, n)
    row = lambda w: pl.BlockSpec((tm, w), lambda i: (i, 0))
    return pl.pallas_call(
        _memkv_kernel,
        grid=(n // tm,),
        in_specs=[row(D_MODEL), _const_spec((1, D_MODEL)), _const_spec(w_mk.shape), _const_spec(w_mv.shape)],
        out_specs=[row(MEM_WIDTH), row(MEM_WIDTH)],
        out_shape=[jax.ShapeDtypeStruct((n, MEM_WIDTH), F32)] * 2,
        compiler_params=_params("parallel"),
        name="memory_kv",
    )(mem2d, g, w_mk, w_mv)


def _tail_kernel(x_ref, a_ref, r_ref, mk_ref, mv_ref, wo_ref, gc_ref, wq_ref, wco_ref, gm_ref, wu_ref, wd_ref, gf_ref,
                 y_ref):
    n_seq = mk_ref.shape[0]
    rows_per_seq = x_ref.shape[0] // n_seq
    x1 = x_ref[...] + _dot(a_ref[...], wo_ref[:ATTN_WIDTH, :]) + _dot(r_ref[...], wo_ref[ATTN_WIDTH:, :])
    q = _dot(_rms(x1, gc_ref[...]), wq_ref[...]).astype(BF16)
    outs = {}

    def cross_chain(b, h):
        rows = slice(b * rows_per_seq, (b + 1) * rows_per_seq)
        sl = slice(h * MEM_HEAD_DIM, (h + 1) * MEM_HEAD_DIM)
        s = _dg(q[rows, sl], mk_ref[b, :, sl].astype(BF16), _NT) * (MEM_HEAD_DIM ** -0.5)
        yield
        p = jnp.exp(s - jnp.max(s, axis=-1, keepdims=True))
        den = jnp.sum(p, axis=-1, keepdims=True)
        yield
        outs[(b, h)] = _dg(p.astype(BF16), mv_ref[b, :, sl].astype(BF16), _NN) * (1.0 / den)
        yield

    _run_interleaved([cross_chain(b, h) for b in range(n_seq) for h in range(MEM_HEADS)])
    o = _cat_rows(*[_cat_lanes(*[outs[(b, h)] for h in range(MEM_HEADS)]) for b in range(n_seq)])
    x2 = x1 + _dot(o, wco_ref[...])
    up = _dot(_rms(x2, gm_ref[...]), wu_ref[...])
    act = jnp.square(jnp.maximum(up, 0.0))
    y_ref[...] = _rms(x2 + _dot(act, wd_ref[...]), gf_ref[...])


def _tail(x2d, a_out, r_out, mk, mv, lw, batch, seq):
    n = batch * seq
    tq = min(TAIL_ROW_TILE, n)
    if seq >= tq:
        assert seq % tq == 0
        seq_per_tile, tiles_per_seq = 1, seq // tq
        mem_spec = pl.BlockSpec((1, N_MEM, MEM_WIDTH), lambda i: (i // tiles_per_seq, 0, 0))
    else:
        assert tq % seq == 0
        seq_per_tile = tq // seq
        mem_spec = pl.BlockSpec((seq_per_tile, N_MEM, MEM_WIDTH), lambda i: (i, 0, 0))
    row = lambda w: pl.BlockSpec((tq, w), lambda i: (i, 0))

    def resident(shape):
        nd = len(shape)
        return pl.BlockSpec(shape, lambda *_: (0,) * nd, pipeline_mode=pl.Buffered(1))

    weights = [lw['w_out'], lw['norm_cross_g'], lw['w_cq'], lw['w_co'], lw['norm_mlp_g'], lw['w_up'], lw['w_down'],
               lw['norm_final_g']]
    return pl.pallas_call(
        _tail_kernel,
        grid=(n // tq,),
        in_specs=[row(D_MODEL), row(ATTN_WIDTH), row(RWKV_WIDTH), mem_spec, mem_spec]
                 + [resident(w.shape) for w in weights],
        out_specs=row(D_MODEL),
        out_shape=jax.ShapeDtypeStruct(x2d.shape, F32),
        compiler_params=_params("parallel"),
        name="tail",
    )(x2d, a_out, r_out, mk, mv, *weights)


def _trunk(x, mk, mv, k_past, v_past, shift_prev, state0, lw, table, n_sub):
    batch, seq = x.shape[0], x.shape[1]
    x2d = x.reshape(batch * seq, D_MODEL)
    q, k, v, zr = _in_proj(x2d, lw['norm_mix_g'], lw['w_in'])
    if k_past is None:
        bias = _rel_bias(table, CHUNK, WINDOW + CHUNK)
        a_out = _prompt_attention(q, k, v, lw['attn_sink'], bias, batch, seq)
        k3 = k.reshape(batch, seq, KV_WIDTH)
        v3 = v.reshape(batch, seq, KV_WIDTH)
        k_buf, v_buf = k3[:, -WINDOW:], v3[:, -WINDOW:]
    else:
        bias = _rel_bias(table, seq, WINDOW + seq)
        a_out = _sample_attention(q, k, v, k_past.reshape(batch * WINDOW, KV_WIDTH),
                                  v_past.reshape(batch * WINDOW, KV_WIDTH), lw['attn_sink'], bias, seq)
        k_buf = jnp.concatenate([k_past, k.reshape(batch, seq, KV_WIDTH)], axis=1)[:, -WINDOW:]
        v_buf = jnp.concatenate([v_past, v.reshape(batch, seq, KV_WIDTH)], axis=1)[:, -WINDOW:]
    r_out, state = _rwkv_mixer(zr, shift_prev, state0, lw, batch, seq, n_sub)
    shift_new = zr.reshape(batch, seq, RWKV_PROJ)[:, -1:]
    y = _tail(x2d, a_out, r_out, mk, mv, lw, batch, seq)
    kv_shape = (batch, WINDOW, KV_HEADS, HEAD_DIM)
    return y.reshape(x.shape), k_buf.reshape(kv_shape), v_buf.reshape(kv_shape), shift_new, state


def kernel(x_prompt, x_sample, mem_prompt, cache_attn_k, cache_attn_v, cache_mem_k, cache_mem_v, state_shift,
           state_wkv, norm_mix_g, w_in, attn_sink, rel_bias_table, rwkv_mu, rwkv_w0, rwkv_w2, rwkv_a0, rwkv_a2,
           rwkv_g2, rwkv_k_k, rwkv_k_a, rwkv_r_k, rwkv_ln_w, rwkv_ln_b, w_out, norm_cross_g, norm_mem_g, w_cq,
           w_mk, w_mv, w_co, norm_mlp_g, w_up, w_down, norm_final_g):
    assert norm_mix_g.shape[0] == 1, "single-layer trunk"
    bp, dec_b = x_prompt.shape[0], x_sample.shape[0]
    vec = lambda p: p[0].reshape(1, -1)
    lw = {
        'norm_mix_g': vec(norm_mix_g), 'w_in': w_in[0].astype(BF16), 'attn_sink': attn_sink[0],
        'rwkv_mu': rwkv_mu[0], 'rwkv_w0': rwkv_w0[0], 'rwkv_w2': rwkv_w2[0], 'rwkv_a0': rwkv_a0[0],
        'rwkv_a2': rwkv_a2[0], 'rwkv_g2': rwkv_g2[0], 'rwkv_k_k': rwkv_k_k[0], 'rwkv_k_a': rwkv_k_a[0],
        'rwkv_r_k': rwkv_r_k[0], 'rwkv_ln_w': rwkv_ln_w[0], 'rwkv_ln_b': rwkv_ln_b[0],
        'w_out': w_out[0].astype(BF16), 'norm_cross_g': vec(norm_cross_g), 'w_cq': w_cq[0].astype(BF16),
        'w_co': w_co[0].astype(BF16), 'norm_mlp_g': vec(norm_mlp_g), 'w_up': w_up[0].astype(BF16),
        'w_down': w_down[0].astype(BF16), 'norm_final_g': norm_final_g.reshape(1, -1),
    }
    mk, mv = _memory_kv(mem_prompt.reshape(bp * N_MEM, D_MODEL), vec(norm_mem_g),
                        w_mk[0].astype(BF16), w_mv[0].astype(BF16))
    mk = mk.reshape(bp, N_MEM, MEM_WIDTH)
    mv = mv.reshape(bp, N_MEM, MEM_WIDTH)
    shift0 = jnp.zeros((bp, 1, RWKV_PROJ), F32)
    wkv0 = jnp.zeros((bp, RWKV_HEADS, HEAD_DIM, HEAD_DIM), F32)
    yp, pk, pv, psh, pS = _trunk(x_prompt, mk, mv, None, None, shift0, wkv0, lw, rel_bias_table, RWKV_SUB_CHUNKS)
    ys, sk, sv, ssh, sS = _trunk(
        x_sample, cache_mem_k[0].reshape(dec_b, N_MEM, MEM_WIDTH), cache_mem_v[0].reshape(dec_b, N_MEM, MEM_WIDTH),
        cache_attn_k[0].reshape(dec_b, WINDOW, KV_WIDTH), cache_attn_v[0].reshape(dec_b, WINDOW, KV_WIDTH),
        state_shift[0], state_wkv[0], lw, rel_bias_table, 1)
    mem_shape = (1, bp, N_MEM, MEM_HEADS, MEM_HEAD_DIM)
    return (yp, ys, pk[None], pv[None], mk.reshape(mem_shape), mv.reshape(mem_shape), psh[None], pS[None],
            sk[None], sv[None], ssh[None], sS[None])
```

```python
import functools
import math

import numpy as np
import jax
import jax.numpy as jnp
from jax import lax
from jax.experimental import pallas as pl
from jax.experimental.pallas import tpu as pltpu

F32 = jnp.float32
BF16 = jnp.bfloat16

D_MODEL = 1024
CHUNK = 64
WINDOW = 128
HEAD_DIM = 64
ATTN_WIDTH = 512
ATTN_HEADS = 8
KV_HEADS = 2
GROUP = 4
KV_WIDTH = 128
RWKV_WIDTH = 512
RWKV_HEADS = 8
DECAY_LORA = 64
AAA_LORA = 64
GATE_LORA = 128
RWKV_PROJ = 1792
IN_PROJ = 2560
N_MEM = 256
MEM_HEADS = 4
MEM_HEAD_DIM = 128
MEM_WIDTH = 512
D_FF = 4096
REL_BUCKETS = 32
REL_MAX_DIST = 128
NORM_EPS = 1e-6
GN_EPS = 64e-5

V7X_VMEM_LIMIT_BYTES = 52 * 1024 * 1024
ROW_TILE = 256
IN_PROJ_ROW_TILE = 512
TAIL_ROW_TILE = 512
ATTN_CHUNKS_PER_STEP = 4
RWKV_SUB_CHUNKS = 8
RWKV_LOCAL_IN_FLIGHT = 3


def _params(*sem):
    return pltpu.CompilerParams(dimension_semantics=sem, vmem_limit_bytes=V7X_VMEM_LIMIT_BYTES)


def _const_spec(shape):
    nd = len(shape)
    return pl.BlockSpec(shape, lambda *_: (0,) * nd)


_NN = ((1,), (0,))
_NT = ((1,), (1,))
_TN = ((0,), (0,))


def _dg(a, b, dims):
    return lax.dot_general(a, b, (dims, ((), ())), preferred_element_type=F32)


def _dot(a, b):
    return _dg(a.astype(BF16), b.astype(BF16), _NN)


def _dot_nt(a, b):
    return _dg(a.astype(BF16), b.astype(BF16), _NT)


def _cat_rows(*xs):
    return jnp.concatenate(xs, axis=0)


def _cat_lanes(*xs):
    return jnp.concatenate(xs, axis=1)


def _run_interleaved(chains):
    active = list(chains)
    while active:
        still = []
        for ch in active:
            try:
                next(ch)
                still.append(ch)
            except StopIteration:
                pass
        active = still


def _run_tasks(tasks):
    finished = set()
    running = {}
    waiting = dict(tasks)
    while waiting or running:
        for name in [n for n, (_, deps) in waiting.items() if all(d in finished or d not in tasks for d in deps)]:
            running[name] = list(waiting.pop(name)[0]())
        for name in list(running):
            alive = []
            for ch in running[name]:
                try:
                    next(ch)
                    alive.append(ch)
                except StopIteration:
                    pass
            if alive:
                running[name] = alive
            else:
                del running[name]
                finished.add(name)


def _rms(x, g):
    return x * lax.rsqrt(jnp.mean(x * x, axis=-1, keepdims=True) + NORM_EPS) * g


def _inproj_kernel(x_ref, g_ref, w_ref, q_ref, k_ref, v_ref, zr_ref):
    h = _rms(x_ref[...], g_ref[...]).astype(BF16)
    q_ref[...] = jnp.dot(h, w_ref[:, :ATTN_WIDTH], preferred_element_type=F32)
    k_ref[...] = jnp.dot(h, w_ref[:, ATTN_WIDTH:ATTN_WIDTH + KV_WIDTH], preferred_element_type=F32)
    v_ref[...] = jnp.dot(h, w_ref[:, ATTN_WIDTH + KV_WIDTH:ATTN_WIDTH + 2 * KV_WIDTH],
                         preferred_element_type=F32)
    zr_ref[...] = jnp.dot(h, w_ref[:, ATTN_WIDTH + 2 * KV_WIDTH:], preferred_element_type=F32)


def _in_proj(x2d, g, w_bf16):
    n = x2d.shape[0]
    tm = min(IN_PROJ_ROW_TILE, n)
    row = lambda w: pl.BlockSpec((tm, w), lambda i: (i, 0))
    return pl.pallas_call(
        _inproj_kernel,
        grid=(n // tm,),
        in_specs=[row(D_MODEL), _const_spec((1, D_MODEL)), _const_spec((D_MODEL, IN_PROJ))],
        out_specs=[row(ATTN_WIDTH), row(KV_WIDTH), row(KV_WIDTH), row(RWKV_PROJ)],
        out_shape=[jax.ShapeDtypeStruct((n, w), F32) for w in (ATTN_WIDTH, KV_WIDTH, KV_WIDTH, RWKV_PROJ)],
        compiler_params=_params("parallel"),
        name="in_proj",
    )(x2d, g, w_bf16)


def _t5_bucket(rel):
    half = REL_BUCKETS // 2
    max_exact = half // 2
    assert REL_MAX_DIST == max_exact * 2 ** 4 and half - max_exact == 2 * 4
    n = np.abs(rel)
    large = max_exact + sum((n * n >= max_exact * max_exact * 2 ** t).astype(np.int64)
                            for t in range(1, half - max_exact))
    return (np.where(rel > 0, half, 0) + np.where(n < max_exact, n, large)).astype(np.int32)


def _bias_kernel(table_ref, bucket_ref, out_ref):
    bucket = bucket_ref[...]
    hits = [bucket == b for b in range(REL_BUCKETS)]
    for h in range(ATTN_HEADS):
        acc = jnp.zeros(bucket.shape, F32)
        for b in range(REL_BUCKETS):
            acc = jnp.where(hits[b], table_ref[b, h], acc)
        out_ref[h] = acc


def _rel_bias(table, n_q, n_k):
    rel = np.arange(n_k)[None, :] - WINDOW - np.arange(n_q)[:, None]
    bucket = jnp.asarray(_t5_bucket(rel))
    bias = pl.pallas_call(
        _bias_kernel,
        in_specs=[pl.BlockSpec(memory_space=pltpu.SMEM), pl.BlockSpec(memory_space=pltpu.VMEM)],
        out_specs=pl.BlockSpec(memory_space=pltpu.VMEM),
        out_shape=jax.ShapeDtypeStruct((ATTN_HEADS, n_q, n_k), F32),
        name="rel_bias",
    )(table, bucket)
    return bias.reshape(KV_HEADS, GROUP * n_q, n_k)


def _group_sinks(sink_ref, n_q):
    row_group = lax.broadcasted_iota(jnp.int32, (GROUP * n_q, 1), 0) // n_q
    sinks = []
    for kvh in range(KV_HEADS):
        sink = jnp.zeros((GROUP * n_q, 1), F32)
        for g in range(GROUP):
            sink = jnp.where(row_group == g, sink_ref[kvh * GROUP + g], sink)
        sinks.append(sink)
    return sinks


def _attn_chain(q, keys, vals, bias, sink, valid, o_ref, rows, kvh):
    n_q = q.shape[0]
    qh = _cat_rows(*[q[:, (kvh * GROUP + g) * HEAD_DIM:(kvh * GROUP + g + 1) * HEAD_DIM]
                     for g in range(GROUP)]).astype(BF16)
    s = _dg(qh, keys, _NT) * (HEAD_DIM ** -0.5) + bias
    if valid is not None:
        s = jnp.where(valid, s, -jnp.inf)
    yield
    m = jnp.maximum(jnp.max(s, axis=-1, keepdims=True), sink)
    p = jnp.exp(s - m)
    den = jnp.sum(p, axis=-1, keepdims=True) + jnp.exp(sink - m)
    yield
    o = _dg(p.astype(BF16), vals, _NN) * (1.0 / den)
    for g in range(GROUP):
        head = kvh * GROUP + g
        o_ref[rows, head * HEAD_DIM:(head + 1) * HEAD_DIM] = o[g * n_q:(g + 1) * n_q]
    yield


def _prompt_attn_kernel(sink_ref, q_ref, kp_ref, kc_ref, vp_ref, vc_ref, bias_ref, o_ref):
    n_chunks = q_ref.shape[0] // CHUNK
    n_k = WINDOW + CHUNK
    k_all = _cat_rows(kp_ref[...], kc_ref[...]).astype(BF16)
    v_all = _cat_rows(vp_ref[...], vc_ref[...]).astype(BF16)
    first_valid = jnp.where(pl.program_id(1) == 0, WINDOW, 0)
    kcol = lax.broadcasted_iota(jnp.int32, (1, n_k), 1)
    sinks = _group_sinks(sink_ref, CHUNK)
    chains = []
    for jj in range(n_chunks):
        rows = slice(jj * CHUNK, (jj + 1) * CHUNK)
        keys = slice(jj * CHUNK, jj * CHUNK + n_k)
        valid = kcol + jj * CHUNK >= first_valid
        for kvh in range(KV_HEADS):
            lanes = slice(kvh * HEAD_DIM, (kvh + 1) * HEAD_DIM)
            chains.append(_attn_chain(q_ref[rows, :], k_all[keys, lanes], v_all[keys, lanes], bias_ref[kvh],
                                      sinks[kvh], valid, o_ref, rows, kvh))
    _run_interleaved(chains)


def _prompt_attention(q, k, v, sink, bias, batch, seq):
    rows = ATTN_CHUNKS_PER_STEP * CHUNK
    steps = seq // rows
    per_window = rows // WINDOW
    q_spec = pl.BlockSpec((rows, ATTN_WIDTH), lambda b, i: (b * steps + i, 0))
    cur_spec = pl.BlockSpec((rows, KV_WIDTH), lambda b, i: (b * steps + i, 0))
    prev_spec = pl.BlockSpec((WINDOW, KV_WIDTH),
                             lambda b, i: (jnp.maximum((b * steps + i) * per_window - 1, 0), 0))
    return pl.pallas_call(
        _prompt_attn_kernel,
        grid=(batch, steps),
        in_specs=[pl.BlockSpec(memory_space=pltpu.SMEM), q_spec, prev_spec, cur_spec, prev_spec, cur_spec,
                  _const_spec(bias.shape)],
        out_specs=q_spec,
        out_shape=jax.ShapeDtypeStruct(q.shape, F32),
        compiler_params=_params("parallel", "parallel"),
        name="prompt_attention",
    )(sink, q, k, k, v, v, bias)


def _sample_attn_kernel(seq, sink_ref, q_ref, kp_ref, kn_ref, vp_ref, vn_ref, bias_ref, o_ref):
    batch = q_ref.shape[0] // seq
    sinks = _group_sinks(sink_ref, seq)
    chains = []
    for b in range(batch):
        rows = slice(b * seq, (b + 1) * seq)
        past = slice(b * WINDOW, (b + 1) * WINDOW)
        k_all = _cat_rows(kp_ref[past, :], kn_ref[rows, :]).astype(BF16)
        v_all = _cat_rows(vp_ref[past, :], vn_ref[rows, :]).astype(BF16)
        for kvh in range(KV_HEADS):
            lanes = slice(kvh * HEAD_DIM, (kvh + 1) * HEAD_DIM)
            chains.append(_attn_chain(q_ref[rows, :], k_all[:, lanes], v_all[:, lanes], bias_ref[kvh], sinks[kvh],
                                      None, o_ref, rows, kvh))
    _run_interleaved(chains)


def _sample_attention(q, k, v, k_past, v_past, sink, bias, seq):
    vmem = pl.BlockSpec(memory_space=pltpu.VMEM)
    return pl.pallas_call(
        functools.partial(_sample_attn_kernel, seq),
        in_specs=[pl.BlockSpec(memory_space=pltpu.SMEM)] + [vmem] * 6,
        out_specs=vmem,
        out_shape=jax.ShapeDtypeStruct(q.shape, F32),
        name="sample_attention",
    )(sink, q, k_past, k, v_past, v, bias)


def _split2(x):
    hi = x.astype(BF16)
    lo = (x - hi.astype(F32)).astype(BF16)
    return hi, lo


def _softplus(x):
    return jnp.maximum(x, 0.0) + jnp.log(1.0 + jnp.exp(-jnp.abs(x)))


def _sigmoid(x):
    return 1.0 / (1.0 + jnp.exp(-x))


def _rwkv_kernel(valid_rows, zr_ref, shift_ref, s0_ref, mu_ref, w0_ref, w2_ref, a0_ref, a2_ref, g2_ref, kk_ref,
                 ka_ref, rk_ref, lnw_ref, lnb_ref, seg_ref, tri_ref,
                 out_ref, s_ref, carry_ref, y_ref, sbd_ref):
    c = pl.program_id(1)
    C = CHUNK
    R = zr_ref.shape[0]
    n_sub = R // C
    pairs = range(RWKV_HEADS // 2)
    PAIR = 2 * HEAD_DIM
    W = RWKV_WIDTH

    @pl.when(c == 0)
    def _():
        carry_ref[0:1, :] = shift_ref[0]
        zero = jnp.zeros((HEAD_DIM, HEAD_DIM), F32)
        for p in pairs:
            sbd_ref[p] = _cat_rows(_cat_lanes(s0_ref[0, 2 * p], zero), _cat_lanes(zero, s0_ref[0, 2 * p + 1]))

    seg = seg_ref[...]
    seg2 = _cat_rows(seg, seg)

    def head_sum(x):
        hi, lo = _split2(x)
        tiles = [_dg(_cat_lanes(hi[:, t * PAIR:(t + 1) * PAIR], lo[:, t * PAIR:(t + 1) * PAIR]), seg2, _NN)
                 for t in range(W // PAIR)]
        return _cat_lanes(*tiles)

    first_row = lax.broadcasted_iota(jnp.int32, (C, 1), 0) == 0
    tri3 = tri_ref[...]

    lane = lax.broadcasted_iota(jnp.int32, (C, PAIR), 1)
    trow = lax.broadcasted_iota(jnp.int32, (C, PAIR), 0)
    even = lane < HEAD_DIM
    tcol = jnp.where(even, lane, lane - HEAD_DIM)
    strict = tcol < trow
    incl = tcol <= trow
    eye = jnp.where(tcol == trow, 1.0, 0.0).astype(F32)
    brow = lax.broadcasted_iota(jnp.int32, (PAIR, PAIR), 0) < HEAD_DIM
    bcol = lax.broadcasted_iota(jnp.int32, (PAIR, PAIR), 1) < HEAD_DIM
    on_diag = brow == bcol

    def bd(x):
        zero = jnp.zeros_like(x)
        return _cat_rows(jnp.where(even, x, zero), jnp.where(even, zero, x))

    def bd2(pair):
        return bd(pair[0]), bd(pair[1])

    def mm(a_pair, w_pair, dims=_NN):
        if dims == _NN:
            first = _dg(_cat_lanes(a_pair[0], a_pair[1]), _cat_rows(w_pair[0], w_pair[0]), dims)
        else:
            first = _dg(_cat_lanes(a_pair[0], a_pair[1]), _cat_lanes(w_pair[0], w_pair[0]), dims)
        return first + _dg(a_pair[0], w_pair[1], dims)

    prepped = {}
    ready = {}
    state = [sbd_ref[p] for p in pairs]

    def prep_chain(j):
        rows = slice(j * C, (j + 1) * C)
        zr = zr_ref[rows, :]
        before = carry_ref[0:1, :] if j == 0 else zr_ref[j * C - 1:j * C, :]
        z_prev = jnp.where(first_row, before, pltpu.roll(zr, 1, axis=0))
        zs = zr + (z_prev - zr) * mu_ref[...]
        r = zs[:, :W]
        k = zs[:, W:2 * W]
        v = zs[:, 2 * W:3 * W]
        wd = zs[:, 3 * W:3 * W + DECAY_LORA]
        ad = zs[:, 3 * W + DECAY_LORA:3 * W + DECAY_LORA + AAA_LORA]
        gd = zs[:, 3 * W + DECAY_LORA + AAA_LORA:]
        w_log = -_softplus(-(w0_ref[...] + _dot(jnp.tanh(wd), w2_ref[...]))) - 0.5
        lw = -jnp.exp(w_log)
        a = _sigmoid(a0_ref[...] + _dot(ad, a2_ref[...]))
        gate = _dot(_sigmoid(gd), g2_ref[...])
        kk = k * kk_ref[...]
        kk = kk * lax.rsqrt(jnp.maximum(head_sum(kk * kk), 1e-24))
        k2 = k * (1.0 + (a - 1.0) * ka_ref[...])
        if (j + 1) * C > valid_rows:
            live = lax.broadcasted_iota(jnp.int32, (C, 1), 0) < valid_rows - j * C
            lw = jnp.where(live, lw, 0.0)
            kk = jnp.where(live, kk, 0.0)
            k2 = jnp.where(live, k2, 0.0)
        bvec = kk * a
        yield
        l1 = lw.astype(BF16)
        rem = lw - l1.astype(F32)
        l2 = rem.astype(BF16)
        l3 = (rem - l2.astype(F32)).astype(BF16)
        sums = _dg(tri3, _cat_rows(l1, l2, l3), _NN)
        li = sums[:C]
        lrev = sums[C:]
        yield
        inv_p = jnp.exp(-li)
        to_end = jnp.exp(lrev)
        prepped[j] = dict(
            at=_split2(-kk * jnp.exp(li - lw)), rt=_split2(r * jnp.exp(li)), bt=_split2(bvec * inv_p),
            kt=_split2(k2 * inv_p), bh=_split2(bvec * to_end), kh=_split2(k2 * to_end), v=_split2(v),
            p_end=jnp.exp(li[C - 1:C, :]), bonus=head_sum(r * k2 * rk_ref[...]) * v, gate=gate)
        yield

    def local_chain(j, p):
        d = prepped[j]
        lanes = slice(p * PAIR, (p + 1) * PAIR)
        cut = lambda pair: (pair[0][:, lanes], pair[1][:, lanes])
        at_p, rt_p, bt_p, kt_p, bh_p, kh_p, v_p = map(cut, (d['at'], d['rt'], d['bt'], d['kt'], d['bh'], d['kh'],
                                                            d['v']))
        left = (_cat_rows(at_p[0], rt_p[0]), _cat_rows(at_p[1], rt_p[1]))
        right = (_cat_rows(bd(bt_p[0]), bd(kt_p[0])), _cat_rows(bd(bt_p[1]), bd(kt_p[1])))
        aa = mm(left, right, _NT)
        yield
        a_ab = jnp.where(strict, aa[:C, :PAIR], 0.0)
        a_ak = jnp.where(strict, aa[:C, PAIR:], 0.0)
        a_rb = jnp.where(incl, aa[C:, :PAIR], 0.0)
        a_rk = jnp.where(incl, aa[C:, PAIR:], 0.0)
        inv = eye + a_ab
        ps = _split2(a_ab)
        power = mm(ps, bd2(ps))
        span = 2
        yield
        while span < C:
            ps = _split2(power)
            pw = bd2(ps)
            ih = _split2(inv)
            if span * 2 < C:
                both = mm((_cat_rows(ih[0], ps[0]), _cat_rows(ih[1], ps[1])), pw)
                inv = inv + both[:C]
                power = both[C:]
            else:
                inv = inv + mm(ih, pw)
            span *= 2
            yield
        ready[(j, p)] = dict(inv=_split2(inv), akrk=_split2(_cat_rows(a_ak, a_rk)), rb=_split2(a_rb), left=left,
                             bhkh=(_cat_rows(bh_p[0], kh_p[0]), _cat_rows(bh_p[1], kh_p[1])), v=v_p,
                             p_end=d['p_end'][:, lanes])

    def state_chain(j, p):
        d = ready.pop((j, p))
        s_prev = state[p]
        v_hi, v_lo = d['v']
        both = mm(d['left'], _split2(s_prev), _NT) + mm(d['akrk'], (bd(v_hi), bd(v_lo)))
        rhs = both[:C]
        y0 = both[C:]
        yield
        u_pair = _split2(mm(d['inv'], bd2(_split2(rhs))))
        yield
        y_ref[j * C:(j + 1) * C, p * PAIR:(p + 1) * PAIR] = y0 + mm(d['rb'], bd2(u_pair))
        t_hi = _cat_rows(u_pair[0], v_hi)
        t_lo = _cat_rows(u_pair[1], v_lo)
        w_hi, w_lo = d['bhkh']
        upd = _dg(_cat_rows(t_hi, t_lo), _cat_rows(w_hi, w_hi), _TN) + _dg(t_hi, w_lo, _TN)
        state[p] = s_prev * d['p_end'] + jnp.where(on_diag, upd, 0.0)
        yield

    def post_chain(j):
        rows = slice(j * C, (j + 1) * C)
        d = prepped.pop(j)
        y = y_ref[rows, :]
        mean = head_sum(y) * (1.0 / HEAD_DIM)
        dev = y - mean
        yield
        var = head_sum(dev * dev) * (1.0 / HEAD_DIM)
        yn = dev * lax.rsqrt(var + GN_EPS) * lnw_ref[...] + lnb_ref[...]
        out_ref[rows, :] = (yn + d['bonus']) * d['gate']
        yield

    tasks = {}
    for j in range(n_sub):
        tasks[('prep', j)] = (lambda j=j: [prep_chain(j)], [('prep', j - 1), ('local', j - RWKV_LOCAL_IN_FLIGHT)])
        tasks[('local', j)] = (lambda j=j: [local_chain(j, p) for p in pairs],
                               [('prep', j), ('local', j - RWKV_LOCAL_IN_FLIGHT)])
        tasks[('state', j)] = (lambda j=j: [state_chain(j, p) for p in pairs], [('local', j), ('state', j - 1)])
        tasks[('post', j)] = (lambda j=j: [post_chain(j)], [('state', j)])
    _run_tasks(tasks)

    carry_ref[0:1, :] = zr_ref[R - 1:R, :]
    for p in pairs:
        sbd_ref[p] = state[p]

    @pl.when(c == pl.num_programs(1) - 1)
    def _():
        for p in pairs:
            s_ref[0, 2 * p] = state[p][:HEAD_DIM, :HEAD_DIM]
            s_ref[0, 2 * p + 1] = state[p][HEAD_DIM:, HEAD_DIM:]


def _rwkv_mixer(zr, shift_prev, state0, lw, batch, seq, n_sub):
    valid = seq
    if seq < CHUNK:
        zr = jnp.pad(zr.reshape(batch, seq, RWKV_PROJ), ((0, 0), (0, CHUNK - seq), (0, 0))).reshape(-1, RWKV_PROJ)
        seq, n_sub = CHUNK, 1
    rows = CHUNK * n_sub
    steps = seq // rows
    seg = jnp.asarray(np.kron(np.eye(2), np.ones((HEAD_DIM, HEAD_DIM))), BF16)
    ones = np.ones((CHUNK, CHUNK))
    tri3 = jnp.asarray(np.concatenate([np.tile(np.tril(ones), (1, 3)), np.tile(np.triu(ones, 1), (1, 3))]), BF16)
    row = lambda name: lw[name].reshape(1, -1)
    params = [row('rwkv_mu'), row('rwkv_w0'), lw['rwkv_w2'].astype(BF16), row('rwkv_a0'),
              lw['rwkv_a2'].astype(BF16), lw['rwkv_g2'].astype(BF16), row('rwkv_k_k'), row('rwkv_k_a'),
              row('rwkv_r_k'), row('rwkv_ln_w'), row('rwkv_ln_b'), seg, tri3]
    state_spec = pl.BlockSpec((1, RWKV_HEADS, HEAD_DIM, HEAD_DIM), lambda b, c: (b, 0, 0, 0))
    out, state = pl.pallas_call(
        functools.partial(_rwkv_kernel, min(valid, rows)),
        grid=(batch, steps),
        in_specs=[pl.BlockSpec((rows, RWKV_PROJ), lambda b, c: (b * steps + c, 0)),
                  pl.BlockSpec((1, 1, RWKV_PROJ), lambda b, c: (b, 0, 0)),
                  state_spec] + [_const_spec(p.shape) for p in params],
        out_specs=[pl.BlockSpec((rows, RWKV_WIDTH), lambda b, c: (b * steps + c, 0)), state_spec],
        out_shape=[jax.ShapeDtypeStruct((batch * seq, RWKV_WIDTH), F32),
                   jax.ShapeDtypeStruct(state0.shape, F32)],
        scratch_shapes=[pltpu.VMEM((8, RWKV_PROJ), F32), pltpu.VMEM((rows, RWKV_WIDTH), F32),
                        pltpu.VMEM((RWKV_HEADS // 2, 2 * HEAD_DIM, 2 * HEAD_DIM), F32)],
        compiler_params=_params("parallel", "arbitrary"),
        name="rwkv_mixer",
    )(zr, shift_prev, state0, *params)
    if valid < seq:
        out = out.reshape(batch, seq, RWKV_WIDTH)[:, :valid].reshape(batch * valid, RWKV_WIDTH)
    return out, state


def _memkv_kernel(m_ref, g_ref, wk_ref, wv_ref, k_ref, v_ref):
    mn = _rms(m_ref[...], g_ref[...]).astype(BF16)
    k_ref[...] = jnp.dot(mn, wk_ref[...], preferred_element_type=F32)
    v_ref[...] = jnp.dot(mn, wv_ref[...], preferred_element_type=F32)


def _memory_kv(mem2d, g, w_mk, w_mv):
    n = mem2d.shape[0]
    tm = min(ROW_TILE, n)
    row = lambda w: pl.BlockSpec((tm, w), lambda i: (i, 0))
    return pl.pallas_call(
        _memkv_kernel,
        grid=(n // tm,),
        in_specs=[row(D_MODEL), _const_spec((1, D_MODEL)), _const_spec(w_mk.shape), _const_spec(w_mv.shape)],
        out_specs=[row(MEM_WIDTH), row(MEM_WIDTH)],
        out_shape=[jax.ShapeDtypeStruct((n, MEM_WIDTH), F32)] * 2,
        compiler_params=_params("parallel"),
        name="memory_kv",
    )(mem2d, g, w_mk, w_mv)


def _tail_kernel(x_ref, a_ref, r_ref, mk_ref, mv_ref, wo_ref, gc_ref, wq_ref, wco_ref, gm_ref, wu_ref, wd_ref, gf_ref,
                 y_ref):
    n_seq = mk_ref.shape[0]
    rows_per_seq = x_ref.shape[0] // n_seq
    x1 = x_ref[...] + _dot(a_ref[...], wo_ref[:ATTN_WIDTH, :]) + _dot(r_ref[...], wo_ref[ATTN_WIDTH:, :])
    q = _dot(_rms(x1, gc_ref[...]), wq_ref[...]).astype(BF16)
    outs = {}

    def cross_chain(b, h):
        rows = slice(b * rows_per_seq, (b + 1) * rows_per_seq)
        sl = slice(h * MEM_HEAD_DIM, (h + 1) * MEM_HEAD_DIM)
        s = _dg(q[rows, sl], mk_ref[b, :, sl].astype(BF16), _NT) * (MEM_HEAD_DIM ** -0.5)
        yield
        p = jnp.exp(s - jnp.max(s, axis=-1, keepdims=True))
        den = jnp.sum(p, axis=-1, keepdims=True)
        yield
        outs[(b, h)] = _dg(p.astype(BF16), mv_ref[b, :, sl].astype(BF16), _NN) * (1.0 / den)
        yield

    _run_interleaved([cross_chain(b, h) for b in range(n_seq) for h in range(MEM_HEADS)])
    o = _cat_rows(*[_cat_lanes(*[outs[(b, h)] for h in range(MEM_HEADS)]) for b in range(n_seq)])
    x2 = x1 + _dot(o, wco_ref[...])
    up = _dot(_rms(x2, gm_ref[...]), wu_ref[...])
    act = jnp.square(jnp.maximum(up, 0.0))
    y_ref[...] = _rms(x2 + _dot(act, wd_ref[...]), gf_ref[...])


def _tail(x2d, a_out, r_out, mk, mv, lw, batch, seq):
    n = batch * seq
    tq = min(TAIL_ROW_TILE, n)
    if seq >= tq:
        assert seq % tq == 0
        seq_per_tile, tiles_per_seq = 1, seq // tq
        mem_spec = pl.BlockSpec((1, N_MEM, MEM_WIDTH), lambda i: (i // tiles_per_seq, 0, 0))
    else:
        assert tq % seq == 0
        seq_per_tile = tq // seq
        mem_spec = pl.BlockSpec((seq_per_tile, N_MEM, MEM_WIDTH), lambda i: (i, 0, 0))
    row = lambda w: pl.BlockSpec((tq, w), lambda i: (i, 0))

    def resident(shape):
        nd = len(shape)
        return pl.BlockSpec(shape, lambda *_: (0,) * nd, pipeline_mode=pl.Buffered(1))

    weights = [lw['w_out'], lw['norm_cross_g'], lw['w_cq'], lw['w_co'], lw['norm_mlp_g'], lw['w_up'], lw['w_down'],
               lw['norm_final_g']]
    return pl.pallas_call(
        _tail_kernel,
        grid=(n // tq,),
        in_specs=[row(D_MODEL), row(ATTN_WIDTH), row(RWKV_WIDTH), mem_spec, mem_spec]
                 + [resident(w.shape) for w in weights],
        out_specs=row(D_MODEL),
        out_shape=jax.ShapeDtypeStruct(x2d.shape, F32),
        compiler_params=_params("parallel"),
        name="tail",
    )(x2d, a_out, r_out, mk, mv, *weights)


def _trunk(x, mk, mv, k_past, v_past, shift_prev, state0, lw, table, n_sub):
    batch, seq = x.shape[0], x.shape[1]
    x2d = x.reshape(batch * seq, D_MODEL)
    q, k, v, zr = _in_proj(x2d, lw['norm_mix_g'], lw['w_in'])
    if k_past is None:
        bias = _rel_bias(table, CHUNK, WINDOW + CHUNK)
        a_out = _prompt_attention(q, k, v, lw['attn_sink'], bias, batch, seq)
        k3 = k.reshape(batch, seq, KV_WIDTH)
        v3 = v.reshape(batch, seq, KV_WIDTH)
        k_buf, v_buf = k3[:, -WINDOW:], v3[:, -WINDOW:]
    else:
        bias = _rel_bias(table, seq, WINDOW + seq)
        a_out = _sample_attention(q, k, v, k_past.reshape(batch * WINDOW, KV_WIDTH),
                                  v_past.reshape(batch * WINDOW, KV_WIDTH), lw['attn_sink'], bias, seq)
        k_buf = jnp.concatenate([k_past, k.reshape(batch, seq, KV_WIDTH)], axis=1)[:, -WINDOW:]
        v_buf = jnp.concatenate([v_past, v.reshape(batch, seq, KV_WIDTH)], axis=1)[:, -WINDOW:]
    r_out, state = _rwkv_mixer(zr, shift_prev, state0, lw, batch, seq, n_sub)
    shift_new = zr.reshape(batch, seq, RWKV_PROJ)[:, -1:]
    y = _tail(x2d, a_out, r_out, mk, mv, lw, batch, seq)
    kv_shape = (batch, WINDOW, KV_HEADS, HEAD_DIM)
    return y.reshape(x.shape), k_buf.reshape(kv_shape), v_buf.reshape(kv_shape), shift_new, state


def kernel(x_prompt, x_sample, mem_prompt, cache_attn_k, cache_attn_v, cache_mem_k, cache_mem_v, state_shift,
           state_wkv, norm_mix_g, w_in, attn_sink, rel_bias_table, rwkv_mu, rwkv_w0, rwkv_w2, rwkv_a0, rwkv_a2,
           rwkv_g2, rwkv_k_k, rwkv_k_a, rwkv_r_k, rwkv_ln_w, rwkv_ln_b, w_out, norm_cross_g, norm_mem_g, w_cq,
           w_mk, w_mv, w_co, norm_mlp_g, w_up, w_down, norm_final_g):
    assert norm_mix_g.shape[0] == 1, "single-layer trunk"
    bp, dec_b = x_prompt.shape[0], x_sample.shape[0]
    vec = lambda p: p[0].reshape(1, -1)
    lw = {
        'norm_mix_g': vec(norm_mix_g), 'w_in': w_in[0].astype(BF16), 'attn_sink': attn_sink[0],
        'rwkv_mu': rwkv_mu[0], 'rwkv_w0': rwkv_w0[0], 'rwkv_w2': rwkv_w2[0], 'rwkv_a0': rwkv_a0[0],
        'rwkv_a2': rwkv_a2[0], 'rwkv_g2': rwkv_g2[0], 'rwkv_k_k': rwkv_k_k[0], 'rwkv_k_a': rwkv_k_a[0],
        'rwkv_r_k': rwkv_r_k[0], 'rwkv_ln_w': rwkv_ln_w[0], 'rwkv_ln_b': rwkv_ln_b[0],
        'w_out': w_out[0].astype(BF16), 'norm_cross_g': vec(norm_cross_g), 'w_cq': w_cq[0].astype(BF16),
        'w_co': w_co[0].astype(BF16), 'norm_mlp_g': vec(norm_mlp_g), 'w_up': w_up[0].astype(BF16),
        'w_down': w_down[0].astype(BF16), 'norm_final_g': norm_final_g.reshape(1, -1),
    }
    mk, mv = _memory_kv(mem_prompt.reshape(bp * N_MEM, D_MODEL), vec(norm_mem_g),
                        w_mk[0].astype(BF16), w_mv[0].astype(BF16))
    mk = mk.reshape(bp, N_MEM, MEM_WIDTH)
    mv = mv.reshape(bp, N_MEM, MEM_WIDTH)
    shift0 = jnp.zeros((bp, 1, RWKV_PROJ), F32)
    wkv0 = jnp.zeros((bp, RWKV_HEADS, HEAD_DIM, HEAD_DIM), F32)
    yp, pk, pv, psh, pS = _trunk(x_prompt, mk, mv, None, None, shift0, wkv0, lw, rel_bias_table, RWKV_SUB_CHUNKS)
    ys, sk, sv, ssh, sS = _trunk(
        x_sample, cache_mem_k[0].reshape(dec_b, N_MEM, MEM_WIDTH), cache_mem_v[0].reshape(dec_b, N_MEM, MEM_WIDTH),
        cache_attn_k[0].reshape(dec_b, WINDOW, KV_WIDTH), cache_attn_v[0].reshape(dec_b, WINDOW, KV_WIDTH),
        state_shift[0], state_wkv[0], lw, rel_bias_table, 1)
    mem_shape = (1, bp, N_MEM, MEM_HEADS, MEM_HEAD_DIM)
    return (yp, ys, pk[None], pv[None], mk.reshape(mem_shape), mv.reshape(mem_shape), psh[None], pS[None],
            sk[None], sv[None], ssh[None], sS[None])
```

```python
import functools
import math

import numpy as np
import jax
import jax.numpy as jnp
from jax import lax
from jax.experimental import pallas as pl
from jax.experimental.pallas import tpu as pltpu

F32 = jnp.float32
BF16 = jnp.bfloat16

D_MODEL = 1024
CHUNK = 64
WINDOW = 128
HEAD_DIM = 64
ATTN_WIDTH = 512
ATTN_HEADS = 8
KV_HEADS = 2
GROUP = 4
KV_WIDTH = 128
RWKV_WIDTH = 512
RWKV_HEADS = 8
DECAY_LORA = 64
AAA_LORA = 64
GATE_LORA = 128
RWKV_PROJ = 1792
IN_PROJ = 2560
N_MEM = 256
MEM_HEADS = 4
MEM_HEAD_DIM = 128
MEM_WIDTH = 512
D_FF = 4096
REL_BUCKETS = 32
REL_MAX_DIST = 128
NORM_EPS = 1e-6
GN_EPS = 64e-5
LOG2_E = math.log2(math.e)

V7X_VMEM_LIMIT_BYTES = 52 * 1024 * 1024
ROW_TILE = 256
IN_PROJ_ROW_TILE = 512
TAIL_ROW_TILE = 512
ATTN_CHUNKS_PER_STEP = 4
RWKV_SUB_CHUNKS = 8
RWKV_LOCAL_IN_FLIGHT = 3


def _params(*sem):
    return pltpu.CompilerParams(dimension_semantics=sem, vmem_limit_bytes=V7X_VMEM_LIMIT_BYTES)


def _const_spec(shape):
    nd = len(shape)
    return pl.BlockSpec(shape, lambda *_: (0,) * nd)


_NN = ((1,), (0,))
_NT = ((1,), (1,))
_TN = ((0,), (0,))


def _dg(a, b, dims):
    return lax.dot_general(a, b, (dims, ((), ())), preferred_element_type=F32)


def _dot(a, b):
    return _dg(a.astype(BF16), b.astype(BF16), _NN)


def _dot_nt(a, b):
    return _dg(a.astype(BF16), b.astype(BF16), _NT)


def _cat_rows(*xs):
    return jnp.concatenate(xs, axis=0)


def _cat_lanes(*xs):
    return jnp.concatenate(xs, axis=1)


def _run_interleaved(chains):
    active = list(chains)
    while active:
        still = []
        for ch in active:
            try:
                next(ch)
                still.append(ch)
            except StopIteration:
                pass
        active = still


def _run_tasks(tasks):
    finished = set()
    running = {}
    waiting = dict(tasks)
    while waiting or running:
        for name in [n for n, (_, deps) in waiting.items() if all(d in finished or d not in tasks for d in deps)]:
            running[name] = list(waiting.pop(name)[0]())
        for name in list(running):
            alive = []
            for ch in running[name]:
                try:
                    next(ch)
                    alive.append(ch)
                except StopIteration:
                    pass
            if alive:
                running[name] = alive
            else:
                del running[name]
                finished.add(name)


def _rms(x, g):
    return x * lax.rsqrt(jnp.mean(x * x, axis=-1, keepdims=True) + NORM_EPS) * g


def _inproj_kernel(x_ref, g_ref, w_ref, q_ref, k_ref, v_ref, zr_ref):
    h = _rms(x_ref[...], g_ref[...]).astype(BF16)
    q = jnp.dot(h, w_ref[:, :ATTN_WIDTH], preferred_element_type=F32)
    q_ref[...] = (q * (HEAD_DIM ** -0.5 * LOG2_E)).astype(BF16)
    k_ref[...] = jnp.dot(h, w_ref[:, ATTN_WIDTH:ATTN_WIDTH + KV_WIDTH], preferred_element_type=F32)
    v_ref[...] = jnp.dot(h, w_ref[:, ATTN_WIDTH + KV_WIDTH:ATTN_WIDTH + 2 * KV_WIDTH],
                         preferred_element_type=F32)
    zr_ref[...] = jnp.dot(h, w_ref[:, ATTN_WIDTH + 2 * KV_WIDTH:], preferred_element_type=F32)


def _in_proj(x2d, g, w_bf16):
    n = x2d.shape[0]
    tm = min(IN_PROJ_ROW_TILE, n)
    row = lambda w: pl.BlockSpec((tm, w), lambda i: (i, 0))
    return pl.pallas_call(
        _inproj_kernel,
        grid=(n // tm,),
        in_specs=[row(D_MODEL), _const_spec((1, D_MODEL)), _const_spec((D_MODEL, IN_PROJ))],
        out_specs=[row(ATTN_WIDTH), row(KV_WIDTH), row(KV_WIDTH), row(RWKV_PROJ)],
        out_shape=[jax.ShapeDtypeStruct((n, ATTN_WIDTH), BF16)]
                  + [jax.ShapeDtypeStruct((n, w), F32) for w in (KV_WIDTH, KV_WIDTH, RWKV_PROJ)],
        compiler_params=_params("parallel"),
        name="in_proj",
    )(x2d, g, w_bf16)


def _t5_bucket(rel):
    half = REL_BUCKETS // 2
    max_exact = half // 2
    assert REL_MAX_DIST == max_exact * 2 ** 4 and half - max_exact == 2 * 4
    n = np.abs(rel)
    large = max_exact + sum((n * n >= max_exact * max_exact * 2 ** t).astype(np.int64)
                            for t in range(1, half - max_exact))
    return (np.where(rel > 0, half, 0) + np.where(n < max_exact, n, large)).astype(np.int32)


def _bias_kernel(table_ref, bucket_ref, out_ref):
    bucket = bucket_ref[...]
    hits = [bucket == b for b in range(REL_BUCKETS)]
    for h in range(ATTN_HEADS):
        acc = jnp.zeros(bucket.shape, F32)
        for b in range(REL_BUCKETS):
            acc = jnp.where(hits[b], table_ref[b, h], acc)
        out_ref[h] = acc * LOG2_E


def _rel_bias(table, n_q, n_k):
    rel = np.arange(n_k)[None, :] - WINDOW - np.arange(n_q)[:, None]
    bucket = jnp.asarray(_t5_bucket(rel))
    bias = pl.pallas_call(
        _bias_kernel,
        in_specs=[pl.BlockSpec(memory_space=pltpu.SMEM), pl.BlockSpec(memory_space=pltpu.VMEM)],
        out_specs=pl.BlockSpec(memory_space=pltpu.VMEM),
        out_shape=jax.ShapeDtypeStruct((ATTN_HEADS, n_q, n_k), F32),
        name="rel_bias",
    )(table, bucket)
    return bias.reshape(KV_HEADS, GROUP * n_q, n_k)


def _group_sinks(sink_ref, n_q):
    row_group = lax.broadcasted_iota(jnp.int32, (GROUP * n_q, 1), 0) // n_q
    sinks = []
    for kvh in range(KV_HEADS):
        sink = jnp.zeros((GROUP * n_q, 1), F32)
        for g in range(GROUP):
            sink = jnp.where(row_group == g, sink_ref[kvh * GROUP + g] * LOG2_E, sink)
        sinks.append(sink)
    return sinks


def _attn_chain(q, keys, vals, bias, sink, valid, o_ref, rows, kvh):
    n_q = q.shape[0]
    qh = _cat_rows(*[q[:, (kvh * GROUP + g) * HEAD_DIM:(kvh * GROUP + g + 1) * HEAD_DIM]
                     for g in range(GROUP)])
    s = _dg(qh, keys, _NT) + bias
    if valid is not None:
        s = jnp.where(valid, s, -jnp.inf)
    yield
    m = jnp.maximum(jnp.max(s, axis=-1, keepdims=True), sink)
    p = jnp.exp2(s - m)
    den = jnp.sum(p, axis=-1, keepdims=True) + jnp.exp2(sink - m)
    yield
    o = _dg(p.astype(BF16), vals, _NN) * (1.0 / den)
    for g in range(GROUP):
        head = kvh * GROUP + g
        o_ref[rows, head * HEAD_DIM:(head + 1) * HEAD_DIM] = o[g * n_q:(g + 1) * n_q]
    yield


def _prompt_attn_kernel(sink_ref, q_ref, kp_ref, kc_ref, vp_ref, vc_ref, bias_ref, o_ref):
    n_chunks = q_ref.shape[0] // CHUNK
    n_k = WINDOW + CHUNK
    k_all = _cat_rows(kp_ref[...], kc_ref[...]).astype(BF16)
    v_all = _cat_rows(vp_ref[...], vc_ref[...]).astype(BF16)
    first_valid = jnp.where(pl.program_id(1) == 0, WINDOW, 0)
    kcol = lax.broadcasted_iota(jnp.int32, (1, n_k), 1)
    sinks = _group_sinks(sink_ref, CHUNK)
    chains = []
    for jj in range(n_chunks):
        rows = slice(jj * CHUNK, (jj + 1) * CHUNK)
        keys = slice(jj * CHUNK, jj * CHUNK + n_k)
        valid = kcol + jj * CHUNK >= first_valid if jj * CHUNK < WINDOW else None
        for kvh in range(KV_HEADS):
            lanes = slice(kvh * HEAD_DIM, (kvh + 1) * HEAD_DIM)
            chains.append(_attn_chain(q_ref[rows, :], k_all[keys, lanes], v_all[keys, lanes], bias_ref[kvh],
                                      sinks[kvh], valid, o_ref, rows, kvh))
    _run_interleaved(chains)


def _prompt_attention(q, k, v, sink, bias, batch, seq):
    rows = ATTN_CHUNKS_PER_STEP * CHUNK
    steps = seq // rows
    per_window = rows // WINDOW
    q_spec = pl.BlockSpec((rows, ATTN_WIDTH), lambda b, i: (b * steps + i, 0))
    cur_spec = pl.BlockSpec((rows, KV_WIDTH), lambda b, i: (b * steps + i, 0))
    prev_spec = pl.BlockSpec((WINDOW, KV_WIDTH),
                             lambda b, i: (jnp.maximum((b * steps + i) * per_window - 1, 0), 0))
    return pl.pallas_call(
        _prompt_attn_kernel,
        grid=(batch, steps),
        in_specs=[pl.BlockSpec(memory_space=pltpu.SMEM), q_spec, prev_spec, cur_spec, prev_spec, cur_spec,
                  _const_spec(bias.shape)],
        out_specs=q_spec,
        out_shape=jax.ShapeDtypeStruct(q.shape, F32),
        compiler_params=_params("parallel", "parallel"),
        name="prompt_attention",
    )(sink, q, k, k, v, v, bias)


def _sample_attn_kernel(seq, sink_ref, q_ref, kp_ref, kn_ref, vp_ref, vn_ref, bias_ref, o_ref):
    batch = q_ref.shape[0] // seq
    sinks = _group_sinks(sink_ref, seq)
    chains = []
    for b in range(batch):
        rows = slice(b * seq, (b + 1) * seq)
        past = slice(b * WINDOW, (b + 1) * WINDOW)
        k_all = _cat_rows(kp_ref[past, :], kn_ref[rows, :]).astype(BF16)
        v_all = _cat_rows(vp_ref[past, :], vn_ref[rows, :]).astype(BF16)
        for kvh in range(KV_HEADS):
            lanes = slice(kvh * HEAD_DIM, (kvh + 1) * HEAD_DIM)
            chains.append(_attn_chain(q_ref[rows, :], k_all[:, lanes], v_all[:, lanes], bias_ref[kvh], sinks[kvh],
                                      None, o_ref, rows, kvh))
    _run_interleaved(chains)


def _sample_attention(q, k, v, k_past, v_past, sink, bias, seq):
    vmem = pl.BlockSpec(memory_space=pltpu.VMEM)
    return pl.pallas_call(
        functools.partial(_sample_attn_kernel, seq),
        in_specs=[pl.BlockSpec(memory_space=pltpu.SMEM)] + [vmem] * 6,
        out_specs=vmem,
        out_shape=jax.ShapeDtypeStruct(q.shape, F32),
        name="sample_attention",
    )(sink, q, k_past, k, v_past, v, bias)


def _split2(x):
    hi = x.astype(BF16)
    lo = (x - hi.astype(F32)).astype(BF16)
    return hi, lo


def _softplus(x):
    return jnp.maximum(x, 0.0) + jnp.log(1.0 + jnp.exp(-jnp.abs(x)))


def _sigmoid(x):
    return 1.0 / (1.0 + jnp.exp(-x))


def _rwkv_kernel(valid_rows, zr_ref, shift_ref, s0_ref, mu_ref, w0_ref, w2_ref, a0_ref, a2_ref, g2_ref, kk_ref,
                 ka_ref, rk_ref, lnw_ref, lnb_ref, seg_ref, tri_ref,
                 out_ref, s_ref, carry_ref, y_ref, sbd_ref):
    c = pl.program_id(1)
    C = CHUNK
    R = zr_ref.shape[0]
    n_sub = R // C
    pairs = range(RWKV_HEADS // 2)
    PAIR = 2 * HEAD_DIM
    W = RWKV_WIDTH

    @pl.when(c == 0)
    def _():
        carry_ref[0:1, :] = shift_ref[0]
        zero = jnp.zeros((HEAD_DIM, HEAD_DIM), F32)
        for p in pairs:
            sbd_ref[p] = _cat_rows(_cat_lanes(s0_ref[0, 2 * p], zero), _cat_lanes(zero, s0_ref[0, 2 * p + 1]))

    seg = seg_ref[...]
    seg2 = _cat_rows(seg, seg)

    def head_sum(x):
        hi, lo = _split2(x)
        tiles = [_dg(_cat_lanes(hi[:, t * PAIR:(t + 1) * PAIR], lo[:, t * PAIR:(t + 1) * PAIR]), seg2, _NN)
                 for t in range(W // PAIR)]
        return _cat_lanes(*tiles)

    first_row = lax.broadcasted_iota(jnp.int32, (C, 1), 0) == 0
    tri3 = tri_ref[...]

    lane = lax.broadcasted_iota(jnp.int32, (C, PAIR), 1)
    trow = lax.broadcasted_iota(jnp.int32, (C, PAIR), 0)
    even = lane < HEAD_DIM
    tcol = jnp.where(even, lane, lane - HEAD_DIM)
    strict = tcol < trow
    incl = tcol <= trow
    eye = jnp.where(tcol == trow, 1.0, 0.0).astype(F32)
    brow = lax.broadcasted_iota(jnp.int32, (PAIR, PAIR), 0) < HEAD_DIM
    bcol = lax.broadcasted_iota(jnp.int32, (PAIR, PAIR), 1) < HEAD_DIM
    on_diag = brow == bcol

    def bd(x):
        zero = jnp.zeros_like(x)
        return _cat_rows(jnp.where(even, x, zero), jnp.where(even, zero, x))

    def bd2(pair):
        return bd(pair[0]), bd(pair[1])

    def mm(a_pair, w_pair, dims=_NN, exact_rows=None):
        if dims == _NN:
            first = _dg(_cat_lanes(a_pair[0], a_pair[1]), _cat_rows(w_pair[0], w_pair[0]), dims)
        else:
            first = _dg(_cat_lanes(a_pair[0], a_pair[1]), _cat_lanes(w_pair[0], w_pair[0]), dims)
        if exact_rows is None:
            return first + _dg(a_pair[0], w_pair[1], dims)
        return _cat_rows(first[:exact_rows] + _dg(a_pair[0][:exact_rows], w_pair[1], dims), first[exact_rows:])

    prepped = {}
    ready = {}
    state = [sbd_ref[p] for p in pairs]

    def prep_chain(j):
        rows = slice(j * C, (j + 1) * C)
        zr = zr_ref[rows, :]
        before = carry_ref[0:1, :] if j == 0 else zr_ref[j * C - 1:j * C, :]
        z_prev = jnp.where(first_row, before, pltpu.roll(zr, 1, axis=0))
        zs = zr + (z_prev - zr) * mu_ref[...]
        r = zs[:, :W]
        k = zs[:, W:2 * W]
        v = zs[:, 2 * W:3 * W]
        wd = zs[:, 3 * W:3 * W + DECAY_LORA]
        ad = zs[:, 3 * W + DECAY_LORA:3 * W + DECAY_LORA + AAA_LORA]
        gd = zs[:, 3 * W + DECAY_LORA + AAA_LORA:]
        w_log = -_softplus(-(w0_ref[...] + _dot(jnp.tanh(wd), w2_ref[...]))) - 0.5
        lw = -jnp.exp(w_log)
        a = _sigmoid(a0_ref[...] + _dot(ad, a2_ref[...]))
        gate = _dot(_sigmoid(gd), g2_ref[...])
        kk = k * kk_ref[...]
        kk = kk * lax.rsqrt(jnp.maximum(head_sum(kk * kk), 1e-24))
        k2 = k * (1.0 + (a - 1.0) * ka_ref[...])
        if (j + 1) * C > valid_rows:
            live = lax.broadcasted_iota(jnp.int32, (C, 1), 0) < valid_rows - j * C
            lw = jnp.where(live, lw, 0.0)
            kk = jnp.where(live, kk, 0.0)
            k2 = jnp.where(live, k2, 0.0)
        bvec = kk * a
        yield
        l1 = lw.astype(BF16)
        rem = lw - l1.astype(F32)
        l2 = rem.astype(BF16)
        l3 = (rem - l2.astype(F32)).astype(BF16)
        sums = _dg(tri3, _cat_rows(l1, l2, l3), _NN)
        li = sums[:C]
        lrev = sums[C:]
        yield
        inv_p = jnp.exp(-li)
        to_end = jnp.exp(lrev)
        prepped[j] = dict(
            at=_split2(-kk * jnp.exp(li - lw)), rt=_split2(r * jnp.exp(li)), bt=_split2(bvec * inv_p),
            kt=_split2(k2 * inv_p), bh=_split2(bvec * to_end), kh=_split2(k2 * to_end), v=_split2(v),
            p_end=jnp.exp(li[C - 1:C, :]), bonus=head_sum(r * k2 * rk_ref[...]) * v, gate=gate)
        yield

    def local_chain(j, p):
        d = prepped[j]
        lanes = slice(p * PAIR, (p + 1) * PAIR)
        cut = lambda pair: (pair[0][:, lanes], pair[1][:, lanes])
        at_p, rt_p, bt_p, kt_p, bh_p, kh_p, v_p = map(cut, (d['at'], d['rt'], d['bt'], d['kt'], d['bh'], d['kh'],
                                                            d['v']))
        left = (_cat_rows(at_p[0], rt_p[0]), _cat_rows(at_p[1], rt_p[1]))
        right = (_cat_rows(bd(bt_p[0]), bd(kt_p[0])), _cat_rows(bd(bt_p[1]), bd(kt_p[1])))
        aa = mm(left, right, _NT, exact_rows=C)
        yield
        a_ab = jnp.where(strict, aa[:C, :PAIR], 0.0)
        a_ak = jnp.where(strict, aa[:C, PAIR:], 0.0)
        a_rb = jnp.where(incl, aa[C:, :PAIR], 0.0)
        a_rk = jnp.where(incl, aa[C:, PAIR:], 0.0)
        inv = eye + a_ab
        ps = _split2(a_ab)
        power = mm(ps, bd2(ps))
        span = 2
        yield
        while span < C:
            ps = _split2(power)
            pw = bd2(ps)
            ih = _split2(inv)
            if span * 2 < C:
                both = mm((_cat_rows(ih[0], ps[0]), _cat_rows(ih[1], ps[1])), pw)
                inv = inv + both[:C]
                power = both[C:]
            else:
                inv = inv + mm(ih, pw)
            span *= 2
            yield
        ready[(j, p)] = dict(inv=_split2(inv), akrk=_split2(_cat_rows(a_ak, a_rk)), rb=a_rb.astype(BF16), left=left,
                             bhkh=(_cat_rows(bh_p[0], kh_p[0]), _cat_rows(bh_p[1], kh_p[1])), v=v_p,
                             p_end=d['p_end'][:, lanes])

    def state_chain(j, p):
        d = ready.pop((j, p))
        s_prev = state[p]
        v_hi, v_lo = d['v']
        both = (mm(d['left'], _split2(s_prev), _NT, exact_rows=C)
                + mm(d['akrk'], (bd(v_hi), bd(v_lo)), exact_rows=C))
        rhs = both[:C]
        y0 = both[C:]
        yield
        u_pair = _split2(mm(d['inv'], bd2(_split2(rhs))))
        yield
        y_ref[j * C:(j + 1) * C, p * PAIR:(p + 1) * PAIR] = y0 + _dg(d['rb'], bd(u_pair[0]), _NN)
        t_hi = _cat_rows(u_pair[0], v_hi)
        t_lo = _cat_rows(u_pair[1], v_lo)
        w_hi, w_lo = d['bhkh']
        upd = _dg(_cat_rows(t_hi, t_lo), _cat_rows(w_hi, w_hi), _TN) + _dg(t_hi, w_lo, _TN)
        state[p] = s_prev * d['p_end'] + jnp.where(on_diag, upd, 0.0)
        yield

    def post_chain(j):
        rows = slice(j * C, (j + 1) * C)
        d = prepped.pop(j)
        y = y_ref[rows, :]
        mean = head_sum(y) * (1.0 / HEAD_DIM)
        dev = y - mean
        yield
        var = head_sum(dev * dev) * (1.0 / HEAD_DIM)
        yn = dev * lax.rsqrt(var + GN_EPS) * lnw_ref[...] + lnb_ref[...]
        out_ref[rows, :] = (yn + d['bonus']) * d['gate']
        yield

    tasks = {}
    for j in range(n_sub):
        tasks[('prep', j)] = (lambda j=j: [prep_chain(j)], [('prep', j - 1), ('local', j - RWKV_LOCAL_IN_FLIGHT)])
        tasks[('local', j)] = (lambda j=j: [local_chain(j, p) for p in pairs],
                               [('prep', j), ('local', j - RWKV_LOCAL_IN_FLIGHT)])
        tasks[('state', j)] = (lambda j=j: [state_chain(j, p) for p in pairs], [('local', j), ('state', j - 1)])
        tasks[('post', j)] = (lambda j=j: [post_chain(j)], [('state', j)])
    _run_tasks(tasks)

    carry_ref[0:1, :] = zr_ref[R - 1:R, :]
    for p in pairs:
        sbd_ref[p] = state[p]

    @pl.when(c == pl.num_programs(1) - 1)
    def _():
        for p in pairs:
            s_ref[0, 2 * p] = state[p][:HEAD_DIM, :HEAD_DIM]
            s_ref[0, 2 * p + 1] = state[p][HEAD_DIM:, HEAD_DIM:]


def _rwkv_mixer(zr, shift_prev, state0, lw, batch, seq, n_sub):
    valid = seq
    if seq < CHUNK:
        zr = jnp.pad(zr.reshape(batch, seq, RWKV_PROJ), ((0, 0), (0, CHUNK - seq), (0, 0))).reshape(-1, RWKV_PROJ)
        seq, n_sub = CHUNK, 1
    rows = CHUNK * n_sub
    steps = seq // rows
    seg = jnp.asarray(np.kron(np.eye(2), np.ones((HEAD_DIM, HEAD_DIM))), BF16)
    ones = np.ones((CHUNK, CHUNK))
    tri3 = jnp.asarray(np.concatenate([np.tile(np.tril(ones), (1, 3)), np.tile(np.triu(ones, 1), (1, 3))]), BF16)
    row = lambda name: lw[name].reshape(1, -1)
    params = [row('rwkv_mu'), row('rwkv_w0'), lw['rwkv_w2'].astype(BF16), row('rwkv_a0'),
              lw['rwkv_a2'].astype(BF16), lw['rwkv_g2'].astype(BF16), row('rwkv_k_k'), row('rwkv_k_a'),
              row('rwkv_r_k'), row('rwkv_ln_w'), row('rwkv_ln_b'), seg, tri3]
    state_spec = pl.BlockSpec((1, RWKV_HEADS, HEAD_DIM, HEAD_DIM), lambda b, c: (b, 0, 0, 0))
    out, state = pl.pallas_call(
        functools.partial(_rwkv_kernel, min(valid, rows)),
        grid=(batch, steps),
        in_specs=[pl.BlockSpec((rows, RWKV_PROJ), lambda b, c: (b * steps + c, 0)),
                  pl.BlockSpec((1, 1, RWKV_PROJ), lambda b, c: (b, 0, 0)),
                  state_spec] + [_const_spec(p.shape) for p in params],
        out_specs=[pl.BlockSpec((rows, RWKV_WIDTH), lambda b, c: (b * steps + c, 0)), state_spec],
        out_shape=[jax.ShapeDtypeStruct((batch * seq, RWKV_WIDTH), F32),
                   jax.ShapeDtypeStruct(state0.shape, F32)],
        scratch_shapes=[pltpu.VMEM((8, RWKV_PROJ), F32), pltpu.VMEM((rows, RWKV_WIDTH), F32),
                        pltpu.VMEM((RWKV_HEADS // 2, 2 * HEAD_DIM, 2 * HEAD_DIM), F32)],
        compiler_params=_params("parallel", "arbitrary"),
        name="rwkv_mixer",
    )(zr, shift_prev, state0, *params)
    if valid < seq:
        out = out.reshape(batch, seq, RWKV_WIDTH)[:, :valid].reshape(batch * valid, RWKV_WIDTH)
    return out, state


def _memkv_kernel(m_ref, g_ref, wk_ref, wv_ref, k_ref, v_ref):
    mn = _rms(m_ref[...], g_ref[...]).astype(BF16)
    k_ref[...] = jnp.dot(mn, wk_ref[...], preferred_element_type=F32)
    v_ref[...] = jnp.dot(mn, wv_ref[...], preferred_element_type=F32)


def _memory_kv(mem2d, g, w_mk, w_mv):
    n = mem2d.shape[0]
    tm = min(ROW_TILE, n)
    row = lambda w: pl.BlockSpec((tm, w), lambda i: (i, 0))
    return pl.pallas_call(
        _memkv_kernel,
        grid=(n // tm,),
        in_specs=[row(D_MODEL), _const_spec((1, D_MODEL)), _const_spec(w_mk.shape), _const_spec(w_mv.shape)],
        out_specs=[row(MEM_WIDTH), row(MEM_WIDTH)],
        out_shape=[jax.ShapeDtypeStruct((n, MEM_WIDTH), F32)] * 2,
        compiler_params=_params("parallel"),
        name="memory_kv",
    )(mem2d, g, w_mk, w_mv)


def _tail_kernel(x_ref, a_ref, r_ref, mk_ref, mv_ref, wo_ref, gc_ref, wq_ref, wco_ref, gm_ref, wu_ref, wd_ref, gf_ref,
                 y_ref):
    n_seq = mk_ref.shape[0]
    rows_per_seq = x_ref.shape[0] // n_seq
    x1 = x_ref[...] + _dot(a_ref[...], wo_ref[:ATTN_WIDTH, :]) + _dot(r_ref[...], wo_ref[ATTN_WIDTH:, :])
    q = _dot(_rms(x1, gc_ref[...]), wq_ref[...]).astype(BF16)
    outs = {}

    def cross_chain(b, h):
        rows = slice(b * rows_per_seq, (b + 1) * rows_per_seq)
        sl = slice(h * MEM_HEAD_DIM, (h + 1) * MEM_HEAD_DIM)
        s = _dg(q[rows, sl], mk_ref[b, :, sl].astype(BF16), _NT) * (MEM_HEAD_DIM ** -0.5)
        yield
        p = jnp.exp(s - jnp.max(s, axis=-1, keepdims=True))
        den = jnp.sum(p, axis=-1, keepdims=True)
        yield
        outs[(b, h)] = _dg(p.astype(BF16), mv_ref[b, :, sl].astype(BF16), _NN) * (1.0 / den)
        yield

    _run_interleaved([cross_chain(b, h) for b in range(n_seq) for h in range(MEM_HEADS)])
    o = _cat_rows(*[_cat_lanes(*[outs[(b, h)] for h in range(MEM_HEADS)]) for b in range(n_seq)])
    x2 = x1 + _dot(o, wco_ref[...])
    up = _dot(_rms(x2, gm_ref[...]), wu_ref[...])
    act = jnp.square(jnp.maximum(up, 0.0))
    y_ref[...] = _rms(x2 + _dot(act, wd_ref[...]), gf_ref[...])


def _tail(x2d, a_out, r_out, mk, mv, lw, batch, seq):
    n = batch * seq
    tq = min(TAIL_ROW_TILE, n)
    if seq >= tq:
        assert seq % tq == 0
        seq_per_tile, tiles_per_seq = 1, seq // tq
        mem_spec = pl.BlockSpec((1, N_MEM, MEM_WIDTH), lambda i: (i // tiles_per_seq, 0, 0))
    else:
        assert tq % seq == 0
        seq_per_tile = tq // seq
        mem_spec = pl.BlockSpec((seq_per_tile, N_MEM, MEM_WIDTH), lambda i: (i, 0, 0))
    row = lambda w: pl.BlockSpec((tq, w), lambda i: (i, 0))

    def resident(shape):
        nd = len(shape)
        return pl.BlockSpec(shape, lambda *_: (0,) * nd, pipeline_mode=pl.Buffered(1))

    weights = [lw['w_out'], lw['norm_cross_g'], lw['w_cq'], lw['w_co'], lw['norm_mlp_g'], lw['w_up'], lw['w_down'],
               lw['norm_final_g']]
    return pl.pallas_call(
        _tail_kernel,
        grid=(n // tq,),
        in_specs=[row(D_MODEL), row(ATTN_WIDTH), row(RWKV_WIDTH), mem_spec, mem_spec]
                 + [resident(w.shape) for w in weights],
        out_specs=row(D_MODEL),
        out_shape=jax.ShapeDtypeStruct(x2d.shape, F32),
        compiler_params=_params("parallel"),
        name="tail",
    )(x2d, a_out, r_out, mk, mv, *weights)


def _trunk(x, mk, mv, k_past, v_past, shift_prev, state0, lw, table, n_sub):
    batch, seq = x.shape[0], x.shape[1]
    x2d = x.reshape(batch * seq, D_MODEL)
    q, k, v, zr = _in_proj(x2d, lw['norm_mix_g'], lw['w_in'])
    if k_past is None:
        bias = _rel_bias(table, CHUNK, WINDOW + CHUNK)
        a_out = _prompt_attention(q, k, v, lw['attn_sink'], bias, batch, seq)
        k3 = k.reshape(batch, seq, KV_WIDTH)
        v3 = v.reshape(batch, seq, KV_WIDTH)
        k_buf, v_buf = k3[:, -WINDOW:], v3[:, -WINDOW:]
    else:
        bias = _rel_bias(table, seq, WINDOW + seq)
        a_out = _sample_attention(q, k, v, k_past.reshape(batch * WINDOW, KV_WIDTH),
                                  v_past.reshape(batch * WINDOW, KV_WIDTH), lw['attn_sink'], bias, seq)
        k_buf = jnp.concatenate([k_past, k.reshape(batch, seq, KV_WIDTH)], axis=1)[:, -WINDOW:]
        v_buf = jnp.concatenate([v_past, v.reshape(batch, seq, KV_WIDTH)], axis=1)[:, -WINDOW:]
    r_out, state = _rwkv_mixer(zr, shift_prev, state0, lw, batch, seq, n_sub)
    shift_new = zr.reshape(batch, seq, RWKV_PROJ)[:, -1:]
    y = _tail(x2d, a_out, r_out, mk, mv, lw, batch, seq)
    kv_shape = (batch, WINDOW, KV_HEADS, HEAD_DIM)
    return y.reshape(x.shape), k_buf.reshape(kv_shape), v_buf.reshape(kv_shape), shift_new, state


def kernel(x_prompt, x_sample, mem_prompt, cache_attn_k, cache_attn_v, cache_mem_k, cache_mem_v, state_shift,
           state_wkv, norm_mix_g, w_in, attn_sink, rel_bias_table, rwkv_mu, rwkv_w0, rwkv_w2, rwkv_a0, rwkv_a2,
           rwkv_g2, rwkv_k_k, rwkv_k_a, rwkv_r_k, rwkv_ln_w, rwkv_ln_b, w_out, norm_cross_g, norm_mem_g, w_cq,
           w_mk, w_mv, w_co, norm_mlp_g, w_up, w_down, norm_final_g):
    assert norm_mix_g.shape[0] == 1, "single-layer trunk"
    bp, dec_b = x_prompt.shape[0], x_sample.shape[0]
    vec = lambda p: p[0].reshape(1, -1)
    lw = {
        'norm_mix_g': vec(norm_mix_g), 'w_in': w_in[0].astype(BF16), 'attn_sink': attn_sink[0],
        'rwkv_mu': rwkv_mu[0], 'rwkv_w0': rwkv_w0[0], 'rwkv_w2': rwkv_w2[0], 'rwkv_a0': rwkv_a0[0],
        'rwkv_a2': rwkv_a2[0], 'rwkv_g2': rwkv_g2[0], 'rwkv_k_k': rwkv_k_k[0], 'rwkv_k_a': rwkv_k_a[0],
        'rwkv_r_k': rwkv_r_k[0], 'rwkv_ln_w': rwkv_ln_w[0], 'rwkv_ln_b': rwkv_ln_b[0],
        'w_out': w_out[0].astype(BF16), 'norm_cross_g': vec(norm_cross_g), 'w_cq': w_cq[0].astype(BF16),
        'w_co': w_co[0].astype(BF16), 'norm_mlp_g': vec(norm_mlp_g), 'w_up': w_up[0].astype(BF16),
        'w_down': w_down[0].astype(BF16), 'norm_final_g': norm_final_g.reshape(1, -1),
    }
    mk, mv = _memory_kv(mem_prompt.reshape(bp * N_MEM, D_MODEL), vec(norm_mem_g),
                        w_mk[0].astype(BF16), w_mv[0].astype(BF16))
    mk = mk.reshape(bp, N_MEM, MEM_WIDTH)
    mv = mv.reshape(bp, N_MEM, MEM_WIDTH)
    shift0 = jnp.zeros((bp, 1, RWKV_PROJ), F32)
    wkv0 = jnp.zeros((bp, RWKV_HEADS, HEAD_DIM, HEAD_DIM), F32)
    yp, pk, pv, psh, pS = _trunk(x_prompt, mk, mv, None, None, shift0, wkv0, lw, rel_bias_table, RWKV_SUB_CHUNKS)
    ys, sk, sv, ssh, sS = _trunk(
        x_sample, cache_mem_k[0].reshape(dec_b, N_MEM, MEM_WIDTH), cache_mem_v[0].reshape(dec_b, N_MEM, MEM_WIDTH),
        cache_attn_k[0].reshape(dec_b, WINDOW, KV_WIDTH), cache_attn_v[0].reshape(dec_b, WINDOW, KV_WIDTH),
        state_shift[0], state_wkv[0], lw, rel_bias_table, 1)
    mem_shape = (1, bp, N_MEM, MEM_HEADS, MEM_HEAD_DIM)
    return (yp, ys, pk[None], pv[None], mk.reshape(mem_shape), mv.reshape(mem_shape), psh[None], pS[None],
            sk[None], sv[None], ssh[None], sS[None])
```

```python
import functools
import math

import numpy as np
import jax
import jax.numpy as jnp
from jax import lax
from jax.experimental import pallas as pl
from jax.experimental.pallas import tpu as pltpu

F32 = jnp.float32
BF16 = jnp.bfloat16

D_MODEL = 1024
CHUNK = 64
WINDOW = 128
HEAD_DIM = 64
ATTN_WIDTH = 512
ATTN_HEADS = 8
KV_HEADS = 2
GROUP = 4
KV_WIDTH = 128
RWKV_WIDTH = 512
RWKV_HEADS = 8
DECAY_LORA = 64
AAA_LORA = 64
GATE_LORA = 128
RWKV_PROJ = 1792
IN_PROJ = 2560
N_MEM = 256
MEM_HEADS = 4
MEM_HEAD_DIM = 128
MEM_WIDTH = 512
D_FF = 4096
REL_BUCKETS = 32
REL_MAX_DIST = 128
NORM_EPS = 1e-6
GN_EPS = 64e-5
LOG2_E = math.log2(math.e)

V7X_VMEM_LIMIT_BYTES = 52 * 1024 * 1024
ROW_TILE = 256
IN_PROJ_ROW_TILE = 512
TAIL_ROW_TILE = 512
RWKV_SUB_CHUNKS = 8
RWKV_LOCAL_IN_FLIGHT = 3


def _params(*sem):
    return pltpu.CompilerParams(dimension_semantics=sem, vmem_limit_bytes=V7X_VMEM_LIMIT_BYTES)


def _const_spec(shape):
    nd = len(shape)
    return pl.BlockSpec(shape, lambda *_: (0,) * nd)


_NN = ((1,), (0,))
_NT = ((1,), (1,))
_TN = ((0,), (0,))


def _dg(a, b, dims):
    return lax.dot_general(a, b, (dims, ((), ())), preferred_element_type=F32)


def _dot(a, b):
    return _dg(a.astype(BF16), b.astype(BF16), _NN)


def _dot_nt(a, b):
    return _dg(a.astype(BF16), b.astype(BF16), _NT)


def _cat_rows(*xs):
    return jnp.concatenate(xs, axis=0)


def _cat_lanes(*xs):
    return jnp.concatenate(xs, axis=1)


def _run_interleaved(chains):
    active = list(chains)
    while active:
        still = []
        for ch in active:
            try:
                next(ch)
                still.append(ch)
            except StopIteration:
                pass
        active = still


def _run_tasks(tasks):
    finished = set()
    running = {}
    waiting = dict(tasks)
    while waiting or running:
        for name in [n for n, (_, deps) in waiting.items() if all(d in finished or d not in tasks for d in deps)]:
            running[name] = list(waiting.pop(name)[0]())
        for name in list(running):
            alive = []
            for ch in running[name]:
                try:
                    next(ch)
                    alive.append(ch)
                except StopIteration:
                    pass
            if alive:
                running[name] = alive
            else:
                del running[name]
                finished.add(name)


def _rms(x, g):
    return x * lax.rsqrt(jnp.mean(x * x, axis=-1, keepdims=True) + NORM_EPS) * g


def _inproj_kernel(x_ref, g_ref, w_ref, q_ref, k_ref, v_ref, zr_ref):
    h = _rms(x_ref[...], g_ref[...]).astype(BF16)
    q = jnp.dot(h, w_ref[:, :ATTN_WIDTH], preferred_element_type=F32)
    q_ref[...] = (q * (HEAD_DIM ** -0.5 * LOG2_E)).astype(BF16)
    k_ref[...] = jnp.dot(h, w_ref[:, ATTN_WIDTH:ATTN_WIDTH + KV_WIDTH], preferred_element_type=F32)
    v_ref[...] = jnp.dot(h, w_ref[:, ATTN_WIDTH + KV_WIDTH:ATTN_WIDTH + 2 * KV_WIDTH],
                         preferred_element_type=F32)
    zr_ref[...] = jnp.dot(h, w_ref[:, ATTN_WIDTH + 2 * KV_WIDTH:], preferred_element_type=F32)


def _in_proj(x2d, g, w_bf16):
    n = x2d.shape[0]
    tm = min(IN_PROJ_ROW_TILE, n)
    row = lambda w: pl.BlockSpec((tm, w), lambda i: (i, 0))
    return pl.pallas_call(
        _inproj_kernel,
        grid=(n // tm,),
        in_specs=[row(D_MODEL), _const_spec((1, D_MODEL)), _const_spec((D_MODEL, IN_PROJ))],
        out_specs=[row(ATTN_WIDTH), row(KV_WIDTH), row(KV_WIDTH), row(RWKV_PROJ)],
        out_shape=[jax.ShapeDtypeStruct((n, ATTN_WIDTH), BF16)]
                  + [jax.ShapeDtypeStruct((n, w), F32) for w in (KV_WIDTH, KV_WIDTH, RWKV_PROJ)],
        compiler_params=_params("parallel"),
        name="in_proj",
    )(x2d, g, w_bf16)


def _t5_bucket(rel):
    half = REL_BUCKETS // 2
    max_exact = half // 2
    assert REL_MAX_DIST == max_exact * 2 ** 4 and half - max_exact == 2 * 4
    n = np.abs(rel)
    large = max_exact + sum((n * n >= max_exact * max_exact * 2 ** t).astype(np.int64)
                            for t in range(1, half - max_exact))
    return (np.where(rel > 0, half, 0) + np.where(n < max_exact, n, large)).astype(np.int32)


def _bias_kernel(table_ref, bucket_ref, out_ref):
    bucket = bucket_ref[...]
    hits = [bucket == b for b in range(REL_BUCKETS)]
    for h in range(ATTN_HEADS):
        acc = jnp.zeros(bucket.shape, F32)
        for b in range(REL_BUCKETS):
            acc = jnp.where(hits[b], table_ref[b, h], acc)
        out_ref[h] = acc * LOG2_E


def _rel_bias(table, n_q, n_k):
    rel = np.arange(n_k)[None, :] - WINDOW - np.arange(n_q)[:, None]
    bucket = jnp.asarray(_t5_bucket(rel))
    bias = pl.pallas_call(
        _bias_kernel,
        in_specs=[pl.BlockSpec(memory_space=pltpu.SMEM), pl.BlockSpec(memory_space=pltpu.VMEM)],
        out_specs=pl.BlockSpec(memory_space=pltpu.VMEM),
        out_shape=jax.ShapeDtypeStruct((ATTN_HEADS, n_q, n_k), F32),
        name="rel_bias",
    )(table, bucket)
    return bias.reshape(KV_HEADS, GROUP * n_q, n_k)


def _group_sinks(sink_ref, n_q):
    row_group = lax.broadcasted_iota(jnp.int32, (GROUP * n_q, 1), 0) // n_q
    sinks = []
    for kvh in range(KV_HEADS):
        sink = jnp.zeros((GROUP * n_q, 1), F32)
        for g in range(GROUP):
            sink = jnp.where(row_group == g, sink_ref[kvh * GROUP + g] * LOG2_E, sink)
        sinks.append(sink)
    return sinks


def _attn_chain(q, keys, vals, bias, sink, valid, o_ref, rows, kvh):
    n_q = q.shape[0]
    qh = _cat_rows(*[q[:, (kvh * GROUP + g) * HEAD_DIM:(kvh * GROUP + g + 1) * HEAD_DIM]
                     for g in range(GROUP)])
    s = _dg(qh, keys, _NT) + bias
    if valid is not None:
        s = jnp.where(valid, s, -jnp.inf)
    yield
    m = jnp.maximum(jnp.max(s, axis=-1, keepdims=True), sink)
    p = jnp.exp2(s - m)
    den = jnp.sum(p, axis=-1, keepdims=True) + jnp.exp2(sink - m)
    yield
    o = _dg(p.astype(BF16), vals, _NN) * (1.0 / den)
    for g in range(GROUP):
        head = kvh * GROUP + g
        o_ref[rows, head * HEAD_DIM:(head + 1) * HEAD_DIM] = o[g * n_q:(g + 1) * n_q]
    yield


def _sample_attn_kernel(seq, sink_ref, q_ref, kp_ref, kn_ref, vp_ref, vn_ref, bias_ref, o_ref):
    batch = q_ref.shape[0] // seq
    sinks = _group_sinks(sink_ref, seq)
    chains = []
    for b in range(batch):
        rows = slice(b * seq, (b + 1) * seq)
        past = slice(b * WINDOW, (b + 1) * WINDOW)
        k_all = _cat_rows(kp_ref[past, :], kn_ref[rows, :]).astype(BF16)
        v_all = _cat_rows(vp_ref[past, :], vn_ref[rows, :]).astype(BF16)
        for kvh in range(KV_HEADS):
            lanes = slice(kvh * HEAD_DIM, (kvh + 1) * HEAD_DIM)
            chains.append(_attn_chain(q_ref[rows, :], k_all[:, lanes], v_all[:, lanes], bias_ref[kvh], sinks[kvh],
                                      None, o_ref, rows, kvh))
    _run_interleaved(chains)


def _sample_attention(q, k, v, k_past, v_past, sink, bias, seq):
    vmem = pl.BlockSpec(memory_space=pltpu.VMEM)
    return pl.pallas_call(
        functools.partial(_sample_attn_kernel, seq),
        in_specs=[pl.BlockSpec(memory_space=pltpu.SMEM)] + [vmem] * 6,
        out_specs=vmem,
        out_shape=jax.ShapeDtypeStruct(q.shape, F32),
        name="sample_attention",
    )(sink, q, k_past, k, v_past, v, bias)


def _split2(x):
    hi = x.astype(BF16)
    lo = (x - hi.astype(F32)).astype(BF16)
    return hi, lo


def _softplus(x):
    return jnp.maximum(x, 0.0) + jnp.log(1.0 + jnp.exp(-jnp.abs(x)))


def _sigmoid(x):
    return 1.0 / (1.0 + jnp.exp(-x))


def _rwkv_kernel(valid_rows, with_attention, *refs):
    (zr_ref, shift_ref, s0_ref, mu_ref, w0_ref, w2_ref, a0_ref, a2_ref, g2_ref, kk_ref, ka_ref, rk_ref, lnw_ref,
     lnb_ref, seg_ref, tri_ref) = refs[:16]
    if with_attention:
        sink_ref, q_ref, kp_ref, kc_ref, vp_ref, vc_ref, bias_ref = refs[16:23]
        out_ref, s_ref, attn_ref, carry_ref, y_ref, sbd_ref = refs[23:]
    else:
        out_ref, s_ref, carry_ref, y_ref, sbd_ref = refs[16:]
    c = pl.program_id(1)
    C = CHUNK
    R = zr_ref.shape[0]
    n_sub = R // C
    pairs = range(RWKV_HEADS // 2)
    PAIR = 2 * HEAD_DIM
    W = RWKV_WIDTH

    @pl.when(c == 0)
    def _():
        carry_ref[0:1, :] = shift_ref[0]
        zero = jnp.zeros((HEAD_DIM, HEAD_DIM), F32)
        for p in pairs:
            sbd_ref[p] = _cat_rows(_cat_lanes(s0_ref[0, 2 * p], zero), _cat_lanes(zero, s0_ref[0, 2 * p + 1]))

    seg = seg_ref[...]
    seg2 = _cat_rows(seg, seg)

    def head_sum(x):
        hi, lo = _split2(x)
        tiles = [_dg(_cat_lanes(hi[:, t * PAIR:(t + 1) * PAIR], lo[:, t * PAIR:(t + 1) * PAIR]), seg2, _NN)
                 for t in range(W // PAIR)]
        return _cat_lanes(*tiles)

    first_row = lax.broadcasted_iota(jnp.int32, (C, 1), 0) == 0
    tri3 = tri_ref[...]

    lane = lax.broadcasted_iota(jnp.int32, (C, PAIR), 1)
    trow = lax.broadcasted_iota(jnp.int32, (C, PAIR), 0)
    even = lane < HEAD_DIM
    tcol = jnp.where(even, lane, lane - HEAD_DIM)
    strict = tcol < trow
    incl = tcol <= trow
    eye = jnp.where(tcol == trow, 1.0, 0.0).astype(F32)
    brow = lax.broadcasted_iota(jnp.int32, (PAIR, PAIR), 0) < HEAD_DIM
    bcol = lax.broadcasted_iota(jnp.int32, (PAIR, PAIR), 1) < HEAD_DIM
    on_diag = brow == bcol

    def bd(x):
        zero = jnp.zeros_like(x)
        return _cat_rows(jnp.where(even, x, zero), jnp.where(even, zero, x))

    def bd2(pair):
        return bd(pair[0]), bd(pair[1])

    def mm(a_pair, w_pair, dims=_NN):
        if dims == _NN:
            first = _dg(_cat_lanes(a_pair[0], a_pair[1]), _cat_rows(w_pair[0], w_pair[0]), dims)
        else:
            first = _dg(_cat_lanes(a_pair[0], a_pair[1]), _cat_lanes(w_pair[0], w_pair[0]), dims)
        return first + _dg(a_pair[0], w_pair[1], dims)

    prepped = {}
    ready = {}
    state = [sbd_ref[p] for p in pairs]

    def prep_chain(j):
        rows = slice(j * C, (j + 1) * C)
        zr = zr_ref[rows, :]
        before = carry_ref[0:1, :] if j == 0 else zr_ref[j * C - 1:j * C, :]
        z_prev = jnp.where(first_row, before, pltpu.roll(zr, 1, axis=0))
        zs = zr + (z_prev - zr) * mu_ref[...]
        r = zs[:, :W]
        k = zs[:, W:2 * W]
        v = zs[:, 2 * W:3 * W]
        wd = zs[:, 3 * W:3 * W + DECAY_LORA]
        ad = zs[:, 3 * W + DECAY_LORA:3 * W + DECAY_LORA + AAA_LORA]
        gd = zs[:, 3 * W + DECAY_LORA + AAA_LORA:]
        w_log = -_softplus(-(w0_ref[...] + _dot(jnp.tanh(wd), w2_ref[...]))) - 0.5
        lw = -jnp.exp(w_log)
        a = _sigmoid(a0_ref[...] + _dot(ad, a2_ref[...]))
        gate = _dot(_sigmoid(gd), g2_ref[...])
        kk = k * kk_ref[...]
        kk = kk * lax.rsqrt(jnp.maximum(head_sum(kk * kk), 1e-24))
        k2 = k * (1.0 + (a - 1.0) * ka_ref[...])
        if (j + 1) * C > valid_rows:
            live = lax.broadcasted_iota(jnp.int32, (C, 1), 0) < valid_rows - j * C
            lw = jnp.where(live, lw, 0.0)
            kk = jnp.where(live, kk, 0.0)
            k2 = jnp.where(live, k2, 0.0)
        bvec = kk * a
        yield
        l1 = lw.astype(BF16)
        rem = lw - l1.astype(F32)
        l2 = rem.astype(BF16)
        l3 = (rem - l2.astype(F32)).astype(BF16)
        sums = _dg(tri3, _cat_rows(l1, l2, l3), _NN)
        li = sums[:C]
        lrev = sums[C:]
        yield
        inv_p = jnp.exp(-li)
        to_end = jnp.exp(lrev)
        prepped[j] = dict(
            at=_split2(-kk * jnp.exp(li - lw)), rt=_split2(r * jnp.exp(li)), bt=_split2(bvec * inv_p),
            kt=_split2(k2 * inv_p), bh=_split2(bvec * to_end), kh=_split2(k2 * to_end), v=_split2(v),
            p_end=jnp.exp(li[C - 1:C, :]), bonus=head_sum(r * k2 * rk_ref[...]) * v, gate=gate)
        yield

    def local_chain(j, p):
        d = prepped[j]
        lanes = slice(p * PAIR, (p + 1) * PAIR)
        cut = lambda pair: (pair[0][:, lanes], pair[1][:, lanes])
        at_p, rt_p, bt_p, kt_p, bh_p, kh_p, v_p = map(cut, (d['at'], d['rt'], d['bt'], d['kt'], d['bh'], d['kh'],
                                                            d['v']))
        left = (_cat_rows(at_p[0], rt_p[0]), _cat_rows(at_p[1], rt_p[1]))
        right = (_cat_rows(bd(bt_p[0]), bd(kt_p[0])), _cat_rows(bd(bt_p[1]), bd(kt_p[1])))
        aa = mm(left, right, _NT)
        yield
        a_ab = jnp.where(strict, aa[:C, :PAIR], 0.0)
        a_ak = jnp.where(strict, aa[:C, PAIR:], 0.0)
        a_rb = jnp.where(incl, aa[C:, :PAIR], 0.0)
        a_rk = jnp.where(incl, aa[C:, PAIR:], 0.0)
        inv = eye + a_ab
        ps = _split2(a_ab)
        power = mm(ps, bd2(ps))
        span = 2
        yield
        while span < C:
            ps = _split2(power)
            pw = bd2(ps)
            ih = _split2(inv)
            if span * 2 < C:
                both = mm((_cat_rows(ih[0], ps[0]), _cat_rows(ih[1], ps[1])), pw)
                inv = inv + both[:C]
                power = both[C:]
            else:
                inv = inv + mm(ih, pw)
            span *= 2
            yield
        ready[(j, p)] = dict(inv=_split2(inv), akrk=_split2(_cat_rows(a_ak, a_rk)), rb=_split2(a_rb), left=left,
                             bhkh=(_cat_rows(bh_p[0], kh_p[0]), _cat_rows(bh_p[1], kh_p[1])), v=v_p,
                             p_end=d['p_end'][:, lanes])

    def state_chain(j, p):
        d = ready.pop((j, p))
        s_prev = state[p]
        v_hi, v_lo = d['v']
        both = mm(d['left'], _split2(s_prev), _NT) + mm(d['akrk'], (bd(v_hi), bd(v_lo)))
        rhs = both[:C]
        y0 = both[C:]
        yield
        u_pair = _split2(mm(d['inv'], bd2(_split2(rhs))))
        yield
        y_ref[j * C:(j + 1) * C, p * PAIR:(p + 1) * PAIR] = y0 + mm(d['rb'], bd2(u_pair))
        t_hi = _cat_rows(u_pair[0], v_hi)
        t_lo = _cat_rows(u_pair[1], v_lo)
        w_hi, w_lo = d['bhkh']
        upd = _dg(_cat_rows(t_hi, t_lo), _cat_rows(w_hi, w_hi), _TN) + _dg(t_hi, w_lo, _TN)
        state[p] = s_prev * d['p_end'] + jnp.where(on_diag, upd, 0.0)
        yield

    def post_chain(j):
        rows = slice(j * C, (j + 1) * C)
        d = prepped.pop(j)
        y = y_ref[rows, :]
        mean = head_sum(y) * (1.0 / HEAD_DIM)
        dev = y - mean
        yield
        var = head_sum(dev * dev) * (1.0 / HEAD_DIM)
        yn = dev * lax.rsqrt(var + GN_EPS) * lnw_ref[...] + lnb_ref[...]
        out_ref[rows, :] = (yn + d['bonus']) * d['gate']
        yield

    tasks = {}
    for j in range(n_sub):
        tasks[('prep', j)] = (lambda j=j: [prep_chain(j)], [('prep', j - 1), ('local', j - RWKV_LOCAL_IN_FLIGHT)])
        tasks[('local', j)] = (lambda j=j: [local_chain(j, p) for p in pairs],
                               [('prep', j), ('local', j - RWKV_LOCAL_IN_FLIGHT)])
        tasks[('state', j)] = (lambda j=j: [state_chain(j, p) for p in pairs], [('local', j), ('state', j - 1)])
        tasks[('post', j)] = (lambda j=j: [post_chain(j)], [('state', j)])
    if with_attention:
        n_k = WINDOW + CHUNK
        k_all = _cat_rows(kp_ref[...], kc_ref[...]).astype(BF16)
        v_all = _cat_rows(vp_ref[...], vc_ref[...]).astype(BF16)
        first_valid = jnp.where(c == 0, WINDOW, 0)
        kcol = lax.broadcasted_iota(jnp.int32, (1, n_k), 1)
        sinks = _group_sinks(sink_ref, CHUNK)

        def attn_chains(j):
            rows = slice(j * C, (j + 1) * C)
            keys = slice(j * C, j * C + n_k)
            valid = kcol + j * C >= first_valid if j * C < WINDOW else None
            return [_attn_chain(q_ref[rows, :], k_all[keys, kvh * HEAD_DIM:(kvh + 1) * HEAD_DIM],
                                v_all[keys, kvh * HEAD_DIM:(kvh + 1) * HEAD_DIM], bias_ref[kvh], sinks[kvh], valid,
                                attn_ref, rows, kvh) for kvh in range(KV_HEADS)]

        for j in range(n_sub):
            tasks[('attn', j)] = (lambda j=j: attn_chains(j), [('attn', j - 1)])
    _run_tasks(tasks)

    carry_ref[0:1, :] = zr_ref[R - 1:R, :]
    for p in pairs:
        sbd_ref[p] = state[p]

    @pl.when(c == pl.num_programs(1) - 1)
    def _():
        for p in pairs:
            s_ref[0, 2 * p] = state[p][:HEAD_DIM, :HEAD_DIM]
            s_ref[0, 2 * p + 1] = state[p][HEAD_DIM:, HEAD_DIM:]


def _rwkv_mixer(zr, shift_prev, state0, lw, batch, seq, n_sub, attention=None):
    valid = seq
    if seq < CHUNK:
        zr = jnp.pad(zr.reshape(batch, seq, RWKV_PROJ), ((0, 0), (0, CHUNK - seq), (0, 0))).reshape(-1, RWKV_PROJ)
        seq, n_sub = CHUNK, 1
    rows = CHUNK * n_sub
    steps = seq // rows
    seg = jnp.asarray(np.kron(np.eye(2), np.ones((HEAD_DIM, HEAD_DIM))), BF16)
    ones = np.ones((CHUNK, CHUNK))
    tri3 = jnp.asarray(np.concatenate([np.tile(np.tril(ones), (1, 3)), np.tile(np.triu(ones, 1), (1, 3))]), BF16)
    row = lambda name: lw[name].reshape(1, -1)
    params = [row('rwkv_mu'), row('rwkv_w0'), lw['rwkv_w2'].astype(BF16), row('rwkv_a0'),
              lw['rwkv_a2'].astype(BF16), lw['rwkv_g2'].astype(BF16), row('rwkv_k_k'), row('rwkv_k_a'),
              row('rwkv_r_k'), row('rwkv_ln_w'), row('rwkv_ln_b'), seg, tri3]
    state_spec = pl.BlockSpec((1, RWKV_HEADS, HEAD_DIM, HEAD_DIM), lambda b, c: (b, 0, 0, 0))
    row_spec = lambda w: pl.BlockSpec((rows, w), lambda b, c: (b * steps + c, 0))
    in_specs = [row_spec(RWKV_PROJ), pl.BlockSpec((1, 1, RWKV_PROJ), lambda b, c: (b, 0, 0)), state_spec]
    in_specs += [_const_spec(p.shape) for p in params]
    out_specs = [row_spec(RWKV_WIDTH), state_spec]
    out_shape = [jax.ShapeDtypeStruct((batch * seq, RWKV_WIDTH), F32), jax.ShapeDtypeStruct(state0.shape, F32)]
    operands = [zr, shift_prev, state0, *params]
    if attention is not None:
        sink, q, k, v, bias = attention
        per_window = rows // WINDOW
        prev_spec = pl.BlockSpec((WINDOW, KV_WIDTH),
                                 lambda b, c: (jnp.maximum((b * steps + c) * per_window - 1, 0), 0))
        in_specs += [pl.BlockSpec(memory_space=pltpu.SMEM), row_spec(ATTN_WIDTH), prev_spec, row_spec(KV_WIDTH),
                     prev_spec, row_spec(KV_WIDTH), _const_spec(bias.shape)]
        out_specs.append(row_spec(ATTN_WIDTH))
        out_shape.append(jax.ShapeDtypeStruct((batch * seq, ATTN_WIDTH), F32))
        operands += [sink, q, k, k, v, v, bias]
    results = pl.pallas_call(
        functools.partial(_rwkv_kernel, min(valid, rows), attention is not None),
        grid=(batch, steps),
        in_specs=in_specs,
        out_specs=out_specs,
        out_shape=out_shape,
        scratch_shapes=[pltpu.VMEM((8, RWKV_PROJ), F32), pltpu.VMEM((rows, RWKV_WIDTH), F32),
                        pltpu.VMEM((RWKV_HEADS // 2, 2 * HEAD_DIM, 2 * HEAD_DIM), F32)],
        compiler_params=_params("parallel", "arbitrary"),
        name="rwkv_mixer",
    )(*operands)
    out, state = results[0], results[1]
    if valid < seq:
        out = out.reshape(batch, seq, RWKV_WIDTH)[:, :valid].reshape(batch * valid, RWKV_WIDTH)
    return (results[2], out, state) if attention is not None else (out, state)


def _memkv_kernel(m_ref, g_ref, wk_ref, wv_ref, k_ref, v_ref):
    mn = _rms(m_ref[...], g_ref[...]).astype(BF16)
    k_ref[...] = jnp.dot(mn, wk_ref[...], preferred_element_type=F32)
    v_ref[...] = jnp.dot(mn, wv_ref[...], preferred_element_type=F32)


def _memory_kv(mem2d, g, w_mk, w_mv):
    n = mem2d.shape[0]
    tm = min(ROW_TILE, n)
    row = lambda w: pl.BlockSpec((tm, w), lambda i: (i, 0))
    return pl.pallas_call(
        _memkv_kernel,
        grid=(n // tm,),
        in_specs=[row(D_MODEL), _const_spec((1, D_MODEL)), _const_spec(w_mk.shape), _const_spec(w_mv.shape)],
        out_specs=[row(MEM_WIDTH), row(MEM_WIDTH)],
        out_shape=[jax.ShapeDtypeStruct((n, MEM_WIDTH), F32)] * 2,
        compiler_params=_params("parallel"),
        name="memory_kv",
    )(mem2d, g, w_mk, w_mv)


def _tail_kernel(x_ref, a_ref, r_ref, mk_ref, mv_ref, wo_ref, gc_ref, wq_ref, wco_ref, gm_ref, wu_ref, wd_ref, gf_ref,
                 y_ref):
    n_seq = mk_ref.shape[0]
    rows_per_seq = x_ref.shape[0] // n_seq
    x1 = x_ref[...] + _dot(a_ref[...], wo_ref[:ATTN_WIDTH, :]) + _dot(r_ref[...], wo_ref[ATTN_WIDTH:, :])
    q = _dot(_rms(x1, gc_ref[...]), wq_ref[...]).astype(BF16)
    outs = {}

    def cross_chain(b, h):
        rows = slice(b * rows_per_seq, (b + 1) * rows_per_seq)
        sl = slice(h * MEM_HEAD_DIM, (h + 1) * MEM_HEAD_DIM)
        s = _dg(q[rows, sl], mk_ref[b, :, sl].astype(BF16), _NT) * (MEM_HEAD_DIM ** -0.5)
        yield
        p = jnp.exp(s - jnp.max(s, axis=-1, keepdims=True))
        den = jnp.sum(p, axis=-1, keepdims=True)
        yield
        outs[(b, h)] = _dg(p.astype(BF16), mv_ref[b, :, sl].astype(BF16), _NN) * (1.0 / den)
        yield

    _run_interleaved([cross_chain(b, h) for b in range(n_seq) for h in range(MEM_HEADS)])
    o = _cat_rows(*[_cat_lanes(*[outs[(b, h)] for h in range(MEM_HEADS)]) for b in range(n_seq)])
    x2 = x1 + _dot(o, wco_ref[...])
    up = _dot(_rms(x2, gm_ref[...]), wu_ref[...])
    act = jnp.square(jnp.maximum(up, 0.0))
    y_ref[...] = _rms(x2 + _dot(act, wd_ref[...]), gf_ref[...])


def _tail(x2d, a_out, r_out, mk, mv, lw, batch, seq):
    n = batch * seq
    tq = min(TAIL_ROW_TILE, n)
    if seq >= tq:
        assert seq % tq == 0
        seq_per_tile, tiles_per_seq = 1, seq // tq
        mem_spec = pl.BlockSpec((1, N_MEM, MEM_WIDTH), lambda i: (i // tiles_per_seq, 0, 0))
    else:
        assert tq % seq == 0
        seq_per_tile = tq // seq
        mem_spec = pl.BlockSpec((seq_per_tile, N_MEM, MEM_WIDTH), lambda i: (i, 0, 0))
    row = lambda w: pl.BlockSpec((tq, w), lambda i: (i, 0))

    def resident(shape):
        nd = len(shape)
        return pl.BlockSpec(shape, lambda *_: (0,) * nd, pipeline_mode=pl.Buffered(1))

    weights = [lw['w_out'], lw['norm_cross_g'], lw['w_cq'], lw['w_co'], lw['norm_mlp_g'], lw['w_up'], lw['w_down'],
               lw['norm_final_g']]
    return pl.pallas_call(
        _tail_kernel,
        grid=(n // tq,),
        in_specs=[row(D_MODEL), row(ATTN_WIDTH), row(RWKV_WIDTH), mem_spec, mem_spec]
                 + [resident(w.shape) for w in weights],
        out_specs=row(D_MODEL),
        out_shape=jax.ShapeDtypeStruct(x2d.shape, F32),
        compiler_params=_params("parallel"),
        name="tail",
    )(x2d, a_out, r_out, mk, mv, *weights)


def _trunk(x, mk, mv, k_past, v_past, shift_prev, state0, lw, table, n_sub):
    batch, seq = x.shape[0], x.shape[1]
    x2d = x.reshape(batch * seq, D_MODEL)
    q, k, v, zr = _in_proj(x2d, lw['norm_mix_g'], lw['w_in'])
    if k_past is None:
        bias = _rel_bias(table, CHUNK, WINDOW + CHUNK)
        a_out, r_out, state = _rwkv_mixer(zr, shift_prev, state0, lw, batch, seq, n_sub,
                                          attention=(lw['attn_sink'], q, k, v, bias))
        k3 = k.reshape(batch, seq, KV_WIDTH)
        v3 = v.reshape(batch, seq, KV_WIDTH)
        k_buf, v_buf = k3[:, -WINDOW:], v3[:, -WINDOW:]
    else:
        bias = _rel_bias(table, seq, WINDOW + seq)
        a_out = _sample_attention(q, k, v, k_past.reshape(batch * WINDOW, KV_WIDTH),
                                  v_past.reshape(batch * WINDOW, KV_WIDTH), lw['attn_sink'], bias, seq)
        k_buf = jnp.concatenate([k_past, k.reshape(batch, seq, KV_WIDTH)], axis=1)[:, -WINDOW:]
        v_buf = jnp.concatenate([v_past, v.reshape(batch, seq, KV_WIDTH)], axis=1)[:, -WINDOW:]
        r_out, state = _rwkv_mixer(zr, shift_prev, state0, lw, batch, seq, n_sub)
    shift_new = zr.reshape(batch, seq, RWKV_PROJ)[:, -1:]
    y = _tail(x2d, a_out, r_out, mk, mv, lw, batch, seq)
    kv_shape = (batch, WINDOW, KV_HEADS, HEAD_DIM)
    return y.reshape(x.shape), k_buf.reshape(kv_shape), v_buf.reshape(kv_shape), shift_new, state


def kernel(x_prompt, x_sample, mem_prompt, cache_attn_k, cache_attn_v, cache_mem_k, cache_mem_v, state_shift,
           state_wkv, norm_mix_g, w_in, attn_sink, rel_bias_table, rwkv_mu, rwkv_w0, rwkv_w2, rwkv_a0, rwkv_a2,
           rwkv_g2, rwkv_k_k, rwkv_k_a, rwkv_r_k, rwkv_ln_w, rwkv_ln_b, w_out, norm_cross_g, norm_mem_g, w_cq,
           w_mk, w_mv, w_co, norm_mlp_g, w_up, w_down, norm_final_g):
    assert norm_mix_g.shape[0] == 1, "single-layer trunk"
    bp, dec_b = x_prompt.shape[0], x_sample.shape[0]
    vec = lambda p: p[0].reshape(1, -1)
    lw = {
        'norm_mix_g': vec(norm_mix_g), 'w_in': w_in[0].astype(BF16), 'attn_sink': attn_sink[0],
        'rwkv_mu': rwkv_mu[0], 'rwkv_w0': rwkv_w0[0], 'rwkv_w2': rwkv_w2[0], 'rwkv_a0': rwkv_a0[0],
        'rwkv_a2': rwkv_a2[0], 'rwkv_g2': rwkv_g2[0], 'rwkv_k_k': rwkv_k_k[0], 'rwkv_k_a': rwkv_k_a[0],
        'rwkv_r_k': rwkv_r_k[0], 'rwkv_ln_w': rwkv_ln_w[0], 'rwkv_ln_b': rwkv_ln_b[0],
        'w_out': w_out[0].astype(BF16), 'norm_cross_g': vec(norm_cross_g), 'w_cq': w_cq[0].astype(BF16),
        'w_co': w_co[0].astype(BF16), 'norm_mlp_g': vec(norm_mlp_g), 'w_up': w_up[0].astype(BF16),
        'w_down': w_down[0].astype(BF16), 'norm_final_g': norm_final_g.reshape(1, -1),
    }
    mk, mv = _memory_kv(mem_prompt.reshape(bp * N_MEM, D_MODEL), vec(norm_mem_g),
                        w_mk[0].astype(BF16), w_mv[0].astype(BF16))
    mk = mk.reshape(bp, N_MEM, MEM_WIDTH)
    mv = mv.reshape(bp, N_MEM, MEM_WIDTH)
    shift0 = jnp.zeros((bp, 1, RWKV_PROJ), F32)
    wkv0 = jnp.zeros((bp, RWKV_HEADS, HEAD_DIM, HEAD_DIM), F32)
    yp, pk, pv, psh, pS = _trunk(x_prompt, mk, mv, None, None, shift0, wkv0, lw, rel_bias_table, RWKV_SUB_CHUNKS)
    ys, sk, sv, ssh, sS = _trunk(
        x_sample, cache_mem_k[0].reshape(dec_b, N_MEM, MEM_WIDTH), cache_mem_v[0].reshape(dec_b, N_MEM, MEM_WIDTH),
        cache_attn_k[0].reshape(dec_b, WINDOW, KV_WIDTH), cache_attn_v[0].reshape(dec_b, WINDOW, KV_WIDTH),
        state_shift[0], state_wkv[0], lw, rel_bias_table, 1)
    mem_shape = (1, bp, N_MEM, MEM_HEADS, MEM_HEAD_DIM)
    return (yp, ys, pk[None], pv[None], mk.reshape(mem_shape), mv.reshape(mem_shape), psh[None], pS[None],
            sk[None], sv[None], ssh[None], sS[None])
```

```python
import functools
import math

import numpy as np
import jax
import jax.numpy as jnp
from jax import lax
from jax.experimental import pallas as pl
from jax.experimental.pallas import tpu as pltpu

F32 = jnp.float32
BF16 = jnp.bfloat16

D_MODEL = 1024
CHUNK = 64
WINDOW = 128
HEAD_DIM = 64
ATTN_WIDTH = 512
ATTN_HEADS = 8
KV_HEADS = 2
GROUP = 4
KV_WIDTH = 128
RWKV_WIDTH = 512
RWKV_HEADS = 8
DECAY_LORA = 64
AAA_LORA = 64
GATE_LORA = 128
RWKV_PROJ = 1792
IN_PROJ = 2560
N_MEM = 256
MEM_HEADS = 4
MEM_HEAD_DIM = 128
MEM_WIDTH = 512
D_FF = 4096
REL_BUCKETS = 32
REL_MAX_DIST = 128
NORM_EPS = 1e-6
GN_EPS = 64e-5
LOG2_E = math.log2(math.e)

V7X_VMEM_LIMIT_BYTES = 52 * 1024 * 1024
ROW_TILE = 256
IN_PROJ_ROW_TILE = 512
PROJ_TILE = 256
TAIL_ROW_TILE = 512
RWKV_SUB_CHUNKS = 8
RWKV_LOCAL_IN_FLIGHT = 3


def _params(*sem):
    return pltpu.CompilerParams(dimension_semantics=sem, vmem_limit_bytes=V7X_VMEM_LIMIT_BYTES)


def _const_spec(shape):
    nd = len(shape)
    return pl.BlockSpec(shape, lambda *_: (0,) * nd)


_NN = ((1,), (0,))
_NT = ((1,), (1,))
_TN = ((0,), (0,))


def _dg(a, b, dims):
    return lax.dot_general(a, b, (dims, ((), ())), preferred_element_type=F32)


def _dot(a, b):
    return _dg(a.astype(BF16), b.astype(BF16), _NN)


def _dot_nt(a, b):
    return _dg(a.astype(BF16), b.astype(BF16), _NT)


def _cat_rows(*xs):
    return jnp.concatenate(xs, axis=0)


def _cat_lanes(*xs):
    return jnp.concatenate(xs, axis=1)


def _run_interleaved(chains):
    active = list(chains)
    while active:
        still = []
        for ch in active:
            try:
                next(ch)
                still.append(ch)
            except StopIteration:
                pass
        active = still


def _run_tasks(tasks):
    finished = set()
    running = {}
    waiting = dict(tasks)
    while waiting or running:
        for name in [n for n, (_, deps) in waiting.items() if all(d in finished or d not in tasks for d in deps)]:
            running[name] = list(waiting.pop(name)[0]())
        for name in list(running):
            alive = []
            for ch in running[name]:
                try:
                    next(ch)
                    alive.append(ch)
                except StopIteration:
                    pass
            if alive:
                running[name] = alive
            else:
                del running[name]
                finished.add(name)


def _rms(x, g):
    return x * lax.rsqrt(jnp.mean(x * x, axis=-1, keepdims=True) + NORM_EPS) * g


def _inproj_kernel(x_ref, g_ref, w_ref, q_ref, k_ref, v_ref, zr_ref):
    h = _rms(x_ref[...], g_ref[...]).astype(BF16)
    q = jnp.dot(h, w_ref[:, :ATTN_WIDTH], preferred_element_type=F32)
    q_ref[...] = (q * (HEAD_DIM ** -0.5 * LOG2_E)).astype(BF16)
    k_ref[...] = jnp.dot(h, w_ref[:, ATTN_WIDTH:ATTN_WIDTH + KV_WIDTH], preferred_element_type=F32)
    v_ref[...] = jnp.dot(h, w_ref[:, ATTN_WIDTH + KV_WIDTH:ATTN_WIDTH + 2 * KV_WIDTH],
                         preferred_element_type=F32)
    zr_ref[...] = jnp.dot(h, w_ref[:, ATTN_WIDTH + 2 * KV_WIDTH:], preferred_element_type=F32)


def _in_proj(x2d, g, w_bf16):
    n = x2d.shape[0]
    tm = min(IN_PROJ_ROW_TILE, n)
    row = lambda w: pl.BlockSpec((tm, w), lambda i: (i, 0))
    return pl.pallas_call(
        _inproj_kernel,
        grid=(n // tm,),
        in_specs=[row(D_MODEL), _const_spec((1, D_MODEL)), _const_spec((D_MODEL, IN_PROJ))],
        out_specs=[row(ATTN_WIDTH), row(KV_WIDTH), row(KV_WIDTH), row(RWKV_PROJ)],
        out_shape=[jax.ShapeDtypeStruct((n, ATTN_WIDTH), BF16)]
                  + [jax.ShapeDtypeStruct((n, w), F32) for w in (KV_WIDTH, KV_WIDTH, RWKV_PROJ)],
        compiler_params=_params("parallel"),
        name="in_proj",
    )(x2d, g, w_bf16)


def _t5_bucket(rel):
    half = REL_BUCKETS // 2
    max_exact = half // 2
    assert REL_MAX_DIST == max_exact * 2 ** 4 and half - max_exact == 2 * 4
    n = np.abs(rel)
    large = max_exact + sum((n * n >= max_exact * max_exact * 2 ** t).astype(np.int64)
                            for t in range(1, half - max_exact))
    return (np.where(rel > 0, half, 0) + np.where(n < max_exact, n, large)).astype(np.int32)


def _bias_kernel(table_ref, bucket_ref, out_ref):
    bucket = bucket_ref[...]
    hits = [bucket == b for b in range(REL_BUCKETS)]
    for h in range(ATTN_HEADS):
        acc = jnp.zeros(bucket.shape, F32)
        for b in range(REL_BUCKETS):
            acc = jnp.where(hits[b], table_ref[b, h], acc)
        out_ref[h] = acc * LOG2_E


def _rel_bias(table, n_q, n_k):
    rel = np.arange(n_k)[None, :] - WINDOW - np.arange(n_q)[:, None]
    bucket = jnp.asarray(_t5_bucket(rel))
    bias = pl.pallas_call(
        _bias_kernel,
        in_specs=[pl.BlockSpec(memory_space=pltpu.SMEM), pl.BlockSpec(memory_space=pltpu.VMEM)],
        out_specs=pl.BlockSpec(memory_space=pltpu.VMEM),
        out_shape=jax.ShapeDtypeStruct((ATTN_HEADS, n_q, n_k), F32),
        name="rel_bias",
    )(table, bucket)
    return bias.reshape(KV_HEADS, GROUP * n_q, n_k)


def _group_sinks(sink_ref, n_q):
    row_group = lax.broadcasted_iota(jnp.int32, (GROUP * n_q, 1), 0) // n_q
    sinks = []
    for kvh in range(KV_HEADS):
        sink = jnp.zeros((GROUP * n_q, 1), F32)
        for g in range(GROUP):
            sink = jnp.where(row_group == g, sink_ref[kvh * GROUP + g] * LOG2_E, sink)
        sinks.append(sink)
    return sinks


def _attn_chain(q, keys, vals, bias, sink, valid, o_ref, rows, kvh):
    n_q = q.shape[0]
    qh = _cat_rows(*[q[:, (kvh * GROUP + g) * HEAD_DIM:(kvh * GROUP + g + 1) * HEAD_DIM]
                     for g in range(GROUP)])
    s = _dg(qh, keys, _NT) + bias
    if valid is not None:
        s = jnp.where(valid, s, -jnp.inf)
    yield
    m = jnp.maximum(jnp.max(s, axis=-1, keepdims=True), sink)
    p = jnp.exp2(s - m)
    den = jnp.sum(p, axis=-1, keepdims=True) + jnp.exp2(sink - m)
    yield
    o = _dg(p.astype(BF16), vals, _NN) * (1.0 / den)
    for g in range(GROUP):
        head = kvh * GROUP + g
        o_ref[rows, head * HEAD_DIM:(head + 1) * HEAD_DIM] = o[g * n_q:(g + 1) * n_q]
    yield


def _sample_attn_kernel(seq, sink_ref, q_ref, kp_ref, kn_ref, vp_ref, vn_ref, bias_ref, o_ref):
    batch = q_ref.shape[0] // seq
    sinks = _group_sinks(sink_ref, seq)
    chains = []
    for b in range(batch):
        rows = slice(b * seq, (b + 1) * seq)
        past = slice(b * WINDOW, (b + 1) * WINDOW)
        k_all = _cat_rows(kp_ref[past, :], kn_ref[rows, :]).astype(BF16)
        v_all = _cat_rows(vp_ref[past, :], vn_ref[rows, :]).astype(BF16)
        for kvh in range(KV_HEADS):
            lanes = slice(kvh * HEAD_DIM, (kvh + 1) * HEAD_DIM)
            chains.append(_attn_chain(q_ref[rows, :], k_all[:, lanes], v_all[:, lanes], bias_ref[kvh], sinks[kvh],
                                      None, o_ref, rows, kvh))
    _run_interleaved(chains)


def _sample_attention(q, k, v, k_past, v_past, sink, bias, seq):
    vmem = pl.BlockSpec(memory_space=pltpu.VMEM)
    return pl.pallas_call(
        functools.partial(_sample_attn_kernel, seq),
        in_specs=[pl.BlockSpec(memory_space=pltpu.SMEM)] + [vmem] * 6,
        out_specs=vmem,
        out_shape=jax.ShapeDtypeStruct(q.shape, F32),
        name="sample_attention",
    )(sink, q, k_past, k, v_past, v, bias)


def _split2(x):
    hi = x.astype(BF16)
    lo = (x - hi.astype(F32)).astype(BF16)
    return hi, lo


def _softplus(x):
    return jnp.maximum(x, 0.0) + jnp.log(1.0 + jnp.exp(-jnp.abs(x)))


def _sigmoid(x):
    return 1.0 / (1.0 + jnp.exp(-x))


def _rwkv_kernel(valid_rows, fused, steps, *refs):
    if fused:
        (xn_ref, x0_ref, gmix_ref, win_ref, shift_ref, s0_ref, mu_ref, w0_ref, w2_ref, a0_ref, a2_ref, g2_ref, kk_ref,
         ka_ref, rk_ref, lnw_ref, lnb_ref, seg_ref, tri_ref, sink_ref, bias_ref) = refs[:21]
        (out_ref, s_ref, attn_ref, ktail_ref, vtail_ref, shiftout_ref, carry_ref, y_ref, sbd_ref, zr_scr, q_scr, k_scr,
         v_scr, kp_ref, vp_ref) = refs[21:]
        g = pl.program_id(0)
        c = lax.rem(g, steps)
        cur = lax.rem(g, 2)
        zr_ref, q_ref, kc_ref, vc_ref = zr_scr.at[cur], q_scr.at[cur], k_scr.at[cur], v_scr.at[cur]
    else:
        (zr_ref, shift_ref, s0_ref, mu_ref, w0_ref, w2_ref, a0_ref, a2_ref, g2_ref, kk_ref, ka_ref, rk_ref, lnw_ref,
         lnb_ref, seg_ref, tri_ref) = refs[:16]
        out_ref, s_ref, carry_ref, y_ref, sbd_ref = refs[16:]
        c = pl.program_id(1)
    C = CHUNK
    R = out_ref.shape[0]
    n_sub = R // C
    pairs = range(RWKV_HEADS // 2)
    PAIR = 2 * HEAD_DIM
    W = RWKV_WIDTH

    @pl.when(c == 0)
    def _():
        carry_ref[0:1, :] = shift_ref[0]
        zero = jnp.zeros((HEAD_DIM, HEAD_DIM), F32)
        for p in pairs:
            sbd_ref[p] = _cat_rows(_cat_lanes(s0_ref[0, 2 * p], zero), _cat_lanes(zero, s0_ref[0, 2 * p + 1]))
        if fused:
            kp_ref[...] = jnp.zeros(kp_ref.shape, F32)
            vp_ref[...] = jnp.zeros(vp_ref.shape, F32)

    def proj_chains(x_ref, slot):
        h = _rms(x_ref[...], gmix_ref[...]).astype(BF16)
        q_end, k_end, v_end = ATTN_WIDTH, ATTN_WIDTH + KV_WIDTH, ATTN_WIDTH + 2 * KV_WIDTH

        def tile(lo, hi):
            z = jnp.dot(h, win_ref[:, lo:hi], preferred_element_type=F32)
            if hi <= q_end:
                q_scr[slot, :, lo:hi] = (z * (HEAD_DIM ** -0.5 * LOG2_E)).astype(BF16)
            elif lo == q_end:
                k_scr[slot] = z[:, :KV_WIDTH]
                v_scr[slot] = z[:, KV_WIDTH:]
            else:
                zr_scr[slot, :, lo - v_end:hi - v_end] = z
            yield

        assert q_end % PROJ_TILE == 0 and v_end - q_end == PROJ_TILE
        return [tile(lo, lo + PROJ_TILE) for lo in range(0, IN_PROJ, PROJ_TILE)]

    if fused:
        @pl.when(g == 0)
        def _():
            _run_interleaved(proj_chains(x0_ref, 0))

    seg = seg_ref[...]
    seg2 = _cat_rows(seg, seg)

    def head_sum(x):
        hi, lo = _split2(x)
        tiles = [_dg(_cat_lanes(hi[:, t * PAIR:(t + 1) * PAIR], lo[:, t * PAIR:(t + 1) * PAIR]), seg2, _NN)
                 for t in range(W // PAIR)]
        return _cat_lanes(*tiles)

    first_row = lax.broadcasted_iota(jnp.int32, (C, 1), 0) == 0
    tri3 = tri_ref[...]

    lane = lax.broadcasted_iota(jnp.int32, (C, PAIR), 1)
    trow = lax.broadcasted_iota(jnp.int32, (C, PAIR), 0)
    even = lane < HEAD_DIM
    tcol = jnp.where(even, lane, lane - HEAD_DIM)
    strict = tcol < trow
    incl = tcol <= trow
    eye = jnp.where(tcol == trow, 1.0, 0.0).astype(F32)
    brow = lax.broadcasted_iota(jnp.int32, (PAIR, PAIR), 0) < HEAD_DIM
    bcol = lax.broadcasted_iota(jnp.int32, (PAIR, PAIR), 1) < HEAD_DIM
    on_diag = brow == bcol

    def bd(x):
        zero = jnp.zeros_like(x)
        return _cat_rows(jnp.where(even, x, zero), jnp.where(even, zero, x))

    def bd2(pair):
        return bd(pair[0]), bd(pair[1])

    def mm(a_pair, w_pair, dims=_NN):
        if dims == _NT:
            w_pair = (w_pair[0].T, w_pair[1].T)
        first = _dg(_cat_lanes(a_pair[0], a_pair[1]), _cat_rows(w_pair[0], w_pair[0]), _NN)
        return first + _dg(a_pair[0], w_pair[1], _NN)

    prepped = {}
    ready = {}
    state = [sbd_ref[p] for p in pairs]

    def prep_chain(j):
        rows = slice(j * C, (j + 1) * C)
        zr = zr_ref[rows, :]
        before = carry_ref[0:1, :] if j == 0 else zr_ref[j * C - 1:j * C, :]
        z_prev = jnp.where(first_row, before, pltpu.roll(zr, 1, axis=0))
        zs = zr + (z_prev - zr) * mu_ref[...]
        r = zs[:, :W]
        k = zs[:, W:2 * W]
        v = zs[:, 2 * W:3 * W]
        wd = zs[:, 3 * W:3 * W + DECAY_LORA]
        ad = zs[:, 3 * W + DECAY_LORA:3 * W + DECAY_LORA + AAA_LORA]
        gd = zs[:, 3 * W + DECAY_LORA + AAA_LORA:]
        w_log = -_softplus(-(w0_ref[...] + _dot(jnp.tanh(wd), w2_ref[...]))) - 0.5
        lw = -jnp.exp(w_log)
        a = _sigmoid(a0_ref[...] + _dot(ad, a2_ref[...]))
        gate = _dot(_sigmoid(gd), g2_ref[...])
        kk = k * kk_ref[...]
        kk = kk * lax.rsqrt(jnp.maximum(head_sum(kk * kk), 1e-24))
        k2 = k * (1.0 + (a - 1.0) * ka_ref[...])
        if (j + 1) * C > valid_rows:
            live = lax.broadcasted_iota(jnp.int32, (C, 1), 0) < valid_rows - j * C
            lw = jnp.where(live, lw, 0.0)
            kk = jnp.where(live, kk, 0.0)
            k2 = jnp.where(live, k2, 0.0)
        bvec = kk * a
        yield
        l1 = lw.astype(BF16)
        rem = lw - l1.astype(F32)
        l2 = rem.astype(BF16)
        l3 = (rem - l2.astype(F32)).astype(BF16)
        sums = _dg(tri3, _cat_rows(l1, l2, l3), _NN)
        li = sums[:C]
        lrev = sums[C:]
        yield
        inv_p = jnp.exp(-li)
        to_end = jnp.exp(lrev)
        prepped[j] = dict(
            at=_split2(-kk * jnp.exp(li - lw)), rt=_split2(r * jnp.exp(li)), bt=_split2(bvec * inv_p),
            kt=_split2(k2 * inv_p), bh=_split2(bvec * to_end), kh=_split2(k2 * to_end), v=_split2(v),
            p_end=jnp.exp(li[C - 1:C, :]), bonus=head_sum(r * k2 * rk_ref[...]) * v, gate=gate)
        yield

    def local_chain(j, p):
        d = prepped[j]
        lanes = slice(p * PAIR, (p + 1) * PAIR)
        cut = lambda pair: (pair[0][:, lanes], pair[1][:, lanes])
        at_p, rt_p, bt_p, kt_p, bh_p, kh_p, v_p = map(cut, (d['at'], d['rt'], d['bt'], d['kt'], d['bh'], d['kh'],
                                                            d['v']))
        left = (_cat_rows(at_p[0], rt_p[0]), _cat_rows(at_p[1], rt_p[1]))
        right = (_cat_rows(bd(bt_p[0]), bd(kt_p[0])), _cat_rows(bd(bt_p[1]), bd(kt_p[1])))
        aa = mm(left, right, _NT)
        yield
        a_ab = jnp.where(strict, aa[:C, :PAIR], 0.0)
        a_ak = jnp.where(strict, aa[:C, PAIR:], 0.0)
        a_rb = jnp.where(incl, aa[C:, :PAIR], 0.0)
        a_rk = jnp.where(incl, aa[C:, PAIR:], 0.0)
        inv = eye + a_ab
        ps = _split2(a_ab)
        power = mm(ps, bd2(ps))
        span = 2
        yield
        while span < C:
            ps = _split2(power)
            pw = bd2(ps)
            ih = _split2(inv)
            if span * 2 < C:
                both = mm((_cat_rows(ih[0], ps[0]), _cat_rows(ih[1], ps[1])), pw)
                inv = inv + both[:C]
                power = both[C:]
            else:
                inv = inv + mm(ih, pw)
            span *= 2
            yield
        ready[(j, p)] = dict(inv=_split2(inv), akrk=_split2(_cat_rows(a_ak, a_rk)), rb=_split2(a_rb), left=left,
                             bhkh=(_cat_rows(bh_p[0], kh_p[0]), _cat_rows(bh_p[1], kh_p[1])), v=v_p,
                             p_end=d['p_end'][:, lanes])

    def state_chain(j, p):
        d = ready.pop((j, p))
        s_prev = state[p]
        v_hi, v_lo = d['v']
        both = mm(d['left'], _split2(s_prev), _NT) + mm(d['akrk'], (bd(v_hi), bd(v_lo)))
        rhs = both[:C]
        y0 = both[C:]
        yield
        u_pair = _split2(mm(d['inv'], bd2(_split2(rhs))))
        yield
        y_ref[j * C:(j + 1) * C, p * PAIR:(p + 1) * PAIR] = y0 + mm(d['rb'], bd2(u_pair))
        t_hi = _cat_rows(u_pair[0], v_hi)
        t_lo = _cat_rows(u_pair[1], v_lo)
        w_hi, w_lo = d['bhkh']
        upd = _dg(_cat_rows(t_hi, t_lo), _cat_rows(w_hi, w_hi), _TN) + _dg(t_hi, w_lo, _TN)
        state[p] = s_prev * d['p_end'] + jnp.where(on_diag, upd, 0.0)
        yield

    def post_chain(j):
        rows = slice(j * C, (j + 1) * C)
        d = prepped.pop(j)
        y = y_ref[rows, :]
        mean = head_sum(y) * (1.0 / HEAD_DIM)
        dev = y - mean
        yield
        var = head_sum(dev * dev) * (1.0 / HEAD_DIM)
        yn = dev * lax.rsqrt(var + GN_EPS) * lnw_ref[...] + lnb_ref[...]
        out_ref[rows, :] = (yn + d['bonus']) * d['gate']
        yield

    tasks = {}
    for j in range(n_sub):
        tasks[('prep', j)] = (lambda j=j: [prep_chain(j)], [('prep', j - 1), ('local', j - RWKV_LOCAL_IN_FLIGHT)])
        tasks[('local', j)] = (lambda j=j: [local_chain(j, p) for p in pairs],
                               [('prep', j), ('local', j - RWKV_LOCAL_IN_FLIGHT)])
        tasks[('state', j)] = (lambda j=j: [state_chain(j, p) for p in pairs], [('local', j), ('state', j - 1)])
        tasks[('post', j)] = (lambda j=j: [post_chain(j)], [('state', j)])
    if fused:
        n_k = WINDOW + CHUNK
        k_all = _cat_rows(kp_ref[...], kc_ref[...]).astype(BF16)
        v_all = _cat_rows(vp_ref[...], vc_ref[...]).astype(BF16)
        first_valid = jnp.where(c == 0, WINDOW, 0)
        kcol = lax.broadcasted_iota(jnp.int32, (1, n_k), 1)
        sinks = _group_sinks(sink_ref, CHUNK)

        def attn_chains(j):
            rows = slice(j * C, (j + 1) * C)
            keys = slice(j * C, j * C + n_k)
            valid = kcol + j * C >= first_valid if j * C < WINDOW else None
            return [_attn_chain(q_ref[rows, :], k_all[keys, kvh * HEAD_DIM:(kvh + 1) * HEAD_DIM],
                                v_all[keys, kvh * HEAD_DIM:(kvh + 1) * HEAD_DIM], bias_ref[kvh], sinks[kvh], valid,
                                attn_ref, rows, kvh) for kvh in range(KV_HEADS)]

        for j in range(n_sub):
            tasks[('attn', j)] = (lambda j=j: attn_chains(j), [('attn', j - 1)])
        for t, chain in enumerate(proj_chains(xn_ref, 1 - cur)):
            tasks[('proj', t)] = (lambda chain=chain: [chain], [('proj', t - 1), ('prep', min(t, n_sub - 1))])
    _run_tasks(tasks)

    carry_ref[0:1, :] = zr_ref[R - 1:R, :]
    for p in pairs:
        sbd_ref[p] = state[p]
    if fused:
        kp_ref[...] = kc_ref[R - WINDOW:R, :]
        vp_ref[...] = vc_ref[R - WINDOW:R, :]

    @pl.when(c == steps - 1)
    def _():
        for p in pairs:
            s_ref[0, 2 * p] = state[p][:HEAD_DIM, :HEAD_DIM]
            s_ref[0, 2 * p + 1] = state[p][HEAD_DIM:, HEAD_DIM:]
        if fused:
            ktail_ref[0] = kc_ref[R - WINDOW:R, :]
            vtail_ref[0] = vc_ref[R - WINDOW:R, :]
            shiftout_ref[0] = zr_ref[R - 1:R, :]


def _rwkv_operands(lw):
    seg = jnp.asarray(np.kron(np.eye(2), np.ones((HEAD_DIM, HEAD_DIM))), BF16)
    ones = np.ones((CHUNK, CHUNK))
    tri3 = jnp.asarray(np.concatenate([np.tile(np.tril(ones), (1, 3)), np.tile(np.triu(ones, 1), (1, 3))]), BF16)
    row = lambda name: lw[name].reshape(1, -1)
    return [row('rwkv_mu'), row('rwkv_w0'), lw['rwkv_w2'].astype(BF16), row('rwkv_a0'),
            lw['rwkv_a2'].astype(BF16), lw['rwkv_g2'].astype(BF16), row('rwkv_k_k'), row('rwkv_k_a'),
            row('rwkv_r_k'), row('rwkv_ln_w'), row('rwkv_ln_b'), seg, tri3]


_RWKV_SCRATCH = [pltpu.VMEM((8, RWKV_PROJ), F32), None,
                 pltpu.VMEM((RWKV_HEADS // 2, 2 * HEAD_DIM, 2 * HEAD_DIM), F32)]


def _rwkv_mixer(zr, shift_prev, state0, lw, batch, seq):
    assert seq <= CHUNK
    if seq < CHUNK:
        zr = jnp.pad(zr.reshape(batch, seq, RWKV_PROJ), ((0, 0), (0, CHUNK - seq), (0, 0))).reshape(-1, RWKV_PROJ)
    params = _rwkv_operands(lw)
    state_spec = pl.BlockSpec((1, RWKV_HEADS, HEAD_DIM, HEAD_DIM), lambda b, c: (b, 0, 0, 0))
    row_spec = lambda w: pl.BlockSpec((CHUNK, w), lambda b, c: (b, 0))
    scratch = list(_RWKV_SCRATCH)
    scratch[1] = pltpu.VMEM((CHUNK, RWKV_WIDTH), F32)
    out, state = pl.pallas_call(
        functools.partial(_rwkv_kernel, seq, False, 1),
        grid=(batch, 1),
        in_specs=[row_spec(RWKV_PROJ), pl.BlockSpec((1, 1, RWKV_PROJ), lambda b, c: (b, 0, 0)), state_spec]
                 + [_const_spec(p.shape) for p in params],
        out_specs=[row_spec(RWKV_WIDTH), state_spec],
        out_shape=[jax.ShapeDtypeStruct((batch * CHUNK, RWKV_WIDTH), F32), jax.ShapeDtypeStruct(state0.shape, F32)],
        scratch_shapes=scratch,
        compiler_params=_params("parallel", "arbitrary"),
        name="rwkv_mixer",
    )(zr, shift_prev, state0, *params)
    if seq < CHUNK:
        out = out.reshape(batch, CHUNK, RWKV_WIDTH)[:, :seq].reshape(batch * seq, RWKV_WIDTH)
    return out, state


def _prompt_mixer(x2d, shift_prev, state0, lw, bias, batch, seq):
    rows = CHUNK * RWKV_SUB_CHUNKS
    steps = seq // rows
    total = batch * steps
    params = _rwkv_operands(lw)
    seq_block = lambda shape: pl.BlockSpec((1,) + shape, lambda g: (g // steps,) + (0,) * len(shape))
    row_spec = lambda w: pl.BlockSpec((rows, w), lambda g: (g, 0))

    def resident(shape):
        nd = len(shape)
        return pl.BlockSpec(shape, lambda g: (0,) * nd, pipeline_mode=pl.Buffered(1))

    in_specs = [pl.BlockSpec((rows, D_MODEL), lambda g: (jnp.minimum(g + 1, total - 1), 0)),
                pl.BlockSpec((rows, D_MODEL), lambda g: (0, 0), pipeline_mode=pl.Buffered(1)),
                resident((1, D_MODEL)), resident((D_MODEL, IN_PROJ)),
                seq_block((1, RWKV_PROJ)), seq_block((RWKV_HEADS, HEAD_DIM, HEAD_DIM))]
    in_specs += [resident(p.shape) for p in params]
    in_specs += [pl.BlockSpec(memory_space=pltpu.SMEM), resident(bias.shape)]
    out_specs = [row_spec(RWKV_WIDTH), seq_block((RWKV_HEADS, HEAD_DIM, HEAD_DIM)), row_spec(ATTN_WIDTH),
                 seq_block((WINDOW, KV_WIDTH)), seq_block((WINDOW, KV_WIDTH)), seq_block((1, RWKV_PROJ))]
    n = batch * seq
    out_shape = [jax.ShapeDtypeStruct((n, RWKV_WIDTH), F32), jax.ShapeDtypeStruct(state0.shape, F32),
                 jax.ShapeDtypeStruct((n, ATTN_WIDTH), F32), jax.ShapeDtypeStruct((batch, WINDOW, KV_WIDTH), F32),
                 jax.ShapeDtypeStruct((batch, WINDOW, KV_WIDTH), F32),
                 jax.ShapeDtypeStruct((batch, 1, RWKV_PROJ), F32)]
    scratch = list(_RWKV_SCRATCH)
    scratch[1] = pltpu.VMEM((rows, RWKV_WIDTH), F32)
    scratch += [pltpu.VMEM((2, rows, RWKV_PROJ), F32), pltpu.VMEM((2, rows, ATTN_WIDTH), BF16),
                pltpu.VMEM((2, rows, KV_WIDTH), F32), pltpu.VMEM((2, rows, KV_WIDTH), F32),
                pltpu.VMEM((WINDOW, KV_WIDTH), F32), pltpu.VMEM((WINDOW, KV_WIDTH), F32)]
    r_out, state, a_out, k_tail, v_tail, shift_new = pl.pallas_call(
        functools.partial(_rwkv_kernel, rows, True, steps),
        grid=(total,),
        in_specs=in_specs,
        out_specs=out_specs,
        out_shape=out_shape,
        scratch_shapes=scratch,
        compiler_params=_params("arbitrary"),
        name="prompt_mixer",
    )(x2d, x2d, lw['norm_mix_g'], lw['w_in'], shift_prev, state0, *params, lw['attn_sink'], bias)
    return a_out, r_out, state, k_tail, v_tail, shift_new


def _memkv_kernel(m_ref, g_ref, wk_ref, wv_ref, k_ref, v_ref):
    mn = _rms(m_ref[...], g_ref[...]).astype(BF16)
    k_ref[...] = jnp.dot(mn, wk_ref[...], preferred_element_type=F32)
    v_ref[...] = jnp.dot(mn, wv_ref[...], preferred_element_type=F32)


def _memory_kv(mem2d, g, w_mk, w_mv):
    n = mem2d.shape[0]
    tm = min(ROW_TILE, n)
    row = lambda w: pl.BlockSpec((tm, w), lambda i: (i, 0))
    return pl.pallas_call(
        _memkv_kernel,
        grid=(n // tm,),
        in_specs=[row(D_MODEL), _const_spec((1, D_MODEL)), _const_spec(w_mk.shape), _const_spec(w_mv.shape)],
        out_specs=[row(MEM_WIDTH), row(MEM_WIDTH)],
        out_shape=[jax.ShapeDtypeStruct((n, MEM_WIDTH), F32)] * 2,
        compiler_params=_params("parallel"),
        name="memory_kv",
    )(mem2d, g, w_mk, w_mv)


def _tail_kernel(x_ref, a_ref, r_ref, mk_ref, mv_ref, wo_ref, gc_ref, wq_ref, wco_ref, gm_ref, wu_ref, wd_ref, gf_ref,
                 y_ref):
    n_seq = mk_ref.shape[0]
    rows_per_seq = x_ref.shape[0] // n_seq
    x1 = x_ref[...] + _dot(a_ref[...], wo_ref[:ATTN_WIDTH, :]) + _dot(r_ref[...], wo_ref[ATTN_WIDTH:, :])
    q = _dot(_rms(x1, gc_ref[...]), wq_ref[...]).astype(BF16)
    outs = {}

    def cross_chain(b, h):
        rows = slice(b * rows_per_seq, (b + 1) * rows_per_seq)
        sl = slice(h * MEM_HEAD_DIM, (h + 1) * MEM_HEAD_DIM)
        s = _dg(q[rows, sl], mk_ref[b, :, sl].astype(BF16), _NT) * (MEM_HEAD_DIM ** -0.5)
        yield
        p = jnp.exp(s - jnp.max(s, axis=-1, keepdims=True))
        den = jnp.sum(p, axis=-1, keepdims=True)
        yield
        outs[(b, h)] = _dg(p.astype(BF16), mv_ref[b, :, sl].astype(BF16), _NN) * (1.0 / den)
        yield

    _run_interleaved([cross_chain(b, h) for b in range(n_seq) for h in range(MEM_HEADS)])
    o = _cat_rows(*[_cat_lanes(*[outs[(b, h)] for h in range(MEM_HEADS)]) for b in range(n_seq)])
    x2 = x1 + _dot(o, wco_ref[...])
    up = _dot(_rms(x2, gm_ref[...]), wu_ref[...])
    act = jnp.square(jnp.maximum(up, 0.0))
    y_ref[...] = _rms(x2 + _dot(act, wd_ref[...]), gf_ref[...])


def _tail(x2d, a_out, r_out, mk, mv, lw, batch, seq):
    n = batch * seq
    tq = min(TAIL_ROW_TILE, n)
    if seq >= tq:
        assert seq % tq == 0
        seq_per_tile, tiles_per_seq = 1, seq // tq
        mem_spec = pl.BlockSpec((1, N_MEM, MEM_WIDTH), lambda i: (i // tiles_per_seq, 0, 0))
    else:
        assert tq % seq == 0
        seq_per_tile = tq // seq
        mem_spec = pl.BlockSpec((seq_per_tile, N_MEM, MEM_WIDTH), lambda i: (i, 0, 0))
    row = lambda w: pl.BlockSpec((tq, w), lambda i: (i, 0))

    def resident(shape):
        nd = len(shape)
        return pl.BlockSpec(shape, lambda *_: (0,) * nd, pipeline_mode=pl.Buffered(1))

    weights = [lw['w_out'], lw['norm_cross_g'], lw['w_cq'], lw['w_co'], lw['norm_mlp_g'], lw['w_up'], lw['w_down'],
               lw['norm_final_g']]
    return pl.pallas_call(
        _tail_kernel,
        grid=(n // tq,),
        in_specs=[row(D_MODEL), row(ATTN_WIDTH), row(RWKV_WIDTH), mem_spec, mem_spec]
                 + [resident(w.shape) for w in weights],
        out_specs=row(D_MODEL),
        out_shape=jax.ShapeDtypeStruct(x2d.shape, F32),
        compiler_params=_params("parallel"),
        name="tail",
    )(x2d, a_out, r_out, mk, mv, *weights)


def _trunk(x, mk, mv, k_past, v_past, shift_prev, state0, lw, table):
    batch, seq = x.shape[0], x.shape[1]
    x2d = x.reshape(batch * seq, D_MODEL)
    if k_past is None:
        bias = _rel_bias(table, CHUNK, WINDOW + CHUNK)
        a_out, r_out, state, k_buf, v_buf, shift_new = _prompt_mixer(x2d, shift_prev, state0, lw, bias, batch, seq)
    else:
        q, k, v, zr = _in_proj(x2d, lw['norm_mix_g'], lw['w_in'])
        bias = _rel_bias(table, seq, WINDOW + seq)
        a_out = _sample_attention(q, k, v, k_past.reshape(batch * WINDOW, KV_WIDTH),
                                  v_past.reshape(batch * WINDOW, KV_WIDTH), lw['attn_sink'], bias, seq)
        k_buf = jnp.concatenate([k_past, k.reshape(batch, seq, KV_WIDTH)], axis=1)[:, -WINDOW:]
        v_buf = jnp.concatenate([v_past, v.reshape(batch, seq, KV_WIDTH)], axis=1)[:, -WINDOW:]
        r_out, state = _rwkv_mixer(zr, shift_prev, state0, lw, batch, seq)
        shift_new = zr.reshape(batch, seq, RWKV_PROJ)[:, -1:]
    y = _tail(x2d, a_out, r_out, mk, mv, lw, batch, seq)
    kv_shape = (batch, WINDOW, KV_HEADS, HEAD_DIM)
    return y.reshape(x.shape), k_buf.reshape(kv_shape), v_buf.reshape(kv_shape), shift_new, state


def kernel(x_prompt, x_sample, mem_prompt, cache_attn_k, cache_attn_v, cache_mem_k, cache_mem_v, state_shift,
           state_wkv, norm_mix_g, w_in, attn_sink, rel_bias_table, rwkv_mu, rwkv_w0, rwkv_w2, rwkv_a0, rwkv_a2,
           rwkv_g2, rwkv_k_k, rwkv_k_a, rwkv_r_k, rwkv_ln_w, rwkv_ln_b, w_out, norm_cross_g, norm_mem_g, w_cq,
           w_mk, w_mv, w_co, norm_mlp_g, w_up, w_down, norm_final_g):
    assert norm_mix_g.shape[0] == 1, "single-layer trunk"
    bp, dec_b = x_prompt.shape[0], x_sample.shape[0]
    vec = lambda p: p[0].reshape(1, -1)
    lw = {
        'norm_mix_g': vec(norm_mix_g), 'w_in': w_in[0].astype(BF16), 'attn_sink': attn_sink[0],
        'rwkv_mu': rwkv_mu[0], 'rwkv_w0': rwkv_w0[0], 'rwkv_w2': rwkv_w2[0], 'rwkv_a0': rwkv_a0[0],
        'rwkv_a2': rwkv_a2[0], 'rwkv_g2': rwkv_g2[0], 'rwkv_k_k': rwkv_k_k[0], 'rwkv_k_a': rwkv_k_a[0],
        'rwkv_r_k': rwkv_r_k[0], 'rwkv_ln_w': rwkv_ln_w[0], 'rwkv_ln_b': rwkv_ln_b[0],
        'w_out': w_out[0].astype(BF16), 'norm_cross_g': vec(norm_cross_g), 'w_cq': w_cq[0].astype(BF16),
        'w_co': w_co[0].astype(BF16), 'norm_mlp_g': vec(norm_mlp_g), 'w_up': w_up[0].astype(BF16),
        'w_down': w_down[0].astype(BF16), 'norm_final_g': norm_final_g.reshape(1, -1),
    }
    mk, mv = _memory_kv(mem_prompt.reshape(bp * N_MEM, D_MODEL), vec(norm_mem_g),
                        w_mk[0].astype(BF16), w_mv[0].astype(BF16))
    mk = mk.reshape(bp, N_MEM, MEM_WIDTH)
    mv = mv.reshape(bp, N_MEM, MEM_WIDTH)
    shift0 = jnp.zeros((bp, 1, RWKV_PROJ), F32)
    wkv0 = jnp.zeros((bp, RWKV_HEADS, HEAD_DIM, HEAD_DIM), F32)
    yp, pk, pv, psh, pS = _trunk(x_prompt, mk, mv, None, None, shift0, wkv0, lw, rel_bias_table)
    ys, sk, sv, ssh, sS = _trunk(
        x_sample, cache_mem_k[0].reshape(dec_b, N_MEM, MEM_WIDTH), cache_mem_v[0].reshape(dec_b, N_MEM, MEM_WIDTH),
        cache_attn_k[0].reshape(dec_b, WINDOW, KV_WIDTH), cache_attn_v[0].reshape(dec_b, WINDOW, KV_WIDTH),
        state_shift[0], state_wkv[0], lw, rel_bias_table)
    mem_shape = (1, bp, N_MEM, MEM_HEADS, MEM_HEAD_DIM)
    return (yp, ys, pk[None], pv[None], mk.reshape(mem_shape), mv.reshape(mem_shape), psh[None], pS[None],
            sk[None], sv[None], ssh[None], sS[None])
```

```python
import functools
import math

import numpy as np
import jax
import jax.numpy as jnp
from jax import lax
from jax.experimental import pallas as pl
from jax.experimental.pallas import tpu as pltpu

F32 = jnp.float32
BF16 = jnp.bfloat16

D_MODEL = 1024
CHUNK = 64
WINDOW = 128
HEAD_DIM = 64
ATTN_WIDTH = 512
ATTN_HEADS = 8
KV_HEADS = 2
GROUP = 4
KV_WIDTH = 128
RWKV_WIDTH = 512
RWKV_HEADS = 8
DECAY_LORA = 64
AAA_LORA = 64
GATE_LORA = 128
RWKV_PROJ = 1792
IN_PROJ = 2560
N_MEM = 256
MEM_HEADS = 4
MEM_HEAD_DIM = 128
MEM_WIDTH = 512
D_FF = 4096
REL_BUCKETS = 32
REL_MAX_DIST = 128
NORM_EPS = 1e-6
GN_EPS = 64e-5
LOG2_E = math.log2(math.e)

V7X_VMEM_LIMIT_BYTES = 52 * 1024 * 1024
ROW_TILE = 256
IN_PROJ_ROW_TILE = 512
PROJ_TILE = 256
TAIL_ROW_TILE = 512
RWKV_SUB_CHUNKS = 8
RWKV_LOCAL_IN_FLIGHT = 2


def _params(*sem):
    return pltpu.CompilerParams(dimension_semantics=sem, vmem_limit_bytes=V7X_VMEM_LIMIT_BYTES)


def _const_spec(shape):
    nd = len(shape)
    return pl.BlockSpec(shape, lambda *_: (0,) * nd)


_NN = ((1,), (0,))
_NT = ((1,), (1,))
_TN = ((0,), (0,))


def _dg(a, b, dims):
    return lax.dot_general(a, b, (dims, ((), ())), preferred_element_type=F32)


def _dot(a, b):
    return _dg(a.astype(BF16), b.astype(BF16), _NN)


def _dot_nt(a, b):
    return _dg(a.astype(BF16), b.astype(BF16), _NT)


def _cat_rows(*xs):
    return jnp.concatenate(xs, axis=0)


def _cat_lanes(*xs):
    return jnp.concatenate(xs, axis=1)


def _run_interleaved(chains):
    active = list(chains)
    while active:
        still = []
        for ch in active:
            try:
                next(ch)
                still.append(ch)
            except StopIteration:
                pass
        active = still


def _run_tasks(tasks):
    finished = set()
    running = {}
    waiting = dict(tasks)
    while waiting or running:
        for name in [n for n, (_, deps) in waiting.items() if all(d in finished or d not in tasks for d in deps)]:
            running[name] = list(waiting.pop(name)[0]())
        for name in list(running):
            alive = []
            for ch in running[name]:
                try:
                    next(ch)
                    alive.append(ch)
                except StopIteration:
                    pass
            if alive:
                running[name] = alive
            else:
                del running[name]
                finished.add(name)


def _rms(x, g):
    return x * lax.rsqrt(jnp.mean(x * x, axis=-1, keepdims=True) + NORM_EPS) * g


def _inproj_kernel(x_ref, g_ref, w_ref, q_ref, k_ref, v_ref, zr_ref):
    h = _rms(x_ref[...], g_ref[...]).astype(BF16)
    q = jnp.dot(h, w_ref[:, :ATTN_WIDTH], preferred_element_type=F32)
    q_ref[...] = (q * (HEAD_DIM ** -0.5 * LOG2_E)).astype(BF16)
    k_ref[...] = jnp.dot(h, w_ref[:, ATTN_WIDTH:ATTN_WIDTH + KV_WIDTH], preferred_element_type=F32)
    v_ref[...] = jnp.dot(h, w_ref[:, ATTN_WIDTH + KV_WIDTH:ATTN_WIDTH + 2 * KV_WIDTH],
                         preferred_element_type=F32)
    zr_ref[...] = jnp.dot(h, w_ref[:, ATTN_WIDTH + 2 * KV_WIDTH:], preferred_element_type=F32)


def _in_proj(x2d, g, w_bf16):
    n = x2d.shape[0]
    tm = min(IN_PROJ_ROW_TILE, n)
    row = lambda w: pl.BlockSpec((tm, w), lambda i: (i, 0))
    return pl.pallas_call(
        _inproj_kernel,
        grid=(n // tm,),
        in_specs=[row(D_MODEL), _const_spec((1, D_MODEL)), _const_spec((D_MODEL, IN_PROJ))],
        out_specs=[row(ATTN_WIDTH), row(KV_WIDTH), row(KV_WIDTH), row(RWKV_PROJ)],
        out_shape=[jax.ShapeDtypeStruct((n, ATTN_WIDTH), BF16)]
                  + [jax.ShapeDtypeStruct((n, w), F32) for w in (KV_WIDTH, KV_WIDTH, RWKV_PROJ)],
        compiler_params=_params("parallel"),
        name="in_proj",
    )(x2d, g, w_bf16)


def _t5_bucket(rel):
    half = REL_BUCKETS // 2
    max_exact = half // 2
    assert REL_MAX_DIST == max_exact * 2 ** 4 and half - max_exact == 2 * 4
    n = np.abs(rel)
    large = max_exact + sum((n * n >= max_exact * max_exact * 2 ** t).astype(np.int64)
                            for t in range(1, half - max_exact))
    return (np.where(rel > 0, half, 0) + np.where(n < max_exact, n, large)).astype(np.int32)


def _bias_kernel(table_ref, bucket_ref, out_ref):
    bucket = bucket_ref[...]
    hits = [bucket == b for b in range(REL_BUCKETS)]
    for h in range(ATTN_HEADS):
        acc = jnp.zeros(bucket.shape, F32)
        for b in range(REL_BUCKETS):
            acc = jnp.where(hits[b], table_ref[b, h], acc)
        out_ref[h] = acc * LOG2_E


def _rel_bias(table, n_q, n_k):
    rel = np.arange(n_k)[None, :] - WINDOW - np.arange(n_q)[:, None]
    bucket = jnp.asarray(_t5_bucket(rel))
    bias = pl.pallas_call(
        _bias_kernel,
        in_specs=[pl.BlockSpec(memory_space=pltpu.SMEM), pl.BlockSpec(memory_space=pltpu.VMEM)],
        out_specs=pl.BlockSpec(memory_space=pltpu.VMEM),
        out_shape=jax.ShapeDtypeStruct((ATTN_HEADS, n_q, n_k), F32),
        name="rel_bias",
    )(table, bucket)
    return bias.reshape(KV_HEADS, GROUP * n_q, n_k)


def _group_sinks(sink_ref, n_q):
    row_group = lax.broadcasted_iota(jnp.int32, (GROUP * n_q, 1), 0) // n_q
    sinks = []
    for kvh in range(KV_HEADS):
        sink = jnp.zeros((GROUP * n_q, 1), F32)
        for g in range(GROUP):
            sink = jnp.where(row_group == g, sink_ref[kvh * GROUP + g] * LOG2_E, sink)
        sinks.append(sink)
    return sinks


def _attn_chain(q, keys, vals, bias, sink, valid, o_ref, rows, kvh):
    n_q = q.shape[0]
    qh = _cat_rows(*[q[:, (kvh * GROUP + g) * HEAD_DIM:(kvh * GROUP + g + 1) * HEAD_DIM]
                     for g in range(GROUP)])
    s = _dg(qh, keys, _NT) + bias
    if valid is not None:
        s = jnp.where(valid, s, -jnp.inf)
    yield
    m = jnp.maximum(jnp.max(s, axis=-1, keepdims=True), sink)
    p = jnp.exp2(s - m)
    den = jnp.sum(p, axis=-1, keepdims=True) + jnp.exp2(sink - m)
    yield
    o = _dg(p.astype(BF16), vals, _NN) * (1.0 / den)
    for g in range(GROUP):
        head = kvh * GROUP + g
        o_ref[rows, head * HEAD_DIM:(head + 1) * HEAD_DIM] = o[g * n_q:(g + 1) * n_q]
    yield


def _sample_attn_kernel(seq, sink_ref, q_ref, kp_ref, kn_ref, vp_ref, vn_ref, bias_ref, o_ref):
    batch = q_ref.shape[0] // seq
    sinks = _group_sinks(sink_ref, seq)
    chains = []
    for b in range(batch):
        rows = slice(b * seq, (b + 1) * seq)
        past = slice(b * WINDOW, (b + 1) * WINDOW)
        k_all = _cat_rows(kp_ref[past, :], kn_ref[rows, :]).astype(BF16)
        v_all = _cat_rows(vp_ref[past, :], vn_ref[rows, :]).astype(BF16)
        for kvh in range(KV_HEADS):
            lanes = slice(kvh * HEAD_DIM, (kvh + 1) * HEAD_DIM)
            chains.append(_attn_chain(q_ref[rows, :], k_all[:, lanes], v_all[:, lanes], bias_ref[kvh], sinks[kvh],
                                      None, o_ref, rows, kvh))
    _run_interleaved(chains)


def _sample_attention(q, k, v, k_past, v_past, sink, bias, seq):
    vmem = pl.BlockSpec(memory_space=pltpu.VMEM)
    return pl.pallas_call(
        functools.partial(_sample_attn_kernel, seq),
        in_specs=[pl.BlockSpec(memory_space=pltpu.SMEM)] + [vmem] * 6,
        out_specs=vmem,
        out_shape=jax.ShapeDtypeStruct(q.shape, F32),
        name="sample_attention",
    )(sink, q, k_past, k, v_past, v, bias)


def _split2(x):
    hi = x.astype(BF16)
    lo = (x - hi.astype(F32)).astype(BF16)
    return hi, lo


def _softplus(x):
    return jnp.maximum(x, 0.0) + jnp.log(1.0 + jnp.exp(-jnp.abs(x)))


def _sigmoid(x):
    return 1.0 / (1.0 + jnp.exp(-x))


def _rwkv_kernel(valid_rows, fused, steps, *refs):
    if fused:
        (xn_ref, x0_ref, gmix_ref, win_ref, shift_ref, s0_ref, mu_ref, w0_ref, w2_ref, a0_ref, a2_ref, g2_ref, kk_ref,
         ka_ref, rk_ref, lnw_ref, lnb_ref, seg_ref, tri_ref, sink_ref, bias_ref) = refs[:21]
        (out_ref, s_ref, attn_ref, ktail_ref, vtail_ref, shiftout_ref, carry_ref, y_ref, sbd_ref, zr_scr, q_scr, k_scr,
         v_scr, kp_ref, vp_ref) = refs[21:]
        g = pl.program_id(0)
        c = lax.rem(g, steps)
        cur = lax.rem(g, 2)
        zr_ref, q_ref, kc_ref, vc_ref = zr_scr.at[cur], q_scr.at[cur], k_scr.at[cur], v_scr.at[cur]
    else:
        (zr_ref, shift_ref, s0_ref, mu_ref, w0_ref, w2_ref, a0_ref, a2_ref, g2_ref, kk_ref, ka_ref, rk_ref, lnw_ref,
         lnb_ref, seg_ref, tri_ref) = refs[:16]
        out_ref, s_ref, carry_ref, y_ref, sbd_ref = refs[16:]
        c = pl.program_id(1)
    C = CHUNK
    R = out_ref.shape[0]
    n_sub = R // C
    pairs = range(RWKV_HEADS // 2)
    PAIR = 2 * HEAD_DIM
    W = RWKV_WIDTH

    @pl.when(c == 0)
    def _():
        carry_ref[0:1, :] = shift_ref[0]
        zero = jnp.zeros((HEAD_DIM, HEAD_DIM), F32)
        for p in pairs:
            sbd_ref[p] = _cat_rows(_cat_lanes(s0_ref[0, 2 * p], zero), _cat_lanes(zero, s0_ref[0, 2 * p + 1]))
        if fused:
            kp_ref[...] = jnp.zeros(kp_ref.shape, F32)
            vp_ref[...] = jnp.zeros(vp_ref.shape, F32)

    def proj_chains(x_ref, slot):
        h = _rms(x_ref[...], gmix_ref[...]).astype(BF16)
        q_end, k_end, v_end = ATTN_WIDTH, ATTN_WIDTH + KV_WIDTH, ATTN_WIDTH + 2 * KV_WIDTH

        def tile(lo, hi):
            z = jnp.dot(h, win_ref[:, lo:hi], preferred_element_type=F32)
            if hi <= q_end:
                q_scr[slot, :, lo:hi] = (z * (HEAD_DIM ** -0.5 * LOG2_E)).astype(BF16)
            elif lo == q_end:
                k_scr[slot] = z[:, :KV_WIDTH]
                v_scr[slot] = z[:, KV_WIDTH:]
            else:
                zr_scr[slot, :, lo - v_end:hi - v_end] = z
            yield

        assert q_end % PROJ_TILE == 0 and v_end - q_end == PROJ_TILE
        return [tile(lo, lo + PROJ_TILE) for lo in range(0, IN_PROJ, PROJ_TILE)]

    if fused:
        @pl.when(g == 0)
        def _():
            _run_interleaved(proj_chains(x0_ref, 0))

    seg = seg_ref[...]
    seg2 = _cat_rows(seg, seg)

    def head_sum(x):
        hi, lo = _split2(x)
        tiles = [_dg(_cat_lanes(hi[:, t * PAIR:(t + 1) * PAIR], lo[:, t * PAIR:(t + 1) * PAIR]), seg2, _NN)
                 for t in range(W // PAIR)]
        return _cat_lanes(*tiles)

    first_row = lax.broadcasted_iota(jnp.int32, (C, 1), 0) == 0
    tri3 = tri_ref[...]

    lane = lax.broadcasted_iota(jnp.int32, (C, PAIR), 1)
    trow = lax.broadcasted_iota(jnp.int32, (C, PAIR), 0)
    even = lane < HEAD_DIM
    tcol = jnp.where(even, lane, lane - HEAD_DIM)
    strict = tcol < trow
    incl = tcol <= trow
    eye = jnp.where(tcol == trow, 1.0, 0.0).astype(F32)
    brow = lax.broadcasted_iota(jnp.int32, (PAIR, PAIR), 0) < HEAD_DIM
    bcol = lax.broadcasted_iota(jnp.int32, (PAIR, PAIR), 1) < HEAD_DIM
    on_diag = brow == bcol

    def bd(x):
        zero = jnp.zeros_like(x)
        return _cat_rows(jnp.where(even, x, zero), jnp.where(even, zero, x))

    def bd2(pair):
        return bd(pair[0]), bd(pair[1])

    def mm(a_pair, w_pair, dims=_NN):
        if dims == _NT:
            w_pair = (w_pair[0].T, w_pair[1].T)
        first = _dg(_cat_lanes(a_pair[0], a_pair[1]), _cat_rows(w_pair[0], w_pair[0]), _NN)
        return first + _dg(a_pair[0], w_pair[1], _NN)

    prepped = {}
    ready = {}
    state = [sbd_ref[p] for p in pairs]

    def prep_chain(j):
        rows = slice(j * C, (j + 1) * C)
        zr = zr_ref[rows, :]
        before = carry_ref[0:1, :] if j == 0 else zr_ref[j * C - 1:j * C, :]
        z_prev = jnp.where(first_row, before, pltpu.roll(zr, 1, axis=0))
        zs = zr + (z_prev - zr) * mu_ref[...]
        r = zs[:, :W]
        k = zs[:, W:2 * W]
        v = zs[:, 2 * W:3 * W]
        wd = zs[:, 3 * W:3 * W + DECAY_LORA]
        ad = zs[:, 3 * W + DECAY_LORA:3 * W + DECAY_LORA + AAA_LORA]
        gd = zs[:, 3 * W + DECAY_LORA + AAA_LORA:]
        w_log = -_softplus(-(w0_ref[...] + _dot(jnp.tanh(wd), w2_ref[...]))) - 0.5
        lw = -jnp.exp(w_log)
        a = _sigmoid(a0_ref[...] + _dot(ad, a2_ref[...]))
        gate = _dot(_sigmoid(gd), g2_ref[...])
        kk = k * kk_ref[...]
        kk = kk * lax.rsqrt(jnp.maximum(head_sum(kk * kk), 1e-24))
        k2 = k * (1.0 + (a - 1.0) * ka_ref[...])
        if (j + 1) * C > valid_rows:
            live = lax.broadcasted_iota(jnp.int32, (C, 1), 0) < valid_rows - j * C
            lw = jnp.where(live, lw, 0.0)
            kk = jnp.where(live, kk, 0.0)
            k2 = jnp.where(live, k2, 0.0)
        bvec = kk * a
        yield
        l1 = lw.astype(BF16)
        rem = lw - l1.astype(F32)
        l2 = rem.astype(BF16)
        l3 = (rem - l2.astype(F32)).astype(BF16)
        sums = _dg(tri3, _cat_rows(l1, l2, l3), _NN)
        li = sums[:C]
        lrev = sums[C:]
        yield
        inv_p = jnp.exp(-li)
        to_end = jnp.exp(lrev)
        prepped[j] = dict(
            at=_split2(-kk * jnp.exp(li - lw)), rt=_split2(r * jnp.exp(li)), bt=_split2(bvec * inv_p),
            kt=_split2(k2 * inv_p), bh=_split2(bvec * to_end), kh=_split2(k2 * to_end), v=_split2(v),
            p_end=jnp.exp(li[C - 1:C, :]), bonus=head_sum(r * k2 * rk_ref[...]) * v, gate=gate)
        yield

    def local_chain(j, p):
        d = prepped[j]
        lanes = slice(p * PAIR, (p + 1) * PAIR)
        cut = lambda pair: (pair[0][:, lanes], pair[1][:, lanes])
        at_p, rt_p, bt_p, kt_p, bh_p, kh_p, v_p = map(cut, (d['at'], d['rt'], d['bt'], d['kt'], d['bh'], d['kh'],
                                                            d['v']))
        left = (_cat_rows(at_p[0], rt_p[0]), _cat_rows(at_p[1], rt_p[1]))
        right = (_cat_rows(bd(bt_p[0]), bd(kt_p[0])), _cat_rows(bd(bt_p[1]), bd(kt_p[1])))
        aa = mm(left, right, _NT)
        yield
        a_ab = jnp.where(strict, aa[:C, :PAIR], 0.0)
        a_ak = jnp.where(strict, aa[:C, PAIR:], 0.0)
        a_rb = jnp.where(incl, aa[C:, :PAIR], 0.0)
        a_rk = jnp.where(incl, aa[C:, PAIR:], 0.0)
        inv = eye + a_ab
        ps = _split2(a_ab)
        power = mm(ps, bd2(ps))
        span = 2
        yield
        while span < C:
            ps = _split2(power)
            pw = bd2(ps)
            ih = _split2(inv)
            if span * 2 < C:
                both = mm((_cat_rows(ih[0], ps[0]), _cat_rows(ih[1], ps[1])), pw)
                inv = inv + both[:C]
                power = both[C:]
            else:
                inv = inv + mm(ih, pw)
            span *= 2
            yield
        ready[(j, p)] = dict(inv=_split2(inv), akrk=_split2(_cat_rows(a_ak, a_rk)), rb=_split2(a_rb), left=left,
                             bhkh=(_cat_rows(bh_p[0], kh_p[0]), _cat_rows(bh_p[1], kh_p[1])), v=v_p,
                             p_end=d['p_end'][:, lanes])

    def state_chain(j, p):
        d = ready.pop((j, p))
        s_prev = state[p]
        v_hi, v_lo = d['v']
        both = mm(d['left'], _split2(s_prev), _NT) + mm(d['akrk'], (bd(v_hi), bd(v_lo)))
        rhs = both[:C]
        y0 = both[C:]
        yield
        u_pair = _split2(mm(d['inv'], bd2(_split2(rhs))))
        yield
        y_ref[j * C:(j + 1) * C, p * PAIR:(p + 1) * PAIR] = y0 + mm(d['rb'], bd2(u_pair))
        t_hi = _cat_rows(u_pair[0], v_hi)
        t_lo = _cat_rows(u_pair[1], v_lo)
        w_hi, w_lo = d['bhkh']
        upd = _dg(_cat_rows(t_hi, t_lo), _cat_rows(w_hi, w_hi), _TN) + _dg(t_hi, w_lo, _TN)
        state[p] = s_prev * d['p_end'] + jnp.where(on_diag, upd, 0.0)
        yield

    def post_chain(j):
        rows = slice(j * C, (j + 1) * C)
        d = prepped.pop(j)
        y = y_ref[rows, :]
        mean = head_sum(y) * (1.0 / HEAD_DIM)
        dev = y - mean
        yield
        var = head_sum(dev * dev) * (1.0 / HEAD_DIM)
        yn = dev * lax.rsqrt(var + GN_EPS) * lnw_ref[...] + lnb_ref[...]
        out_ref[rows, :] = (yn + d['bonus']) * d['gate']
        yield

    tasks = {}
    for j in range(n_sub):
        tasks[('prep', j)] = (lambda j=j: [prep_chain(j)], [('prep', j - 1), ('local', j - RWKV_LOCAL_IN_FLIGHT)])
        tasks[('local', j)] = (lambda j=j: [local_chain(j, p) for p in pairs],
                               [('prep', j), ('local', j - RWKV_LOCAL_IN_FLIGHT)])
        tasks[('state', j)] = (lambda j=j: [state_chain(j, p) for p in pairs], [('local', j), ('state', j - 1)])
        tasks[('post', j)] = (lambda j=j: [post_chain(j)], [('state', j)])
    if fused:
        n_k = WINDOW + CHUNK
        k_all = _cat_rows(kp_ref[...], kc_ref[...]).astype(BF16)
        v_all = _cat_rows(vp_ref[...], vc_ref[...]).astype(BF16)
        first_valid = jnp.where(c == 0, WINDOW, 0)
        kcol = lax.broadcasted_iota(jnp.int32, (1, n_k), 1)
        sinks = _group_sinks(sink_ref, CHUNK)

        def attn_chains(j):
            rows = slice(j * C, (j + 1) * C)
            keys = slice(j * C, j * C + n_k)
            valid = kcol + j * C >= first_valid if j * C < WINDOW else None
            return [_attn_chain(q_ref[rows, :], k_all[keys, kvh * HEAD_DIM:(kvh + 1) * HEAD_DIM],
                                v_all[keys, kvh * HEAD_DIM:(kvh + 1) * HEAD_DIM], bias_ref[kvh], sinks[kvh], valid,
                                attn_ref, rows, kvh) for kvh in range(KV_HEADS)]

        for j in range(n_sub):
            tasks[('attn', j)] = (lambda j=j: attn_chains(j), [('attn', j - 1)])
        for t, chain in enumerate(proj_chains(xn_ref, 1 - cur)):
            tasks[('proj', t)] = (lambda chain=chain: [chain], [('proj', t - 1), ('prep', min(t, n_sub - 1))])
    _run_tasks(tasks)

    carry_ref[0:1, :] = zr_ref[R - 1:R, :]
    for p in pairs:
        sbd_ref[p] = state[p]
    if fused:
        kp_ref[...] = kc_ref[R - WINDOW:R, :]
        vp_ref[...] = vc_ref[R - WINDOW:R, :]

    @pl.when(c == steps - 1)
    def _():
        for p in pairs:
            s_ref[0, 2 * p] = state[p][:HEAD_DIM, :HEAD_DIM]
            s_ref[0, 2 * p + 1] = state[p][HEAD_DIM:, HEAD_DIM:]
        if fused:
            ktail_ref[0] = kc_ref[R - WINDOW:R, :]
            vtail_ref[0] = vc_ref[R - WINDOW:R, :]
            shiftout_ref[0] = zr_ref[R - 1:R, :]


def _rwkv_operands(lw):
    seg = jnp.asarray(np.kron(np.eye(2), np.ones((HEAD_DIM, HEAD_DIM))), BF16)
    ones = np.ones((CHUNK, CHUNK))
    tri3 = jnp.asarray(np.concatenate([np.tile(np.tril(ones), (1, 3)), np.tile(np.triu(ones, 1), (1, 3))]), BF16)
    row = lambda name: lw[name].reshape(1, -1)
    return [row('rwkv_mu'), row('rwkv_w0'), lw['rwkv_w2'].astype(BF16), row('rwkv_a0'),
            lw['rwkv_a2'].astype(BF16), lw['rwkv_g2'].astype(BF16), row('rwkv_k_k'), row('rwkv_k_a'),
            row('rwkv_r_k'), row('rwkv_ln_w'), row('rwkv_ln_b'), seg, tri3]


_RWKV_SCRATCH = [pltpu.VMEM((8, RWKV_PROJ), F32), None,
                 pltpu.VMEM((RWKV_HEADS // 2, 2 * HEAD_DIM, 2 * HEAD_DIM), F32)]


def _rwkv_mixer(zr, shift_prev, state0, lw, batch, seq):
    assert seq <= CHUNK
    if seq < CHUNK:
        zr = jnp.pad(zr.reshape(batch, seq, RWKV_PROJ), ((0, 0), (0, CHUNK - seq), (0, 0))).reshape(-1, RWKV_PROJ)
    params = _rwkv_operands(lw)
    state_spec = pl.BlockSpec((1, RWKV_HEADS, HEAD_DIM, HEAD_DIM), lambda b, c: (b, 0, 0, 0))
    row_spec = lambda w: pl.BlockSpec((CHUNK, w), lambda b, c: (b, 0))
    scratch = list(_RWKV_SCRATCH)
    scratch[1] = pltpu.VMEM((CHUNK, RWKV_WIDTH), F32)
    out, state = pl.pallas_call(
        functools.partial(_rwkv_kernel, seq, False, 1),
        grid=(batch, 1),
        in_specs=[row_spec(RWKV_PROJ), pl.BlockSpec((1, 1, RWKV_PROJ), lambda b, c: (b, 0, 0)), state_spec]
                 + [_const_spec(p.shape) for p in params],
        out_specs=[row_spec(RWKV_WIDTH), state_spec],
        out_shape=[jax.ShapeDtypeStruct((batch * CHUNK, RWKV_WIDTH), F32), jax.ShapeDtypeStruct(state0.shape, F32)],
        scratch_shapes=scratch,
        compiler_params=_params("parallel", "arbitrary"),
        name="rwkv_mixer",
    )(zr, shift_prev, state0, *params)
    if seq < CHUNK:
        out = out.reshape(batch, CHUNK, RWKV_WIDTH)[:, :seq].reshape(batch * seq, RWKV_WIDTH)
    return out, state


def _prompt_mixer(x2d, shift_prev, state0, lw, bias, batch, seq):
    rows = CHUNK * RWKV_SUB_CHUNKS
    steps = seq // rows
    total = batch * steps
    params = _rwkv_operands(lw)
    seq_block = lambda shape: pl.BlockSpec((1,) + shape, lambda g: (g // steps,) + (0,) * len(shape))
    row_spec = lambda w: pl.BlockSpec((rows, w), lambda g: (g, 0))

    def resident(shape):
        nd = len(shape)
        return pl.BlockSpec(shape, lambda g: (0,) * nd, pipeline_mode=pl.Buffered(1))

    in_specs = [pl.BlockSpec((rows, D_MODEL), lambda g: (jnp.minimum(g + 1, total - 1), 0)),
                pl.BlockSpec((rows, D_MODEL), lambda g: (0, 0), pipeline_mode=pl.Buffered(1)),
                resident((1, D_MODEL)), resident((D_MODEL, IN_PROJ)),
                seq_block((1, RWKV_PROJ)), seq_block((RWKV_HEADS, HEAD_DIM, HEAD_DIM))]
    in_specs += [resident(p.shape) for p in params]
    in_specs += [pl.BlockSpec(memory_space=pltpu.SMEM), resident(bias.shape)]
    out_specs = [row_spec(RWKV_WIDTH), seq_block((RWKV_HEADS, HEAD_DIM, HEAD_DIM)), row_spec(ATTN_WIDTH),
                 seq_block((WINDOW, KV_WIDTH)), seq_block((WINDOW, KV_WIDTH)), seq_block((1, RWKV_PROJ))]
    n = batch * seq
    out_shape = [jax.ShapeDtypeStruct((n, RWKV_WIDTH), F32), jax.ShapeDtypeStruct(state0.shape, F32),
                 jax.ShapeDtypeStruct((n, ATTN_WIDTH), F32), jax.ShapeDtypeStruct((batch, WINDOW, KV_WIDTH), F32),
                 jax.ShapeDtypeStruct((batch, WINDOW, KV_WIDTH), F32),
                 jax.ShapeDtypeStruct((batch, 1, RWKV_PROJ), F32)]
    scratch = list(_RWKV_SCRATCH)
    scratch[1] = pltpu.VMEM((rows, RWKV_WIDTH), F32)
    scratch += [pltpu.VMEM((2, rows, RWKV_PROJ), F32), pltpu.VMEM((2, rows, ATTN_WIDTH), BF16),
                pltpu.VMEM((2, rows, KV_WIDTH), F32), pltpu.VMEM((2, rows, KV_WIDTH), F32),
                pltpu.VMEM((WINDOW, KV_WIDTH), F32), pltpu.VMEM((WINDOW, KV_WIDTH), F32)]
    r_out, state, a_out, k_tail, v_tail, shift_new = pl.pallas_call(
        functools.partial(_rwkv_kernel, rows, True, steps),
        grid=(total,),
        in_specs=in_specs,
        out_specs=out_specs,
        out_shape=out_shape,
        scratch_shapes=scratch,
        compiler_params=_params("arbitrary"),
        name="prompt_mixer",
    )(x2d, x2d, lw['norm_mix_g'], lw['w_in'], shift_prev, state0, *params, lw['attn_sink'], bias)
    return a_out, r_out, state, k_tail, v_tail, shift_new


def _memkv_kernel(m_ref, g_ref, wk_ref, wv_ref, k_ref, v_ref):
    mn = _rms(m_ref[...], g_ref[...]).astype(BF16)
    k_ref[...] = jnp.dot(mn, wk_ref[...], preferred_element_type=F32)
    v_ref[...] = jnp.dot(mn, wv_ref[...], preferred_element_type=F32)


def _memory_kv(mem2d, g, w_mk, w_mv):
    n = mem2d.shape[0]
    tm = min(ROW_TILE, n)
    row = lambda w: pl.BlockSpec((tm, w), lambda i: (i, 0))
    return pl.pallas_call(
        _memkv_kernel,
        grid=(n // tm,),
        in_specs=[row(D_MODEL), _const_spec((1, D_MODEL)), _const_spec(w_mk.shape), _const_spec(w_mv.shape)],
        out_specs=[row(MEM_WIDTH), row(MEM_WIDTH)],
        out_shape=[jax.ShapeDtypeStruct((n, MEM_WIDTH), F32)] * 2,
        compiler_params=_params("parallel"),
        name="memory_kv",
    )(mem2d, g, w_mk, w_mv)


def _tail_kernel(x_ref, a_ref, r_ref, mk_ref, mv_ref, wo_ref, gc_ref, wq_ref, wco_ref, gm_ref, wu_ref, wd_ref, gf_ref,
                 y_ref):
    n_seq = mk_ref.shape[0]
    rows_per_seq = x_ref.shape[0] // n_seq
    x1 = x_ref[...] + _dot(a_ref[...], wo_ref[:ATTN_WIDTH, :]) + _dot(r_ref[...], wo_ref[ATTN_WIDTH:, :])
    q = _dot(_rms(x1, gc_ref[...]), wq_ref[...]).astype(BF16)
    outs = {}

    def cross_chain(b, h):
        rows = slice(b * rows_per_seq, (b + 1) * rows_per_seq)
        sl = slice(h * MEM_HEAD_DIM, (h + 1) * MEM_HEAD_DIM)
        s = _dg(q[rows, sl], mk_ref[b, :, sl].astype(BF16), _NT) * (MEM_HEAD_DIM ** -0.5)
        yield
        p = jnp.exp(s - jnp.max(s, axis=-1, keepdims=True))
        den = jnp.sum(p, axis=-1, keepdims=True)
        yield
        outs[(b, h)] = _dg(p.astype(BF16), mv_ref[b, :, sl].astype(BF16), _NN) * (1.0 / den)
        yield

    _run_interleaved([cross_chain(b, h) for b in range(n_seq) for h in range(MEM_HEADS)])
    o = _cat_rows(*[_cat_lanes(*[outs[(b, h)] for h in range(MEM_HEADS)]) for b in range(n_seq)])
    x2 = x1 + _dot(o, wco_ref[...])
    up = _dot(_rms(x2, gm_ref[...]), wu_ref[...])
    act = jnp.square(jnp.maximum(up, 0.0))
    y_ref[...] = _rms(x2 + _dot(act, wd_ref[...]), gf_ref[...])


def _tail(x2d, a_out, r_out, mk, mv, lw, batch, seq):
    n = batch * seq
    tq = min(TAIL_ROW_TILE, n)
    if seq >= tq:
        assert seq % tq == 0
        seq_per_tile, tiles_per_seq = 1, seq // tq
        mem_spec = pl.BlockSpec((1, N_MEM, MEM_WIDTH), lambda i: (i // tiles_per_seq, 0, 0))
    else:
        assert tq % seq == 0
        seq_per_tile = tq // seq
        mem_spec = pl.BlockSpec((seq_per_tile, N_MEM, MEM_WIDTH), lambda i: (i, 0, 0))
    row = lambda w: pl.BlockSpec((tq, w), lambda i: (i, 0))

    def resident(shape):
        nd = len(shape)
        return pl.BlockSpec(shape, lambda *_: (0,) * nd, pipeline_mode=pl.Buffered(1))

    weights = [lw['w_out'], lw['norm_cross_g'], lw['w_cq'], lw['w_co'], lw['norm_mlp_g'], lw['w_up'], lw['w_down'],
               lw['norm_final_g']]
    return pl.pallas_call(
        _tail_kernel,
        grid=(n // tq,),
        in_specs=[row(D_MODEL), row(ATTN_WIDTH), row(RWKV_WIDTH), mem_spec, mem_spec]
                 + [resident(w.shape) for w in weights],
        out_specs=row(D_MODEL),
        out_shape=jax.ShapeDtypeStruct(x2d.shape, F32),
        compiler_params=_params("parallel"),
        name="tail",
    )(x2d, a_out, r_out, mk, mv, *weights)


def _trunk(x, mk, mv, k_past, v_past, shift_prev, state0, lw, table):
    batch, seq = x.shape[0], x.shape[1]
    x2d = x.reshape(batch * seq, D_MODEL)
    if k_past is None:
        bias = _rel_bias(table, CHUNK, WINDOW + CHUNK)
        a_out, r_out, state, k_buf, v_buf, shift_new = _prompt_mixer(x2d, shift_prev, state0, lw, bias, batch, seq)
    else:
        q, k, v, zr = _in_proj(x2d, lw['norm_mix_g'], lw['w_in'])
        bias = _rel_bias(table, seq, WINDOW + seq)
        a_out = _sample_attention(q, k, v, k_past.reshape(batch * WINDOW, KV_WIDTH),
                                  v_past.reshape(batch * WINDOW, KV_WIDTH), lw['attn_sink'], bias, seq)
        k_buf = jnp.concatenate([k_past, k.reshape(batch, seq, KV_WIDTH)], axis=1)[:, -WINDOW:]
        v_buf = jnp.concatenate([v_past, v.reshape(batch, seq, KV_WIDTH)], axis=1)[:, -WINDOW:]
        r_out, state = _rwkv_mixer(zr, shift_prev, state0, lw, batch, seq)
        shift_new = zr.reshape(batch, seq, RWKV_PROJ)[:, -1:]
    y = _tail(x2d, a_out, r_out, mk, mv, lw, batch, seq)
    kv_shape = (batch, WINDOW, KV_HEADS, HEAD_DIM)
    return y.reshape(x.shape), k_buf.reshape(kv_shape), v_buf.reshape(kv_shape), shift_new, state


def kernel(x_prompt, x_sample, mem_prompt, cache_attn_k, cache_attn_v, cache_mem_k, cache_mem_v, state_shift,
           state_wkv, norm_mix_g, w_in, attn_sink, rel_bias_table, rwkv_mu, rwkv_w0, rwkv_w2, rwkv_a0, rwkv_a2,
           rwkv_g2, rwkv_k_k, rwkv_k_a, rwkv_r_k, rwkv_ln_w, rwkv_ln_b, w_out, norm_cross_g, norm_mem_g, w_cq,
           w_mk, w_mv, w_co, norm_mlp_g, w_up, w_down, norm_final_g):
    assert norm_mix_g.shape[0] == 1, "single-layer trunk"
    bp, dec_b = x_prompt.shape[0], x_sample.shape[0]
    vec = lambda p: p[0].reshape(1, -1)
    lw = {
        'norm_mix_g': vec(norm_mix_g), 'w_in': w_in[0].astype(BF16), 'attn_sink': attn_sink[0],
        'rwkv_mu': rwkv_mu[0], 'rwkv_w0': rwkv_w0[0], 'rwkv_w2': rwkv_w2[0], 'rwkv_a0': rwkv_a0[0],
        'rwkv_a2': rwkv_a2[0], 'rwkv_g2': rwkv_g2[0], 'rwkv_k_k': rwkv_k_k[0], 'rwkv_k_a': rwkv_k_a[0],
        'rwkv_r_k': rwkv_r_k[0], 'rwkv_ln_w': rwkv_ln_w[0], 'rwkv_ln_b': rwkv_ln_b[0],
        'w_out': w_out[0].astype(BF16), 'norm_cross_g': vec(norm_cross_g), 'w_cq': w_cq[0].astype(BF16),
        'w_co': w_co[0].astype(BF16), 'norm_mlp_g': vec(norm_mlp_g), 'w_up': w_up[0].astype(BF16),
        'w_down': w_down[0].astype(BF16), 'norm_final_g': norm_final_g.reshape(1, -1),
    }
    mk, mv = _memory_kv(mem_prompt.reshape(bp * N_MEM, D_MODEL), vec(norm_mem_g),
                        w_mk[0].astype(BF16), w_mv[0].astype(BF16))
    mk = mk.reshape(bp, N_MEM, MEM_WIDTH)
    mv = mv.reshape(bp, N_MEM, MEM_WIDTH)
    shift0 = jnp.zeros((bp, 1, RWKV_PROJ), F32)
    wkv0 = jnp.zeros((bp, RWKV_HEADS, HEAD_DIM, HEAD_DIM), F32)
    yp, pk, pv, psh, pS = _trunk(x_prompt, mk, mv, None, None, shift0, wkv0, lw, rel_bias_table)
    ys, sk, sv, ssh, sS = _trunk(
        x_sample, cache_mem_k[0].reshape(dec_b, N_MEM, MEM_WIDTH), cache_mem_v[0].reshape(dec_b, N_MEM, MEM_WIDTH),
        cache_attn_k[0].reshape(dec_b, WINDOW, KV_WIDTH), cache_attn_v[0].reshape(dec_b, WINDOW, KV_WIDTH),
        state_shift[0], state_wkv[0], lw, rel_bias_table)
    mem_shape = (1, bp, N_MEM, MEM_HEADS, MEM_HEAD_DIM)
    return (yp, ys, pk[None], pv[None], mk.reshape(mem_shape), mv.reshape(mem_shape), psh[None], pS[None],
            sk[None], sv[None], ssh[None], sS[None])
```

```python
import functools
import math

import numpy as np
import jax
import jax.numpy as jnp
from jax import lax
from jax.experimental import pallas as pl
from jax.experimental.pallas import tpu as pltpu

F32 = jnp.float32
BF16 = jnp.bfloat16

D_MODEL = 1024
CHUNK = 64
WINDOW = 128
HEAD_DIM = 64
ATTN_WIDTH = 512
ATTN_HEADS = 8
KV_HEADS = 2
GROUP = 4
KV_WIDTH = 128
RWKV_WIDTH = 512
RWKV_HEADS = 8
DECAY_LORA = 64
AAA_LORA = 64
GATE_LORA = 128
RWKV_PROJ = 1792
IN_PROJ = 2560
N_MEM = 256
MEM_HEADS = 4
MEM_HEAD_DIM = 128
MEM_WIDTH = 512
D_FF = 4096
REL_BUCKETS = 32
REL_MAX_DIST = 128
NORM_EPS = 1e-6
GN_EPS = 64e-5
LOG2_E = math.log2(math.e)

V7X_VMEM_LIMIT_BYTES = 52 * 1024 * 1024
ROW_TILE = 256
IN_PROJ_ROW_TILE = 512
PROJ_TILE = 256
TAIL_ROW_TILE = 512
RWKV_SUB_CHUNKS = 4
RWKV_LOCAL_IN_FLIGHT = 3


def _params(*sem):
    return pltpu.CompilerParams(dimension_semantics=sem, vmem_limit_bytes=V7X_VMEM_LIMIT_BYTES)


def _const_spec(shape):
    nd = len(shape)
    return pl.BlockSpec(shape, lambda *_: (0,) * nd)


_NN = ((1,), (0,))
_NT = ((1,), (1,))
_TN = ((0,), (0,))


def _dg(a, b, dims):
    return lax.dot_general(a, b, (dims, ((), ())), preferred_element_type=F32)


def _dot(a, b):
    return _dg(a.astype(BF16), b.astype(BF16), _NN)


def _dot_nt(a, b):
    return _dg(a.astype(BF16), b.astype(BF16), _NT)


def _cat_rows(*xs):
    return jnp.concatenate(xs, axis=0)


def _cat_lanes(*xs):
    return jnp.concatenate(xs, axis=1)


def _run_interleaved(chains):
    active = list(chains)
    while active:
        still = []
        for ch in active:
            try:
                next(ch)
                still.append(ch)
            except StopIteration:
                pass
        active = still


def _run_tasks(tasks):
    finished = set()
    running = {}
    waiting = dict(tasks)
    while waiting or running:
        for name in [n for n, (_, deps) in waiting.items() if all(d in finished or d not in tasks for d in deps)]:
            running[name] = list(waiting.pop(name)[0]())
        for name in list(running):
            alive = []
            for ch in running[name]:
                try:
                    next(ch)
                    alive.append(ch)
                except StopIteration:
                    pass
            if alive:
                running[name] = alive
            else:
                del running[name]
                finished.add(name)


def _rms(x, g):
    return x * lax.rsqrt(jnp.mean(x * x, axis=-1, keepdims=True) + NORM_EPS) * g


def _inproj_kernel(x_ref, g_ref, w_ref, q_ref, k_ref, v_ref, zr_ref):
    h = _rms(x_ref[...], g_ref[...]).astype(BF16)
    q = jnp.dot(h, w_ref[:, :ATTN_WIDTH], preferred_element_type=F32)
    q_ref[...] = (q * (HEAD_DIM ** -0.5 * LOG2_E)).astype(BF16)
    k_ref[...] = jnp.dot(h, w_ref[:, ATTN_WIDTH:ATTN_WIDTH + KV_WIDTH], preferred_element_type=F32)
    v_ref[...] = jnp.dot(h, w_ref[:, ATTN_WIDTH + KV_WIDTH:ATTN_WIDTH + 2 * KV_WIDTH],
                         preferred_element_type=F32)
    zr_ref[...] = jnp.dot(h, w_ref[:, ATTN_WIDTH + 2 * KV_WIDTH:], preferred_element_type=F32)


def _in_proj(x2d, g, w_bf16):
    n = x2d.shape[0]
    tm = min(IN_PROJ_ROW_TILE, n)
    row = lambda w: pl.BlockSpec((tm, w), lambda i: (i, 0))
    return pl.pallas_call(
        _inproj_kernel,
        grid=(n // tm,),
        in_specs=[row(D_MODEL), _const_spec((1, D_MODEL)), _const_spec((D_MODEL, IN_PROJ))],
        out_specs=[row(ATTN_WIDTH), row(KV_WIDTH), row(KV_WIDTH), row(RWKV_PROJ)],
        out_shape=[jax.ShapeDtypeStruct((n, ATTN_WIDTH), BF16)]
                  + [jax.ShapeDtypeStruct((n, w), F32) for w in (KV_WIDTH, KV_WIDTH, RWKV_PROJ)],
        compiler_params=_params("parallel"),
        name="in_proj",
    )(x2d, g, w_bf16)


def _t5_bucket(rel):
    half = REL_BUCKETS // 2
    max_exact = half // 2
    assert REL_MAX_DIST == max_exact * 2 ** 4 and half - max_exact == 2 * 4
    n = np.abs(rel)
    large = max_exact + sum((n * n >= max_exact * max_exact * 2 ** t).astype(np.int64)
                            for t in range(1, half - max_exact))
    return (np.where(rel > 0, half, 0) + np.where(n < max_exact, n, large)).astype(np.int32)


def _bias_kernel(table_ref, bucket_ref, out_ref):
    bucket = bucket_ref[...]
    hits = [bucket == b for b in range(REL_BUCKETS)]
    for h in range(ATTN_HEADS):
        acc = jnp.zeros(bucket.shape, F32)
        for b in range(REL_BUCKETS):
            acc = jnp.where(hits[b], table_ref[b, h], acc)
        out_ref[h] = acc * LOG2_E


def _rel_bias(table, n_q, n_k):
    rel = np.arange(n_k)[None, :] - WINDOW - np.arange(n_q)[:, None]
    bucket = jnp.asarray(_t5_bucket(rel))
    bias = pl.pallas_call(
        _bias_kernel,
        in_specs=[pl.BlockSpec(memory_space=pltpu.SMEM), pl.BlockSpec(memory_space=pltpu.VMEM)],
        out_specs=pl.BlockSpec(memory_space=pltpu.VMEM),
        out_shape=jax.ShapeDtypeStruct((ATTN_HEADS, n_q, n_k), F32),
        name="rel_bias",
    )(table, bucket)
    return bias.reshape(KV_HEADS, GROUP * n_q, n_k)


def _group_sinks(sink_ref, n_q):
    row_group = lax.broadcasted_iota(jnp.int32, (GROUP * n_q, 1), 0) // n_q
    sinks = []
    for kvh in range(KV_HEADS):
        sink = jnp.zeros((GROUP * n_q, 1), F32)
        for g in range(GROUP):
            sink = jnp.where(row_group == g, sink_ref[kvh * GROUP + g] * LOG2_E, sink)
        sinks.append(sink)
    return sinks


def _attn_chain(q, keys, vals, bias, sink, valid, o_ref, rows, kvh):
    n_q = q.shape[0]
    qh = _cat_rows(*[q[:, (kvh * GROUP + g) * HEAD_DIM:(kvh * GROUP + g + 1) * HEAD_DIM]
                     for g in range(GROUP)])
    s = _dg(qh, keys, _NT) + bias
    if valid is not None:
        s = jnp.where(valid, s, -jnp.inf)
    yield
    m = jnp.maximum(jnp.max(s, axis=-1, keepdims=True), sink)
    p = jnp.exp2(s - m)
    den = jnp.sum(p, axis=-1, keepdims=True) + jnp.exp2(sink - m)
    yield
    o = _dg(p.astype(BF16), vals, _NN) * (1.0 / den)
    for g in range(GROUP):
        head = kvh * GROUP + g
        o_ref[rows, head * HEAD_DIM:(head + 1) * HEAD_DIM] = o[g * n_q:(g + 1) * n_q]
    yield


def _sample_attn_kernel(seq, sink_ref, q_ref, kp_ref, kn_ref, vp_ref, vn_ref, bias_ref, o_ref):
    batch = q_ref.shape[0] // seq
    sinks = _group_sinks(sink_ref, seq)
    chains = []
    for b in range(batch):
        rows = slice(b * seq, (b + 1) * seq)
        past = slice(b * WINDOW, (b + 1) * WINDOW)
        k_all = _cat_rows(kp_ref[past, :], kn_ref[rows, :]).astype(BF16)
        v_all = _cat_rows(vp_ref[past, :], vn_ref[rows, :]).astype(BF16)
        for kvh in range(KV_HEADS):
            lanes = slice(kvh * HEAD_DIM, (kvh + 1) * HEAD_DIM)
            chains.append(_attn_chain(q_ref[rows, :], k_all[:, lanes], v_all[:, lanes], bias_ref[kvh], sinks[kvh],
                                      None, o_ref, rows, kvh))
    _run_interleaved(chains)


def _sample_attention(q, k, v, k_past, v_past, sink, bias, seq):
    vmem = pl.BlockSpec(memory_space=pltpu.VMEM)
    return pl.pallas_call(
        functools.partial(_sample_attn_kernel, seq),
        in_specs=[pl.BlockSpec(memory_space=pltpu.SMEM)] + [vmem] * 6,
        out_specs=vmem,
        out_shape=jax.ShapeDtypeStruct(q.shape, F32),
        name="sample_attention",
    )(sink, q, k_past, k, v_past, v, bias)


def _split2(x):
    hi = x.astype(BF16)
    lo = (x - hi.astype(F32)).astype(BF16)
    return hi, lo


def _softplus(x):
    return jnp.maximum(x, 0.0) + jnp.log(1.0 + jnp.exp(-jnp.abs(x)))


def _sigmoid(x):
    return 1.0 / (1.0 + jnp.exp(-x))


def _rwkv_kernel(valid_rows, fused, steps, *refs):
    if fused:
        (xn_ref, x0_ref, gmix_ref, win_ref, shift_ref, s0_ref, mu_ref, w0_ref, w2_ref, a0_ref, a2_ref, g2_ref, kk_ref,
         ka_ref, rk_ref, lnw_ref, lnb_ref, seg_ref, tri_ref, sink_ref, bias_ref) = refs[:21]
        (out_ref, s_ref, attn_ref, ktail_ref, vtail_ref, shiftout_ref, carry_ref, y_ref, sbd_ref, zr_scr, q_scr, k_scr,
         v_scr, kp_ref, vp_ref) = refs[21:]
        g = pl.program_id(0)
        c = lax.rem(g, steps)
        cur = lax.rem(g, 2)
        zr_ref, q_ref, kc_ref, vc_ref = zr_scr.at[cur], q_scr.at[cur], k_scr.at[cur], v_scr.at[cur]
    else:
        (zr_ref, shift_ref, s0_ref, mu_ref, w0_ref, w2_ref, a0_ref, a2_ref, g2_ref, kk_ref, ka_ref, rk_ref, lnw_ref,
         lnb_ref, seg_ref, tri_ref) = refs[:16]
        out_ref, s_ref, carry_ref, y_ref, sbd_ref = refs[16:]
        c = pl.program_id(1)
    C = CHUNK
    R = out_ref.shape[0]
    n_sub = R // C
    pairs = range(RWKV_HEADS // 2)
    PAIR = 2 * HEAD_DIM
    W = RWKV_WIDTH

    @pl.when(c == 0)
    def _():
        carry_ref[0:1, :] = shift_ref[0]
        zero = jnp.zeros((HEAD_DIM, HEAD_DIM), F32)
        for p in pairs:
            sbd_ref[p] = _cat_rows(_cat_lanes(s0_ref[0, 2 * p], zero), _cat_lanes(zero, s0_ref[0, 2 * p + 1]))
        if fused:
            kp_ref[...] = jnp.zeros(kp_ref.shape, F32)
            vp_ref[...] = jnp.zeros(vp_ref.shape, F32)

    def proj_chains(x_ref, slot):
        h = _rms(x_ref[...], gmix_ref[...]).astype(BF16)
        q_end, k_end, v_end = ATTN_WIDTH, ATTN_WIDTH + KV_WIDTH, ATTN_WIDTH + 2 * KV_WIDTH

        def tile(lo, hi):
            z = jnp.dot(h, win_ref[:, lo:hi], preferred_element_type=F32)
            if hi <= q_end:
                q_scr[slot, :, lo:hi] = (z * (HEAD_DIM ** -0.5 * LOG2_E)).astype(BF16)
            elif lo == q_end:
                k_scr[slot] = z[:, :KV_WIDTH]
                v_scr[slot] = z[:, KV_WIDTH:]
            else:
                zr_scr[slot, :, lo - v_end:hi - v_end] = z
            yield

        assert q_end % PROJ_TILE == 0 and v_end - q_end == PROJ_TILE
        return [tile(lo, lo + PROJ_TILE) for lo in range(0, IN_PROJ, PROJ_TILE)]

    if fused:
        @pl.when(g == 0)
        def _():
            _run_interleaved(proj_chains(x0_ref, 0))

    seg = seg_ref[...]
    seg2 = _cat_rows(seg, seg)

    def head_sum(x):
        hi, lo = _split2(x)
        tiles = [_dg(_cat_lanes(hi[:, t * PAIR:(t + 1) * PAIR], lo[:, t * PAIR:(t + 1) * PAIR]), seg2, _NN)
                 for t in range(W // PAIR)]
        return _cat_lanes(*tiles)

    first_row = lax.broadcasted_iota(jnp.int32, (C, 1), 0) == 0
    tri3 = tri_ref[...]

    lane = lax.broadcasted_iota(jnp.int32, (C, PAIR), 1)
    trow = lax.broadcasted_iota(jnp.int32, (C, PAIR), 0)
    even = lane < HEAD_DIM
    tcol = jnp.where(even, lane, lane - HEAD_DIM)
    strict = tcol < trow
    incl = tcol <= trow
    eye = jnp.where(tcol == trow, 1.0, 0.0).astype(F32)
    brow = lax.broadcasted_iota(jnp.int32, (PAIR, PAIR), 0) < HEAD_DIM
    bcol = lax.broadcasted_iota(jnp.int32, (PAIR, PAIR), 1) < HEAD_DIM
    on_diag = brow == bcol

    def bd(x):
        zero = jnp.zeros_like(x)
        return _cat_rows(jnp.where(even, x, zero), jnp.where(even, zero, x))

    def bd2(pair):
        return bd(pair[0]), bd(pair[1])

    def mm(a_pair, w_pair, dims=_NN):
        if dims == _NT:
            w_pair = (w_pair[0].T, w_pair[1].T)
        first = _dg(_cat_lanes(a_pair[0], a_pair[1]), _cat_rows(w_pair[0], w_pair[0]), _NN)
        return first + _dg(a_pair[0], w_pair[1], _NN)

    prepped = {}
    ready = {}
    state = [sbd_ref[p] for p in pairs]

    def prep_chain(j):
        rows = slice(j * C, (j + 1) * C)
        zr = zr_ref[rows, :]
        before = carry_ref[0:1, :] if j == 0 else zr_ref[j * C - 1:j * C, :]
        z_prev = jnp.where(first_row, before, pltpu.roll(zr, 1, axis=0))
        zs = zr + (z_prev - zr) * mu_ref[...]
        r = zs[:, :W]
        k = zs[:, W:2 * W]
        v = zs[:, 2 * W:3 * W]
        wd = zs[:, 3 * W:3 * W + DECAY_LORA]
        ad = zs[:, 3 * W + DECAY_LORA:3 * W + DECAY_LORA + AAA_LORA]
        gd = zs[:, 3 * W + DECAY_LORA + AAA_LORA:]
        w_log = -_softplus(-(w0_ref[...] + _dot(jnp.tanh(wd), w2_ref[...]))) - 0.5
        lw = -jnp.exp(w_log)
        a = _sigmoid(a0_ref[...] + _dot(ad, a2_ref[...]))
        gate = _dot(_sigmoid(gd), g2_ref[...])
        kk = k * kk_ref[...]
        kk = kk * lax.rsqrt(jnp.maximum(head_sum(kk * kk), 1e-24))
        k2 = k * (1.0 + (a - 1.0) * ka_ref[...])
        if (j + 1) * C > valid_rows:
            live = lax.broadcasted_iota(jnp.int32, (C, 1), 0) < valid_rows - j * C
            lw = jnp.where(live, lw, 0.0)
            kk = jnp.where(live, kk, 0.0)
            k2 = jnp.where(live, k2, 0.0)
        bvec = kk * a
        yield
        l1 = lw.astype(BF16)
        rem = lw - l1.astype(F32)
        l2 = rem.astype(BF16)
        l3 = (rem - l2.astype(F32)).astype(BF16)
        sums = _dg(tri3, _cat_rows(l1, l2, l3), _NN)
        li = sums[:C]
        lrev = sums[C:]
        yield
        inv_p = jnp.exp(-li)
        to_end = jnp.exp(lrev)
        prepped[j] = dict(
            at=_split2(-kk * jnp.exp(li - lw)), rt=_split2(r * jnp.exp(li)), bt=_split2(bvec * inv_p),
            kt=_split2(k2 * inv_p), bh=_split2(bvec * to_end), kh=_split2(k2 * to_end), v=_split2(v),
            p_end=jnp.exp(li[C - 1:C, :]), bonus=head_sum(r * k2 * rk_ref[...]) * v, gate=gate)
        yield

    def local_chain(j, p):
        d = prepped[j]
        lanes = slice(p * PAIR, (p + 1) * PAIR)
        cut = lambda pair: (pair[0][:, lanes], pair[1][:, lanes])
        at_p, rt_p, bt_p, kt_p, bh_p, kh_p, v_p = map(cut, (d['at'], d['rt'], d['bt'], d['kt'], d['bh'], d['kh'],
                                                            d['v']))
        left = (_cat_rows(at_p[0], rt_p[0]), _cat_rows(at_p[1], rt_p[1]))
        right = (_cat_rows(bd(bt_p[0]), bd(kt_p[0])), _cat_rows(bd(bt_p[1]), bd(kt_p[1])))
        aa = mm(left, right, _NT)
        yield
        a_ab = jnp.where(strict, aa[:C, :PAIR], 0.0)
        a_ak = jnp.where(strict, aa[:C, PAIR:], 0.0)
        a_rb = jnp.where(incl, aa[C:, :PAIR], 0.0)
        a_rk = jnp.where(incl, aa[C:, PAIR:], 0.0)
        inv = eye + a_ab
        ps = _split2(a_ab)
        power = mm(ps, bd2(ps))
        span = 2
        yield
        while span < C:
            ps = _split2(power)
            pw = bd2(ps)
            ih = _split2(inv)
            if span * 2 < C:
                both = mm((_cat_rows(ih[0], ps[0]), _cat_rows(ih[1], ps[1])), pw)
                inv = inv + both[:C]
                power = both[C:]
            else:
                inv = inv + mm(ih, pw)
            span *= 2
            yield
        ready[(j, p)] = dict(inv=_split2(inv), akrk=_split2(_cat_rows(a_ak, a_rk)), rb=_split2(a_rb), left=left,
                             bhkh=(_cat_rows(bh_p[0], kh_p[0]), _cat_rows(bh_p[1], kh_p[1])), v=v_p,
                             p_end=d['p_end'][:, lanes])

    def state_chain(j, p):
        d = ready.pop((j, p))
        s_prev = state[p]
        v_hi, v_lo = d['v']
        both = mm(d['left'], _split2(s_prev), _NT) + mm(d['akrk'], (bd(v_hi), bd(v_lo)))
        rhs = both[:C]
        y0 = both[C:]
        yield
        u_pair = _split2(mm(d['inv'], bd2(_split2(rhs))))
        yield
        y_ref[j * C:(j + 1) * C, p * PAIR:(p + 1) * PAIR] = y0 + mm(d['rb'], bd2(u_pair))
        t_hi = _cat_rows(u_pair[0], v_hi)
        t_lo = _cat_rows(u_pair[1], v_lo)
        w_hi, w_lo = d['bhkh']
        upd = _dg(_cat_rows(t_hi, t_lo), _cat_rows(w_hi, w_hi), _TN) + _dg(t_hi, w_lo, _TN)
        state[p] = s_prev * d['p_end'] + jnp.where(on_diag, upd, 0.0)
        yield

    def post_chain(j):
        rows = slice(j * C, (j + 1) * C)
        d = prepped.pop(j)
        y = y_ref[rows, :]
        mean = head_sum(y) * (1.0 / HEAD_DIM)
        dev = y - mean
        yield
        var = head_sum(dev * dev) * (1.0 / HEAD_DIM)
        yn = dev * lax.rsqrt(var + GN_EPS) * lnw_ref[...] + lnb_ref[...]
        out_ref[rows, :] = (yn + d['bonus']) * d['gate']
        yield

    tasks = {}
    for j in range(n_sub):
        tasks[('prep', j)] = (lambda j=j: [prep_chain(j)], [('prep', j - 1), ('local', j - RWKV_LOCAL_IN_FLIGHT)])
        tasks[('local', j)] = (lambda j=j: [local_chain(j, p) for p in pairs],
                               [('prep', j), ('local', j - RWKV_LOCAL_IN_FLIGHT)])
        tasks[('state', j)] = (lambda j=j: [state_chain(j, p) for p in pairs], [('local', j), ('state', j - 1)])
        tasks[('post', j)] = (lambda j=j: [post_chain(j)], [('state', j)])
    if fused:
        n_k = WINDOW + CHUNK
        k_all = _cat_rows(kp_ref[...], kc_ref[...]).astype(BF16)
        v_all = _cat_rows(vp_ref[...], vc_ref[...]).astype(BF16)
        first_valid = jnp.where(c == 0, WINDOW, 0)
        kcol = lax.broadcasted_iota(jnp.int32, (1, n_k), 1)
        sinks = _group_sinks(sink_ref, CHUNK)

        def attn_chains(j):
            rows = slice(j * C, (j + 1) * C)
            keys = slice(j * C, j * C + n_k)
            valid = kcol + j * C >= first_valid if j * C < WINDOW else None
            return [_attn_chain(q_ref[rows, :], k_all[keys, kvh * HEAD_DIM:(kvh + 1) * HEAD_DIM],
                                v_all[keys, kvh * HEAD_DIM:(kvh + 1) * HEAD_DIM], bias_ref[kvh], sinks[kvh], valid,
                                attn_ref, rows, kvh) for kvh in range(KV_HEADS)]

        for j in range(n_sub):
            tasks[('attn', j)] = (lambda j=j: attn_chains(j), [('attn', j - 1)])
        for t, chain in enumerate(proj_chains(xn_ref, 1 - cur)):
            tasks[('proj', t)] = (lambda chain=chain: [chain], [('proj', t - 1), ('prep', min(t, n_sub - 1))])
    _run_tasks(tasks)

    carry_ref[0:1, :] = zr_ref[R - 1:R, :]
    for p in pairs:
        sbd_ref[p] = state[p]
    if fused:
        kp_ref[...] = kc_ref[R - WINDOW:R, :]
        vp_ref[...] = vc_ref[R - WINDOW:R, :]

    @pl.when(c == steps - 1)
    def _():
        for p in pairs:
            s_ref[0, 2 * p] = state[p][:HEAD_DIM, :HEAD_DIM]
            s_ref[0, 2 * p + 1] = state[p][HEAD_DIM:, HEAD_DIM:]
        if fused:
            ktail_ref[0] = kc_ref[R - WINDOW:R, :]
            vtail_ref[0] = vc_ref[R - WINDOW:R, :]
            shiftout_ref[0] = zr_ref[R - 1:R, :]


def _rwkv_operands(lw):
    seg = jnp.asarray(np.kron(np.eye(2), np.ones((HEAD_DIM, HEAD_DIM))), BF16)
    ones = np.ones((CHUNK, CHUNK))
    tri3 = jnp.asarray(np.concatenate([np.tile(np.tril(ones), (1, 3)), np.tile(np.triu(ones, 1), (1, 3))]), BF16)
    row = lambda name: lw[name].reshape(1, -1)
    return [row('rwkv_mu'), row('rwkv_w0'), lw['rwkv_w2'].astype(BF16), row('rwkv_a0'),
            lw['rwkv_a2'].astype(BF16), lw['rwkv_g2'].astype(BF16), row('rwkv_k_k'), row('rwkv_k_a'),
            row('rwkv_r_k'), row('rwkv_ln_w'), row('rwkv_ln_b'), seg, tri3]


_RWKV_SCRATCH = [pltpu.VMEM((8, RWKV_PROJ), F32), None,
                 pltpu.VMEM((RWKV_HEADS // 2, 2 * HEAD_DIM, 2 * HEAD_DIM), F32)]


def _rwkv_mixer(zr, shift_prev, state0, lw, batch, seq):
    assert seq <= CHUNK
    if seq < CHUNK:
        zr = jnp.pad(zr.reshape(batch, seq, RWKV_PROJ), ((0, 0), (0, CHUNK - seq), (0, 0))).reshape(-1, RWKV_PROJ)
    params = _rwkv_operands(lw)
    state_spec = pl.BlockSpec((1, RWKV_HEADS, HEAD_DIM, HEAD_DIM), lambda b, c: (b, 0, 0, 0))
    row_spec = lambda w: pl.BlockSpec((CHUNK, w), lambda b, c: (b, 0))
    scratch = list(_RWKV_SCRATCH)
    scratch[1] = pltpu.VMEM((CHUNK, RWKV_WIDTH), F32)
    out, state = pl.pallas_call(
        functools.partial(_rwkv_kernel, seq, False, 1),
        grid=(batch, 1),
        in_specs=[row_spec(RWKV_PROJ), pl.BlockSpec((1, 1, RWKV_PROJ), lambda b, c: (b, 0, 0)), state_spec]
                 + [_const_spec(p.shape) for p in params],
        out_specs=[row_spec(RWKV_WIDTH), state_spec],
        out_shape=[jax.ShapeDtypeStruct((batch * CHUNK, RWKV_WIDTH), F32), jax.ShapeDtypeStruct(state0.shape, F32)],
        scratch_shapes=scratch,
        compiler_params=_params("parallel", "arbitrary"),
        name="rwkv_mixer",
    )(zr, shift_prev, state0, *params)
    if seq < CHUNK:
        out = out.reshape(batch, CHUNK, RWKV_WIDTH)[:, :seq].reshape(batch * seq, RWKV_WIDTH)
    return out, state


def _prompt_mixer(x2d, shift_prev, state0, lw, bias, batch, seq):
    rows = CHUNK * RWKV_SUB_CHUNKS
    steps = seq // rows
    total = batch * steps
    params = _rwkv_operands(lw)
    seq_block = lambda shape: pl.BlockSpec((1,) + shape, lambda g: (g // steps,) + (0,) * len(shape))
    row_spec = lambda w: pl.BlockSpec((rows, w), lambda g: (g, 0))

    def resident(shape):
        nd = len(shape)
        return pl.BlockSpec(shape, lambda g: (0,) * nd, pipeline_mode=pl.Buffered(1))

    in_specs = [pl.BlockSpec((rows, D_MODEL), lambda g: (jnp.minimum(g + 1, total - 1), 0)),
                pl.BlockSpec((rows, D_MODEL), lambda g: (0, 0), pipeline_mode=pl.Buffered(1)),
                resident((1, D_MODEL)), resident((D_MODEL, IN_PROJ)),
                seq_block((1, RWKV_PROJ)), seq_block((RWKV_HEADS, HEAD_DIM, HEAD_DIM))]
    in_specs += [resident(p.shape) for p in params]
    in_specs += [pl.BlockSpec(memory_space=pltpu.SMEM), resident(bias.shape)]
    out_specs = [row_spec(RWKV_WIDTH), seq_block((RWKV_HEADS, HEAD_DIM, HEAD_DIM)), row_spec(ATTN_WIDTH),
                 seq_block((WINDOW, KV_WIDTH)), seq_block((WINDOW, KV_WIDTH)), seq_block((1, RWKV_PROJ))]
    n = batch * seq
    out_shape = [jax.ShapeDtypeStruct((n, RWKV_WIDTH), F32), jax.ShapeDtypeStruct(state0.shape, F32),
                 jax.ShapeDtypeStruct((n, ATTN_WIDTH), F32), jax.ShapeDtypeStruct((batch, WINDOW, KV_WIDTH), F32),
                 jax.ShapeDtypeStruct((batch, WINDOW, KV_WIDTH), F32),
                 jax.ShapeDtypeStruct((batch, 1, RWKV_PROJ), F32)]
    scratch = list(_RWKV_SCRATCH)
    scratch[1] = pltpu.VMEM((rows, RWKV_WIDTH), F32)
    scratch += [pltpu.VMEM((2, rows, RWKV_PROJ), F32), pltpu.VMEM((2, rows, ATTN_WIDTH), BF16),
                pltpu.VMEM((2, rows, KV_WIDTH), F32), pltpu.VMEM((2, rows, KV_WIDTH), F32),
                pltpu.VMEM((WINDOW, KV_WIDTH), F32), pltpu.VMEM((WINDOW, KV_WIDTH), F32)]
    r_out, state, a_out, k_tail, v_tail, shift_new = pl.pallas_call(
        functools.partial(_rwkv_kernel, rows, True, steps),
        grid=(total,),
        in_specs=in_specs,
        out_specs=out_specs,
        out_shape=out_shape,
        scratch_shapes=scratch,
        compiler_params=_params("arbitrary"),
        name="prompt_mixer",
    )(x2d, x2d, lw['norm_mix_g'], lw['w_in'], shift_prev, state0, *params, lw['attn_sink'], bias)
    return a_out, r_out, state, k_tail, v_tail, shift_new


def _memkv_kernel(m_ref, g_ref, wk_ref, wv_ref, k_ref, v_ref):
    mn = _rms(m_ref[...], g_ref[...]).astype(BF16)
    k_ref[...] = jnp.dot(mn, wk_ref[...], preferred_element_type=F32)
    v_ref[...] = jnp.dot(mn, wv_ref[...], preferred_element_type=F32)


def _memory_kv(mem2d, g, w_mk, w_mv):
    n = mem2d.shape[0]
    tm = min(ROW_TILE, n)
    row = lambda w: pl.BlockSpec((tm, w), lambda i: (i, 0))
    return pl.pallas_call(
        _memkv_kernel,
        grid=(n // tm,),
        in_specs=[row(D_MODEL), _const_spec((1, D_MODEL)), _const_spec(w_mk.shape), _const_spec(w_mv.shape)],
        out_specs=[row(MEM_WIDTH), row(MEM_WIDTH)],
        out_shape=[jax.ShapeDtypeStruct((n, MEM_WIDTH), F32)] * 2,
        compiler_params=_params("parallel"),
        name="memory_kv",
    )(mem2d, g, w_mk, w_mv)


def _tail_kernel(x_ref, a_ref, r_ref, mk_ref, mv_ref, wo_ref, gc_ref, wq_ref, wco_ref, gm_ref, wu_ref, wd_ref, gf_ref,
                 y_ref):
    n_seq = mk_ref.shape[0]
    rows_per_seq = x_ref.shape[0] // n_seq
    x1 = x_ref[...] + _dot(a_ref[...], wo_ref[:ATTN_WIDTH, :]) + _dot(r_ref[...], wo_ref[ATTN_WIDTH:, :])
    q = _dot(_rms(x1, gc_ref[...]), wq_ref[...]).astype(BF16)
    outs = {}

    def cross_chain(b, h):
        rows = slice(b * rows_per_seq, (b + 1) * rows_per_seq)
        sl = slice(h * MEM_HEAD_DIM, (h + 1) * MEM_HEAD_DIM)
        s = _dg(q[rows, sl], mk_ref[b, :, sl].astype(BF16), _NT) * (MEM_HEAD_DIM ** -0.5)
        yield
        p = jnp.exp(s - jnp.max(s, axis=-1, keepdims=True))
        den = jnp.sum(p, axis=-1, keepdims=True)
        yield
        outs[(b, h)] = _dg(p.astype(BF16), mv_ref[b, :, sl].astype(BF16), _NN) * (1.0 / den)
        yield

    _run_interleaved([cross_chain(b, h) for b in range(n_seq) for h in range(MEM_HEADS)])
    o = _cat_rows(*[_cat_lanes(*[outs[(b, h)] for h in range(MEM_HEADS)]) for b in range(n_seq)])
    x2 = x1 + _dot(o, wco_ref[...])
    up = _dot(_rms(x2, gm_ref[...]), wu_ref[...])
    act = jnp.square(jnp.maximum(up, 0.0))
    y_ref[...] = _rms(x2 + _dot(act, wd_ref[...]), gf_ref[...])


def _tail(x2d, a_out, r_out, mk, mv, lw, batch, seq):
    n = batch * seq
    tq = min(TAIL_ROW_TILE, n)
    if seq >= tq:
        assert seq % tq == 0
        seq_per_tile, tiles_per_seq = 1, seq // tq
        mem_spec = pl.BlockSpec((1, N_MEM, MEM_WIDTH), lambda i: (i // tiles_per_seq, 0, 0))
    else:
        assert tq % seq == 0
        seq_per_tile = tq // seq
        mem_spec = pl.BlockSpec((seq_per_tile, N_MEM, MEM_WIDTH), lambda i: (i, 0, 0))
    row = lambda w: pl.BlockSpec((tq, w), lambda i: (i, 0))

    def resident(shape):
        nd = len(shape)
        return pl.BlockSpec(shape, lambda *_: (0,) * nd, pipeline_mode=pl.Buffered(1))

    weights = [lw['w_out'], lw['norm_cross_g'], lw['w_cq'], lw['w_co'], lw['norm_mlp_g'], lw['w_up'], lw['w_down'],
               lw['norm_final_g']]
    return pl.pallas_call(
        _tail_kernel,
        grid=(n // tq,),
        in_specs=[row(D_MODEL), row(ATTN_WIDTH), row(RWKV_WIDTH), mem_spec, mem_spec]
                 + [resident(w.shape) for w in weights],
        out_specs=row(D_MODEL),
        out_shape=jax.ShapeDtypeStruct(x2d.shape, F32),
        compiler_params=_params("parallel"),
        name="tail",
    )(x2d, a_out, r_out, mk, mv, *weights)


def _trunk(x, mk, mv, k_past, v_past, shift_prev, state0, lw, table):
    batch, seq = x.shape[0], x.shape[1]
    x2d = x.reshape(batch * seq, D_MODEL)
    if k_past is None:
        bias = _rel_bias(table, CHUNK, WINDOW + CHUNK)
        a_out, r_out, state, k_buf, v_buf, shift_new = _prompt_mixer(x2d, shift_prev, state0, lw, bias, batch, seq)
    else:
        q, k, v, zr = _in_proj(x2d, lw['norm_mix_g'], lw['w_in'])
        bias = _rel_bias(table, seq, WINDOW + seq)
        a_out = _sample_attention(q, k, v, k_past.reshape(batch * WINDOW, KV_WIDTH),
                                  v_past.reshape(batch * WINDOW, KV_WIDTH), lw['attn_sink'], bias, seq)
        k_buf = jnp.concatenate([k_past, k.reshape(batch, seq, KV_WIDTH)], axis=1)[:, -WINDOW:]
        v_buf = jnp.concatenate([v_past, v.reshape(batch, seq, KV_WIDTH)], axis=1)[:, -WINDOW:]
        r_out, state = _rwkv_mixer(zr, shift_prev, state0, lw, batch, seq)
        shift_new = zr.reshape(batch, seq, RWKV_PROJ)[:, -1:]
    y = _tail(x2d, a_out, r_out, mk, mv, lw, batch, seq)
    kv_shape = (batch, WINDOW, KV_HEADS, HEAD_DIM)
    return y.reshape(x.shape), k_buf.reshape(kv_shape), v_buf.reshape(kv_shape), shift_new, state


def kernel(x_prompt, x_sample, mem_prompt, cache_attn_k, cache_attn_v, cache_mem_k, cache_mem_v, state_shift,
           state_wkv, norm_mix_g, w_in, attn_sink, rel_bias_table, rwkv_mu, rwkv_w0, rwkv_w2, rwkv_a0, rwkv_a2,
           rwkv_g2, rwkv_k_k, rwkv_k_a, rwkv_r_k, rwkv_ln_w, rwkv_ln_b, w_out, norm_cross_g, norm_mem_g, w_cq,
           w_mk, w_mv, w_co, norm_mlp_g, w_up, w_down, norm_final_g):
    assert norm_mix_g.shape[0] == 1, "single-layer trunk"
    bp, dec_b = x_prompt.shape[0], x_sample.shape[0]
    vec = lambda p: p[0].reshape(1, -1)
    lw = {
        'norm_mix_g': vec(norm_mix_g), 'w_in': w_in[0].astype(BF16), 'attn_sink': attn_sink[0],
        'rwkv_mu': rwkv_mu[0], 'rwkv_w0': rwkv_w0[0], 'rwkv_w2': rwkv_w2[0], 'rwkv_a0': rwkv_a0[0],
        'rwkv_a2': rwkv_a2[0], 'rwkv_g2': rwkv_g2[0], 'rwkv_k_k': rwkv_k_k[0], 'rwkv_k_a': rwkv_k_a[0],
        'rwkv_r_k': rwkv_r_k[0], 'rwkv_ln_w': rwkv_ln_w[0], 'rwkv_ln_b': rwkv_ln_b[0],
        'w_out': w_out[0].astype(BF16), 'norm_cross_g': vec(norm_cross_g), 'w_cq': w_cq[0].astype(BF16),
        'w_co': w_co[0].astype(BF16), 'norm_mlp_g': vec(norm_mlp_g), 'w_up': w_up[0].astype(BF16),
        'w_down': w_down[0].astype(BF16), 'norm_final_g': norm_final_g.reshape(1, -1),
    }
    mk, mv = _memory_kv(mem_prompt.reshape(bp * N_MEM, D_MODEL), vec(norm_mem_g),
                        w_mk[0].astype(BF16), w_mv[0].astype(BF16))
    mk = mk.reshape(bp, N_MEM, MEM_WIDTH)
    mv = mv.reshape(bp, N_MEM, MEM_WIDTH)
    shift0 = jnp.zeros((bp, 1, RWKV_PROJ), F32)
    wkv0 = jnp.zeros((bp, RWKV_HEADS, HEAD_DIM, HEAD_DIM), F32)
    yp, pk, pv, psh, pS = _trunk(x_prompt, mk, mv, None, None, shift0, wkv0, lw, rel_bias_table)
    ys, sk, sv, ssh, sS = _trunk(
        x_sample, cache_mem_k[0].reshape(dec_b, N_MEM, MEM_WIDTH), cache_mem_v[0].reshape(dec_b, N_MEM, MEM_WIDTH),
        cache_attn_k[0].reshape(dec_b, WINDOW, KV_WIDTH), cache_attn_v[0].reshape(dec_b, WINDOW, KV_WIDTH),
        state_shift[0], state_wkv[0], lw, rel_bias_table)
    mem_shape = (1, bp, N_MEM, MEM_HEADS, MEM_HEAD_DIM)
    return (yp, ys, pk[None], pv[None], mk.reshape(mem_shape), mv.reshape(mem_shape), psh[None], pS[None],
            sk[None], sv[None], ssh[None], sS[None])
```

```python
import functools
import math

import numpy as np
import jax
import jax.numpy as jnp
from jax import lax
from jax.experimental import pallas as pl
from jax.experimental.pallas import tpu as pltpu

F32 = jnp.float32
BF16 = jnp.bfloat16

D_MODEL = 1024
CHUNK = 64
WINDOW = 128
HEAD_DIM = 64
ATTN_WIDTH = 512
ATTN_HEADS = 8
KV_HEADS = 2
GROUP = 4
KV_WIDTH = 128
RWKV_WIDTH = 512
RWKV_HEADS = 8
DECAY_LORA = 64
AAA_LORA = 64
GATE_LORA = 128
RWKV_PROJ = 1792
IN_PROJ = 2560
N_MEM = 256
MEM_HEADS = 4
MEM_HEAD_DIM = 128
MEM_WIDTH = 512
D_FF = 4096
REL_BUCKETS = 32
REL_MAX_DIST = 128
NORM_EPS = 1e-6
GN_EPS = 64e-5
LOG2_E = math.log2(math.e)

V7X_VMEM_LIMIT_BYTES = 58 * 1024 * 1024
ROW_TILE = 256
IN_PROJ_ROW_TILE = 512
PROJ_TILE = 256
TAIL_ROW_TILE = 512
MLP_TILE = 1024
RWKV_SUB_CHUNKS = 4
RWKV_LOCAL_IN_FLIGHT = 3


def _params(*sem):
    return pltpu.CompilerParams(dimension_semantics=sem, vmem_limit_bytes=V7X_VMEM_LIMIT_BYTES)


def _const_spec(shape):
    nd = len(shape)
    return pl.BlockSpec(shape, lambda *_: (0,) * nd)


_NN = ((1,), (0,))
_NT = ((1,), (1,))
_TN = ((0,), (0,))


def _dg(a, b, dims):
    return lax.dot_general(a, b, (dims, ((), ())), preferred_element_type=F32)


def _dot(a, b):
    return _dg(a.astype(BF16), b.astype(BF16), _NN)


def _dot_nt(a, b):
    return _dg(a.astype(BF16), b.astype(BF16), _NT)


def _cat_rows(*xs):
    return jnp.concatenate(xs, axis=0)


def _cat_lanes(*xs):
    return jnp.concatenate(xs, axis=1)


def _run_interleaved(chains):
    active = list(chains)
    while active:
        still = []
        for ch in active:
            try:
                next(ch)
                still.append(ch)
            except StopIteration:
                pass
        active = still


def _run_tasks(tasks):
    finished = set()
    running = {}
    waiting = dict(tasks)
    while waiting or running:
        for name in [n for n, (_, deps) in waiting.items() if all(d in finished or d not in tasks for d in deps)]:
            running[name] = list(waiting.pop(name)[0]())
        for name in list(running):
            alive = []
            for ch in running[name]:
                try:
                    next(ch)
                    alive.append(ch)
                except StopIteration:
                    pass
            if alive:
                running[name] = alive
            else:
                del running[name]
                finished.add(name)


def _rms(x, g):
    return x * lax.rsqrt(jnp.mean(x * x, axis=-1, keepdims=True) + NORM_EPS) * g


def _inproj_kernel(x_ref, g_ref, w_ref, q_ref, k_ref, v_ref, zr_ref):
    h = _rms(x_ref[...], g_ref[...]).astype(BF16)
    q = jnp.dot(h, w_ref[:, :ATTN_WIDTH], preferred_element_type=F32)
    q_ref[...] = (q * (HEAD_DIM ** -0.5 * LOG2_E)).astype(BF16)
    k_ref[...] = jnp.dot(h, w_ref[:, ATTN_WIDTH:ATTN_WIDTH + KV_WIDTH], preferred_element_type=F32)
    v_ref[...] = jnp.dot(h, w_ref[:, ATTN_WIDTH + KV_WIDTH:ATTN_WIDTH + 2 * KV_WIDTH],
                         preferred_element_type=F32)
    zr_ref[...] = jnp.dot(h, w_ref[:, ATTN_WIDTH + 2 * KV_WIDTH:], preferred_element_type=F32)


def _in_proj(x2d, g, w_bf16):
    n = x2d.shape[0]
    tm = min(IN_PROJ_ROW_TILE, n)
    row = lambda w: pl.BlockSpec((tm, w), lambda i: (i, 0))
    return pl.pallas_call(
        _inproj_kernel,
        grid=(n // tm,),
        in_specs=[row(D_MODEL), _const_spec((1, D_MODEL)), _const_spec((D_MODEL, IN_PROJ))],
        out_specs=[row(ATTN_WIDTH), row(KV_WIDTH), row(KV_WIDTH), row(RWKV_PROJ)],
        out_shape=[jax.ShapeDtypeStruct((n, ATTN_WIDTH), BF16)]
                  + [jax.ShapeDtypeStruct((n, w), F32) for w in (KV_WIDTH, KV_WIDTH, RWKV_PROJ)],
        compiler_params=_params("parallel"),
        name="in_proj",
    )(x2d, g, w_bf16)


def _t5_bucket(rel):
    half = REL_BUCKETS // 2
    max_exact = half // 2
    assert REL_MAX_DIST == max_exact * 2 ** 4 and half - max_exact == 2 * 4
    n = np.abs(rel)
    large = max_exact + sum((n * n >= max_exact * max_exact * 2 ** t).astype(np.int64)
                            for t in range(1, half - max_exact))
    return (np.where(rel > 0, half, 0) + np.where(n < max_exact, n, large)).astype(np.int32)


def _bias_kernel(table_ref, bucket_ref, out_ref):
    bucket = bucket_ref[...]
    hits = [bucket == b for b in range(REL_BUCKETS)]
    for h in range(ATTN_HEADS):
        acc = jnp.zeros(bucket.shape, F32)
        for b in range(REL_BUCKETS):
            acc = jnp.where(hits[b], table_ref[b, h], acc)
        out_ref[h] = acc * LOG2_E


def _rel_bias(table, n_q, n_k):
    rel = np.arange(n_k)[None, :] - WINDOW - np.arange(n_q)[:, None]
    bucket = jnp.asarray(_t5_bucket(rel))
    bias = pl.pallas_call(
        _bias_kernel,
        in_specs=[pl.BlockSpec(memory_space=pltpu.SMEM), pl.BlockSpec(memory_space=pltpu.VMEM)],
        out_specs=pl.BlockSpec(memory_space=pltpu.VMEM),
        out_shape=jax.ShapeDtypeStruct((ATTN_HEADS, n_q, n_k), F32),
        name="rel_bias",
    )(table, bucket)
    return bias.reshape(KV_HEADS, GROUP * n_q, n_k)


def _group_sinks(sink_ref, n_q):
    row_group = lax.broadcasted_iota(jnp.int32, (GROUP * n_q, 1), 0) // n_q
    sinks = []
    for kvh in range(KV_HEADS):
        sink = jnp.zeros((GROUP * n_q, 1), F32)
        for g in range(GROUP):
            sink = jnp.where(row_group == g, sink_ref[kvh * GROUP + g] * LOG2_E, sink)
        sinks.append(sink)
    return sinks


def _attn_chain(q, keys, vals, bias, sink, valid, o_ref, rows, kvh):
    n_q = q.shape[0]
    qh = _cat_rows(*[q[:, (kvh * GROUP + g) * HEAD_DIM:(kvh * GROUP + g + 1) * HEAD_DIM]
                     for g in range(GROUP)])
    s = _dg(qh, keys, _NT) + bias
    if valid is not None:
        s = jnp.where(valid, s, -jnp.inf)
    yield
    m = jnp.maximum(jnp.max(s, axis=-1, keepdims=True), sink)
    p = jnp.exp2(s - m)
    den = jnp.sum(p, axis=-1, keepdims=True) + jnp.exp2(sink - m)
    yield
    o = _dg(p.astype(BF16), vals, _NN) * (1.0 / den)
    for g in range(GROUP):
        head = kvh * GROUP + g
        o_ref[rows, head * HEAD_DIM:(head + 1) * HEAD_DIM] = o[g * n_q:(g + 1) * n_q]
    yield


def _sample_attn_kernel(seq, sink_ref, q_ref, kp_ref, kn_ref, vp_ref, vn_ref, bias_ref, o_ref):
    batch = q_ref.shape[0] // seq
    sinks = _group_sinks(sink_ref, seq)
    chains = []
    for b in range(batch):
        rows = slice(b * seq, (b + 1) * seq)
        past = slice(b * WINDOW, (b + 1) * WINDOW)
        k_all = _cat_rows(kp_ref[past, :], kn_ref[rows, :]).astype(BF16)
        v_all = _cat_rows(vp_ref[past, :], vn_ref[rows, :]).astype(BF16)
        for kvh in range(KV_HEADS):
            lanes = slice(kvh * HEAD_DIM, (kvh + 1) * HEAD_DIM)
            chains.append(_attn_chain(q_ref[rows, :], k_all[:, lanes], v_all[:, lanes], bias_ref[kvh], sinks[kvh],
                                      None, o_ref, rows, kvh))
    _run_interleaved(chains)


def _sample_attention(q, k, v, k_past, v_past, sink, bias, seq):
    vmem = pl.BlockSpec(memory_space=pltpu.VMEM)
    return pl.pallas_call(
        functools.partial(_sample_attn_kernel, seq),
        in_specs=[pl.BlockSpec(memory_space=pltpu.SMEM)] + [vmem] * 6,
        out_specs=vmem,
        out_shape=jax.ShapeDtypeStruct(q.shape, F32),
        name="sample_attention",
    )(sink, q, k_past, k, v_past, v, bias)


def _split2(x):
    hi = x.astype(BF16)
    lo = (x - hi.astype(F32)).astype(BF16)
    return hi, lo


def _softplus(x):
    return jnp.maximum(x, 0.0) + jnp.log(1.0 + jnp.exp(-jnp.abs(x)))


def _sigmoid(x):
    return 1.0 / (1.0 + jnp.exp(-x))


def _rwkv_kernel(valid_rows, fused, steps, *refs):
    if fused:
        (xn_ref, x0_ref, gmix_ref, win_ref, shift_ref, s0_ref, mu_ref, w0_ref, w2_ref, a0_ref, a2_ref, g2_ref, kk_ref,
         ka_ref, rk_ref, lnw_ref, lnb_ref, seg_ref, tri_ref, sink_ref, bias_ref, xp_ref, mk_ref, mv_ref, wo_ref, gc_ref,
         wq_ref, wco_ref, gm_ref, wu_ref, wd_ref, gf_ref) = refs[:32]
        (yout_ref, s_ref, ktail_ref, vtail_ref, shiftout_ref, carry_ref, y_ref, sbd_ref, zr_scr, q_scr, k_scr,
         v_scr, kp_ref, vp_ref, a_scr, r_scr) = refs[32:]
        g = pl.program_id(0)
        c = lax.rem(g, steps)
        cur = lax.rem(g, 2)
        zr_ref, q_ref, kc_ref, vc_ref = zr_scr.at[cur], q_scr.at[cur], k_scr.at[cur], v_scr.at[cur]
        out_ref, attn_ref = r_scr.at[cur], a_scr.at[cur]
    else:
        (zr_ref, shift_ref, s0_ref, mu_ref, w0_ref, w2_ref, a0_ref, a2_ref, g2_ref, kk_ref, ka_ref, rk_ref, lnw_ref,
         lnb_ref, seg_ref, tri_ref) = refs[:16]
        out_ref, s_ref, carry_ref, y_ref, sbd_ref = refs[16:]
        c = pl.program_id(1)
    C = CHUNK
    R = y_ref.shape[0]
    n_sub = R // C
    pairs = range(RWKV_HEADS // 2)
    PAIR = 2 * HEAD_DIM
    W = RWKV_WIDTH

    @pl.when(c == 0)
    def _():
        carry_ref[0:1, :] = shift_ref[0]
        zero = jnp.zeros((HEAD_DIM, HEAD_DIM), F32)
        for p in pairs:
            sbd_ref[p] = _cat_rows(_cat_lanes(s0_ref[0, 2 * p], zero), _cat_lanes(zero, s0_ref[0, 2 * p + 1]))
        if fused:
            kp_ref[...] = jnp.zeros(kp_ref.shape, F32)
            vp_ref[...] = jnp.zeros(vp_ref.shape, F32)

    def proj_chains(x_ref, slot):
        h = _rms(x_ref[...], gmix_ref[...]).astype(BF16)
        q_end, k_end, v_end = ATTN_WIDTH, ATTN_WIDTH + KV_WIDTH, ATTN_WIDTH + 2 * KV_WIDTH

        def tile(lo, hi):
            z = jnp.dot(h, win_ref[:, lo:hi], preferred_element_type=F32)
            if hi <= q_end:
                q_scr[slot, :, lo:hi] = (z * (HEAD_DIM ** -0.5 * LOG2_E)).astype(BF16)
            elif lo == q_end:
                k_scr[slot] = z[:, :KV_WIDTH]
                v_scr[slot] = z[:, KV_WIDTH:]
            else:
                zr_scr[slot, :, lo - v_end:hi - v_end] = z
            yield

        assert q_end % PROJ_TILE == 0 and v_end - q_end == PROJ_TILE
        return [tile(lo, lo + PROJ_TILE) for lo in range(0, IN_PROJ, PROJ_TILE)]

    if fused:
        @pl.when(g == 0)
        def _():
            _run_interleaved(proj_chains(x0_ref, 0))
            a_scr[...] = jnp.zeros(a_scr.shape, F32)
            r_scr[...] = jnp.zeros(r_scr.shape, F32)

        def tail_chain():
            a_prev, r_prev = a_scr.at[1 - cur], r_scr.at[1 - cur]
            x1 = xp_ref[...] + _dot(a_prev[...], wo_ref[:ATTN_WIDTH, :]) + _dot(r_prev[...], wo_ref[ATTN_WIDTH:, :])
            yield
            q = _dot(_rms(x1, gc_ref[...]), wq_ref[...]).astype(BF16)
            yield
            head = lambda h: slice(h * MEM_HEAD_DIM, (h + 1) * MEM_HEAD_DIM)
            scores = [_dg(q[:, head(h)], mk_ref[0, :, head(h)].astype(BF16), _NT) * (MEM_HEAD_DIM ** -0.5)
                      for h in range(MEM_HEADS)]
            yield
            outs = []
            for h in range(MEM_HEADS):
                p = jnp.exp(scores[h] - jnp.max(scores[h], axis=-1, keepdims=True))
                den = jnp.sum(p, axis=-1, keepdims=True)
                outs.append(_dg(p.astype(BF16), mv_ref[0, :, head(h)].astype(BF16), _NN) * (1.0 / den))
            yield
            x2 = x1 + _dot(_cat_lanes(*outs), wco_ref[...])
            hm = _rms(x2, gm_ref[...]).astype(BF16)
            yield
            acc = x2
            for lo in range(0, D_FF, MLP_TILE):
                up = jnp.dot(hm, wu_ref[:, lo:lo + MLP_TILE], preferred_element_type=F32)
                act = jnp.square(jnp.maximum(up, 0.0)).astype(BF16)
                acc = acc + jnp.dot(act, wd_ref[lo:lo + MLP_TILE, :], preferred_element_type=F32)
                yield
            yout_ref[...] = _rms(acc, gf_ref[...])
            yield

    seg = seg_ref[...]
    seg2 = _cat_rows(seg, seg)

    def head_sum(x):
        hi, lo = _split2(x)
        tiles = [_dg(_cat_lanes(hi[:, t * PAIR:(t + 1) * PAIR], lo[:, t * PAIR:(t + 1) * PAIR]), seg2, _NN)
                 for t in range(W // PAIR)]
        return _cat_lanes(*tiles)

    first_row = lax.broadcasted_iota(jnp.int32, (C, 1), 0) == 0
    tri3 = tri_ref[...]

    lane = lax.broadcasted_iota(jnp.int32, (C, PAIR), 1)
    trow = lax.broadcasted_iota(jnp.int32, (C, PAIR), 0)
    even = lane < HEAD_DIM
    tcol = jnp.where(even, lane, lane - HEAD_DIM)
    strict = tcol < trow
    incl = tcol <= trow
    eye = jnp.where(tcol == trow, 1.0, 0.0).astype(F32)
    brow = lax.broadcasted_iota(jnp.int32, (PAIR, PAIR), 0) < HEAD_DIM
    bcol = lax.broadcasted_iota(jnp.int32, (PAIR, PAIR), 1) < HEAD_DIM
    on_diag = brow == bcol

    def bd(x):
        zero = jnp.zeros_like(x)
        return _cat_rows(jnp.where(even, x, zero), jnp.where(even, zero, x))

    def bd2(pair):
        return bd(pair[0]), bd(pair[1])

    def mm(a_pair, w_pair, dims=_NN):
        if dims == _NT:
            w_pair = (w_pair[0].T, w_pair[1].T)
        first = _dg(_cat_lanes(a_pair[0], a_pair[1]), _cat_rows(w_pair[0], w_pair[0]), _NN)
        return first + _dg(a_pair[0], w_pair[1], _NN)

    prepped = {}
    ready = {}
    state = [sbd_ref[p] for p in pairs]

    def prep_chain(j):
        rows = slice(j * C, (j + 1) * C)
        zr = zr_ref[rows, :]
        before = carry_ref[0:1, :] if j == 0 else zr_ref[j * C - 1:j * C, :]
        z_prev = jnp.where(first_row, before, pltpu.roll(zr, 1, axis=0))
        zs = zr + (z_prev - zr) * mu_ref[...]
        r = zs[:, :W]
        k = zs[:, W:2 * W]
        v = zs[:, 2 * W:3 * W]
        wd = zs[:, 3 * W:3 * W + DECAY_LORA]
        ad = zs[:, 3 * W + DECAY_LORA:3 * W + DECAY_LORA + AAA_LORA]
        gd = zs[:, 3 * W + DECAY_LORA + AAA_LORA:]
        w_log = -_softplus(-(w0_ref[...] + _dot(jnp.tanh(wd), w2_ref[...]))) - 0.5
        lw = -jnp.exp(w_log)
        a = _sigmoid(a0_ref[...] + _dot(ad, a2_ref[...]))
        gate = _dot(_sigmoid(gd), g2_ref[...])
        kk = k * kk_ref[...]
        kk = kk * lax.rsqrt(jnp.maximum(head_sum(kk * kk), 1e-24))
        k2 = k * (1.0 + (a - 1.0) * ka_ref[...])
        if (j + 1) * C > valid_rows:
            live = lax.broadcasted_iota(jnp.int32, (C, 1), 0) < valid_rows - j * C
            lw = jnp.where(live, lw, 0.0)
            kk = jnp.where(live, kk, 0.0)
            k2 = jnp.where(live, k2, 0.0)
        bvec = kk * a
        yield
        l1 = lw.astype(BF16)
        rem = lw - l1.astype(F32)
        l2 = rem.astype(BF16)
        l3 = (rem - l2.astype(F32)).astype(BF16)
        sums = _dg(tri3, _cat_rows(l1, l2, l3), _NN)
        li = sums[:C]
        lrev = sums[C:]
        yield
        inv_p = jnp.exp(-li)
        to_end = jnp.exp(lrev)
        prepped[j] = dict(
            at=_split2(-kk * jnp.exp(li - lw)), rt=_split2(r * jnp.exp(li)), bt=_split2(bvec * inv_p),
            kt=_split2(k2 * inv_p), bh=_split2(bvec * to_end), kh=_split2(k2 * to_end), v=_split2(v),
            p_end=jnp.exp(li[C - 1:C, :]), bonus=head_sum(r * k2 * rk_ref[...]) * v, gate=gate)
        yield

    def local_chain(j, p):
        d = prepped[j]
        lanes = slice(p * PAIR, (p + 1) * PAIR)
        cut = lambda pair: (pair[0][:, lanes], pair[1][:, lanes])
        at_p, rt_p, bt_p, kt_p, bh_p, kh_p, v_p = map(cut, (d['at'], d['rt'], d['bt'], d['kt'], d['bh'], d['kh'],
                                                            d['v']))
        left = (_cat_rows(at_p[0], rt_p[0]), _cat_rows(at_p[1], rt_p[1]))
        right = (_cat_rows(bd(bt_p[0]), bd(kt_p[0])), _cat_rows(bd(bt_p[1]), bd(kt_p[1])))
        aa = mm(left, right, _NT)
        yield
        a_ab = jnp.where(strict, aa[:C, :PAIR], 0.0)
        a_ak = jnp.where(strict, aa[:C, PAIR:], 0.0)
        a_rb = jnp.where(incl, aa[C:, :PAIR], 0.0)
        a_rk = jnp.where(incl, aa[C:, PAIR:], 0.0)
        inv = eye + a_ab
        ps = _split2(a_ab)
        power = mm(ps, bd2(ps))
        span = 2
        yield
        while span < C:
            ps = _split2(power)
            pw = bd2(ps)
            ih = _split2(inv)
            if span * 2 < C:
                both = mm((_cat_rows(ih[0], ps[0]), _cat_rows(ih[1], ps[1])), pw)
                inv = inv + both[:C]
                power = both[C:]
            else:
                inv = inv + mm(ih, pw)
            span *= 2
            yield
        ready[(j, p)] = dict(inv=_split2(inv), akrk=_split2(_cat_rows(a_ak, a_rk)), rb=_split2(a_rb), left=left,
                             bhkh=(_cat_rows(bh_p[0], kh_p[0]), _cat_rows(bh_p[1], kh_p[1])), v=v_p,
                             p_end=d['p_end'][:, lanes])

    def state_chain(j, p):
        d = ready.pop((j, p))
        s_prev = state[p]
        v_hi, v_lo = d['v']
        both = mm(d['left'], _split2(s_prev), _NT) + mm(d['akrk'], (bd(v_hi), bd(v_lo)))
        rhs = both[:C]
        y0 = both[C:]
        yield
        u_pair = _split2(mm(d['inv'], bd2(_split2(rhs))))
        yield
        y_ref[j * C:(j + 1) * C, p * PAIR:(p + 1) * PAIR] = y0 + mm(d['rb'], bd2(u_pair))
        t_hi = _cat_rows(u_pair[0], v_hi)
        t_lo = _cat_rows(u_pair[1], v_lo)
        w_hi, w_lo = d['bhkh']
        upd = _dg(_cat_rows(t_hi, t_lo), _cat_rows(w_hi, w_hi), _TN) + _dg(t_hi, w_lo, _TN)
        state[p] = s_prev * d['p_end'] + jnp.where(on_diag, upd, 0.0)
        yield

    def post_chain(j):
        rows = slice(j * C, (j + 1) * C)
        d = prepped.pop(j)
        y = y_ref[rows, :]
        mean = head_sum(y) * (1.0 / HEAD_DIM)
        dev = y - mean
        yield
        var = head_sum(dev * dev) * (1.0 / HEAD_DIM)
        yn = dev * lax.rsqrt(var + GN_EPS) * lnw_ref[...] + lnb_ref[...]
        out_ref[rows, :] = (yn + d['bonus']) * d['gate']
        yield

    tasks = {}
    for j in range(n_sub):
        tasks[('prep', j)] = (lambda j=j: [prep_chain(j)], [('prep', j - 1), ('local', j - RWKV_LOCAL_IN_FLIGHT)])
        tasks[('local', j)] = (lambda j=j: [local_chain(j, p) for p in pairs],
                               [('prep', j), ('local', j - RWKV_LOCAL_IN_FLIGHT)])
        tasks[('state', j)] = (lambda j=j: [state_chain(j, p) for p in pairs], [('local', j), ('state', j - 1)])
        tasks[('post', j)] = (lambda j=j: [post_chain(j)], [('state', j)])
    if fused:
        n_k = WINDOW + CHUNK
        k_all = _cat_rows(kp_ref[...], kc_ref[...]).astype(BF16)
        v_all = _cat_rows(vp_ref[...], vc_ref[...]).astype(BF16)
        first_valid = jnp.where(c == 0, WINDOW, 0)
        kcol = lax.broadcasted_iota(jnp.int32, (1, n_k), 1)
        sinks = _group_sinks(sink_ref, CHUNK)

        def attn_chains(j):
            rows = slice(j * C, (j + 1) * C)
            keys = slice(j * C, j * C + n_k)
            valid = kcol + j * C >= first_valid if j * C < WINDOW else None
            return [_attn_chain(q_ref[rows, :], k_all[keys, kvh * HEAD_DIM:(kvh + 1) * HEAD_DIM],
                                v_all[keys, kvh * HEAD_DIM:(kvh + 1) * HEAD_DIM], bias_ref[kvh], sinks[kvh], valid,
                                attn_ref, rows, kvh) for kvh in range(KV_HEADS)]

        for j in range(n_sub):
            tasks[('attn', j)] = (lambda j=j: attn_chains(j), [('attn', j - 1)])
        for t, chain in enumerate(proj_chains(xn_ref, 1 - cur)):
            tasks[('proj', t)] = (lambda chain=chain: [chain], [('proj', t - 1), ('prep', min(t, n_sub - 1))])
        tasks[('tail', 0)] = (lambda: [tail_chain()], [])
    _run_tasks(tasks)

    carry_ref[0:1, :] = zr_ref[R - 1:R, :]
    for p in pairs:
        sbd_ref[p] = state[p]
    if fused:
        kp_ref[...] = kc_ref[R - WINDOW:R, :]
        vp_ref[...] = vc_ref[R - WINDOW:R, :]

    @pl.when(c == steps - 1)
    def _():
        for p in pairs:
            s_ref[0, 2 * p] = state[p][:HEAD_DIM, :HEAD_DIM]
            s_ref[0, 2 * p + 1] = state[p][HEAD_DIM:, HEAD_DIM:]
        if fused:
            ktail_ref[0] = kc_ref[R - WINDOW:R, :]
            vtail_ref[0] = vc_ref[R - WINDOW:R, :]
            shiftout_ref[0] = zr_ref[R - 1:R, :]


def _rwkv_operands(lw):
    seg = jnp.asarray(np.kron(np.eye(2), np.ones((HEAD_DIM, HEAD_DIM))), BF16)
    ones = np.ones((CHUNK, CHUNK))
    tri3 = jnp.asarray(np.concatenate([np.tile(np.tril(ones), (1, 3)), np.tile(np.triu(ones, 1), (1, 3))]), BF16)
    row = lambda name: lw[name].reshape(1, -1)
    return [row('rwkv_mu'), row('rwkv_w0'), lw['rwkv_w2'].astype(BF16), row('rwkv_a0'),
            lw['rwkv_a2'].astype(BF16), lw['rwkv_g2'].astype(BF16), row('rwkv_k_k'), row('rwkv_k_a'),
            row('rwkv_r_k'), row('rwkv_ln_w'), row('rwkv_ln_b'), seg, tri3]


_RWKV_SCRATCH = [pltpu.VMEM((8, RWKV_PROJ), F32), None,
                 pltpu.VMEM((RWKV_HEADS // 2, 2 * HEAD_DIM, 2 * HEAD_DIM), F32)]


def _rwkv_mixer(zr, shift_prev, state0, lw, batch, seq):
    assert seq <= CHUNK
    if seq < CHUNK:
        zr = jnp.pad(zr.reshape(batch, seq, RWKV_PROJ), ((0, 0), (0, CHUNK - seq), (0, 0))).reshape(-1, RWKV_PROJ)
    params = _rwkv_operands(lw)
    state_spec = pl.BlockSpec((1, RWKV_HEADS, HEAD_DIM, HEAD_DIM), lambda b, c: (b, 0, 0, 0))
    row_spec = lambda w: pl.BlockSpec((CHUNK, w), lambda b, c: (b, 0))
    scratch = list(_RWKV_SCRATCH)
    scratch[1] = pltpu.VMEM((CHUNK, RWKV_WIDTH), F32)
    out, state = pl.pallas_call(
        functools.partial(_rwkv_kernel, seq, False, 1),
        grid=(batch, 1),
        in_specs=[row_spec(RWKV_PROJ), pl.BlockSpec((1, 1, RWKV_PROJ), lambda b, c: (b, 0, 0)), state_spec]
                 + [_const_spec(p.shape) for p in params],
        out_specs=[row_spec(RWKV_WIDTH), state_spec],
        out_shape=[jax.ShapeDtypeStruct((batch * CHUNK, RWKV_WIDTH), F32), jax.ShapeDtypeStruct(state0.shape, F32)],
        scratch_shapes=scratch,
        compiler_params=_params("parallel", "arbitrary"),
        name="rwkv_mixer",
    )(zr, shift_prev, state0, *params)
    if seq < CHUNK:
        out = out.reshape(batch, CHUNK, RWKV_WIDTH)[:, :seq].reshape(batch * seq, RWKV_WIDTH)
    return out, state


def _prompt_layer(x2d, mk, mv, shift_prev, state0, lw, bias, batch, seq):
    rows = CHUNK * RWKV_SUB_CHUNKS
    steps = seq // rows
    total = batch * steps
    params = _rwkv_operands(lw)
    seq_of = lambda g: jnp.minimum(g // steps, batch - 1)
    seq_block = lambda shape: pl.BlockSpec((1,) + shape, lambda g: (seq_of(g),) + (0,) * len(shape))
    prev = lambda g: jnp.maximum(g - 1, 0)

    def resident(shape):
        nd = len(shape)
        return pl.BlockSpec(shape, lambda g: (0,) * nd, pipeline_mode=pl.Buffered(1))

    in_specs = [pl.BlockSpec((rows, D_MODEL), lambda g: (jnp.minimum(g + 1, total - 1), 0)),
                pl.BlockSpec((rows, D_MODEL), lambda g: (0, 0), pipeline_mode=pl.Buffered(1)),
                resident((1, D_MODEL)), resident((D_MODEL, IN_PROJ)),
                seq_block((1, RWKV_PROJ)), seq_block((RWKV_HEADS, HEAD_DIM, HEAD_DIM))]
    in_specs += [resident(p.shape) for p in params]
    in_specs += [pl.BlockSpec(memory_space=pltpu.SMEM), resident(bias.shape)]
    tail_weights = [lw['w_out'], lw['norm_cross_g'], lw['w_cq'], lw['w_co'], lw['norm_mlp_g'], lw['w_up'],
                    lw['w_down'], lw['norm_final_g']]
    mem_spec = pl.BlockSpec((1, N_MEM, MEM_WIDTH), lambda g: (prev(g) // steps, 0, 0))
    in_specs += [pl.BlockSpec((rows, D_MODEL), lambda g: (prev(g), 0)), mem_spec, mem_spec]
    in_specs += [resident(w.shape) for w in tail_weights]
    out_specs = [pl.BlockSpec((rows, D_MODEL), lambda g: (prev(g), 0)), seq_block((RWKV_HEADS, HEAD_DIM, HEAD_DIM)),
                 seq_block((WINDOW, KV_WIDTH)), seq_block((WINDOW, KV_WIDTH)), seq_block((1, RWKV_PROJ))]
    n = batch * seq
    out_shape = [jax.ShapeDtypeStruct((n, D_MODEL), F32), jax.ShapeDtypeStruct(state0.shape, F32),
                 jax.ShapeDtypeStruct((batch, WINDOW, KV_WIDTH), F32),
                 jax.ShapeDtypeStruct((batch, WINDOW, KV_WIDTH), F32),
                 jax.ShapeDtypeStruct((batch, 1, RWKV_PROJ), F32)]
    scratch = list(_RWKV_SCRATCH)
    scratch[1] = pltpu.VMEM((rows, RWKV_WIDTH), F32)
    scratch += [pltpu.VMEM((2, rows, RWKV_PROJ), F32), pltpu.VMEM((2, rows, ATTN_WIDTH), BF16),
                pltpu.VMEM((2, rows, KV_WIDTH), F32), pltpu.VMEM((2, rows, KV_WIDTH), F32),
                pltpu.VMEM((WINDOW, KV_WIDTH), F32), pltpu.VMEM((WINDOW, KV_WIDTH), F32),
                pltpu.VMEM((2, rows, ATTN_WIDTH), F32), pltpu.VMEM((2, rows, RWKV_WIDTH), F32)]
    return pl.pallas_call(
        functools.partial(_rwkv_kernel, rows, True, steps),
        grid=(total + 1,),
        in_specs=in_specs,
        out_specs=out_specs,
        out_shape=out_shape,
        scratch_shapes=scratch,
        compiler_params=_params("arbitrary"),
        name="prompt_layer",
    )(x2d, x2d, lw['norm_mix_g'], lw['w_in'], shift_prev, state0, *params, lw['attn_sink'], bias, x2d, mk, mv,
      *tail_weights)


def _memkv_kernel(m_ref, g_ref, wk_ref, wv_ref, k_ref, v_ref):
    mn = _rms(m_ref[...], g_ref[...]).astype(BF16)
    k_ref[...] = jnp.dot(mn, wk_ref[...], preferred_element_type=F32)
    v_ref[...] = jnp.dot(mn, wv_ref[...], preferred_element_type=F32)


def _memory_kv(mem2d, g, w_mk, w_mv):
    n = mem2d.shape[0]
    tm = min(ROW_TILE, n)
    row = lambda w: pl.BlockSpec((tm, w), lambda i: (i, 0))
    return pl.pallas_call(
        _memkv_kernel,
        grid=(n // tm,),
        in_specs=[row(D_MODEL), _const_spec((1, D_MODEL)), _const_spec(w_mk.shape), _const_spec(w_mv.shape)],
        out_specs=[row(MEM_WIDTH), row(MEM_WIDTH)],
        out_shape=[jax.ShapeDtypeStruct((n, MEM_WIDTH), F32)] * 2,
        compiler_params=_params("parallel"),
        name="memory_kv",
    )(mem2d, g, w_mk, w_mv)


def _tail_kernel(x_ref, a_ref, r_ref, mk_ref, mv_ref, wo_ref, gc_ref, wq_ref, wco_ref, gm_ref, wu_ref, wd_ref, gf_ref,
                 y_ref):
    n_seq = mk_ref.shape[0]
    rows_per_seq = x_ref.shape[0] // n_seq
    x1 = x_ref[...] + _dot(a_ref[...], wo_ref[:ATTN_WIDTH, :]) + _dot(r_ref[...], wo_ref[ATTN_WIDTH:, :])
    q = _dot(_rms(x1, gc_ref[...]), wq_ref[...]).astype(BF16)
    outs = {}

    def cross_chain(b, h):
        rows = slice(b * rows_per_seq, (b + 1) * rows_per_seq)
        sl = slice(h * MEM_HEAD_DIM, (h + 1) * MEM_HEAD_DIM)
        s = _dg(q[rows, sl], mk_ref[b, :, sl].astype(BF16), _NT) * (MEM_HEAD_DIM ** -0.5)
        yield
        p = jnp.exp(s - jnp.max(s, axis=-1, keepdims=True))
        den = jnp.sum(p, axis=-1, keepdims=True)
        yield
        outs[(b, h)] = _dg(p.astype(BF16), mv_ref[b, :, sl].astype(BF16), _NN) * (1.0 / den)
        yield

    _run_interleaved([cross_chain(b, h) for b in range(n_seq) for h in range(MEM_HEADS)])
    o = _cat_rows(*[_cat_lanes(*[outs[(b, h)] for h in range(MEM_HEADS)]) for b in range(n_seq)])
    x2 = x1 + _dot(o, wco_ref[...])
    up = _dot(_rms(x2, gm_ref[...]), wu_ref[...])
    act = jnp.square(jnp.maximum(up, 0.0))
    y_ref[...] = _rms(x2 + _dot(act, wd_ref[...]), gf_ref[...])


def _tail(x2d, a_out, r_out, mk, mv, lw, batch, seq):
    n = batch * seq
    tq = min(TAIL_ROW_TILE, n)
    if seq >= tq:
        assert seq % tq == 0
        seq_per_tile, tiles_per_seq = 1, seq // tq
        mem_spec = pl.BlockSpec((1, N_MEM, MEM_WIDTH), lambda i: (i // tiles_per_seq, 0, 0))
    else:
        assert tq % seq == 0
        seq_per_tile = tq // seq
        mem_spec = pl.BlockSpec((seq_per_tile, N_MEM, MEM_WIDTH), lambda i: (i, 0, 0))
    row = lambda w: pl.BlockSpec((tq, w), lambda i: (i, 0))

    def resident(shape):
        nd = len(shape)
        return pl.BlockSpec(shape, lambda *_: (0,) * nd, pipeline_mode=pl.Buffered(1))

    weights = [lw['w_out'], lw['norm_cross_g'], lw['w_cq'], lw['w_co'], lw['norm_mlp_g'], lw['w_up'], lw['w_down'],
               lw['norm_final_g']]
    return pl.pallas_call(
        _tail_kernel,
        grid=(n // tq,),
        in_specs=[row(D_MODEL), row(ATTN_WIDTH), row(RWKV_WIDTH), mem_spec, mem_spec]
                 + [resident(w.shape) for w in weights],
        out_specs=row(D_MODEL),
        out_shape=jax.ShapeDtypeStruct(x2d.shape, F32),
        compiler_params=_params("parallel"),
        name="tail",
    )(x2d, a_out, r_out, mk, mv, *weights)


def _trunk(x, mk, mv, k_past, v_past, shift_prev, state0, lw, table):
    batch, seq = x.shape[0], x.shape[1]
    x2d = x.reshape(batch * seq, D_MODEL)
    if k_past is None:
        bias = _rel_bias(table, CHUNK, WINDOW + CHUNK)
        y, state, k_buf, v_buf, shift_new = _prompt_layer(x2d, mk, mv, shift_prev, state0, lw, bias, batch, seq)
    else:
        q, k, v, zr = _in_proj(x2d, lw['norm_mix_g'], lw['w_in'])
        bias = _rel_bias(table, seq, WINDOW + seq)
        a_out = _sample_attention(q, k, v, k_past.reshape(batch * WINDOW, KV_WIDTH),
                                  v_past.reshape(batch * WINDOW, KV_WIDTH), lw['attn_sink'], bias, seq)
        k_buf = jnp.concatenate([k_past, k.reshape(batch, seq, KV_WIDTH)], axis=1)[:, -WINDOW:]
        v_buf = jnp.concatenate([v_past, v.reshape(batch, seq, KV_WIDTH)], axis=1)[:, -WINDOW:]
        r_out, state = _rwkv_mixer(zr, shift_prev, state0, lw, batch, seq)
        shift_new = zr.reshape(batch, seq, RWKV_PROJ)[:, -1:]
        y = _tail(x2d, a_out, r_out, mk, mv, lw, batch, seq)
    kv_shape = (batch, WINDOW, KV_HEADS, HEAD_DIM)
    return y.reshape(x.shape), k_buf.reshape(kv_shape), v_buf.reshape(kv_shape), shift_new, state


def kernel(x_prompt, x_sample, mem_prompt, cache_attn_k, cache_attn_v, cache_mem_k, cache_mem_v, state_shift,
           state_wkv, norm_mix_g, w_in, attn_sink, rel_bias_table, rwkv_mu, rwkv_w0, rwkv_w2, rwkv_a0, rwkv_a2,
           rwkv_g2, rwkv_k_k, rwkv_k_a, rwkv_r_k, rwkv_ln_w, rwkv_ln_b, w_out, norm_cross_g, norm_mem_g, w_cq,
           w_mk, w_mv, w_co, norm_mlp_g, w_up, w_down, norm_final_g):
    assert norm_mix_g.shape[0] == 1, "single-layer trunk"
    bp, dec_b = x_prompt.shape[0], x_sample.shape[0]
    vec = lambda p: p[0].reshape(1, -1)
    lw = {
        'norm_mix_g': vec(norm_mix_g), 'w_in': w_in[0].astype(BF16), 'attn_sink': attn_sink[0],
        'rwkv_mu': rwkv_mu[0], 'rwkv_w0': rwkv_w0[0], 'rwkv_w2': rwkv_w2[0], 'rwkv_a0': rwkv_a0[0],
        'rwkv_a2': rwkv_a2[0], 'rwkv_g2': rwkv_g2[0], 'rwkv_k_k': rwkv_k_k[0], 'rwkv_k_a': rwkv_k_a[0],
        'rwkv_r_k': rwkv_r_k[0], 'rwkv_ln_w': rwkv_ln_w[0], 'rwkv_ln_b': rwkv_ln_b[0],
        'w_out': w_out[0].astype(BF16), 'norm_cross_g': vec(norm_cross_g), 'w_cq': w_cq[0].astype(BF16),
        'w_co': w_co[0].astype(BF16), 'norm_mlp_g': vec(norm_mlp_g), 'w_up': w_up[0].astype(BF16),
        'w_down': w_down[0].astype(BF16), 'norm_final_g': norm_final_g.reshape(1, -1),
    }
    mk, mv = _memory_kv(mem_prompt.reshape(bp * N_MEM, D_MODEL), vec(norm_mem_g),
                        w_mk[0].astype(BF16), w_mv[0].astype(BF16))
    mk = mk.reshape(bp, N_MEM, MEM_WIDTH)
    mv = mv.reshape(bp, N_MEM, MEM_WIDTH)
    shift0 = jnp.zeros((bp, 1, RWKV_PROJ), F32)
    wkv0 = jnp.zeros((bp, RWKV_HEADS, HEAD_DIM, HEAD_DIM), F32)
    yp, pk, pv, psh, pS = _trunk(x_prompt, mk, mv, None, None, shift0, wkv0, lw, rel_bias_table)
    ys, sk, sv, ssh, sS = _trunk(
        x_sample, cache_mem_k[0].reshape(dec_b, N_MEM, MEM_WIDTH), cache_mem_v[0].reshape(dec_b, N_MEM, MEM_WIDTH),
        cache_attn_k[0].reshape(dec_b, WINDOW, KV_WIDTH), cache_attn_v[0].reshape(dec_b, WINDOW, KV_WIDTH),
        state_shift[0], state_wkv[0], lw, rel_bias_table)
    mem_shape = (1, bp, N_MEM, MEM_HEADS, MEM_HEAD_DIM)
    return (yp, ys, pk[None], pv[None], mk.reshape(mem_shape), mv.reshape(mem_shape), psh[None], pS[None],
            sk[None], sv[None], ssh[None], sS[None])
```

```python
import functools
import math

import numpy as np
import jax
import jax.numpy as jnp
from jax import lax
from jax.experimental import pallas as pl
from jax.experimental.pallas import tpu as pltpu

F32 = jnp.float32
BF16 = jnp.bfloat16

D_MODEL = 1024
CHUNK = 64
WINDOW = 128
HEAD_DIM = 64
ATTN_WIDTH = 512
ATTN_HEADS = 8
KV_HEADS = 2
GROUP = 4
KV_WIDTH = 128
RWKV_WIDTH = 512
RWKV_HEADS = 8
DECAY_LORA = 64
AAA_LORA = 64
GATE_LORA = 128
RWKV_PROJ = 1792
IN_PROJ = 2560
N_MEM = 256
MEM_HEADS = 4
MEM_HEAD_DIM = 128
MEM_WIDTH = 512
D_FF = 4096
REL_BUCKETS = 32
REL_MAX_DIST = 128
NORM_EPS = 1e-6
GN_EPS = 64e-5
LOG2_E = math.log2(math.e)

V7X_VMEM_LIMIT_BYTES = 52 * 1024 * 1024
ROW_TILE = 256
IN_PROJ_ROW_TILE = 512
PROJ_TILE = 256
TAIL_ROW_TILE = 512
WEIGHT_CAST_STEPS = 8
RWKV_SUB_CHUNKS = 4
RWKV_LOCAL_IN_FLIGHT = 3


def _params(*sem):
    return pltpu.CompilerParams(dimension_semantics=sem, vmem_limit_bytes=V7X_VMEM_LIMIT_BYTES)


def _const_spec(shape):
    nd = len(shape)
    return pl.BlockSpec(shape, lambda *_: (0,) * nd)


_NN = ((1,), (0,))
_NT = ((1,), (1,))
_TN = ((0,), (0,))


def _dg(a, b, dims):
    return lax.dot_general(a, b, (dims, ((), ())), preferred_element_type=F32)


def _dot(a, b):
    return _dg(a.astype(BF16), b.astype(BF16), _NN)


def _dot_nt(a, b):
    return _dg(a.astype(BF16), b.astype(BF16), _NT)


def _cat_rows(*xs):
    return jnp.concatenate(xs, axis=0)


def _cat_lanes(*xs):
    return jnp.concatenate(xs, axis=1)


def _run_interleaved(chains):
    active = list(chains)
    while active:
        still = []
        for ch in active:
            try:
                next(ch)
                still.append(ch)
            except StopIteration:
                pass
        active = still


def _run_tasks(tasks):
    finished = set()
    running = {}
    waiting = dict(tasks)
    while waiting or running:
        for name in [n for n, (_, deps) in waiting.items() if all(d in finished or d not in tasks for d in deps)]:
            running[name] = list(waiting.pop(name)[0]())
        for name in list(running):
            alive = []
            for ch in running[name]:
                try:
                    next(ch)
                    alive.append(ch)
                except StopIteration:
                    pass
            if alive:
                running[name] = alive
            else:
                del running[name]
                finished.add(name)


def _rms(x, g):
    return x * lax.rsqrt(jnp.mean(x * x, axis=-1, keepdims=True) + NORM_EPS) * g


def _cast_kernel(*refs):
    n = len(refs) // 2
    for src, dst in zip(refs[:n], refs[n:]):
        dst[...] = src[...].astype(BF16)


def _to_bf16(*weights):
    specs = [pl.BlockSpec((w.shape[0] // WEIGHT_CAST_STEPS, w.shape[1]), lambda i: (i, 0)) for w in weights]
    return pl.pallas_call(
        _cast_kernel,
        grid=(WEIGHT_CAST_STEPS,),
        in_specs=specs,
        out_specs=specs,
        out_shape=[jax.ShapeDtypeStruct(w.shape, BF16) for w in weights],
        compiler_params=_params("parallel"),
        name="cast_weights",
    )(*weights)


def _inproj_kernel(x_ref, g_ref, w_ref, q_ref, k_ref, v_ref, zr_ref):
    h = _rms(x_ref[...], g_ref[...]).astype(BF16)
    q = jnp.dot(h, w_ref[:, :ATTN_WIDTH], preferred_element_type=F32)
    q_ref[...] = (q * (HEAD_DIM ** -0.5 * LOG2_E)).astype(BF16)
    k_ref[...] = jnp.dot(h, w_ref[:, ATTN_WIDTH:ATTN_WIDTH + KV_WIDTH], preferred_element_type=F32)
    v_ref[...] = jnp.dot(h, w_ref[:, ATTN_WIDTH + KV_WIDTH:ATTN_WIDTH + 2 * KV_WIDTH],
                         preferred_element_type=F32)
    zr_ref[...] = jnp.dot(h, w_ref[:, ATTN_WIDTH + 2 * KV_WIDTH:], preferred_element_type=F32)


def _in_proj(x2d, g, w_bf16):
    n = x2d.shape[0]
    tm = min(IN_PROJ_ROW_TILE, n)
    row = lambda w: pl.BlockSpec((tm, w), lambda i: (i, 0))
    return pl.pallas_call(
        _inproj_kernel,
        grid=(n // tm,),
        in_specs=[row(D_MODEL), _const_spec((1, D_MODEL)), _const_spec((D_MODEL, IN_PROJ))],
        out_specs=[row(ATTN_WIDTH), row(KV_WIDTH), row(KV_WIDTH), row(RWKV_PROJ)],
        out_shape=[jax.ShapeDtypeStruct((n, ATTN_WIDTH), BF16)]
                  + [jax.ShapeDtypeStruct((n, w), F32) for w in (KV_WIDTH, KV_WIDTH, RWKV_PROJ)],
        compiler_params=_params("parallel"),
        name="in_proj",
    )(x2d, g, w_bf16)


def _t5_bucket(rel):
    half = REL_BUCKETS // 2
    max_exact = half // 2
    assert REL_MAX_DIST == max_exact * 2 ** 4 and half - max_exact == 2 * 4
    n = np.abs(rel)
    large = max_exact + sum((n * n >= max_exact * max_exact * 2 ** t).astype(np.int64)
                            for t in range(1, half - max_exact))
    return (np.where(rel > 0, half, 0) + np.where(n < max_exact, n, large)).astype(np.int32)


def _bias_kernel(table_ref, bucket_ref, out_ref):
    bucket = bucket_ref[...]
    hits = [bucket == b for b in range(REL_BUCKETS)]
    for h in range(ATTN_HEADS):
        acc = jnp.zeros(bucket.shape, F32)
        for b in range(REL_BUCKETS):
            acc = jnp.where(hits[b], table_ref[b, h], acc)
        out_ref[h] = acc * LOG2_E


def _rel_bias(table, n_q, n_k):
    rel = np.arange(n_k)[None, :] - WINDOW - np.arange(n_q)[:, None]
    bucket = jnp.asarray(_t5_bucket(rel))
    bias = pl.pallas_call(
        _bias_kernel,
        in_specs=[pl.BlockSpec(memory_space=pltpu.SMEM), pl.BlockSpec(memory_space=pltpu.VMEM)],
        out_specs=pl.BlockSpec(memory_space=pltpu.VMEM),
        out_shape=jax.ShapeDtypeStruct((ATTN_HEADS, n_q, n_k), F32),
        name="rel_bias",
    )(table, bucket)
    return bias.reshape(KV_HEADS, GROUP * n_q, n_k)


def _group_sinks(sink_ref, n_q):
    row_group = lax.broadcasted_iota(jnp.int32, (GROUP * n_q, 1), 0) // n_q
    sinks = []
    for kvh in range(KV_HEADS):
        sink = jnp.zeros((GROUP * n_q, 1), F32)
        for g in range(GROUP):
            sink = jnp.where(row_group == g, sink_ref[kvh * GROUP + g] * LOG2_E, sink)
        sinks.append(sink)
    return sinks


def _attn_chain(q, keys, vals, bias, sink, valid, o_ref, rows, kvh):
    n_q = q.shape[0]
    qh = _cat_rows(*[q[:, (kvh * GROUP + g) * HEAD_DIM:(kvh * GROUP + g + 1) * HEAD_DIM]
                     for g in range(GROUP)])
    s = _dg(qh, keys, _NT) + bias
    if valid is not None:
        s = jnp.where(valid, s, -jnp.inf)
    yield
    m = jnp.maximum(jnp.max(s, axis=-1, keepdims=True), sink)
    p = jnp.exp2(s - m)
    den = jnp.sum(p, axis=-1, keepdims=True) + jnp.exp2(sink - m)
    yield
    o = _dg(p.astype(BF16), vals, _NN) * (1.0 / den)
    for g in range(GROUP):
        head = kvh * GROUP + g
        o_ref[rows, head * HEAD_DIM:(head + 1) * HEAD_DIM] = o[g * n_q:(g + 1) * n_q]
    yield


def _sample_attn_kernel(seq, sink_ref, q_ref, kp_ref, kn_ref, vp_ref, vn_ref, bias_ref, o_ref):
    batch = q_ref.shape[0] // seq
    sinks = _group_sinks(sink_ref, seq)
    chains = []
    for b in range(batch):
        rows = slice(b * seq, (b + 1) * seq)
        past = slice(b * WINDOW, (b + 1) * WINDOW)
        k_all = _cat_rows(kp_ref[past, :], kn_ref[rows, :]).astype(BF16)
        v_all = _cat_rows(vp_ref[past, :], vn_ref[rows, :]).astype(BF16)
        for kvh in range(KV_HEADS):
            lanes = slice(kvh * HEAD_DIM, (kvh + 1) * HEAD_DIM)
            chains.append(_attn_chain(q_ref[rows, :], k_all[:, lanes], v_all[:, lanes], bias_ref[kvh], sinks[kvh],
                                      None, o_ref, rows, kvh))
    _run_interleaved(chains)


def _sample_attention(q, k, v, k_past, v_past, sink, bias, seq):
    vmem = pl.BlockSpec(memory_space=pltpu.VMEM)
    return pl.pallas_call(
        functools.partial(_sample_attn_kernel, seq),
        in_specs=[pl.BlockSpec(memory_space=pltpu.SMEM)] + [vmem] * 6,
        out_specs=vmem,
        out_shape=jax.ShapeDtypeStruct(q.shape, F32),
        name="sample_attention",
    )(sink, q, k_past, k, v_past, v, bias)


def _split2(x):
    hi = x.astype(BF16)
    lo = (x - hi.astype(F32)).astype(BF16)
    return hi, lo


def _softplus(x):
    return jnp.maximum(x, 0.0) + jnp.log(1.0 + jnp.exp(-jnp.abs(x)))


def _sigmoid(x):
    return 1.0 / (1.0 + jnp.exp(-x))


def _rwkv_kernel(valid_rows, fused, steps, *refs):
    if fused:
        (xn_ref, x0_ref, gmix_ref, win_ref, shift_ref, s0_ref, mu_ref, w0_ref, w2_ref, a0_ref, a2_ref, g2_ref, kk_ref,
         ka_ref, rk_ref, lnw_ref, lnb_ref, seg_ref, tri_ref, sink_ref, bias_ref) = refs[:21]
        (out_ref, s_ref, attn_ref, ktail_ref, vtail_ref, shiftout_ref, carry_ref, y_ref, sbd_ref, zr_scr, q_scr, k_scr,
         v_scr, kp_ref, vp_ref) = refs[21:]
        g = pl.program_id(0)
        c = lax.rem(g, steps)
        cur = lax.rem(g, 2)
        zr_ref, q_ref, kc_ref, vc_ref = zr_scr.at[cur], q_scr.at[cur], k_scr.at[cur], v_scr.at[cur]
    else:
        (zr_ref, shift_ref, s0_ref, mu_ref, w0_ref, w2_ref, a0_ref, a2_ref, g2_ref, kk_ref, ka_ref, rk_ref, lnw_ref,
         lnb_ref, seg_ref, tri_ref) = refs[:16]
        out_ref, s_ref, y_ref = refs[16:]
    C = CHUNK
    R = out_ref.shape[0]
    n_sub = R // C
    pairs = range(RWKV_HEADS // 2)
    PAIR = 2 * HEAD_DIM
    W = RWKV_WIDTH

    def block_diagonal(s0, p):
        zero = jnp.zeros((HEAD_DIM, HEAD_DIM), F32)
        return _cat_rows(_cat_lanes(s0[2 * p], zero), _cat_lanes(zero, s0[2 * p + 1]))

    if fused:
        @pl.when(c == 0)
        def _():
            carry_ref[0:1, :] = shift_ref[0]
            for p in pairs:
                sbd_ref[p] = block_diagonal(s0_ref.at[0], p)
            kp_ref[...] = jnp.zeros(kp_ref.shape, F32)
            vp_ref[...] = jnp.zeros(vp_ref.shape, F32)

    def proj_chains(x_ref, slot):
        h = _rms(x_ref[...], gmix_ref[...]).astype(BF16)
        q_end, k_end, v_end = ATTN_WIDTH, ATTN_WIDTH + KV_WIDTH, ATTN_WIDTH + 2 * KV_WIDTH

        def tile(lo, hi):
            z = jnp.dot(h, win_ref[:, lo:hi], preferred_element_type=F32)
            if hi <= q_end:
                q_scr[slot, :, lo:hi] = (z * (HEAD_DIM ** -0.5 * LOG2_E)).astype(BF16)
            elif lo == q_end:
                k_scr[slot] = z[:, :KV_WIDTH]
                v_scr[slot] = z[:, KV_WIDTH:]
            else:
                zr_scr[slot, :, lo - v_end:hi - v_end] = z
            yield

        assert q_end % PROJ_TILE == 0 and v_end - q_end == PROJ_TILE
        return [tile(lo, lo + PROJ_TILE) for lo in range(0, IN_PROJ, PROJ_TILE)]

    if fused:
        @pl.when(g == 0)
        def _():
            _run_interleaved(proj_chains(x0_ref, 0))

    seg = seg_ref[...]
    seg2 = _cat_rows(seg, seg)

    def head_sum(x):
        hi, lo = _split2(x)
        tiles = [_dg(_cat_lanes(hi[:, t * PAIR:(t + 1) * PAIR], lo[:, t * PAIR:(t + 1) * PAIR]), seg2, _NN)
                 for t in range(W // PAIR)]
        return _cat_lanes(*tiles)

    first_row = lax.broadcasted_iota(jnp.int32, (C, 1), 0) == 0
    tri3 = tri_ref[...]

    lane = lax.broadcasted_iota(jnp.int32, (C, PAIR), 1)
    trow = lax.broadcasted_iota(jnp.int32, (C, PAIR), 0)
    even = lane < HEAD_DIM
    tcol = jnp.where(even, lane, lane - HEAD_DIM)
    strict = tcol < trow
    incl = tcol <= trow
    eye = jnp.where(tcol == trow, 1.0, 0.0).astype(F32)
    brow = lax.broadcasted_iota(jnp.int32, (PAIR, PAIR), 0) < HEAD_DIM
    bcol = lax.broadcasted_iota(jnp.int32, (PAIR, PAIR), 1) < HEAD_DIM
    on_diag = brow == bcol

    def bd(x):
        zero = jnp.zeros_like(x)
        return _cat_rows(jnp.where(even, x, zero), jnp.where(even, zero, x))

    def bd2(pair):
        return bd(pair[0]), bd(pair[1])

    def mm(a_pair, w_pair, dims=_NN):
        if dims == _NT:
            w_pair = (w_pair[0].T, w_pair[1].T)
        first = _dg(_cat_lanes(a_pair[0], a_pair[1]), _cat_rows(w_pair[0], w_pair[0]), _NN)
        return first + _dg(a_pair[0], w_pair[1], _NN)

    prepped = {}
    ready = {}
    if fused:
        state = {p: sbd_ref[p] for p in pairs}
        skey = lambda j, p: p
    else:
        state = {(j, p): block_diagonal(s0_ref.at[j], p) for j in range(n_sub) for p in pairs}
        skey = lambda j, p: (j, p)

    def prep_chain(j):
        rows = slice(j * C, (j + 1) * C)
        zr = zr_ref[rows, :]
        if not fused:
            before = shift_ref[j]
        else:
            before = carry_ref[0:1, :] if j == 0 else zr_ref[j * C - 1:j * C, :]
        z_prev = jnp.where(first_row, before, pltpu.roll(zr, 1, axis=0))
        zs = zr + (z_prev - zr) * mu_ref[...]
        r = zs[:, :W]
        k = zs[:, W:2 * W]
        v = zs[:, 2 * W:3 * W]
        wd = zs[:, 3 * W:3 * W + DECAY_LORA]
        ad = zs[:, 3 * W + DECAY_LORA:3 * W + DECAY_LORA + AAA_LORA]
        gd = zs[:, 3 * W + DECAY_LORA + AAA_LORA:]
        w_log = -_softplus(-(w0_ref[...] + _dot(jnp.tanh(wd), w2_ref[...]))) - 0.5
        lw = -jnp.exp(w_log)
        a = _sigmoid(a0_ref[...] + _dot(ad, a2_ref[...]))
        gate = _dot(_sigmoid(gd), g2_ref[...])
        kk = k * kk_ref[...]
        kk = kk * lax.rsqrt(jnp.maximum(head_sum(kk * kk), 1e-24))
        k2 = k * (1.0 + (a - 1.0) * ka_ref[...])
        if valid_rows < C:
            live = lax.broadcasted_iota(jnp.int32, (C, 1), 0) < valid_rows
            lw = jnp.where(live, lw, 0.0)
            kk = jnp.where(live, kk, 0.0)
            k2 = jnp.where(live, k2, 0.0)
        bvec = kk * a
        yield
        l1 = lw.astype(BF16)
        rem = lw - l1.astype(F32)
        l2 = rem.astype(BF16)
        l3 = (rem - l2.astype(F32)).astype(BF16)
        sums = _dg(tri3, _cat_rows(l1, l2, l3), _NN)
        li = sums[:C]
        lrev = sums[C:]
        yield
        inv_p = jnp.exp(-li)
        to_end = jnp.exp(lrev)
        prepped[j] = dict(
            at=_split2(-kk * jnp.exp(li - lw)), rt=_split2(r * jnp.exp(li)), bt=_split2(bvec * inv_p),
            kt=_split2(k2 * inv_p), bh=_split2(bvec * to_end), kh=_split2(k2 * to_end), v=_split2(v),
            p_end=jnp.exp(li[C - 1:C, :]), bonus=head_sum(r * k2 * rk_ref[...]) * v, gate=gate)
        yield

    def local_chain(j, p):
        d = prepped[j]
        lanes = slice(p * PAIR, (p + 1) * PAIR)
        cut = lambda pair: (pair[0][:, lanes], pair[1][:, lanes])
        at_p, rt_p, bt_p, kt_p, bh_p, kh_p, v_p = map(cut, (d['at'], d['rt'], d['bt'], d['kt'], d['bh'], d['kh'],
                                                            d['v']))
        left = (_cat_rows(at_p[0], rt_p[0]), _cat_rows(at_p[1], rt_p[1]))
        right = (_cat_rows(bd(bt_p[0]), bd(kt_p[0])), _cat_rows(bd(bt_p[1]), bd(kt_p[1])))
        aa = mm(left, right, _NT)
        yield
        a_ab = jnp.where(strict, aa[:C, :PAIR], 0.0)
        a_ak = jnp.where(strict, aa[:C, PAIR:], 0.0)
        a_rb = jnp.where(incl, aa[C:, :PAIR], 0.0)
        a_rk = jnp.where(incl, aa[C:, PAIR:], 0.0)
        inv = eye + a_ab
        ps = _split2(a_ab)
        power = mm(ps, bd2(ps))
        span = 2
        yield
        while span < C:
            ps = _split2(power)
            pw = bd2(ps)
            ih = _split2(inv)
            if span * 2 < C:
                both = mm((_cat_rows(ih[0], ps[0]), _cat_rows(ih[1], ps[1])), pw)
                inv = inv + both[:C]
                power = both[C:]
            else:
                inv = inv + mm(ih, pw)
            span *= 2
            yield
        ready[(j, p)] = dict(inv=_split2(inv), akrk=_split2(_cat_rows(a_ak, a_rk)), rb=_split2(a_rb), left=left,
                             bhkh=(_cat_rows(bh_p[0], kh_p[0]), _cat_rows(bh_p[1], kh_p[1])), v=v_p,
                             p_end=d['p_end'][:, lanes])

    def state_chain(j, p):
        d = ready.pop((j, p))
        s_prev = state[skey(j, p)]
        v_hi, v_lo = d['v']
        both = mm(d['left'], _split2(s_prev), _NT) + mm(d['akrk'], (bd(v_hi), bd(v_lo)))
        rhs = both[:C]
        y0 = both[C:]
        yield
        u_pair = _split2(mm(d['inv'], bd2(_split2(rhs))))
        yield
        y_ref[j * C:(j + 1) * C, p * PAIR:(p + 1) * PAIR] = y0 + mm(d['rb'], bd2(u_pair))
        t_hi = _cat_rows(u_pair[0], v_hi)
        t_lo = _cat_rows(u_pair[1], v_lo)
        w_hi, w_lo = d['bhkh']
        upd = _dg(_cat_rows(t_hi, t_lo), _cat_rows(w_hi, w_hi), _TN) + _dg(t_hi, w_lo, _TN)
        state[skey(j, p)] = s_prev * d['p_end'] + jnp.where(on_diag, upd, 0.0)
        yield

    def post_chain(j):
        rows = slice(j * C, (j + 1) * C)
        d = prepped.pop(j)
        y = y_ref[rows, :]
        mean = head_sum(y) * (1.0 / HEAD_DIM)
        dev = y - mean
        yield
        var = head_sum(dev * dev) * (1.0 / HEAD_DIM)
        yn = dev * lax.rsqrt(var + GN_EPS) * lnw_ref[...] + lnb_ref[...]
        out_ref[rows, :] = (yn + d['bonus']) * d['gate']
        yield

    tasks = {}
    for j in range(n_sub):
        tasks[('prep', j)] = (lambda j=j: [prep_chain(j)], [('prep', j - 1), ('local', j - RWKV_LOCAL_IN_FLIGHT)])
        tasks[('local', j)] = (lambda j=j: [local_chain(j, p) for p in pairs],
                               [('prep', j), ('local', j - RWKV_LOCAL_IN_FLIGHT)])
        tasks[('state', j)] = (lambda j=j: [state_chain(j, p) for p in pairs],
                               [('local', j)] + ([('state', j - 1)] if fused else []))
        tasks[('post', j)] = (lambda j=j: [post_chain(j)], [('state', j)])
    if fused:
        n_k = WINDOW + CHUNK
        k_all = _cat_rows(kp_ref[...], kc_ref[...]).astype(BF16)
        v_all = _cat_rows(vp_ref[...], vc_ref[...]).astype(BF16)
        first_valid = jnp.where(c == 0, WINDOW, 0)
        kcol = lax.broadcasted_iota(jnp.int32, (1, n_k), 1)
        sinks = _group_sinks(sink_ref, CHUNK)

        def attn_chains(j):
            rows = slice(j * C, (j + 1) * C)
            keys = slice(j * C, j * C + n_k)
            valid = kcol + j * C >= first_valid if j * C < WINDOW else None
            return [_attn_chain(q_ref[rows, :], k_all[keys, kvh * HEAD_DIM:(kvh + 1) * HEAD_DIM],
                                v_all[keys, kvh * HEAD_DIM:(kvh + 1) * HEAD_DIM], bias_ref[kvh], sinks[kvh], valid,
                                attn_ref, rows, kvh) for kvh in range(KV_HEADS)]

        for j in range(n_sub):
            tasks[('attn', j)] = (lambda j=j: attn_chains(j), [('attn', j - 1)])
        for t, chain in enumerate(proj_chains(xn_ref, 1 - cur)):
            tasks[('proj', t)] = (lambda chain=chain: [chain], [('proj', t - 1), ('prep', min(t, n_sub - 1))])
    _run_tasks(tasks)

    def store_state(dst, s_bd, p):
        dst[2 * p] = s_bd[:HEAD_DIM, :HEAD_DIM]
        dst[2 * p + 1] = s_bd[HEAD_DIM:, HEAD_DIM:]

    if not fused:
        for j in range(n_sub):
            for p in pairs:
                store_state(s_ref.at[j], state[(j, p)], p)
        return
    carry_ref[0:1, :] = zr_ref[R - 1:R, :]
    for p in pairs:
        sbd_ref[p] = state[p]
    kp_ref[...] = kc_ref[R - WINDOW:R, :]
    vp_ref[...] = vc_ref[R - WINDOW:R, :]

    @pl.when(c == steps - 1)
    def _():
        for p in pairs:
            store_state(s_ref.at[0], state[p], p)
        ktail_ref[0] = kc_ref[R - WINDOW:R, :]
        vtail_ref[0] = vc_ref[R - WINDOW:R, :]
        shiftout_ref[0] = zr_ref[R - 1:R, :]


def _rwkv_operands(lw):
    seg = jnp.asarray(np.kron(np.eye(2), np.ones((HEAD_DIM, HEAD_DIM))), BF16)
    ones = np.ones((CHUNK, CHUNK))
    tri3 = jnp.asarray(np.concatenate([np.tile(np.tril(ones), (1, 3)), np.tile(np.triu(ones, 1), (1, 3))]), BF16)
    row = lambda name: lw[name].reshape(1, -1)
    return [row('rwkv_mu'), row('rwkv_w0'), lw['rwkv_w2'].astype(BF16), row('rwkv_a0'),
            lw['rwkv_a2'].astype(BF16), lw['rwkv_g2'].astype(BF16), row('rwkv_k_k'), row('rwkv_k_a'),
            row('rwkv_r_k'), row('rwkv_ln_w'), row('rwkv_ln_b'), seg, tri3]


def _rwkv_mixer(zr, shift_prev, state0, lw, batch, seq):
    assert seq <= CHUNK
    if seq < CHUNK:
        zr = jnp.pad(zr.reshape(batch, seq, RWKV_PROJ), ((0, 0), (0, CHUNK - seq), (0, 0))).reshape(-1, RWKV_PROJ)
    params = _rwkv_operands(lw)
    vmem = pl.BlockSpec(memory_space=pltpu.VMEM)
    out, state = pl.pallas_call(
        functools.partial(_rwkv_kernel, seq, False, 1),
        in_specs=[vmem] * (3 + len(params)),
        out_specs=[vmem, vmem],
        out_shape=[jax.ShapeDtypeStruct((batch * CHUNK, RWKV_WIDTH), F32), jax.ShapeDtypeStruct(state0.shape, F32)],
        scratch_shapes=[pltpu.VMEM((batch * CHUNK, RWKV_WIDTH), F32)],
        compiler_params=pltpu.CompilerParams(vmem_limit_bytes=V7X_VMEM_LIMIT_BYTES),
        name="rwkv_mixer",
    )(zr, shift_prev, state0, *params)
    if seq < CHUNK:
        out = out.reshape(batch, CHUNK, RWKV_WIDTH)[:, :seq].reshape(batch * seq, RWKV_WIDTH)
    return out, state


def _prompt_mixer(x2d, shift_prev, state0, lw, bias, batch, seq):
    rows = CHUNK * RWKV_SUB_CHUNKS
    steps = seq // rows
    total = batch * steps
    params = _rwkv_operands(lw)
    seq_block = lambda shape: pl.BlockSpec((1,) + shape, lambda g: (g // steps,) + (0,) * len(shape))
    row_spec = lambda w: pl.BlockSpec((rows, w), lambda g: (g, 0))

    def resident(shape):
        nd = len(shape)
        return pl.BlockSpec(shape, lambda g: (0,) * nd, pipeline_mode=pl.Buffered(1))

    in_specs = [pl.BlockSpec((rows, D_MODEL), lambda g: (jnp.minimum(g + 1, total - 1), 0)),
                pl.BlockSpec((rows, D_MODEL), lambda g: (0, 0), pipeline_mode=pl.Buffered(1)),
                resident((1, D_MODEL)), resident((D_MODEL, IN_PROJ)),
                seq_block((1, RWKV_PROJ)), seq_block((RWKV_HEADS, HEAD_DIM, HEAD_DIM))]
    in_specs += [resident(p.shape) for p in params]
    in_specs += [pl.BlockSpec(memory_space=pltpu.SMEM), resident(bias.shape)]
    out_specs = [row_spec(RWKV_WIDTH), seq_block((RWKV_HEADS, HEAD_DIM, HEAD_DIM)), row_spec(ATTN_WIDTH),
                 seq_block((WINDOW, KV_WIDTH)), seq_block((WINDOW, KV_WIDTH)), seq_block((1, RWKV_PROJ))]
    n = batch * seq
    out_shape = [jax.ShapeDtypeStruct((n, RWKV_WIDTH), F32), jax.ShapeDtypeStruct(state0.shape, F32),
                 jax.ShapeDtypeStruct((n, ATTN_WIDTH), F32), jax.ShapeDtypeStruct((batch, WINDOW, KV_WIDTH), F32),
                 jax.ShapeDtypeStruct((batch, WINDOW, KV_WIDTH), F32),
                 jax.ShapeDtypeStruct((batch, 1, RWKV_PROJ), F32)]
    scratch = [pltpu.VMEM((8, RWKV_PROJ), F32), pltpu.VMEM((rows, RWKV_WIDTH), F32),
               pltpu.VMEM((RWKV_HEADS // 2, 2 * HEAD_DIM, 2 * HEAD_DIM), F32),
               pltpu.VMEM((2, rows, RWKV_PROJ), F32), pltpu.VMEM((2, rows, ATTN_WIDTH), BF16),
                pltpu.VMEM((2, rows, KV_WIDTH), F32), pltpu.VMEM((2, rows, KV_WIDTH), F32),
                pltpu.VMEM((WINDOW, KV_WIDTH), F32), pltpu.VMEM((WINDOW, KV_WIDTH), F32)]
    r_out, state, a_out, k_tail, v_tail, shift_new = pl.pallas_call(
        functools.partial(_rwkv_kernel, rows, True, steps),
        grid=(total,),
        in_specs=in_specs,
        out_specs=out_specs,
        out_shape=out_shape,
        scratch_shapes=scratch,
        compiler_params=_params("arbitrary"),
        name="prompt_mixer",
    )(x2d, x2d, lw['norm_mix_g'], lw['w_in'], shift_prev, state0, *params, lw['attn_sink'], bias)
    return a_out, r_out, state, k_tail, v_tail, shift_new


def _memkv_kernel(m_ref, g_ref, wk_ref, wv_ref, k_ref, v_ref):
    mn = _rms(m_ref[...], g_ref[...]).astype(BF16)
    k_ref[...] = jnp.dot(mn, wk_ref[...], preferred_element_type=F32)
    v_ref[...] = jnp.dot(mn, wv_ref[...], preferred_element_type=F32)


def _memory_kv(mem2d, g, w_mk, w_mv):
    n = mem2d.shape[0]
    tm = min(ROW_TILE, n)
    row = lambda w: pl.BlockSpec((tm, w), lambda i: (i, 0))
    return pl.pallas_call(
        _memkv_kernel,
        grid=(n // tm,),
        in_specs=[row(D_MODEL), _const_spec((1, D_MODEL)), _const_spec(w_mk.shape), _const_spec(w_mv.shape)],
        out_specs=[row(MEM_WIDTH), row(MEM_WIDTH)],
        out_shape=[jax.ShapeDtypeStruct((n, MEM_WIDTH), F32)] * 2,
        compiler_params=_params("parallel"),
        name="memory_kv",
    )(mem2d, g, w_mk, w_mv)


def _tail_kernel(x_ref, a_ref, r_ref, mk_ref, mv_ref, wo_ref, gc_ref, wq_ref, wco_ref, gm_ref, wu_ref, wd_ref, gf_ref,
                 y_ref):
    n_seq = mk_ref.shape[0]
    rows_per_seq = x_ref.shape[0] // n_seq
    x1 = x_ref[...] + _dot(a_ref[...], wo_ref[:ATTN_WIDTH, :]) + _dot(r_ref[...], wo_ref[ATTN_WIDTH:, :])
    q = _dot(_rms(x1, gc_ref[...]), wq_ref[...]).astype(BF16)
    outs = {}

    def cross_chain(b, h):
        rows = slice(b * rows_per_seq, (b + 1) * rows_per_seq)
        sl = slice(h * MEM_HEAD_DIM, (h + 1) * MEM_HEAD_DIM)
        s = _dg(q[rows, sl], mk_ref[b, :, sl].astype(BF16), _NT) * (MEM_HEAD_DIM ** -0.5)
        yield
        p = jnp.exp(s - jnp.max(s, axis=-1, keepdims=True))
        den = jnp.sum(p, axis=-1, keepdims=True)
        yield
        outs[(b, h)] = _dg(p.astype(BF16), mv_ref[b, :, sl].astype(BF16), _NN) * (1.0 / den)
        yield

    _run_interleaved([cross_chain(b, h) for b in range(n_seq) for h in range(MEM_HEADS)])
    o = _cat_rows(*[_cat_lanes(*[outs[(b, h)] for h in range(MEM_HEADS)]) for b in range(n_seq)])
    x2 = x1 + _dot(o, wco_ref[...])
    up = _dot(_rms(x2, gm_ref[...]), wu_ref[...])
    act = jnp.square(jnp.maximum(up, 0.0))
    y_ref[...] = _rms(x2 + _dot(act, wd_ref[...]), gf_ref[...])


def _tail(x2d, a_out, r_out, mk, mv, lw, batch, seq):
    n = batch * seq
    tq = min(TAIL_ROW_TILE, n)
    if seq >= tq:
        assert seq % tq == 0
        seq_per_tile, tiles_per_seq = 1, seq // tq
        mem_spec = pl.BlockSpec((1, N_MEM, MEM_WIDTH), lambda i: (i // tiles_per_seq, 0, 0))
    else:
        assert tq % seq == 0
        seq_per_tile = tq // seq
        mem_spec = pl.BlockSpec((seq_per_tile, N_MEM, MEM_WIDTH), lambda i: (i, 0, 0))
    row = lambda w: pl.BlockSpec((tq, w), lambda i: (i, 0))

    def resident(shape):
        nd = len(shape)
        return pl.BlockSpec(shape, lambda *_: (0,) * nd, pipeline_mode=pl.Buffered(1))

    weights = [lw['w_out'], lw['norm_cross_g'], lw['w_cq'], lw['w_co'], lw['norm_mlp_g'], lw['w_up'], lw['w_down'],
               lw['norm_final_g']]
    return pl.pallas_call(
        _tail_kernel,
        grid=(n // tq,),
        in_specs=[row(D_MODEL), row(ATTN_WIDTH), row(RWKV_WIDTH), mem_spec, mem_spec]
                 + [resident(w.shape) for w in weights],
        out_specs=row(D_MODEL),
        out_shape=jax.ShapeDtypeStruct(x2d.shape, F32),
        compiler_params=_params("parallel"),
        name="tail",
    )(x2d, a_out, r_out, mk, mv, *weights)


def _trunk(x, mk, mv, k_past, v_past, shift_prev, state0, lw, table):
    batch, seq = x.shape[0], x.shape[1]
    x2d = x.reshape(batch * seq, D_MODEL)
    if k_past is None:
        bias = _rel_bias(table, CHUNK, WINDOW + CHUNK)
        a_out, r_out, state, k_buf, v_buf, shift_new = _prompt_mixer(x2d, shift_prev, state0, lw, bias, batch, seq)
    else:
        q, k, v, zr = _in_proj(x2d, lw['norm_mix_g'], lw['w_in'])
        bias = _rel_bias(table, seq, WINDOW + seq)
        a_out = _sample_attention(q, k, v, k_past.reshape(batch * WINDOW, KV_WIDTH),
                                  v_past.reshape(batch * WINDOW, KV_WIDTH), lw['attn_sink'], bias, seq)
        k_buf = jnp.concatenate([k_past, k.reshape(batch, seq, KV_WIDTH)], axis=1)[:, -WINDOW:]
        v_buf = jnp.concatenate([v_past, v.reshape(batch, seq, KV_WIDTH)], axis=1)[:, -WINDOW:]
        r_out, state = _rwkv_mixer(zr, shift_prev, state0, lw, batch, seq)
        shift_new = zr.reshape(batch, seq, RWKV_PROJ)[:, -1:]
    y = _tail(x2d, a_out, r_out, mk, mv, lw, batch, seq)
    kv_shape = (batch, WINDOW, KV_HEADS, HEAD_DIM)
    return y.reshape(x.shape), k_buf.reshape(kv_shape), v_buf.reshape(kv_shape), shift_new, state


def kernel(x_prompt, x_sample, mem_prompt, cache_attn_k, cache_attn_v, cache_mem_k, cache_mem_v, state_shift,
           state_wkv, norm_mix_g, w_in, attn_sink, rel_bias_table, rwkv_mu, rwkv_w0, rwkv_w2, rwkv_a0, rwkv_a2,
           rwkv_g2, rwkv_k_k, rwkv_k_a, rwkv_r_k, rwkv_ln_w, rwkv_ln_b, w_out, norm_cross_g, norm_mem_g, w_cq,
           w_mk, w_mv, w_co, norm_mlp_g, w_up, w_down, norm_final_g):
    assert norm_mix_g.shape[0] == 1, "single-layer trunk"
    bp, dec_b = x_prompt.shape[0], x_sample.shape[0]
    vec = lambda p: p[0].reshape(1, -1)
    w_in_b, w_out_b, w_cq_b, w_co_b, w_up_b, w_down_b, w_mk_b, w_mv_b = _to_bf16(
        w_in[0], w_out[0], w_cq[0], w_co[0], w_up[0], w_down[0], w_mk[0], w_mv[0])
    lw = {
        'norm_mix_g': vec(norm_mix_g), 'w_in': w_in_b, 'attn_sink': attn_sink[0],
        'rwkv_mu': rwkv_mu[0], 'rwkv_w0': rwkv_w0[0], 'rwkv_w2': rwkv_w2[0], 'rwkv_a0': rwkv_a0[0],
        'rwkv_a2': rwkv_a2[0], 'rwkv_g2': rwkv_g2[0], 'rwkv_k_k': rwkv_k_k[0], 'rwkv_k_a': rwkv_k_a[0],
        'rwkv_r_k': rwkv_r_k[0], 'rwkv_ln_w': rwkv_ln_w[0], 'rwkv_ln_b': rwkv_ln_b[0],
        'w_out': w_out_b, 'norm_cross_g': vec(norm_cross_g), 'w_cq': w_cq_b,
        'w_co': w_co_b, 'norm_mlp_g': vec(norm_mlp_g), 'w_up': w_up_b,
        'w_down': w_down_b, 'norm_final_g': norm_final_g.reshape(1, -1),
    }
    mk, mv = _memory_kv(mem_prompt.reshape(bp * N_MEM, D_MODEL), vec(norm_mem_g), w_mk_b, w_mv_b)
    mk = mk.reshape(bp, N_MEM, MEM_WIDTH)
    mv = mv.reshape(bp, N_MEM, MEM_WIDTH)
    shift0 = jnp.zeros((bp, 1, RWKV_PROJ), F32)
    wkv0 = jnp.zeros((bp, RWKV_HEADS, HEAD_DIM, HEAD_DIM), F32)
    yp, pk, pv, psh, pS = _trunk(x_prompt, mk, mv, None, None, shift0, wkv0, lw, rel_bias_table)
    ys, sk, sv, ssh, sS = _trunk(
        x_sample, cache_mem_k[0].reshape(dec_b, N_MEM, MEM_WIDTH), cache_mem_v[0].reshape(dec_b, N_MEM, MEM_WIDTH),
        cache_attn_k[0].reshape(dec_b, WINDOW, KV_WIDTH), cache_attn_v[0].reshape(dec_b, WINDOW, KV_WIDTH),
        state_shift[0], state_wkv[0], lw, rel_bias_table)
    mem_shape = (1, bp, N_MEM, MEM_HEADS, MEM_HEAD_DIM)
    return (yp, ys, pk[None], pv[None], mk.reshape(mem_shape), mv.reshape(mem_shape), psh[None], pS[None],
            sk[None], sv[None], ssh[None], sS[None])
```

```python
import functools
import math

import numpy as np
import jax
import jax.numpy as jnp
from jax import lax
from jax.experimental import pallas as pl
from jax.experimental.pallas import tpu as pltpu

F32 = jnp.float32
BF16 = jnp.bfloat16

D_MODEL = 1024
CHUNK = 64
WINDOW = 128
HEAD_DIM = 64
ATTN_WIDTH = 512
ATTN_HEADS = 8
KV_HEADS = 2
GROUP = 4
KV_WIDTH = 128
RWKV_WIDTH = 512
RWKV_HEADS = 8
DECAY_LORA = 64
AAA_LORA = 64
GATE_LORA = 128
RWKV_PROJ = 1792
IN_PROJ = 2560
N_MEM = 256
MEM_HEADS = 4
MEM_HEAD_DIM = 128
MEM_WIDTH = 512
D_FF = 4096
REL_BUCKETS = 32
REL_MAX_DIST = 128
NORM_EPS = 1e-6
GN_EPS = 64e-5
LOG2_E = math.log2(math.e)

V7X_VMEM_LIMIT_BYTES = 52 * 1024 * 1024
MEMORY_KV_ROW_TILE = 512
IN_PROJ_ROW_TILE = 512
PROJ_TILE = 256
TAIL_ROW_TILE = 512
WEIGHT_CAST_STEPS = 8
RWKV_SUB_CHUNKS = 4
RWKV_LOCAL_IN_FLIGHT = 3


def _params(*sem):
    return pltpu.CompilerParams(dimension_semantics=sem, vmem_limit_bytes=V7X_VMEM_LIMIT_BYTES)


def _const_spec(shape):
    nd = len(shape)
    return pl.BlockSpec(shape, lambda *_: (0,) * nd)


_NN = ((1,), (0,))
_NT = ((1,), (1,))
_TN = ((0,), (0,))


def _dg(a, b, dims):
    return lax.dot_general(a, b, (dims, ((), ())), preferred_element_type=F32)


def _dot(a, b):
    return _dg(a.astype(BF16), b.astype(BF16), _NN)


def _cat_rows(*xs):
    return jnp.concatenate(xs, axis=0)


def _cat_lanes(*xs):
    return jnp.concatenate(xs, axis=1)


def _run_interleaved(chains):
    active = list(chains)
    while active:
        still = []
        for ch in active:
            try:
                next(ch)
                still.append(ch)
            except StopIteration:
                pass
        active = still


def _run_tasks(tasks):
    finished = set()
    running = {}
    waiting = dict(tasks)
    while waiting or running:
        for name in [n for n, (_, deps) in waiting.items() if all(d in finished or d not in tasks for d in deps)]:
            running[name] = list(waiting.pop(name)[0]())
        for name in list(running):
            alive = []
            for ch in running[name]:
                try:
                    next(ch)
                    alive.append(ch)
                except StopIteration:
                    pass
            if alive:
                running[name] = alive
            else:
                del running[name]
                finished.add(name)


def _rms(x, g):
    return x * lax.rsqrt(jnp.mean(x * x, axis=-1, keepdims=True) + NORM_EPS) * g


def _cast_kernel(*refs):
    n = len(refs) // 2
    for src, dst in zip(refs[:n], refs[n:]):
        dst[...] = src[...].astype(BF16)


def _to_bf16(*weights):
    specs = [pl.BlockSpec((w.shape[0] // WEIGHT_CAST_STEPS, w.shape[1]), lambda i: (i, 0)) for w in weights]
    return pl.pallas_call(
        _cast_kernel,
        grid=(WEIGHT_CAST_STEPS,),
        in_specs=specs,
        out_specs=specs,
        out_shape=[jax.ShapeDtypeStruct(w.shape, BF16) for w in weights],
        compiler_params=_params("parallel"),
        name="cast_weights",
    )(*weights)


def _inproj_kernel(x_ref, g_ref, w_ref, q_ref, k_ref, v_ref, zr_ref):
    h = _rms(x_ref[...], g_ref[...]).astype(BF16)
    q = jnp.dot(h, w_ref[:, :ATTN_WIDTH], preferred_element_type=F32)
    q_ref[...] = (q * (HEAD_DIM ** -0.5 * LOG2_E)).astype(BF16)
    k_ref[...] = jnp.dot(h, w_ref[:, ATTN_WIDTH:ATTN_WIDTH + KV_WIDTH], preferred_element_type=F32)
    v_ref[...] = jnp.dot(h, w_ref[:, ATTN_WIDTH + KV_WIDTH:ATTN_WIDTH + 2 * KV_WIDTH],
                         preferred_element_type=F32)
    zr_ref[...] = jnp.dot(h, w_ref[:, ATTN_WIDTH + 2 * KV_WIDTH:], preferred_element_type=F32)


def _in_proj(x2d, g, w_bf16):
    n = x2d.shape[0]
    tm = min(IN_PROJ_ROW_TILE, n)
    row = lambda w: pl.BlockSpec((tm, w), lambda i: (i, 0))
    return pl.pallas_call(
        _inproj_kernel,
        grid=(n // tm,),
        in_specs=[row(D_MODEL), _const_spec((1, D_MODEL)), _const_spec((D_MODEL, IN_PROJ))],
        out_specs=[row(ATTN_WIDTH), row(KV_WIDTH), row(KV_WIDTH), row(RWKV_PROJ)],
        out_shape=[jax.ShapeDtypeStruct((n, ATTN_WIDTH), BF16)]
                  + [jax.ShapeDtypeStruct((n, w), F32) for w in (KV_WIDTH, KV_WIDTH, RWKV_PROJ)],
        compiler_params=_params("parallel"),
        name="in_proj",
    )(x2d, g, w_bf16)


def _t5_bucket(rel):
    half = REL_BUCKETS // 2
    max_exact = half // 2
    assert REL_MAX_DIST == max_exact * 2 ** 4 and half - max_exact == 2 * 4
    n = np.abs(rel)
    large = max_exact + sum((n * n >= max_exact * max_exact * 2 ** t).astype(np.int64)
                            for t in range(1, half - max_exact))
    return (np.where(rel > 0, half, 0) + np.where(n < max_exact, n, large)).astype(np.int32)


def _bias_kernel(table_ref, bucket_ref, out_ref):
    bucket = bucket_ref[...]
    hits = [bucket == b for b in range(REL_BUCKETS)]
    for h in range(ATTN_HEADS):
        acc = jnp.zeros(bucket.shape, F32)
        for b in range(REL_BUCKETS):
            acc = jnp.where(hits[b], table_ref[b, h], acc)
        out_ref[h] = acc * LOG2_E


def _rel_bias(table, n_q, n_k):
    rel = np.arange(n_k)[None, :] - WINDOW - np.arange(n_q)[:, None]
    bucket = jnp.asarray(_t5_bucket(rel))
    bias = pl.pallas_call(
        _bias_kernel,
        in_specs=[pl.BlockSpec(memory_space=pltpu.SMEM), pl.BlockSpec(memory_space=pltpu.VMEM)],
        out_specs=pl.BlockSpec(memory_space=pltpu.VMEM),
        out_shape=jax.ShapeDtypeStruct((ATTN_HEADS, n_q, n_k), F32),
        name="rel_bias",
    )(table, bucket)
    return bias.reshape(KV_HEADS, GROUP * n_q, n_k)


def _group_sinks(sink_ref, n_q):
    row_group = lax.broadcasted_iota(jnp.int32, (GROUP * n_q, 1), 0) // n_q
    sinks = []
    for kvh in range(KV_HEADS):
        sink = jnp.zeros((GROUP * n_q, 1), F32)
        for g in range(GROUP):
            sink = jnp.where(row_group == g, sink_ref[kvh * GROUP + g] * LOG2_E, sink)
        sinks.append(sink)
    return sinks


def _attn_chain(q, keys, vals, bias, sink, valid, o_ref, rows, kvh):
    n_q = q.shape[0]
    qh = _cat_rows(*[q[:, (kvh * GROUP + g) * HEAD_DIM:(kvh * GROUP + g + 1) * HEAD_DIM]
                     for g in range(GROUP)])
    s = _dg(qh, keys, _NT) + bias
    if valid is not None:
        s = jnp.where(valid, s, -jnp.inf)
    yield
    m = jnp.maximum(jnp.max(s, axis=-1, keepdims=True), sink)
    p = jnp.exp2(s - m)
    den = jnp.sum(p, axis=-1, keepdims=True) + jnp.exp2(sink - m)
    yield
    o = _dg(p.astype(BF16), vals, _NN) * (1.0 / den)
    for g in range(GROUP):
        head = kvh * GROUP + g
        o_ref[rows, head * HEAD_DIM:(head + 1) * HEAD_DIM] = o[g * n_q:(g + 1) * n_q]
    yield


def _sample_attn_kernel(seq, sink_ref, q_ref, kp_ref, kn_ref, vp_ref, vn_ref, bias_ref, o_ref):
    batch = q_ref.shape[0] // seq
    sinks = _group_sinks(sink_ref, seq)
    chains = []
    for b in range(batch):
        rows = slice(b * seq, (b + 1) * seq)
        past = slice(b * WINDOW, (b + 1) * WINDOW)
        k_all = _cat_rows(kp_ref[past, :], kn_ref[rows, :]).astype(BF16)
        v_all = _cat_rows(vp_ref[past, :], vn_ref[rows, :]).astype(BF16)
        for kvh in range(KV_HEADS):
            lanes = slice(kvh * HEAD_DIM, (kvh + 1) * HEAD_DIM)
            chains.append(_attn_chain(q_ref[rows, :], k_all[:, lanes], v_all[:, lanes], bias_ref[kvh], sinks[kvh],
                                      None, o_ref, rows, kvh))
    _run_interleaved(chains)


def _sample_attention(q, k, v, k_past, v_past, sink, bias, seq):
    vmem = pl.BlockSpec(memory_space=pltpu.VMEM)
    return pl.pallas_call(
        functools.partial(_sample_attn_kernel, seq),
        in_specs=[pl.BlockSpec(memory_space=pltpu.SMEM)] + [vmem] * 6,
        out_specs=vmem,
        out_shape=jax.ShapeDtypeStruct(q.shape, F32),
        name="sample_attention",
    )(sink, q, k_past, k, v_past, v, bias)


def _split2(x):
    hi = x.astype(BF16)
    lo = (x - hi.astype(F32)).astype(BF16)
    return hi, lo


def _softplus(x):
    return jnp.maximum(x, 0.0) + jnp.log(1.0 + jnp.exp(-jnp.abs(x)))


def _sigmoid(x):
    return 1.0 / (1.0 + jnp.exp(-x))


def _rwkv_kernel(valid_rows, fused, steps, *refs):
    if fused:
        (xn_ref, x0_ref, gmix_ref, win_ref, shift_ref, s0_ref, mu_ref, w0_ref, w2_ref, a0_ref, a2_ref, g2_ref, kk_ref,
         ka_ref, rk_ref, lnw_ref, lnb_ref, seg_ref, tri_ref, sink_ref, bias_ref) = refs[:21]
        (out_ref, s_ref, attn_ref, ktail_ref, vtail_ref, shiftout_ref, carry_ref, y_ref, sbd_ref, zr_scr, q_scr, k_scr,
         v_scr, kp_ref, vp_ref) = refs[21:]
        g = pl.program_id(0)
        c = lax.rem(g, steps)
        cur = lax.rem(g, 2)
        zr_ref, q_ref, kc_ref, vc_ref = zr_scr.at[cur], q_scr.at[cur], k_scr.at[cur], v_scr.at[cur]
    else:
        (zr_ref, shift_ref, s0_ref, mu_ref, w0_ref, w2_ref, a0_ref, a2_ref, g2_ref, kk_ref, ka_ref, rk_ref, lnw_ref,
         lnb_ref, seg_ref, tri_ref) = refs[:16]
        out_ref, s_ref, y_ref = refs[16:]
    C = CHUNK
    R = out_ref.shape[0]
    n_sub = R // C
    pairs = range(RWKV_HEADS // 2)
    PAIR = 2 * HEAD_DIM
    W = RWKV_WIDTH

    def block_diagonal(s0, p):
        zero = jnp.zeros((HEAD_DIM, HEAD_DIM), F32)
        return _cat_rows(_cat_lanes(s0[2 * p], zero), _cat_lanes(zero, s0[2 * p + 1]))

    if fused:
        @pl.when(c == 0)
        def _():
            carry_ref[0:1, :] = shift_ref[0]
            for p in pairs:
                sbd_ref[p] = block_diagonal(s0_ref.at[0], p)
            kp_ref[...] = jnp.zeros(kp_ref.shape, F32)
            vp_ref[...] = jnp.zeros(vp_ref.shape, F32)

    def proj_chains(x_ref, slot):
        h = _rms(x_ref[...], gmix_ref[...]).astype(BF16)
        q_end, k_end, v_end = ATTN_WIDTH, ATTN_WIDTH + KV_WIDTH, ATTN_WIDTH + 2 * KV_WIDTH

        def tile(lo, hi):
            z = jnp.dot(h, win_ref[:, lo:hi], preferred_element_type=F32)
            if hi <= q_end:
                q_scr[slot, :, lo:hi] = (z * (HEAD_DIM ** -0.5 * LOG2_E)).astype(BF16)
            elif lo == q_end:
                k_scr[slot] = z[:, :KV_WIDTH]
                v_scr[slot] = z[:, KV_WIDTH:]
            else:
                zr_scr[slot, :, lo - v_end:hi - v_end] = z
            yield

        assert q_end % PROJ_TILE == 0 and v_end - q_end == PROJ_TILE
        return [tile(lo, lo + PROJ_TILE) for lo in range(0, IN_PROJ, PROJ_TILE)]

    if fused:
        @pl.when(g == 0)
        def _():
            _run_interleaved(proj_chains(x0_ref, 0))

    seg = seg_ref[...]
    seg2 = _cat_rows(seg, seg)

    def head_sum(x):
        hi, lo = _split2(x)
        tiles = [_dg(_cat_lanes(hi[:, t * PAIR:(t + 1) * PAIR], lo[:, t * PAIR:(t + 1) * PAIR]), seg2, _NN)
                 for t in range(W // PAIR)]
        return _cat_lanes(*tiles)

    first_row = lax.broadcasted_iota(jnp.int32, (C, 1), 0) == 0
    tri3 = tri_ref[...]

    lane = lax.broadcasted_iota(jnp.int32, (C, PAIR), 1)
    trow = lax.broadcasted_iota(jnp.int32, (C, PAIR), 0)
    even = lane < HEAD_DIM
    tcol = jnp.where(even, lane, lane - HEAD_DIM)
    strict = tcol < trow
    incl = tcol <= trow
    eye = jnp.where(tcol == trow, 1.0, 0.0).astype(F32)
    brow = lax.broadcasted_iota(jnp.int32, (PAIR, PAIR), 0) < HEAD_DIM
    bcol = lax.broadcasted_iota(jnp.int32, (PAIR, PAIR), 1) < HEAD_DIM
    on_diag = brow == bcol

    def bd(x):
        zero = jnp.zeros_like(x)
        return _cat_rows(jnp.where(even, x, zero), jnp.where(even, zero, x))

    def bd2(pair):
        return bd(pair[0]), bd(pair[1])

    def mm(a_pair, w_pair, dims=_NN):
        if dims == _NT:
            w_pair = (w_pair[0].T, w_pair[1].T)
        first = _dg(_cat_lanes(a_pair[0], a_pair[1]), _cat_rows(w_pair[0], w_pair[0]), _NN)
        return first + _dg(a_pair[0], w_pair[1], _NN)

    prepped = {}
    ready = {}
    if fused:
        state = {p: sbd_ref[p] for p in pairs}
        skey = lambda j, p: p
    else:
        state = {(j, p): block_diagonal(s0_ref.at[j], p) for j in range(n_sub) for p in pairs}
        skey = lambda j, p: (j, p)

    def prep_chain(j):
        rows = slice(j * C, (j + 1) * C)
        zr = zr_ref[rows, :]
        if not fused:
            before = shift_ref[j]
        else:
            before = carry_ref[0:1, :] if j == 0 else zr_ref[j * C - 1:j * C, :]
        z_prev = jnp.where(first_row, before, pltpu.roll(zr, 1, axis=0))
        zs = zr + (z_prev - zr) * mu_ref[...]
        r = zs[:, :W]
        k = zs[:, W:2 * W]
        v = zs[:, 2 * W:3 * W]
        wd = zs[:, 3 * W:3 * W + DECAY_LORA]
        ad = zs[:, 3 * W + DECAY_LORA:3 * W + DECAY_LORA + AAA_LORA]
        gd = zs[:, 3 * W + DECAY_LORA + AAA_LORA:]
        w_log = -_softplus(-(w0_ref[...] + _dot(jnp.tanh(wd), w2_ref[...]))) - 0.5
        lw = -jnp.exp(w_log)
        a = _sigmoid(a0_ref[...] + _dot(ad, a2_ref[...]))
        gate = _dot(_sigmoid(gd), g2_ref[...])
        kk = k * kk_ref[...]
        kk = kk * lax.rsqrt(jnp.maximum(head_sum(kk * kk), 1e-24))
        k2 = k * (1.0 + (a - 1.0) * ka_ref[...])
        if valid_rows < C:
            live = lax.broadcasted_iota(jnp.int32, (C, 1), 0) < valid_rows
            lw = jnp.where(live, lw, 0.0)
            kk = jnp.where(live, kk, 0.0)
            k2 = jnp.where(live, k2, 0.0)
        bvec = kk * a
        yield
        l1 = lw.astype(BF16)
        rem = lw - l1.astype(F32)
        l2 = rem.astype(BF16)
        l3 = (rem - l2.astype(F32)).astype(BF16)
        sums = _dg(tri3, _cat_rows(l1, l2, l3), _NN)
        li = sums[:C]
        lrev = sums[C:]
        yield
        inv_p = jnp.exp(-li)
        to_end = jnp.exp(lrev)
        prepped[j] = dict(
            at=_split2(-kk * jnp.exp(li - lw)), rt=_split2(r * jnp.exp(li)), bt=_split2(bvec * inv_p),
            kt=_split2(k2 * inv_p), bh=_split2(bvec * to_end), kh=_split2(k2 * to_end), v=_split2(v),
            p_end=jnp.exp(li[C - 1:C, :]), bonus=head_sum(r * k2 * rk_ref[...]) * v, gate=gate)
        yield

    def local_chain(j, p):
        d = prepped[j]
        lanes = slice(p * PAIR, (p + 1) * PAIR)
        cut = lambda pair: (pair[0][:, lanes], pair[1][:, lanes])
        at_p, rt_p, bt_p, kt_p, bh_p, kh_p, v_p = map(cut, (d['at'], d['rt'], d['bt'], d['kt'], d['bh'], d['kh'],
                                                            d['v']))
        left = (_cat_rows(at_p[0], rt_p[0]), _cat_rows(at_p[1], rt_p[1]))
        right = (_cat_rows(bd(bt_p[0]), bd(kt_p[0])), _cat_rows(bd(bt_p[1]), bd(kt_p[1])))
        aa = mm(left, right, _NT)
        yield
        a_ab = jnp.where(strict, aa[:C, :PAIR], 0.0)
        a_ak = jnp.where(strict, aa[:C, PAIR:], 0.0)
        a_rb = jnp.where(incl, aa[C:, :PAIR], 0.0)
        a_rk = jnp.where(incl, aa[C:, PAIR:], 0.0)
        inv = eye + a_ab
        ps = _split2(a_ab)
        power = mm(ps, bd2(ps))
        span = 2
        yield
        while span < C:
            ps = _split2(power)
            pw = bd2(ps)
            ih = _split2(inv)
            if span * 2 < C:
                both = mm((_cat_rows(ih[0], ps[0]), _cat_rows(ih[1], ps[1])), pw)
                inv = inv + both[:C]
                power = both[C:]
            else:
                inv = inv + mm(ih, pw)
            span *= 2
            yield
        ready[(j, p)] = dict(inv=_split2(inv), akrk=_split2(_cat_rows(a_ak, a_rk)), rb=_split2(a_rb), left=left,
                             bhkh=(_cat_rows(bh_p[0], kh_p[0]), _cat_rows(bh_p[1], kh_p[1])), v=v_p,
                             p_end=d['p_end'][:, lanes])

    def state_chain(j, p):
        d = ready.pop((j, p))
        s_prev = state[skey(j, p)]
        v_hi, v_lo = d['v']
        both = mm(d['left'], _split2(s_prev), _NT) + mm(d['akrk'], (bd(v_hi), bd(v_lo)))
        rhs = both[:C]
        y0 = both[C:]
        yield
        u_pair = _split2(mm(d['inv'], bd2(_split2(rhs))))
        yield
        y_ref[j * C:(j + 1) * C, p * PAIR:(p + 1) * PAIR] = y0 + mm(d['rb'], bd2(u_pair))
        t_hi = _cat_rows(u_pair[0], v_hi)
        t_lo = _cat_rows(u_pair[1], v_lo)
        w_hi, w_lo = d['bhkh']
        upd = _dg(_cat_rows(t_hi, t_lo), _cat_rows(w_hi, w_hi), _TN) + _dg(t_hi, w_lo, _TN)
        state[skey(j, p)] = s_prev * d['p_end'] + jnp.where(on_diag, upd, 0.0)
        yield

    def post_chain(j):
        rows = slice(j * C, (j + 1) * C)
        d = prepped.pop(j)
        y = y_ref[rows, :]
        mean = head_sum(y) * (1.0 / HEAD_DIM)
        dev = y - mean
        yield
        var = head_sum(dev * dev) * (1.0 / HEAD_DIM)
        yn = dev * lax.rsqrt(var + GN_EPS) * lnw_ref[...] + lnb_ref[...]
        out_ref[rows, :] = (yn + d['bonus']) * d['gate']
        yield

    tasks = {}
    for j in range(n_sub):
        tasks[('prep', j)] = (lambda j=j: [prep_chain(j)], [('prep', j - 1), ('local', j - RWKV_LOCAL_IN_FLIGHT)])
        tasks[('local', j)] = (lambda j=j: [local_chain(j, p) for p in pairs],
                               [('prep', j), ('local', j - RWKV_LOCAL_IN_FLIGHT)])
        tasks[('state', j)] = (lambda j=j: [state_chain(j, p) for p in pairs],
                               [('local', j)] + ([('state', j - 1)] if fused else []))
        tasks[('post', j)] = (lambda j=j: [post_chain(j)], [('state', j)])
    if fused:
        n_k = WINDOW + CHUNK
        k_all = _cat_rows(kp_ref[...], kc_ref[...]).astype(BF16)
        v_all = _cat_rows(vp_ref[...], vc_ref[...]).astype(BF16)
        first_valid = jnp.where(c == 0, WINDOW, 0)
        kcol = lax.broadcasted_iota(jnp.int32, (1, n_k), 1)
        sinks = _group_sinks(sink_ref, CHUNK)

        def attn_chains(j):
            rows = slice(j * C, (j + 1) * C)
            keys = slice(j * C, j * C + n_k)
            valid = kcol + j * C >= first_valid if j * C < WINDOW else None
            return [_attn_chain(q_ref[rows, :], k_all[keys, kvh * HEAD_DIM:(kvh + 1) * HEAD_DIM],
                                v_all[keys, kvh * HEAD_DIM:(kvh + 1) * HEAD_DIM], bias_ref[kvh], sinks[kvh], valid,
                                attn_ref, rows, kvh) for kvh in range(KV_HEADS)]

        for j in range(n_sub):
            tasks[('attn', j)] = (lambda j=j: attn_chains(j), [('attn', j - 1)])
        for t, chain in enumerate(proj_chains(xn_ref, 1 - cur)):
            tasks[('proj', t)] = (lambda chain=chain: [chain], [('proj', t - 1), ('prep', min(t, n_sub - 1))])
    _run_tasks(tasks)

    def store_state(dst, s_bd, p):
        dst[2 * p] = s_bd[:HEAD_DIM, :HEAD_DIM]
        dst[2 * p + 1] = s_bd[HEAD_DIM:, HEAD_DIM:]

    if not fused:
        for j in range(n_sub):
            for p in pairs:
                store_state(s_ref.at[j], state[(j, p)], p)
        return
    carry_ref[0:1, :] = zr_ref[R - 1:R, :]
    for p in pairs:
        sbd_ref[p] = state[p]
    kp_ref[...] = kc_ref[R - WINDOW:R, :]
    vp_ref[...] = vc_ref[R - WINDOW:R, :]

    @pl.when(c == steps - 1)
    def _():
        for p in pairs:
            store_state(s_ref.at[0], state[p], p)
        ktail_ref[0] = kc_ref[R - WINDOW:R, :]
        vtail_ref[0] = vc_ref[R - WINDOW:R, :]
        shiftout_ref[0] = zr_ref[R - 1:R, :]


def _rwkv_operands(lw):
    seg = jnp.asarray(np.kron(np.eye(2), np.ones((HEAD_DIM, HEAD_DIM))), BF16)
    ones = np.ones((CHUNK, CHUNK))
    tri3 = jnp.asarray(np.concatenate([np.tile(np.tril(ones), (1, 3)), np.tile(np.triu(ones, 1), (1, 3))]), BF16)
    row = lambda name: lw[name].reshape(1, -1)
    return [row('rwkv_mu'), row('rwkv_w0'), lw['rwkv_w2'].astype(BF16), row('rwkv_a0'),
            lw['rwkv_a2'].astype(BF16), lw['rwkv_g2'].astype(BF16), row('rwkv_k_k'), row('rwkv_k_a'),
            row('rwkv_r_k'), row('rwkv_ln_w'), row('rwkv_ln_b'), seg, tri3]


def _rwkv_mixer(zr, shift_prev, state0, lw, batch, seq):
    assert seq <= CHUNK
    if seq < CHUNK:
        zr = jnp.pad(zr.reshape(batch, seq, RWKV_PROJ), ((0, 0), (0, CHUNK - seq), (0, 0))).reshape(-1, RWKV_PROJ)
    params = _rwkv_operands(lw)
    vmem = pl.BlockSpec(memory_space=pltpu.VMEM)
    out, state = pl.pallas_call(
        functools.partial(_rwkv_kernel, seq, False, 1),
        in_specs=[vmem] * (3 + len(params)),
        out_specs=[vmem, vmem],
        out_shape=[jax.ShapeDtypeStruct((batch * CHUNK, RWKV_WIDTH), F32), jax.ShapeDtypeStruct(state0.shape, F32)],
        scratch_shapes=[pltpu.VMEM((batch * CHUNK, RWKV_WIDTH), F32)],
        compiler_params=pltpu.CompilerParams(vmem_limit_bytes=V7X_VMEM_LIMIT_BYTES),
        name="rwkv_mixer",
    )(zr, shift_prev, state0, *params)
    if seq < CHUNK:
        out = out.reshape(batch, CHUNK, RWKV_WIDTH)[:, :seq].reshape(batch * seq, RWKV_WIDTH)
    return out, state


def _prompt_mixer(x2d, shift_prev, state0, lw, bias, batch, seq):
    rows = CHUNK * RWKV_SUB_CHUNKS
    steps = seq // rows
    total = batch * steps
    params = _rwkv_operands(lw)
    seq_block = lambda shape: pl.BlockSpec((1,) + shape, lambda g: (g // steps,) + (0,) * len(shape))
    row_spec = lambda w: pl.BlockSpec((rows, w), lambda g: (g, 0))

    def resident(shape):
        nd = len(shape)
        return pl.BlockSpec(shape, lambda g: (0,) * nd, pipeline_mode=pl.Buffered(1))

    in_specs = [pl.BlockSpec((rows, D_MODEL), lambda g: (jnp.minimum(g + 1, total - 1), 0)),
                pl.BlockSpec((rows, D_MODEL), lambda g: (0, 0), pipeline_mode=pl.Buffered(1)),
                resident((1, D_MODEL)), resident((D_MODEL, IN_PROJ)),
                seq_block((1, RWKV_PROJ)), seq_block((RWKV_HEADS, HEAD_DIM, HEAD_DIM))]
    in_specs += [resident(p.shape) for p in params]
    in_specs += [pl.BlockSpec(memory_space=pltpu.SMEM), resident(bias.shape)]
    out_specs = [row_spec(RWKV_WIDTH), seq_block((RWKV_HEADS, HEAD_DIM, HEAD_DIM)), row_spec(ATTN_WIDTH),
                 seq_block((WINDOW, KV_WIDTH)), seq_block((WINDOW, KV_WIDTH)), seq_block((1, RWKV_PROJ))]
    n = batch * seq
    out_shape = [jax.ShapeDtypeStruct((n, RWKV_WIDTH), F32), jax.ShapeDtypeStruct(state0.shape, F32),
                 jax.ShapeDtypeStruct((n, ATTN_WIDTH), F32), jax.ShapeDtypeStruct((batch, WINDOW, KV_WIDTH), F32),
                 jax.ShapeDtypeStruct((batch, WINDOW, KV_WIDTH), F32),
                 jax.ShapeDtypeStruct((batch, 1, RWKV_PROJ), F32)]
    scratch = [pltpu.VMEM((8, RWKV_PROJ), F32), pltpu.VMEM((rows, RWKV_WIDTH), F32),
               pltpu.VMEM((RWKV_HEADS // 2, 2 * HEAD_DIM, 2 * HEAD_DIM), F32),
               pltpu.VMEM((2, rows, RWKV_PROJ), F32), pltpu.VMEM((2, rows, ATTN_WIDTH), BF16),
                pltpu.VMEM((2, rows, KV_WIDTH), F32), pltpu.VMEM((2, rows, KV_WIDTH), F32),
                pltpu.VMEM((WINDOW, KV_WIDTH), F32), pltpu.VMEM((WINDOW, KV_WIDTH), F32)]
    r_out, state, a_out, k_tail, v_tail, shift_new = pl.pallas_call(
        functools.partial(_rwkv_kernel, rows, True, steps),
        grid=(total,),
        in_specs=in_specs,
        out_specs=out_specs,
        out_shape=out_shape,
        scratch_shapes=scratch,
        compiler_params=_params("arbitrary"),
        name="prompt_mixer",
    )(x2d, x2d, lw['norm_mix_g'], lw['w_in'], shift_prev, state0, *params, lw['attn_sink'], bias)
    return a_out, r_out, state, k_tail, v_tail, shift_new


def _memkv_kernel(m_ref, g_ref, wk_ref, wv_ref, k_ref, v_ref):
    mn = _rms(m_ref[...], g_ref[...]).astype(BF16)
    k_ref[...] = jnp.dot(mn, wk_ref[...], preferred_element_type=F32)
    v_ref[...] = jnp.dot(mn, wv_ref[...], preferred_element_type=F32)


def _memory_kv(mem2d, g, w_mk, w_mv):
    n = mem2d.shape[0]
    tm = min(MEMORY_KV_ROW_TILE, n)
    row = lambda w: pl.BlockSpec((tm, w), lambda i: (i, 0))
    return pl.pallas_call(
        _memkv_kernel,
        grid=(n // tm,),
        in_specs=[row(D_MODEL), _const_spec((1, D_MODEL)), _const_spec(w_mk.shape), _const_spec(w_mv.shape)],
        out_specs=[row(MEM_WIDTH), row(MEM_WIDTH)],
        out_shape=[jax.ShapeDtypeStruct((n, MEM_WIDTH), F32)] * 2,
        compiler_params=_params("parallel"),
        name="memory_kv",
    )(mem2d, g, w_mk, w_mv)


def _tail_kernel(x_ref, a_ref, r_ref, mk_ref, mv_ref, wo_ref, gc_ref, wq_ref, wco_ref, gm_ref, wu_ref, wd_ref, gf_ref,
                 y_ref):
    n_seq = mk_ref.shape[0]
    rows_per_seq = x_ref.shape[0] // n_seq
    x1 = x_ref[...] + _dot(a_ref[...], wo_ref[:ATTN_WIDTH, :]) + _dot(r_ref[...], wo_ref[ATTN_WIDTH:, :])
    q = _dot(_rms(x1, gc_ref[...]), wq_ref[...]).astype(BF16)
    outs = {}

    def cross_chain(b, h):
        rows = slice(b * rows_per_seq, (b + 1) * rows_per_seq)
        sl = slice(h * MEM_HEAD_DIM, (h + 1) * MEM_HEAD_DIM)
        s = _dg(q[rows, sl], mk_ref[b, :, sl].astype(BF16), _NT) * (MEM_HEAD_DIM ** -0.5)
        yield
        p = jnp.exp(s - jnp.max(s, axis=-1, keepdims=True))
        den = jnp.sum(p, axis=-1, keepdims=True)
        yield
        outs[(b, h)] = _dg(p.astype(BF16), mv_ref[b, :, sl].astype(BF16), _NN) * (1.0 / den)
        yield

    _run_interleaved([cross_chain(b, h) for b in range(n_seq) for h in range(MEM_HEADS)])
    o = _cat_rows(*[_cat_lanes(*[outs[(b, h)] for h in range(MEM_HEADS)]) for b in range(n_seq)])
    x2 = x1 + _dot(o, wco_ref[...])
    up = _dot(_rms(x2, gm_ref[...]), wu_ref[...])
    act = jnp.square(jnp.maximum(up, 0.0))
    y_ref[...] = _rms(x2 + _dot(act, wd_ref[...]), gf_ref[...])


def _tail(x2d, a_out, r_out, mk, mv, lw, batch, seq):
    n = batch * seq
    tq = min(TAIL_ROW_TILE, n)
    if seq >= tq:
        assert seq % tq == 0
        seq_per_tile, tiles_per_seq = 1, seq // tq
        mem_spec = pl.BlockSpec((1, N_MEM, MEM_WIDTH), lambda i: (i // tiles_per_seq, 0, 0))
    else:
        assert tq % seq == 0
        seq_per_tile = tq // seq
        mem_spec = pl.BlockSpec((seq_per_tile, N_MEM, MEM_WIDTH), lambda i: (i, 0, 0))
    row = lambda w: pl.BlockSpec((tq, w), lambda i: (i, 0))

    def resident(shape):
        nd = len(shape)
        return pl.BlockSpec(shape, lambda *_: (0,) * nd, pipeline_mode=pl.Buffered(1))

    weights = [lw['w_out'], lw['norm_cross_g'], lw['w_cq'], lw['w_co'], lw['norm_mlp_g'], lw['w_up'], lw['w_down'],
               lw['norm_final_g']]
    return pl.pallas_call(
        _tail_kernel,
        grid=(n // tq,),
        in_specs=[row(D_MODEL), row(ATTN_WIDTH), row(RWKV_WIDTH), mem_spec, mem_spec]
                 + [resident(w.shape) for w in weights],
        out_specs=row(D_MODEL),
        out_shape=jax.ShapeDtypeStruct(x2d.shape, F32),
        compiler_params=_params("parallel"),
        name="tail",
    )(x2d, a_out, r_out, mk, mv, *weights)


def _trunk(x, mk, mv, k_past, v_past, shift_prev, state0, lw, table):
    batch, seq = x.shape[0], x.shape[1]
    x2d = x.reshape(batch * seq, D_MODEL)
    if k_past is None:
        bias = _rel_bias(table, CHUNK, WINDOW + CHUNK)
        a_out, r_out, state, k_buf, v_buf, shift_new = _prompt_mixer(x2d, shift_prev, state0, lw, bias, batch, seq)
    else:
        q, k, v, zr = _in_proj(x2d, lw['norm_mix_g'], lw['w_in'])
        bias = _rel_bias(table, seq, WINDOW + seq)
        a_out = _sample_attention(q, k, v, k_past.reshape(batch * WINDOW, KV_WIDTH),
                                  v_past.reshape(batch * WINDOW, KV_WIDTH), lw['attn_sink'], bias, seq)
        k_buf = jnp.concatenate([k_past, k.reshape(batch, seq, KV_WIDTH)], axis=1)[:, -WINDOW:]
        v_buf = jnp.concatenate([v_past, v.reshape(batch, seq, KV_WIDTH)], axis=1)[:, -WINDOW:]
        r_out, state = _rwkv_mixer(zr, shift_prev, state0, lw, batch, seq)
        shift_new = zr.reshape(batch, seq, RWKV_PROJ)[:, -1:]
    y = _tail(x2d, a_out, r_out, mk, mv, lw, batch, seq)
    kv_shape = (batch, WINDOW, KV_HEADS, HEAD_DIM)
    return y.reshape(x.shape), k_buf.reshape(kv_shape), v_buf.reshape(kv_shape), shift_new, state


def kernel(x_prompt, x_sample, mem_prompt, cache_attn_k, cache_attn_v, cache_mem_k, cache_mem_v, state_shift,
           state_wkv, norm_mix_g, w_in, attn_sink, rel_bias_table, rwkv_mu, rwkv_w0, rwkv_w2, rwkv_a0, rwkv_a2,
           rwkv_g2, rwkv_k_k, rwkv_k_a, rwkv_r_k, rwkv_ln_w, rwkv_ln_b, w_out, norm_cross_g, norm_mem_g, w_cq,
           w_mk, w_mv, w_co, norm_mlp_g, w_up, w_down, norm_final_g):
    assert norm_mix_g.shape[0] == 1, "single-layer trunk"
    bp, dec_b = x_prompt.shape[0], x_sample.shape[0]
    vec = lambda p: p[0].reshape(1, -1)
    w_in_b, w_out_b, w_cq_b, w_co_b, w_up_b, w_down_b, w_mk_b, w_mv_b = _to_bf16(
        w_in[0], w_out[0], w_cq[0], w_co[0], w_up[0], w_down[0], w_mk[0], w_mv[0])
    lw = {
        'norm_mix_g': vec(norm_mix_g), 'w_in': w_in_b, 'attn_sink': attn_sink[0],
        'rwkv_mu': rwkv_mu[0], 'rwkv_w0': rwkv_w0[0], 'rwkv_w2': rwkv_w2[0], 'rwkv_a0': rwkv_a0[0],
        'rwkv_a2': rwkv_a2[0], 'rwkv_g2': rwkv_g2[0], 'rwkv_k_k': rwkv_k_k[0], 'rwkv_k_a': rwkv_k_a[0],
        'rwkv_r_k': rwkv_r_k[0], 'rwkv_ln_w': rwkv_ln_w[0], 'rwkv_ln_b': rwkv_ln_b[0],
        'w_out': w_out_b, 'norm_cross_g': vec(norm_cross_g), 'w_cq': w_cq_b,
        'w_co': w_co_b, 'norm_mlp_g': vec(norm_mlp_g), 'w_up': w_up_b,
        'w_down': w_down_b, 'norm_final_g': norm_final_g.reshape(1, -1),
    }
    mk, mv = _memory_kv(mem_prompt.reshape(bp * N_MEM, D_MODEL), vec(norm_mem_g), w_mk_b, w_mv_b)
    mk = mk.reshape(bp, N_MEM, MEM_WIDTH)
    mv = mv.reshape(bp, N_MEM, MEM_WIDTH)
    shift0 = jnp.zeros((bp, 1, RWKV_PROJ), F32)
    wkv0 = jnp.zeros((bp, RWKV_HEADS, HEAD_DIM, HEAD_DIM), F32)
    yp, pk, pv, psh, pS = _trunk(x_prompt, mk, mv, None, None, shift0, wkv0, lw, rel_bias_table)
    ys, sk, sv, ssh, sS = _trunk(
        x_sample, cache_mem_k[0].reshape(dec_b, N_MEM, MEM_WIDTH), cache_mem_v[0].reshape(dec_b, N_MEM, MEM_WIDTH),
        cache_attn_k[0].reshape(dec_b, WINDOW, KV_WIDTH), cache_attn_v[0].reshape(dec_b, WINDOW, KV_WIDTH),
        state_shift[0], state_wkv[0], lw, rel_bias_table)
    mem_shape = (1, bp, N_MEM, MEM_HEADS, MEM_HEAD_DIM)
    return (yp, ys, pk[None], pv[None], mk.reshape(mem_shape), mv.reshape(mem_shape), psh[None], pS[None],
            sk[None], sv[None], ssh[None], sS[None])
```

```python
import functools
import math

import numpy as np
import jax
import jax.numpy as jnp
from jax import lax
from jax.experimental import pallas as pl
from jax.experimental.pallas import tpu as pltpu

F32 = jnp.float32
BF16 = jnp.bfloat16

D_MODEL = 1024
CHUNK = 64
WINDOW = 128
HEAD_DIM = 64
ATTN_WIDTH = 512
ATTN_HEADS = 8
KV_HEADS = 2
GROUP = 4
KV_WIDTH = 128
RWKV_WIDTH = 512
RWKV_HEADS = 8
DECAY_LORA = 64
AAA_LORA = 64
GATE_LORA = 128
RWKV_PROJ = 1792
IN_PROJ = 2560
N_MEM = 256
MEM_HEADS = 4
MEM_HEAD_DIM = 128
MEM_WIDTH = 512
D_FF = 4096
REL_BUCKETS = 32
REL_MAX_DIST = 128
NORM_EPS = 1e-6
GN_EPS = 64e-5
LOG2_E = math.log2(math.e)

V7X_VMEM_LIMIT_BYTES = 52 * 1024 * 1024
MEMORY_KV_ROW_TILE = 512
IN_PROJ_ROW_TILE = 512
PROJ_TILE = 256
TAIL_ROW_TILE = 512
WEIGHT_CAST_STEPS = 8
RWKV_SUB_CHUNKS = 4
RWKV_LOCAL_IN_FLIGHT = 3


def _params(*sem):
    return pltpu.CompilerParams(dimension_semantics=sem, vmem_limit_bytes=V7X_VMEM_LIMIT_BYTES)


def _const_spec(shape):
    nd = len(shape)
    return pl.BlockSpec(shape, lambda *_: (0,) * nd)


_NN = ((1,), (0,))
_NT = ((1,), (1,))
_TN = ((0,), (0,))


def _dg(a, b, dims):
    return lax.dot_general(a, b, (dims, ((), ())), preferred_element_type=F32)


def _dot(a, b):
    return _dg(a.astype(BF16), b.astype(BF16), _NN)


def _cat_rows(*xs):
    return jnp.concatenate(xs, axis=0)


def _cat_lanes(*xs):
    return jnp.concatenate(xs, axis=1)


def _run_interleaved(chains):
    active = list(chains)
    while active:
        still = []
        for ch in active:
            try:
                next(ch)
                still.append(ch)
            except StopIteration:
                pass
        active = still


def _run_tasks(tasks):
    finished = set()
    running = {}
    waiting = dict(tasks)
    while waiting or running:
        for name in [n for n, (_, deps) in waiting.items() if all(d in finished or d not in tasks for d in deps)]:
            running[name] = list(waiting.pop(name)[0]())
        for name in list(running):
            alive = []
            for ch in running[name]:
                try:
                    next(ch)
                    alive.append(ch)
                except StopIteration:
                    pass
            if alive:
                running[name] = alive
            else:
                del running[name]
                finished.add(name)


def _rms(x, g):
    return x * lax.rsqrt(jnp.mean(x * x, axis=-1, keepdims=True) + NORM_EPS) * g


def _cast_kernel(*refs):
    n = len(refs) // 2
    for src, dst in zip(refs[:n], refs[n:]):
        dst[...] = src[...].astype(BF16)


def _to_bf16(*weights):
    specs = [pl.BlockSpec((w.shape[0] // WEIGHT_CAST_STEPS, w.shape[1]), lambda i: (i, 0)) for w in weights]
    return pl.pallas_call(
        _cast_kernel,
        grid=(WEIGHT_CAST_STEPS,),
        in_specs=specs,
        out_specs=specs,
        out_shape=[jax.ShapeDtypeStruct(w.shape, BF16) for w in weights],
        compiler_params=_params("parallel"),
        name="cast_weights",
    )(*weights)


def _inproj_kernel(x_ref, g_ref, w_ref, q_ref, k_ref, v_ref, zr_ref):
    h = _rms(x_ref[...], g_ref[...]).astype(BF16)
    q = jnp.dot(h, w_ref[:, :ATTN_WIDTH], preferred_element_type=F32)
    q_ref[...] = (q * (HEAD_DIM ** -0.5 * LOG2_E)).astype(BF16)
    k_ref[...] = jnp.dot(h, w_ref[:, ATTN_WIDTH:ATTN_WIDTH + KV_WIDTH], preferred_element_type=F32)
    v_ref[...] = jnp.dot(h, w_ref[:, ATTN_WIDTH + KV_WIDTH:ATTN_WIDTH + 2 * KV_WIDTH],
                         preferred_element_type=F32)
    zr_ref[...] = jnp.dot(h, w_ref[:, ATTN_WIDTH + 2 * KV_WIDTH:], preferred_element_type=F32)


def _in_proj(x2d, g, w_bf16):
    n = x2d.shape[0]
    tm = min(IN_PROJ_ROW_TILE, n)
    row = lambda w: pl.BlockSpec((tm, w), lambda i: (i, 0))
    return pl.pallas_call(
        _inproj_kernel,
        grid=(n // tm,),
        in_specs=[row(D_MODEL), _const_spec((1, D_MODEL)), _const_spec((D_MODEL, IN_PROJ))],
        out_specs=[row(ATTN_WIDTH), row(KV_WIDTH), row(KV_WIDTH), row(RWKV_PROJ)],
        out_shape=[jax.ShapeDtypeStruct((n, ATTN_WIDTH), BF16)]
                  + [jax.ShapeDtypeStruct((n, w), F32) for w in (KV_WIDTH, KV_WIDTH, RWKV_PROJ)],
        compiler_params=_params("parallel"),
        name="in_proj",
    )(x2d, g, w_bf16)


def _t5_bucket(rel):
    half = REL_BUCKETS // 2
    max_exact = half // 2
    assert REL_MAX_DIST == max_exact * 2 ** 4 and half - max_exact == 2 * 4
    n = np.abs(rel)
    large = max_exact + sum((n * n >= max_exact * max_exact * 2 ** t).astype(np.int64)
                            for t in range(1, half - max_exact))
    return (np.where(rel > 0, half, 0) + np.where(n < max_exact, n, large)).astype(np.int32)


def _bias_kernel(table_ref, bucket_ref, out_ref):
    bucket = bucket_ref[...]
    hits = [bucket == b for b in range(REL_BUCKETS)]
    for h in range(ATTN_HEADS):
        acc = jnp.zeros(bucket.shape, F32)
        for b in range(REL_BUCKETS):
            acc = jnp.where(hits[b], table_ref[b, h], acc)
        out_ref[h] = acc * LOG2_E


def _rel_bias(table, n_q, n_k):
    rel = np.arange(n_k)[None, :] - WINDOW - np.arange(n_q)[:, None]
    bucket = jnp.asarray(_t5_bucket(rel))
    bias = pl.pallas_call(
        _bias_kernel,
        in_specs=[pl.BlockSpec(memory_space=pltpu.SMEM), pl.BlockSpec(memory_space=pltpu.VMEM)],
        out_specs=pl.BlockSpec(memory_space=pltpu.VMEM),
        out_shape=jax.ShapeDtypeStruct((ATTN_HEADS, n_q, n_k), F32),
        name="rel_bias",
    )(table, bucket)
    return bias.reshape(KV_HEADS, GROUP * n_q, n_k)


def _group_sinks(sink_ref, n_q):
    row_group = lax.broadcasted_iota(jnp.int32, (GROUP * n_q, 1), 0) // n_q
    sinks = []
    for kvh in range(KV_HEADS):
        sink = jnp.zeros((GROUP * n_q, 1), F32)
        for g in range(GROUP):
            sink = jnp.where(row_group == g, sink_ref[kvh * GROUP + g] * LOG2_E, sink)
        sinks.append(sink)
    return sinks


def _attn_chain(q, keys, vals, bias, sink, valid, o_ref, rows, kvh):
    n_q = q.shape[0]
    qh = _cat_rows(*[q[:, (kvh * GROUP + g) * HEAD_DIM:(kvh * GROUP + g + 1) * HEAD_DIM]
                     for g in range(GROUP)])
    s = _dg(qh, keys, _NT) + bias
    if valid is not None:
        s = jnp.where(valid, s, -jnp.inf)
    yield
    m = jnp.maximum(jnp.max(s, axis=-1, keepdims=True), sink)
    p = jnp.exp2(s - m)
    den = jnp.sum(p, axis=-1, keepdims=True) + jnp.exp2(sink - m)
    yield
    o = _dg(p.astype(BF16), vals, _NN) * (1.0 / den)
    for g in range(GROUP):
        head = kvh * GROUP + g
        o_ref[rows, head * HEAD_DIM:(head + 1) * HEAD_DIM] = o[g * n_q:(g + 1) * n_q].astype(o_ref.dtype)
    yield


def _sample_attn_kernel(seq, sink_ref, q_ref, kp_ref, kn_ref, vp_ref, vn_ref, bias_ref, o_ref):
    batch = q_ref.shape[0] // seq
    sinks = _group_sinks(sink_ref, seq)
    chains = []
    for b in range(batch):
        rows = slice(b * seq, (b + 1) * seq)
        past = slice(b * WINDOW, (b + 1) * WINDOW)
        k_all = _cat_rows(kp_ref[past, :], kn_ref[rows, :]).astype(BF16)
        v_all = _cat_rows(vp_ref[past, :], vn_ref[rows, :]).astype(BF16)
        for kvh in range(KV_HEADS):
            lanes = slice(kvh * HEAD_DIM, (kvh + 1) * HEAD_DIM)
            chains.append(_attn_chain(q_ref[rows, :], k_all[:, lanes], v_all[:, lanes], bias_ref[kvh], sinks[kvh],
                                      None, o_ref, rows, kvh))
    _run_interleaved(chains)


def _sample_attention(q, k, v, k_past, v_past, sink, bias, seq):
    vmem = pl.BlockSpec(memory_space=pltpu.VMEM)
    return pl.pallas_call(
        functools.partial(_sample_attn_kernel, seq),
        in_specs=[pl.BlockSpec(memory_space=pltpu.SMEM)] + [vmem] * 6,
        out_specs=vmem,
        out_shape=jax.ShapeDtypeStruct(q.shape, BF16),
        name="sample_attention",
    )(sink, q, k_past, k, v_past, v, bias)


def _split2(x):
    hi = x.astype(BF16)
    lo = (x - hi.astype(F32)).astype(BF16)
    return hi, lo


def _softplus(x):
    return jnp.maximum(x, 0.0) + jnp.log(1.0 + jnp.exp(-jnp.abs(x)))


def _sigmoid(x):
    return 1.0 / (1.0 + jnp.exp(-x))


def _rwkv_kernel(valid_rows, fused, steps, *refs):
    if fused:
        (xn_ref, x0_ref, gmix_ref, win_ref, shift_ref, s0_ref, mu_ref, w0_ref, w2_ref, a0_ref, a2_ref, g2_ref, kk_ref,
         ka_ref, rk_ref, lnw_ref, lnb_ref, seg_ref, tri_ref, sink_ref, bias_ref) = refs[:21]
        (out_ref, s_ref, attn_ref, ktail_ref, vtail_ref, shiftout_ref, carry_ref, y_ref, sbd_ref, zr_scr, q_scr, k_scr,
         v_scr, kp_ref, vp_ref) = refs[21:]
        g = pl.program_id(0)
        c = lax.rem(g, steps)
        cur = lax.rem(g, 2)
        zr_ref, q_ref, kc_ref, vc_ref = zr_scr.at[cur], q_scr.at[cur], k_scr.at[cur], v_scr.at[cur]
    else:
        (zr_ref, shift_ref, s0_ref, mu_ref, w0_ref, w2_ref, a0_ref, a2_ref, g2_ref, kk_ref, ka_ref, rk_ref, lnw_ref,
         lnb_ref, seg_ref, tri_ref) = refs[:16]
        out_ref, s_ref, y_ref = refs[16:]
    C = CHUNK
    R = out_ref.shape[0]
    n_sub = R // C
    pairs = range(RWKV_HEADS // 2)
    PAIR = 2 * HEAD_DIM
    W = RWKV_WIDTH

    def block_diagonal(s0, p):
        zero = jnp.zeros((HEAD_DIM, HEAD_DIM), F32)
        return _cat_rows(_cat_lanes(s0[2 * p], zero), _cat_lanes(zero, s0[2 * p + 1]))

    if fused:
        @pl.when(c == 0)
        def _():
            carry_ref[0:1, :] = shift_ref[0]
            for p in pairs:
                sbd_ref[p] = block_diagonal(s0_ref.at[0], p)
            kp_ref[...] = jnp.zeros(kp_ref.shape, F32)
            vp_ref[...] = jnp.zeros(vp_ref.shape, F32)

    def proj_chains(x_ref, slot):
        h = _rms(x_ref[...], gmix_ref[...]).astype(BF16)
        q_end, k_end, v_end = ATTN_WIDTH, ATTN_WIDTH + KV_WIDTH, ATTN_WIDTH + 2 * KV_WIDTH

        def tile(lo, hi):
            z = jnp.dot(h, win_ref[:, lo:hi], preferred_element_type=F32)
            if hi <= q_end:
                q_scr[slot, :, lo:hi] = (z * (HEAD_DIM ** -0.5 * LOG2_E)).astype(BF16)
            elif lo == q_end:
                k_scr[slot] = z[:, :KV_WIDTH]
                v_scr[slot] = z[:, KV_WIDTH:]
            else:
                zr_scr[slot, :, lo - v_end:hi - v_end] = z
            yield

        assert q_end % PROJ_TILE == 0 and v_end - q_end == PROJ_TILE
        return [tile(lo, lo + PROJ_TILE) for lo in range(0, IN_PROJ, PROJ_TILE)]

    if fused:
        @pl.when(g == 0)
        def _():
            _run_interleaved(proj_chains(x0_ref, 0))

    seg = seg_ref[...]
    seg2 = _cat_rows(seg, seg)

    def head_sum(x):
        hi, lo = _split2(x)
        tiles = [_dg(_cat_lanes(hi[:, t * PAIR:(t + 1) * PAIR], lo[:, t * PAIR:(t + 1) * PAIR]), seg2, _NN)
                 for t in range(W // PAIR)]
        return _cat_lanes(*tiles)

    first_row = lax.broadcasted_iota(jnp.int32, (C, 1), 0) == 0
    tri3 = tri_ref[...]

    lane = lax.broadcasted_iota(jnp.int32, (C, PAIR), 1)
    trow = lax.broadcasted_iota(jnp.int32, (C, PAIR), 0)
    even = lane < HEAD_DIM
    tcol = jnp.where(even, lane, lane - HEAD_DIM)
    strict = tcol < trow
    incl = tcol <= trow
    eye = jnp.where(tcol == trow, 1.0, 0.0).astype(F32)
    brow = lax.broadcasted_iota(jnp.int32, (PAIR, PAIR), 0) < HEAD_DIM
    bcol = lax.broadcasted_iota(jnp.int32, (PAIR, PAIR), 1) < HEAD_DIM
    on_diag = brow == bcol

    def bd(x):
        zero = jnp.zeros_like(x)
        return _cat_rows(jnp.where(even, x, zero), jnp.where(even, zero, x))

    def bd2(pair):
        return bd(pair[0]), bd(pair[1])

    def mm(a_pair, w_pair, dims=_NN):
        if dims == _NT:
            w_pair = (w_pair[0].T, w_pair[1].T)
        first = _dg(_cat_lanes(a_pair[0], a_pair[1]), _cat_rows(w_pair[0], w_pair[0]), _NN)
        return first + _dg(a_pair[0], w_pair[1], _NN)

    prepped = {}
    ready = {}
    if fused:
        state = {p: sbd_ref[p] for p in pairs}
        skey = lambda j, p: p
    else:
        state = {(j, p): block_diagonal(s0_ref.at[j], p) for j in range(n_sub) for p in pairs}
        skey = lambda j, p: (j, p)

    def prep_chain(j):
        rows = slice(j * C, (j + 1) * C)
        zr = zr_ref[rows, :]
        if not fused:
            before = shift_ref[j]
        else:
            before = carry_ref[0:1, :] if j == 0 else zr_ref[j * C - 1:j * C, :]
        z_prev = jnp.where(first_row, before, pltpu.roll(zr, 1, axis=0))
        zs = zr + (z_prev - zr) * mu_ref[...]
        r = zs[:, :W]
        k = zs[:, W:2 * W]
        v = zs[:, 2 * W:3 * W]
        wd = zs[:, 3 * W:3 * W + DECAY_LORA]
        ad = zs[:, 3 * W + DECAY_LORA:3 * W + DECAY_LORA + AAA_LORA]
        gd = zs[:, 3 * W + DECAY_LORA + AAA_LORA:]
        w_log = -_softplus(-(w0_ref[...] + _dot(jnp.tanh(wd), w2_ref[...]))) - 0.5
        lw = -jnp.exp(w_log)
        a = _sigmoid(a0_ref[...] + _dot(ad, a2_ref[...]))
        gate = _dot(_sigmoid(gd), g2_ref[...])
        kk = k * kk_ref[...]
        kk = kk * lax.rsqrt(jnp.maximum(head_sum(kk * kk), 1e-24))
        k2 = k * (1.0 + (a - 1.0) * ka_ref[...])
        if valid_rows < C:
            live = lax.broadcasted_iota(jnp.int32, (C, 1), 0) < valid_rows
            lw = jnp.where(live, lw, 0.0)
            kk = jnp.where(live, kk, 0.0)
            k2 = jnp.where(live, k2, 0.0)
        bvec = kk * a
        yield
        l1 = lw.astype(BF16)
        rem = lw - l1.astype(F32)
        l2 = rem.astype(BF16)
        l3 = (rem - l2.astype(F32)).astype(BF16)
        sums = _dg(tri3, _cat_rows(l1, l2, l3), _NN)
        li = sums[:C]
        lrev = sums[C:]
        yield
        inv_p = jnp.exp(-li)
        to_end = jnp.exp(lrev)
        prepped[j] = dict(
            at=_split2(-kk * jnp.exp(li - lw)), rt=_split2(r * jnp.exp(li)), bt=_split2(bvec * inv_p),
            kt=_split2(k2 * inv_p), bh=_split2(bvec * to_end), kh=_split2(k2 * to_end), v=_split2(v),
            p_end=jnp.exp(li[C - 1:C, :]), bonus=head_sum(r * k2 * rk_ref[...]) * v, gate=gate)
        yield

    def local_chain(j, p):
        d = prepped[j]
        lanes = slice(p * PAIR, (p + 1) * PAIR)
        cut = lambda pair: (pair[0][:, lanes], pair[1][:, lanes])
        at_p, rt_p, bt_p, kt_p, bh_p, kh_p, v_p = map(cut, (d['at'], d['rt'], d['bt'], d['kt'], d['bh'], d['kh'],
                                                            d['v']))
        left = (_cat_rows(at_p[0], rt_p[0]), _cat_rows(at_p[1], rt_p[1]))
        right = (_cat_rows(bd(bt_p[0]), bd(kt_p[0])), _cat_rows(bd(bt_p[1]), bd(kt_p[1])))
        aa = mm(left, right, _NT)
        yield
        a_ab = jnp.where(strict, aa[:C, :PAIR], 0.0)
        a_ak = jnp.where(strict, aa[:C, PAIR:], 0.0)
        a_rb = jnp.where(incl, aa[C:, :PAIR], 0.0)
        a_rk = jnp.where(incl, aa[C:, PAIR:], 0.0)
        inv = eye + a_ab
        ps = _split2(a_ab)
        power = mm(ps, bd2(ps))
        span = 2
        yield
        while span < C:
            ps = _split2(power)
            pw = bd2(ps)
            ih = _split2(inv)
            if span * 2 < C:
                both = mm((_cat_rows(ih[0], ps[0]), _cat_rows(ih[1], ps[1])), pw)
                inv = inv + both[:C]
                power = both[C:]
            else:
                inv = inv + mm(ih, pw)
            span *= 2
            yield
        ready[(j, p)] = dict(inv=_split2(inv), akrk=_split2(_cat_rows(a_ak, a_rk)), rb=_split2(a_rb), left=left,
                             bhkh=(_cat_rows(bh_p[0], kh_p[0]), _cat_rows(bh_p[1], kh_p[1])), v=v_p,
                             p_end=d['p_end'][:, lanes])

    def state_chain(j, p):
        d = ready.pop((j, p))
        s_prev = state[skey(j, p)]
        v_hi, v_lo = d['v']
        both = mm(d['left'], _split2(s_prev), _NT) + mm(d['akrk'], (bd(v_hi), bd(v_lo)))
        rhs = both[:C]
        y0 = both[C:]
        yield
        u_pair = _split2(mm(d['inv'], bd2(_split2(rhs))))
        yield
        y_ref[j * C:(j + 1) * C, p * PAIR:(p + 1) * PAIR] = y0 + mm(d['rb'], bd2(u_pair))
        t_hi = _cat_rows(u_pair[0], v_hi)
        t_lo = _cat_rows(u_pair[1], v_lo)
        w_hi, w_lo = d['bhkh']
        upd = _dg(_cat_rows(t_hi, t_lo), _cat_rows(w_hi, w_hi), _TN) + _dg(t_hi, w_lo, _TN)
        state[skey(j, p)] = s_prev * d['p_end'] + jnp.where(on_diag, upd, 0.0)
        yield

    def post_chain(j):
        rows = slice(j * C, (j + 1) * C)
        d = prepped.pop(j)
        y = y_ref[rows, :]
        mean = head_sum(y) * (1.0 / HEAD_DIM)
        dev = y - mean
        yield
        var = head_sum(dev * dev) * (1.0 / HEAD_DIM)
        yn = dev * lax.rsqrt(var + GN_EPS) * lnw_ref[...] + lnb_ref[...]
        out_ref[rows, :] = ((yn + d['bonus']) * d['gate']).astype(out_ref.dtype)
        yield

    tasks = {}
    for j in range(n_sub):
        tasks[('prep', j)] = (lambda j=j: [prep_chain(j)], [('prep', j - 1), ('local', j - RWKV_LOCAL_IN_FLIGHT)])
        tasks[('local', j)] = (lambda j=j: [local_chain(j, p) for p in pairs],
                               [('prep', j), ('local', j - RWKV_LOCAL_IN_FLIGHT)])
        tasks[('state', j)] = (lambda j=j: [state_chain(j, p) for p in pairs],
                               [('local', j)] + ([('state', j - 1)] if fused else []))
        tasks[('post', j)] = (lambda j=j: [post_chain(j)], [('state', j)])
    if fused:
        n_k = WINDOW + CHUNK
        k_all = _cat_rows(kp_ref[...], kc_ref[...]).astype(BF16)
        v_all = _cat_rows(vp_ref[...], vc_ref[...]).astype(BF16)
        first_valid = jnp.where(c == 0, WINDOW, 0)
        kcol = lax.broadcasted_iota(jnp.int32, (1, n_k), 1)
        sinks = _group_sinks(sink_ref, CHUNK)

        def attn_chains(j):
            rows = slice(j * C, (j + 1) * C)
            keys = slice(j * C, j * C + n_k)
            valid = kcol + j * C >= first_valid if j * C < WINDOW else None
            return [_attn_chain(q_ref[rows, :], k_all[keys, kvh * HEAD_DIM:(kvh + 1) * HEAD_DIM],
                                v_all[keys, kvh * HEAD_DIM:(kvh + 1) * HEAD_DIM], bias_ref[kvh], sinks[kvh], valid,
                                attn_ref, rows, kvh) for kvh in range(KV_HEADS)]

        for j in range(n_sub):
            tasks[('attn', j)] = (lambda j=j: attn_chains(j), [('attn', j - 1)])
        for t, chain in enumerate(proj_chains(xn_ref, 1 - cur)):
            tasks[('proj', t)] = (lambda chain=chain: [chain], [('proj', t - 1), ('prep', min(t, n_sub - 1))])
    _run_tasks(tasks)

    def store_state(dst, s_bd, p):
        dst[2 * p] = s_bd[:HEAD_DIM, :HEAD_DIM]
        dst[2 * p + 1] = s_bd[HEAD_DIM:, HEAD_DIM:]

    if not fused:
        for j in range(n_sub):
            for p in pairs:
                store_state(s_ref.at[j], state[(j, p)], p)
        return
    carry_ref[0:1, :] = zr_ref[R - 1:R, :]
    for p in pairs:
        sbd_ref[p] = state[p]
    kp_ref[...] = kc_ref[R - WINDOW:R, :]
    vp_ref[...] = vc_ref[R - WINDOW:R, :]

    @pl.when(c == steps - 1)
    def _():
        for p in pairs:
            store_state(s_ref.at[0], state[p], p)
        ktail_ref[0] = kc_ref[R - WINDOW:R, :]
        vtail_ref[0] = vc_ref[R - WINDOW:R, :]
        shiftout_ref[0] = zr_ref[R - 1:R, :]


def _rwkv_operands(lw):
    seg = jnp.asarray(np.kron(np.eye(2), np.ones((HEAD_DIM, HEAD_DIM))), BF16)
    ones = np.ones((CHUNK, CHUNK))
    tri3 = jnp.asarray(np.concatenate([np.tile(np.tril(ones), (1, 3)), np.tile(np.triu(ones, 1), (1, 3))]), BF16)
    row = lambda name: lw[name].reshape(1, -1)
    return [row('rwkv_mu'), row('rwkv_w0'), lw['rwkv_w2'].astype(BF16), row('rwkv_a0'),
            lw['rwkv_a2'].astype(BF16), lw['rwkv_g2'].astype(BF16), row('rwkv_k_k'), row('rwkv_k_a'),
            row('rwkv_r_k'), row('rwkv_ln_w'), row('rwkv_ln_b'), seg, tri3]


def _rwkv_mixer(zr, shift_prev, state0, lw, batch, seq):
    assert seq <= CHUNK
    if seq < CHUNK:
        zr = jnp.pad(zr.reshape(batch, seq, RWKV_PROJ), ((0, 0), (0, CHUNK - seq), (0, 0))).reshape(-1, RWKV_PROJ)
    params = _rwkv_operands(lw)
    vmem = pl.BlockSpec(memory_space=pltpu.VMEM)
    out, state = pl.pallas_call(
        functools.partial(_rwkv_kernel, seq, False, 1),
        in_specs=[vmem] * (3 + len(params)),
        out_specs=[vmem, vmem],
        out_shape=[jax.ShapeDtypeStruct((batch * CHUNK, RWKV_WIDTH), BF16), jax.ShapeDtypeStruct(state0.shape, F32)],
        scratch_shapes=[pltpu.VMEM((batch * CHUNK, RWKV_WIDTH), F32)],
        compiler_params=pltpu.CompilerParams(vmem_limit_bytes=V7X_VMEM_LIMIT_BYTES),
        name="rwkv_mixer",
    )(zr, shift_prev, state0, *params)
    if seq < CHUNK:
        out = out.reshape(batch, CHUNK, RWKV_WIDTH)[:, :seq].reshape(batch * seq, RWKV_WIDTH)
    return out, state


def _prompt_mixer(x2d, shift_prev, state0, lw, bias, batch, seq):
    rows = CHUNK * RWKV_SUB_CHUNKS
    steps = seq // rows
    total = batch * steps
    params = _rwkv_operands(lw)
    seq_block = lambda shape: pl.BlockSpec((1,) + shape, lambda g: (g // steps,) + (0,) * len(shape))
    row_spec = lambda w: pl.BlockSpec((rows, w), lambda g: (g, 0))

    def resident(shape):
        nd = len(shape)
        return pl.BlockSpec(shape, lambda g: (0,) * nd, pipeline_mode=pl.Buffered(1))

    in_specs = [pl.BlockSpec((rows, D_MODEL), lambda g: (jnp.minimum(g + 1, total - 1), 0)),
                pl.BlockSpec((rows, D_MODEL), lambda g: (0, 0), pipeline_mode=pl.Buffered(1)),
                resident((1, D_MODEL)), resident((D_MODEL, IN_PROJ)),
                seq_block((1, RWKV_PROJ)), seq_block((RWKV_HEADS, HEAD_DIM, HEAD_DIM))]
    in_specs += [resident(p.shape) for p in params]
    in_specs += [pl.BlockSpec(memory_space=pltpu.SMEM), resident(bias.shape)]
    out_specs = [row_spec(RWKV_WIDTH), seq_block((RWKV_HEADS, HEAD_DIM, HEAD_DIM)), row_spec(ATTN_WIDTH),
                 seq_block((WINDOW, KV_WIDTH)), seq_block((WINDOW, KV_WIDTH)), seq_block((1, RWKV_PROJ))]
    n = batch * seq
    out_shape = [jax.ShapeDtypeStruct((n, RWKV_WIDTH), BF16), jax.ShapeDtypeStruct(state0.shape, F32),
                 jax.ShapeDtypeStruct((n, ATTN_WIDTH), BF16), jax.ShapeDtypeStruct((batch, WINDOW, KV_WIDTH), F32),
                 jax.ShapeDtypeStruct((batch, WINDOW, KV_WIDTH), F32),
                 jax.ShapeDtypeStruct((batch, 1, RWKV_PROJ), F32)]
    scratch = [pltpu.VMEM((8, RWKV_PROJ), F32), pltpu.VMEM((rows, RWKV_WIDTH), F32),
               pltpu.VMEM((RWKV_HEADS // 2, 2 * HEAD_DIM, 2 * HEAD_DIM), F32),
               pltpu.VMEM((2, rows, RWKV_PROJ), F32), pltpu.VMEM((2, rows, ATTN_WIDTH), BF16),
                pltpu.VMEM((2, rows, KV_WIDTH), F32), pltpu.VMEM((2, rows, KV_WIDTH), F32),
                pltpu.VMEM((WINDOW, KV_WIDTH), F32), pltpu.VMEM((WINDOW, KV_WIDTH), F32)]
    r_out, state, a_out, k_tail, v_tail, shift_new = pl.pallas_call(
        functools.partial(_rwkv_kernel, rows, True, steps),
        grid=(total,),
        in_specs=in_specs,
        out_specs=out_specs,
        out_shape=out_shape,
        scratch_shapes=scratch,
        compiler_params=_params("arbitrary"),
        name="prompt_mixer",
    )(x2d, x2d, lw['norm_mix_g'], lw['w_in'], shift_prev, state0, *params, lw['attn_sink'], bias)
    return a_out, r_out, state, k_tail, v_tail, shift_new


def _memkv_kernel(m_ref, g_ref, wk_ref, wv_ref, k_ref, v_ref):
    mn = _rms(m_ref[...], g_ref[...]).astype(BF16)
    k_ref[...] = jnp.dot(mn, wk_ref[...], preferred_element_type=F32)
    v_ref[...] = jnp.dot(mn, wv_ref[...], preferred_element_type=F32)


def _memory_kv(mem2d, g, w_mk, w_mv):
    n = mem2d.shape[0]
    tm = min(MEMORY_KV_ROW_TILE, n)
    row = lambda w: pl.BlockSpec((tm, w), lambda i: (i, 0))
    return pl.pallas_call(
        _memkv_kernel,
        grid=(n // tm,),
        in_specs=[row(D_MODEL), _const_spec((1, D_MODEL)), _const_spec(w_mk.shape), _const_spec(w_mv.shape)],
        out_specs=[row(MEM_WIDTH), row(MEM_WIDTH)],
        out_shape=[jax.ShapeDtypeStruct((n, MEM_WIDTH), F32)] * 2,
        compiler_params=_params("parallel"),
        name="memory_kv",
    )(mem2d, g, w_mk, w_mv)


def _tail_kernel(x_ref, a_ref, r_ref, mk_ref, mv_ref, wo_ref, gc_ref, wq_ref, wco_ref, gm_ref, wu_ref, wd_ref, gf_ref,
                 y_ref):
    n_seq = mk_ref.shape[0]
    rows_per_seq = x_ref.shape[0] // n_seq
    x1 = x_ref[...] + _dot(a_ref[...], wo_ref[:ATTN_WIDTH, :]) + _dot(r_ref[...], wo_ref[ATTN_WIDTH:, :])
    q = _dot(_rms(x1, gc_ref[...]), wq_ref[...]).astype(BF16)
    outs = {}

    def cross_chain(b, h):
        rows = slice(b * rows_per_seq, (b + 1) * rows_per_seq)
        sl = slice(h * MEM_HEAD_DIM, (h + 1) * MEM_HEAD_DIM)
        s = _dg(q[rows, sl], mk_ref[b, :, sl].astype(BF16), _NT) * (MEM_HEAD_DIM ** -0.5)
        yield
        p = jnp.exp(s - jnp.max(s, axis=-1, keepdims=True))
        den = jnp.sum(p, axis=-1, keepdims=True)
        yield
        outs[(b, h)] = _dg(p.astype(BF16), mv_ref[b, :, sl].astype(BF16), _NN) * (1.0 / den)
        yield

    _run_interleaved([cross_chain(b, h) for b in range(n_seq) for h in range(MEM_HEADS)])
    o = _cat_rows(*[_cat_lanes(*[outs[(b, h)] for h in range(MEM_HEADS)]) for b in range(n_seq)])
    x2 = x1 + _dot(o, wco_ref[...])
    up = _dot(_rms(x2, gm_ref[...]), wu_ref[...])
    act = jnp.square(jnp.maximum(up, 0.0))
    y_ref[...] = _rms(x2 + _dot(act, wd_ref[...]), gf_ref[...])


def _tail(x2d, a_out, r_out, mk, mv, lw, batch, seq):
    n = batch * seq
    tq = min(TAIL_ROW_TILE, n)
    if seq >= tq:
        assert seq % tq == 0
        seq_per_tile, tiles_per_seq = 1, seq // tq
        mem_spec = pl.BlockSpec((1, N_MEM, MEM_WIDTH), lambda i: (i // tiles_per_seq, 0, 0))
    else:
        assert tq % seq == 0
        seq_per_tile = tq // seq
        mem_spec = pl.BlockSpec((seq_per_tile, N_MEM, MEM_WIDTH), lambda i: (i, 0, 0))
    row = lambda w: pl.BlockSpec((tq, w), lambda i: (i, 0))

    def resident(shape):
        nd = len(shape)
        return pl.BlockSpec(shape, lambda *_: (0,) * nd, pipeline_mode=pl.Buffered(1))

    weights = [lw['w_out'], lw['norm_cross_g'], lw['w_cq'], lw['w_co'], lw['norm_mlp_g'], lw['w_up'], lw['w_down'],
               lw['norm_final_g']]
    return pl.pallas_call(
        _tail_kernel,
        grid=(n // tq,),
        in_specs=[row(D_MODEL), row(ATTN_WIDTH), row(RWKV_WIDTH), mem_spec, mem_spec]
                 + [resident(w.shape) for w in weights],
        out_specs=row(D_MODEL),
        out_shape=jax.ShapeDtypeStruct(x2d.shape, F32),
        compiler_params=_params("parallel"),
        name="tail",
    )(x2d, a_out, r_out, mk, mv, *weights)


def _trunk(x, mk, mv, k_past, v_past, shift_prev, state0, lw, table):
    batch, seq = x.shape[0], x.shape[1]
    x2d = x.reshape(batch * seq, D_MODEL)
    if k_past is None:
        bias = _rel_bias(table, CHUNK, WINDOW + CHUNK)
        a_out, r_out, state, k_buf, v_buf, shift_new = _prompt_mixer(x2d, shift_prev, state0, lw, bias, batch, seq)
    else:
        q, k, v, zr = _in_proj(x2d, lw['norm_mix_g'], lw['w_in'])
        bias = _rel_bias(table, seq, WINDOW + seq)
        a_out = _sample_attention(q, k, v, k_past.reshape(batch * WINDOW, KV_WIDTH),
                                  v_past.reshape(batch * WINDOW, KV_WIDTH), lw['attn_sink'], bias, seq)
        k_buf = jnp.concatenate([k_past, k.reshape(batch, seq, KV_WIDTH)], axis=1)[:, -WINDOW:]
        v_buf = jnp.concatenate([v_past, v.reshape(batch, seq, KV_WIDTH)], axis=1)[:, -WINDOW:]
        r_out, state = _rwkv_mixer(zr, shift_prev, state0, lw, batch, seq)
        shift_new = zr.reshape(batch, seq, RWKV_PROJ)[:, -1:]
    y = _tail(x2d, a_out, r_out, mk, mv, lw, batch, seq)
    kv_shape = (batch, WINDOW, KV_HEADS, HEAD_DIM)
    return y.reshape(x.shape), k_buf.reshape(kv_shape), v_buf.reshape(kv_shape), shift_new, state


def kernel(x_prompt, x_sample, mem_prompt, cache_attn_k, cache_attn_v, cache_mem_k, cache_mem_v, state_shift,
           state_wkv, norm_mix_g, w_in, attn_sink, rel_bias_table, rwkv_mu, rwkv_w0, rwkv_w2, rwkv_a0, rwkv_a2,
           rwkv_g2, rwkv_k_k, rwkv_k_a, rwkv_r_k, rwkv_ln_w, rwkv_ln_b, w_out, norm_cross_g, norm_mem_g, w_cq,
           w_mk, w_mv, w_co, norm_mlp_g, w_up, w_down, norm_final_g):
    assert norm_mix_g.shape[0] == 1, "single-layer trunk"
    bp, dec_b = x_prompt.shape[0], x_sample.shape[0]
    vec = lambda p: p[0].reshape(1, -1)
    w_in_b, w_out_b, w_cq_b, w_co_b, w_up_b, w_down_b, w_mk_b, w_mv_b = _to_bf16(
        w_in[0], w_out[0], w_cq[0], w_co[0], w_up[0], w_down[0], w_mk[0], w_mv[0])
    lw = {
        'norm_mix_g': vec(norm_mix_g), 'w_in': w_in_b, 'attn_sink': attn_sink[0],
        'rwkv_mu': rwkv_mu[0], 'rwkv_w0': rwkv_w0[0], 'rwkv_w2': rwkv_w2[0], 'rwkv_a0': rwkv_a0[0],
        'rwkv_a2': rwkv_a2[0], 'rwkv_g2': rwkv_g2[0], 'rwkv_k_k': rwkv_k_k[0], 'rwkv_k_a': rwkv_k_a[0],
        'rwkv_r_k': rwkv_r_k[0], 'rwkv_ln_w': rwkv_ln_w[0], 'rwkv_ln_b': rwkv_ln_b[0],
        'w_out': w_out_b, 'norm_cross_g': vec(norm_cross_g), 'w_cq': w_cq_b,
        'w_co': w_co_b, 'norm_mlp_g': vec(norm_mlp_g), 'w_up': w_up_b,
        'w_down': w_down_b, 'norm_final_g': norm_final_g.reshape(1, -1),
    }
    mk, mv = _memory_kv(mem_prompt.reshape(bp * N_MEM, D_MODEL), vec(norm_mem_g), w_mk_b, w_mv_b)
    mk = mk.reshape(bp, N_MEM, MEM_WIDTH)
    mv = mv.reshape(bp, N_MEM, MEM_WIDTH)
    shift0 = jnp.zeros((bp, 1, RWKV_PROJ), F32)
    wkv0 = jnp.zeros((bp, RWKV_HEADS, HEAD_DIM, HEAD_DIM), F32)
    yp, pk, pv, psh, pS = _trunk(x_prompt, mk, mv, None, None, shift0, wkv0, lw, rel_bias_table)
    ys, sk, sv, ssh, sS = _trunk(
        x_sample, cache_mem_k[0].reshape(dec_b, N_MEM, MEM_WIDTH), cache_mem_v[0].reshape(dec_b, N_MEM, MEM_WIDTH),
        cache_attn_k[0].reshape(dec_b, WINDOW, KV_WIDTH), cache_attn_v[0].reshape(dec_b, WINDOW, KV_WIDTH),
        state_shift[0], state_wkv[0], lw, rel_bias_table)
    mem_shape = (1, bp, N_MEM, MEM_HEADS, MEM_HEAD_DIM)
    return (yp, ys, pk[None], pv[None], mk.reshape(mem_shape), mv.reshape(mem_shape), psh[None], pS[None],
            sk[None], sv[None], ssh[None], sS[None])
```

```python
import functools
import math

import numpy as np
import jax
import jax.numpy as jnp
from jax import lax
from jax.experimental import pallas as pl
from jax.experimental.pallas import tpu as pltpu

F32 = jnp.float32
BF16 = jnp.bfloat16

D_MODEL = 1024
CHUNK = 64
WINDOW = 128
HEAD_DIM = 64
ATTN_WIDTH = 512
ATTN_HEADS = 8
KV_HEADS = 2
GROUP = 4
KV_WIDTH = 128
RWKV_WIDTH = 512
RWKV_HEADS = 8
DECAY_LORA = 64
AAA_LORA = 64
GATE_LORA = 128
RWKV_PROJ = 1792
IN_PROJ = 2560
N_MEM = 256
MEM_HEADS = 4
MEM_HEAD_DIM = 128
MEM_WIDTH = 512
D_FF = 4096
REL_BUCKETS = 32
REL_MAX_DIST = 128
NORM_EPS = 1e-6
GN_EPS = 64e-5
LOG2_E = math.log2(math.e)

V7X_VMEM_LIMIT_BYTES = 52 * 1024 * 1024
MEMORY_KV_ROW_TILE = 512
IN_PROJ_ROW_TILE = 512
PROJ_TILE = 256
TAIL_ROW_TILE = 512
WEIGHT_CAST_STEPS = 8
RWKV_SUB_CHUNKS = 4
RWKV_LOCAL_IN_FLIGHT = 3


def _params(*sem):
    return pltpu.CompilerParams(dimension_semantics=sem, vmem_limit_bytes=V7X_VMEM_LIMIT_BYTES)


def _const_spec(shape):
    nd = len(shape)
    return pl.BlockSpec(shape, lambda *_: (0,) * nd)


_NN = ((1,), (0,))
_NT = ((1,), (1,))
_TN = ((0,), (0,))


def _dg(a, b, dims):
    return lax.dot_general(a, b, (dims, ((), ())), preferred_element_type=F32)


def _dot(a, b):
    return _dg(a.astype(BF16), b.astype(BF16), _NN)


def _cat_rows(*xs):
    return jnp.concatenate(xs, axis=0)


def _cat_lanes(*xs):
    return jnp.concatenate(xs, axis=1)


def _run_interleaved(chains):
    active = list(chains)
    while active:
        still = []
        for ch in active:
            try:
                next(ch)
                still.append(ch)
            except StopIteration:
                pass
        active = still


def _run_tasks(tasks):
    finished = set()
    running = {}
    waiting = dict(tasks)
    while waiting or running:
        for name in [n for n, (_, deps) in waiting.items() if all(d in finished or d not in tasks for d in deps)]:
            running[name] = list(waiting.pop(name)[0]())
        for name in list(running):
            alive = []
            for ch in running[name]:
                try:
                    next(ch)
                    alive.append(ch)
                except StopIteration:
                    pass
            if alive:
                running[name] = alive
            else:
                del running[name]
                finished.add(name)


def _rms(x, g):
    return x * lax.rsqrt(jnp.mean(x * x, axis=-1, keepdims=True) + NORM_EPS) * g


def _cast_kernel(*refs):
    n = len(refs) // 2
    for src, dst in zip(refs[:n], refs[n:]):
        dst[...] = src[...].astype(BF16)


def _to_bf16(*weights):
    specs = [pl.BlockSpec((w.shape[0] // WEIGHT_CAST_STEPS, w.shape[1]), lambda i: (i, 0)) for w in weights]
    return pl.pallas_call(
        _cast_kernel,
        grid=(WEIGHT_CAST_STEPS,),
        in_specs=specs,
        out_specs=specs,
        out_shape=[jax.ShapeDtypeStruct(w.shape, BF16) for w in weights],
        compiler_params=_params("parallel"),
        name="cast_weights",
    )(*weights)


def _inproj_kernel(x_ref, g_ref, w_ref, q_ref, k_ref, v_ref, zr_ref):
    h = _rms(x_ref[...], g_ref[...]).astype(BF16)
    q = jnp.dot(h, w_ref[:, :ATTN_WIDTH], preferred_element_type=F32)
    q_ref[...] = (q * (HEAD_DIM ** -0.5 * LOG2_E)).astype(BF16)
    k_ref[...] = jnp.dot(h, w_ref[:, ATTN_WIDTH:ATTN_WIDTH + KV_WIDTH], preferred_element_type=F32)
    v_ref[...] = jnp.dot(h, w_ref[:, ATTN_WIDTH + KV_WIDTH:ATTN_WIDTH + 2 * KV_WIDTH],
                         preferred_element_type=F32)
    zr_ref[...] = jnp.dot(h, w_ref[:, ATTN_WIDTH + 2 * KV_WIDTH:], preferred_element_type=F32)


def _in_proj(x2d, g, w_bf16):
    n = x2d.shape[0]
    tm = min(IN_PROJ_ROW_TILE, n)
    row = lambda w: pl.BlockSpec((tm, w), lambda i: (i, 0))
    return pl.pallas_call(
        _inproj_kernel,
        grid=(n // tm,),
        in_specs=[row(D_MODEL), _const_spec((1, D_MODEL)), _const_spec((D_MODEL, IN_PROJ))],
        out_specs=[row(ATTN_WIDTH), row(KV_WIDTH), row(KV_WIDTH), row(RWKV_PROJ)],
        out_shape=[jax.ShapeDtypeStruct((n, ATTN_WIDTH), BF16)]
                  + [jax.ShapeDtypeStruct((n, w), F32) for w in (KV_WIDTH, KV_WIDTH, RWKV_PROJ)],
        compiler_params=_params("parallel"),
        name="in_proj",
    )(x2d, g, w_bf16)


def _t5_bucket(rel):
    half = REL_BUCKETS // 2
    max_exact = half // 2
    assert REL_MAX_DIST == max_exact * 2 ** 4 and half - max_exact == 2 * 4
    n = np.abs(rel)
    large = max_exact + sum((n * n >= max_exact * max_exact * 2 ** t).astype(np.int64)
                            for t in range(1, half - max_exact))
    return (np.where(rel > 0, half, 0) + np.where(n < max_exact, n, large)).astype(np.int32)


def _bias_kernel(table_ref, bucket_ref, out_ref):
    bucket = bucket_ref[...]
    hits = [bucket == b for b in range(REL_BUCKETS)]
    for h in range(ATTN_HEADS):
        acc = jnp.zeros(bucket.shape, F32)
        for b in range(REL_BUCKETS):
            acc = jnp.where(hits[b], table_ref[b, h], acc)
        out_ref[h] = acc * LOG2_E


def _rel_bias(table, n_q, n_k):
    rel = np.arange(n_k)[None, :] - WINDOW - np.arange(n_q)[:, None]
    bucket = jnp.asarray(_t5_bucket(rel))
    bias = pl.pallas_call(
        _bias_kernel,
        in_specs=[pl.BlockSpec(memory_space=pltpu.SMEM), pl.BlockSpec(memory_space=pltpu.VMEM)],
        out_specs=pl.BlockSpec(memory_space=pltpu.VMEM),
        out_shape=jax.ShapeDtypeStruct((ATTN_HEADS, n_q, n_k), F32),
        name="rel_bias",
    )(table, bucket)
    return bias.reshape(KV_HEADS, GROUP * n_q, n_k)


def _group_sinks(sink_ref, n_q):
    row_group = lax.broadcasted_iota(jnp.int32, (GROUP * n_q, 1), 0) // n_q
    sinks = []
    for kvh in range(KV_HEADS):
        sink = jnp.zeros((GROUP * n_q, 1), F32)
        for g in range(GROUP):
            sink = jnp.where(row_group == g, sink_ref[kvh * GROUP + g] * LOG2_E, sink)
        sinks.append(sink)
    return sinks


def _attn_chain(q, keys, vals, bias, sink, valid, o_ref, rows, kvh):
    n_q = q.shape[0]
    qh = _cat_rows(*[q[:, (kvh * GROUP + g) * HEAD_DIM:(kvh * GROUP + g + 1) * HEAD_DIM]
                     for g in range(GROUP)])
    s = _dg(qh, keys, _NT) + bias
    if valid is not None:
        s = jnp.where(valid, s, -jnp.inf)
    yield
    m = jnp.maximum(jnp.max(s, axis=-1, keepdims=True), sink)
    p = jnp.exp2(s - m)
    den = jnp.sum(p, axis=-1, keepdims=True) + jnp.exp2(sink - m)
    yield
    o = _dg(p.astype(BF16), vals, _NN) * (1.0 / den)
    for g in range(GROUP):
        head = kvh * GROUP + g
        o_ref[rows, head * HEAD_DIM:(head + 1) * HEAD_DIM] = o[g * n_q:(g + 1) * n_q].astype(o_ref.dtype)
    yield


def _sample_attn_kernel(seq, sink_ref, q_ref, kp_ref, kn_ref, vp_ref, vn_ref, bias_ref, o_ref):
    batch = q_ref.shape[0] // seq
    sinks = _group_sinks(sink_ref, seq)
    chains = []
    for b in range(batch):
        rows = slice(b * seq, (b + 1) * seq)
        past = slice(b * WINDOW, (b + 1) * WINDOW)
        k_all = _cat_rows(kp_ref[past, :], kn_ref[rows, :]).astype(BF16)
        v_all = _cat_rows(vp_ref[past, :], vn_ref[rows, :]).astype(BF16)
        for kvh in range(KV_HEADS):
            lanes = slice(kvh * HEAD_DIM, (kvh + 1) * HEAD_DIM)
            chains.append(_attn_chain(q_ref[rows, :], k_all[:, lanes], v_all[:, lanes], bias_ref[kvh], sinks[kvh],
                                      None, o_ref, rows, kvh))
    _run_interleaved(chains)


def _sample_attention(q, k, v, k_past, v_past, sink, bias, seq):
    vmem = pl.BlockSpec(memory_space=pltpu.VMEM)
    return pl.pallas_call(
        functools.partial(_sample_attn_kernel, seq),
        in_specs=[pl.BlockSpec(memory_space=pltpu.SMEM)] + [vmem] * 6,
        out_specs=vmem,
        out_shape=jax.ShapeDtypeStruct(q.shape, BF16),
        name="sample_attention",
    )(sink, q, k_past, k, v_past, v, bias)


def _split2(x):
    hi = x.astype(BF16)
    lo = (x - hi.astype(F32)).astype(BF16)
    return hi, lo


def _softplus(x):
    return jnp.maximum(x, 0.0) + jnp.log(1.0 + jnp.exp(-jnp.abs(x)))


def _sigmoid(x):
    return 1.0 / (1.0 + jnp.exp(-x))


def _rwkv_kernel(valid_rows, fused, steps, *refs):
    if fused:
        (xn_ref, x0_ref, gmix_ref, win_ref, shift_ref, s0_ref, mu_ref, w0_ref, w2_ref, a0_ref, a2_ref, g2_ref, kk_ref,
         ka_ref, rk_ref, lnw_ref, lnb_ref, seg_ref, tri_ref, sink_ref, bias_ref) = refs[:21]
        (out_ref, s_ref, attn_ref, ktail_ref, vtail_ref, shiftout_ref, carry_ref, y_ref, sbd_ref, zr_scr, q_scr, k_scr,
         v_scr, kp_ref, vp_ref) = refs[21:]
        g = pl.program_id(0)
        c = lax.rem(g, steps)
        cur = lax.rem(g, 2)
        zr_ref, q_ref, kc_ref, vc_ref = zr_scr.at[cur], q_scr.at[cur], k_scr.at[cur], v_scr.at[cur]
    else:
        (zr_ref, shift_ref, s0_ref, mu_ref, w0_ref, w2_ref, a0_ref, a2_ref, g2_ref, kk_ref, ka_ref, rk_ref, lnw_ref,
         lnb_ref, seg_ref, tri_ref) = refs[:16]
        out_ref, s_ref, y_ref = refs[16:]
    C = CHUNK
    R = out_ref.shape[0]
    n_sub = R // C
    pairs = range(RWKV_HEADS // 2)
    PAIR = 2 * HEAD_DIM
    W = RWKV_WIDTH

    def block_diagonal(s0, p):
        zero = jnp.zeros((HEAD_DIM, HEAD_DIM), F32)
        return _cat_rows(_cat_lanes(s0[2 * p], zero), _cat_lanes(zero, s0[2 * p + 1]))

    if fused:
        @pl.when(c == 0)
        def _():
            carry_ref[0:1, :] = shift_ref[0]
            for p in pairs:
                sbd_ref[p] = block_diagonal(s0_ref.at[0], p)
            kp_ref[...] = jnp.zeros(kp_ref.shape, F32)
            vp_ref[...] = jnp.zeros(vp_ref.shape, F32)

    def proj_chains(x_ref, slot):
        h = _rms(x_ref[...], gmix_ref[...]).astype(BF16)
        q_end, k_end, v_end = ATTN_WIDTH, ATTN_WIDTH + KV_WIDTH, ATTN_WIDTH + 2 * KV_WIDTH

        def tile(lo, hi):
            z = jnp.dot(h, win_ref[:, lo:hi], preferred_element_type=F32)
            if hi <= q_end:
                q_scr[slot, :, lo:hi] = (z * (HEAD_DIM ** -0.5 * LOG2_E)).astype(BF16)
            elif lo == q_end:
                k_scr[slot] = z[:, :KV_WIDTH]
                v_scr[slot] = z[:, KV_WIDTH:]
            else:
                zr_scr[slot, :, lo - v_end:hi - v_end] = z
            yield

        assert q_end % PROJ_TILE == 0 and v_end - q_end == PROJ_TILE
        return [tile(lo, lo + PROJ_TILE) for lo in range(0, IN_PROJ, PROJ_TILE)]

    if fused:
        @pl.when(g == 0)
        def _():
            _run_interleaved(proj_chains(x0_ref, 0))

    seg = seg_ref[...]
    seg2 = _cat_rows(seg, seg)

    def head_sum(x):
        hi, lo = _split2(x)
        tiles = [_dg(_cat_lanes(hi[:, t * PAIR:(t + 1) * PAIR], lo[:, t * PAIR:(t + 1) * PAIR]), seg2, _NN)
                 for t in range(W // PAIR)]
        return _cat_lanes(*tiles)

    first_row = lax.broadcasted_iota(jnp.int32, (C, 1), 0) == 0
    tri3 = tri_ref[...]

    lane = lax.broadcasted_iota(jnp.int32, (C, PAIR), 1)
    trow = lax.broadcasted_iota(jnp.int32, (C, PAIR), 0)
    even = lane < HEAD_DIM
    tcol = jnp.where(even, lane, lane - HEAD_DIM)
    strict = tcol < trow
    incl = tcol <= trow
    eye = jnp.where(tcol == trow, 1.0, 0.0).astype(F32)
    brow = lax.broadcasted_iota(jnp.int32, (PAIR, PAIR), 0) < HEAD_DIM
    bcol = lax.broadcasted_iota(jnp.int32, (PAIR, PAIR), 1) < HEAD_DIM
    on_diag = brow == bcol

    def bd(x):
        zero = jnp.zeros_like(x)
        return _cat_rows(jnp.where(even, x, zero), jnp.where(even, zero, x))

    def bd2(pair):
        return bd(pair[0]), bd(pair[1])

    def mm(a_pair, w_pair, dims=_NN):
        if dims == _NT:
            w_pair = (w_pair[0].T, w_pair[1].T)
        first = _dg(_cat_lanes(a_pair[0], a_pair[1]), _cat_rows(w_pair[0], w_pair[0]), _NN)
        return first + _dg(a_pair[0], w_pair[1], _NN)

    prepped = {}
    ready = {}
    if fused:
        state = {p: sbd_ref[p] for p in pairs}
        skey = lambda j, p: p
    else:
        state = {(j, p): block_diagonal(s0_ref.at[j], p) for j in range(n_sub) for p in pairs}
        skey = lambda j, p: (j, p)

    def prep_chain(j):
        rows = slice(j * C, (j + 1) * C)
        zr = zr_ref[rows, :]
        if not fused:
            before = shift_ref[j]
        else:
            before = carry_ref[0:1, :] if j == 0 else zr_ref[j * C - 1:j * C, :]
        z_prev = jnp.where(first_row, before, pltpu.roll(zr, 1, axis=0))
        zs = zr + (z_prev - zr) * mu_ref[...]
        r = zs[:, :W]
        k = zs[:, W:2 * W]
        v = zs[:, 2 * W:3 * W]
        wd = zs[:, 3 * W:3 * W + DECAY_LORA]
        ad = zs[:, 3 * W + DECAY_LORA:3 * W + DECAY_LORA + AAA_LORA]
        gd = zs[:, 3 * W + DECAY_LORA + AAA_LORA:]
        w_log = -_softplus(-(w0_ref[...] + _dot(jnp.tanh(wd), w2_ref[...]))) - 0.5
        lw = -jnp.exp(w_log)
        a = _sigmoid(a0_ref[...] + _dot(ad, a2_ref[...]))
        gate = _dot(_sigmoid(gd), g2_ref[...])
        kk = k * kk_ref[...]
        kk = kk * lax.rsqrt(jnp.maximum(head_sum(kk * kk), 1e-24))
        k2 = k * (1.0 + (a - 1.0) * ka_ref[...])
        if valid_rows < C:
            live = lax.broadcasted_iota(jnp.int32, (C, 1), 0) < valid_rows
            lw = jnp.where(live, lw, 0.0)
            kk = jnp.where(live, kk, 0.0)
            k2 = jnp.where(live, k2, 0.0)
        bvec = kk * a
        yield
        l1 = lw.astype(BF16)
        rem = lw - l1.astype(F32)
        l2 = rem.astype(BF16)
        l3 = (rem - l2.astype(F32)).astype(BF16)
        sums = _dg(tri3, _cat_rows(l1, l2, l3), _NN)
        li = sums[:C]
        lrev = sums[C:]
        yield
        inv_p = jnp.exp(-li)
        to_end = jnp.exp(lrev)
        prepped[j] = dict(
            at=_split2(-kk * jnp.exp(li - lw)), rt=_split2(r * jnp.exp(li)), bt=_split2(bvec * inv_p),
            kt=_split2(k2 * inv_p), bh=_split2(bvec * to_end), kh=_split2(k2 * to_end), v=_split2(v),
            p_end=jnp.exp(li[C - 1:C, :]), bonus=head_sum(r * k2 * rk_ref[...]) * v, gate=gate)
        yield

    def local_chain(j, p):
        d = prepped[j]
        lanes = slice(p * PAIR, (p + 1) * PAIR)
        cut = lambda pair: (pair[0][:, lanes], pair[1][:, lanes])
        at_p, rt_p, bt_p, kt_p, bh_p, kh_p, v_p = map(cut, (d['at'], d['rt'], d['bt'], d['kt'], d['bh'], d['kh'],
                                                            d['v']))
        left = (_cat_rows(at_p[0], rt_p[0]), _cat_rows(at_p[1], rt_p[1]))
        right = (_cat_rows(bd(bt_p[0]), bd(kt_p[0])), _cat_rows(bd(bt_p[1]), bd(kt_p[1])))
        aa = mm(left, right, _NT)
        yield
        a_ab = jnp.where(strict, aa[:C, :PAIR], 0.0)
        a_ak = jnp.where(strict, aa[:C, PAIR:], 0.0)
        a_rb = jnp.where(incl, aa[C:, :PAIR], 0.0)
        a_rk = jnp.where(incl, aa[C:, PAIR:], 0.0)
        inv = eye + a_ab
        ps = _split2(a_ab)
        power = mm(ps, bd2(ps))
        span = 2
        yield
        while span < C:
            ps = _split2(power)
            pw = bd2(ps)
            ih = _split2(inv)
            if span * 2 < C:
                both = mm((_cat_rows(ih[0], ps[0]), _cat_rows(ih[1], ps[1])), pw)
                inv = inv + both[:C]
                power = both[C:]
            else:
                inv = inv + mm(ih, pw)
            span *= 2
            yield
        ready[(j, p)] = dict(inv=_split2(inv), akrk=_split2(_cat_rows(a_ak, a_rk)), rb=_split2(a_rb), left=left,
                             bhkh=(_cat_rows(bh_p[0], kh_p[0]), _cat_rows(bh_p[1], kh_p[1])), v=v_p,
                             p_end=d['p_end'][:, lanes])

    def state_chain(j, p):
        d = ready.pop((j, p))
        s_prev = state[skey(j, p)]
        v_hi, v_lo = d['v']
        s_hi, s_lo = _split2(s_prev)
        lhs = (_cat_lanes(d['left'][0], d['akrk'][0]), _cat_lanes(d['left'][1], d['akrk'][1]))
        both = mm(lhs, (_cat_rows(s_hi.T, bd(v_hi)), _cat_rows(s_lo.T, bd(v_lo))))
        rhs = both[:C]
        y0 = both[C:]
        yield
        u_pair = _split2(mm(d['inv'], bd2(_split2(rhs))))
        yield
        y_ref[j * C:(j + 1) * C, p * PAIR:(p + 1) * PAIR] = y0 + mm(d['rb'], bd2(u_pair))
        t_hi = _cat_rows(u_pair[0], v_hi)
        t_lo = _cat_rows(u_pair[1], v_lo)
        w_hi, w_lo = d['bhkh']
        upd = _dg(_cat_rows(t_hi, t_lo), _cat_rows(w_hi, w_hi), _TN) + _dg(t_hi, w_lo, _TN)
        state[skey(j, p)] = s_prev * d['p_end'] + jnp.where(on_diag, upd, 0.0)
        yield

    def post_chain(j):
        rows = slice(j * C, (j + 1) * C)
        d = prepped.pop(j)
        y = y_ref[rows, :]
        mean = head_sum(y) * (1.0 / HEAD_DIM)
        dev = y - mean
        yield
        var = head_sum(dev * dev) * (1.0 / HEAD_DIM)
        yn = dev * lax.rsqrt(var + GN_EPS) * lnw_ref[...] + lnb_ref[...]
        out_ref[rows, :] = ((yn + d['bonus']) * d['gate']).astype(out_ref.dtype)
        yield

    tasks = {}
    for j in range(n_sub):
        tasks[('prep', j)] = (lambda j=j: [prep_chain(j)], [('prep', j - 1), ('local', j - RWKV_LOCAL_IN_FLIGHT)])
        tasks[('local', j)] = (lambda j=j: [local_chain(j, p) for p in pairs],
                               [('prep', j), ('local', j - RWKV_LOCAL_IN_FLIGHT)])
        tasks[('state', j)] = (lambda j=j: [state_chain(j, p) for p in pairs],
                               [('local', j)] + ([('state', j - 1)] if fused else []))
        tasks[('post', j)] = (lambda j=j: [post_chain(j)], [('state', j)])
    if fused:
        n_k = WINDOW + CHUNK
        k_all = _cat_rows(kp_ref[...], kc_ref[...]).astype(BF16)
        v_all = _cat_rows(vp_ref[...], vc_ref[...]).astype(BF16)
        first_valid = jnp.where(c == 0, WINDOW, 0)
        kcol = lax.broadcasted_iota(jnp.int32, (1, n_k), 1)
        sinks = _group_sinks(sink_ref, CHUNK)

        def attn_chains(j):
            rows = slice(j * C, (j + 1) * C)
            keys = slice(j * C, j * C + n_k)
            valid = kcol + j * C >= first_valid if j * C < WINDOW else None
            return [_attn_chain(q_ref[rows, :], k_all[keys, kvh * HEAD_DIM:(kvh + 1) * HEAD_DIM],
                                v_all[keys, kvh * HEAD_DIM:(kvh + 1) * HEAD_DIM], bias_ref[kvh], sinks[kvh], valid,
                                attn_ref, rows, kvh) for kvh in range(KV_HEADS)]

        for j in range(n_sub):
            tasks[('attn', j)] = (lambda j=j: attn_chains(j), [('attn', j - 1)])
        for t, chain in enumerate(proj_chains(xn_ref, 1 - cur)):
            tasks[('proj', t)] = (lambda chain=chain: [chain], [('proj', t - 1), ('prep', min(t, n_sub - 1))])
    _run_tasks(tasks)

    def store_state(dst, s_bd, p):
        dst[2 * p] = s_bd[:HEAD_DIM, :HEAD_DIM]
        dst[2 * p + 1] = s_bd[HEAD_DIM:, HEAD_DIM:]

    if not fused:
        for j in range(n_sub):
            for p in pairs:
                store_state(s_ref.at[j], state[(j, p)], p)
        return
    carry_ref[0:1, :] = zr_ref[R - 1:R, :]
    for p in pairs:
        sbd_ref[p] = state[p]
    kp_ref[...] = kc_ref[R - WINDOW:R, :]
    vp_ref[...] = vc_ref[R - WINDOW:R, :]

    @pl.when(c == steps - 1)
    def _():
        for p in pairs:
            store_state(s_ref.at[0], state[p], p)
        ktail_ref[0] = kc_ref[R - WINDOW:R, :]
        vtail_ref[0] = vc_ref[R - WINDOW:R, :]
        shiftout_ref[0] = zr_ref[R - 1:R, :]


def _rwkv_operands(lw):
    seg = jnp.asarray(np.kron(np.eye(2), np.ones((HEAD_DIM, HEAD_DIM))), BF16)
    ones = np.ones((CHUNK, CHUNK))
    tri3 = jnp.asarray(np.concatenate([np.tile(np.tril(ones), (1, 3)), np.tile(np.triu(ones, 1), (1, 3))]), BF16)
    row = lambda name: lw[name].reshape(1, -1)
    return [row('rwkv_mu'), row('rwkv_w0'), lw['rwkv_w2'].astype(BF16), row('rwkv_a0'),
            lw['rwkv_a2'].astype(BF16), lw['rwkv_g2'].astype(BF16), row('rwkv_k_k'), row('rwkv_k_a'),
            row('rwkv_r_k'), row('rwkv_ln_w'), row('rwkv_ln_b'), seg, tri3]


def _rwkv_mixer(zr, shift_prev, state0, lw, batch, seq):
    assert seq <= CHUNK
    if seq < CHUNK:
        zr = jnp.pad(zr.reshape(batch, seq, RWKV_PROJ), ((0, 0), (0, CHUNK - seq), (0, 0))).reshape(-1, RWKV_PROJ)
    params = _rwkv_operands(lw)
    vmem = pl.BlockSpec(memory_space=pltpu.VMEM)
    out, state = pl.pallas_call(
        functools.partial(_rwkv_kernel, seq, False, 1),
        in_specs=[vmem] * (3 + len(params)),
        out_specs=[vmem, vmem],
        out_shape=[jax.ShapeDtypeStruct((batch * CHUNK, RWKV_WIDTH), BF16), jax.ShapeDtypeStruct(state0.shape, F32)],
        scratch_shapes=[pltpu.VMEM((batch * CHUNK, RWKV_WIDTH), F32)],
        compiler_params=pltpu.CompilerParams(vmem_limit_bytes=V7X_VMEM_LIMIT_BYTES),
        name="rwkv_mixer",
    )(zr, shift_prev, state0, *params)
    if seq < CHUNK:
        out = out.reshape(batch, CHUNK, RWKV_WIDTH)[:, :seq].reshape(batch * seq, RWKV_WIDTH)
    return out, state


def _prompt_mixer(x2d, shift_prev, state0, lw, bias, batch, seq):
    rows = CHUNK * RWKV_SUB_CHUNKS
    steps = seq // rows
    total = batch * steps
    params = _rwkv_operands(lw)
    seq_block = lambda shape: pl.BlockSpec((1,) + shape, lambda g: (g // steps,) + (0,) * len(shape))
    row_spec = lambda w: pl.BlockSpec((rows, w), lambda g: (g, 0))

    def resident(shape):
        nd = len(shape)
        return pl.BlockSpec(shape, lambda g: (0,) * nd, pipeline_mode=pl.Buffered(1))

    in_specs = [pl.BlockSpec((rows, D_MODEL), lambda g: (jnp.minimum(g + 1, total - 1), 0)),
                pl.BlockSpec((rows, D_MODEL), lambda g: (0, 0), pipeline_mode=pl.Buffered(1)),
                resident((1, D_MODEL)), resident((D_MODEL, IN_PROJ)),
                seq_block((1, RWKV_PROJ)), seq_block((RWKV_HEADS, HEAD_DIM, HEAD_DIM))]
    in_specs += [resident(p.shape) for p in params]
    in_specs += [pl.BlockSpec(memory_space=pltpu.SMEM), resident(bias.shape)]
    out_specs = [row_spec(RWKV_WIDTH), seq_block((RWKV_HEADS, HEAD_DIM, HEAD_DIM)), row_spec(ATTN_WIDTH),
                 seq_block((WINDOW, KV_WIDTH)), seq_block((WINDOW, KV_WIDTH)), seq_block((1, RWKV_PROJ))]
    n = batch * seq
    out_shape = [jax.ShapeDtypeStruct((n, RWKV_WIDTH), BF16), jax.ShapeDtypeStruct(state0.shape, F32),
                 jax.ShapeDtypeStruct((n, ATTN_WIDTH), BF16), jax.ShapeDtypeStruct((batch, WINDOW, KV_WIDTH), F32),
                 jax.ShapeDtypeStruct((batch, WINDOW, KV_WIDTH), F32),
                 jax.ShapeDtypeStruct((batch, 1, RWKV_PROJ), F32)]
    scratch = [pltpu.VMEM((8, RWKV_PROJ), F32), pltpu.VMEM((rows, RWKV_WIDTH), F32),
               pltpu.VMEM((RWKV_HEADS // 2, 2 * HEAD_DIM, 2 * HEAD_DIM), F32),
               pltpu.VMEM((2, rows, RWKV_PROJ), F32), pltpu.VMEM((2, rows, ATTN_WIDTH), BF16),
                pltpu.VMEM((2, rows, KV_WIDTH), F32), pltpu.VMEM((2, rows, KV_WIDTH), F32),
                pltpu.VMEM((WINDOW, KV_WIDTH), F32), pltpu.VMEM((WINDOW, KV_WIDTH), F32)]
    r_out, state, a_out, k_tail, v_tail, shift_new = pl.pallas_call(
        functools.partial(_rwkv_kernel, rows, True, steps),
        grid=(total,),
        in_specs=in_specs,
        out_specs=out_specs,
        out_shape=out_shape,
        scratch_shapes=scratch,
        compiler_params=_params("arbitrary"),
        name="prompt_mixer",
    )(x2d, x2d, lw['norm_mix_g'], lw['w_in'], shift_prev, state0, *params, lw['attn_sink'], bias)
    return a_out, r_out, state, k_tail, v_tail, shift_new


def _memkv_kernel(m_ref, g_ref, wk_ref, wv_ref, k_ref, v_ref):
    mn = _rms(m_ref[...], g_ref[...]).astype(BF16)
    k_ref[...] = jnp.dot(mn, wk_ref[...], preferred_element_type=F32)
    v_ref[...] = jnp.dot(mn, wv_ref[...], preferred_element_type=F32)


def _memory_kv(mem2d, g, w_mk, w_mv):
    n = mem2d.shape[0]
    tm = min(MEMORY_KV_ROW_TILE, n)
    row = lambda w: pl.BlockSpec((tm, w), lambda i: (i, 0))
    return pl.pallas_call(
        _memkv_kernel,
        grid=(n // tm,),
        in_specs=[row(D_MODEL), _const_spec((1, D_MODEL)), _const_spec(w_mk.shape), _const_spec(w_mv.shape)],
        out_specs=[row(MEM_WIDTH), row(MEM_WIDTH)],
        out_shape=[jax.ShapeDtypeStruct((n, MEM_WIDTH), F32)] * 2,
        compiler_params=_params("parallel"),
        name="memory_kv",
    )(mem2d, g, w_mk, w_mv)


def _tail_kernel(x_ref, a_ref, r_ref, mk_ref, mv_ref, wo_ref, gc_ref, wq_ref, wco_ref, gm_ref, wu_ref, wd_ref, gf_ref,
                 y_ref):
    n_seq = mk_ref.shape[0]
    rows_per_seq = x_ref.shape[0] // n_seq
    x1 = x_ref[...] + _dot(a_ref[...], wo_ref[:ATTN_WIDTH, :]) + _dot(r_ref[...], wo_ref[ATTN_WIDTH:, :])
    q = _dot(_rms(x1, gc_ref[...]), wq_ref[...]).astype(BF16)
    outs = {}

    def cross_chain(b, h):
        rows = slice(b * rows_per_seq, (b + 1) * rows_per_seq)
        sl = slice(h * MEM_HEAD_DIM, (h + 1) * MEM_HEAD_DIM)
        s = _dg(q[rows, sl], mk_ref[b, :, sl].astype(BF16), _NT) * (MEM_HEAD_DIM ** -0.5)
        yield
        p = jnp.exp(s - jnp.max(s, axis=-1, keepdims=True))
        den = jnp.sum(p, axis=-1, keepdims=True)
        yield
        outs[(b, h)] = _dg(p.astype(BF16), mv_ref[b, :, sl].astype(BF16), _NN) * (1.0 / den)
        yield

    _run_interleaved([cross_chain(b, h) for b in range(n_seq) for h in range(MEM_HEADS)])
    o = _cat_rows(*[_cat_lanes(*[outs[(b, h)] for h in range(MEM_HEADS)]) for b in range(n_seq)])
    x2 = x1 + _dot(o, wco_ref[...])
    up = _dot(_rms(x2, gm_ref[...]), wu_ref[...])
    act = jnp.square(jnp.maximum(up, 0.0))
    y_ref[...] = _rms(x2 + _dot(act, wd_ref[...]), gf_ref[...])


def _tail(x2d, a_out, r_out, mk, mv, lw, batch, seq):
    n = batch * seq
    tq = min(TAIL_ROW_TILE, n)
    if seq >= tq:
        assert seq % tq == 0
        seq_per_tile, tiles_per_seq = 1, seq // tq
        mem_spec = pl.BlockSpec((1, N_MEM, MEM_WIDTH), lambda i: (i // tiles_per_seq, 0, 0))
    else:
        assert tq % seq == 0
        seq_per_tile = tq // seq
        mem_spec = pl.BlockSpec((seq_per_tile, N_MEM, MEM_WIDTH), lambda i: (i, 0, 0))
    row = lambda w: pl.BlockSpec((tq, w), lambda i: (i, 0))

    def resident(shape):
        nd = len(shape)
        return pl.BlockSpec(shape, lambda *_: (0,) * nd, pipeline_mode=pl.Buffered(1))

    weights = [lw['w_out'], lw['norm_cross_g'], lw['w_cq'], lw['w_co'], lw['norm_mlp_g'], lw['w_up'], lw['w_down'],
               lw['norm_final_g']]
    return pl.pallas_call(
        _tail_kernel,
        grid=(n // tq,),
        in_specs=[row(D_MODEL), row(ATTN_WIDTH), row(RWKV_WIDTH), mem_spec, mem_spec]
                 + [resident(w.shape) for w in weights],
        out_specs=row(D_MODEL),
        out_shape=jax.ShapeDtypeStruct(x2d.shape, F32),
        compiler_params=_params("parallel"),
        name="tail",
    )(x2d, a_out, r_out, mk, mv, *weights)


def _trunk(x, mk, mv, k_past, v_past, shift_prev, state0, lw, table):
    batch, seq = x.shape[0], x.shape[1]
    x2d = x.reshape(batch * seq, D_MODEL)
    if k_past is None:
        bias = _rel_bias(table, CHUNK, WINDOW + CHUNK)
        a_out, r_out, state, k_buf, v_buf, shift_new = _prompt_mixer(x2d, shift_prev, state0, lw, bias, batch, seq)
    else:
        q, k, v, zr = _in_proj(x2d, lw['norm_mix_g'], lw['w_in'])
        bias = _rel_bias(table, seq, WINDOW + seq)
        a_out = _sample_attention(q, k, v, k_past.reshape(batch * WINDOW, KV_WIDTH),
                                  v_past.reshape(batch * WINDOW, KV_WIDTH), lw['attn_sink'], bias, seq)
        k_buf = jnp.concatenate([k_past, k.reshape(batch, seq, KV_WIDTH)], axis=1)[:, -WINDOW:]
        v_buf = jnp.concatenate([v_past, v.reshape(batch, seq, KV_WIDTH)], axis=1)[:, -WINDOW:]
        r_out, state = _rwkv_mixer(zr, shift_prev, state0, lw, batch, seq)
        shift_new = zr.reshape(batch, seq, RWKV_PROJ)[:, -1:]
    y = _tail(x2d, a_out, r_out, mk, mv, lw, batch, seq)
    kv_shape = (batch, WINDOW, KV_HEADS, HEAD_DIM)
    return y.reshape(x.shape), k_buf.reshape(kv_shape), v_buf.reshape(kv_shape), shift_new, state


def kernel(x_prompt, x_sample, mem_prompt, cache_attn_k, cache_attn_v, cache_mem_k, cache_mem_v, state_shift,
           state_wkv, norm_mix_g, w_in, attn_sink, rel_bias_table, rwkv_mu, rwkv_w0, rwkv_w2, rwkv_a0, rwkv_a2,
           rwkv_g2, rwkv_k_k, rwkv_k_a, rwkv_r_k, rwkv_ln_w, rwkv_ln_b, w_out, norm_cross_g, norm_mem_g, w_cq,
           w_mk, w_mv, w_co, norm_mlp_g, w_up, w_down, norm_final_g):
    assert norm_mix_g.shape[0] == 1, "single-layer trunk"
    bp, dec_b = x_prompt.shape[0], x_sample.shape[0]
    vec = lambda p: p[0].reshape(1, -1)
    w_in_b, w_out_b, w_cq_b, w_co_b, w_up_b, w_down_b, w_mk_b, w_mv_b = _to_bf16(
        w_in[0], w_out[0], w_cq[0], w_co[0], w_up[0], w_down[0], w_mk[0], w_mv[0])
    lw = {
        'norm_mix_g': vec(norm_mix_g), 'w_in': w_in_b, 'attn_sink': attn_sink[0],
        'rwkv_mu': rwkv_mu[0], 'rwkv_w0': rwkv_w0[0], 'rwkv_w2': rwkv_w2[0], 'rwkv_a0': rwkv_a0[0],
        'rwkv_a2': rwkv_a2[0], 'rwkv_g2': rwkv_g2[0], 'rwkv_k_k': rwkv_k_k[0], 'rwkv_k_a': rwkv_k_a[0],
        'rwkv_r_k': rwkv_r_k[0], 'rwkv_ln_w': rwkv_ln_w[0], 'rwkv_ln_b': rwkv_ln_b[0],
        'w_out': w_out_b, 'norm_cross_g': vec(norm_cross_g), 'w_cq': w_cq_b,
        'w_co': w_co_b, 'norm_mlp_g': vec(norm_mlp_g), 'w_up': w_up_b,
        'w_down': w_down_b, 'norm_final_g': norm_final_g.reshape(1, -1),
    }
    mk, mv = _memory_kv(mem_prompt.reshape(bp * N_MEM, D_MODEL), vec(norm_mem_g), w_mk_b, w_mv_b)
    mk = mk.reshape(bp, N_MEM, MEM_WIDTH)
    mv = mv.reshape(bp, N_MEM, MEM_WIDTH)
    shift0 = jnp.zeros((bp, 1, RWKV_PROJ), F32)
    wkv0 = jnp.zeros((bp, RWKV_HEADS, HEAD_DIM, HEAD_DIM), F32)
    yp, pk, pv, psh, pS = _trunk(x_prompt, mk, mv, None, None, shift0, wkv0, lw, rel_bias_table)
    ys, sk, sv, ssh, sS = _trunk(
        x_sample, cache_mem_k[0].reshape(dec_b, N_MEM, MEM_WIDTH), cache_mem_v[0].reshape(dec_b, N_MEM, MEM_WIDTH),
        cache_attn_k[0].reshape(dec_b, WINDOW, KV_WIDTH), cache_attn_v[0].reshape(dec_b, WINDOW, KV_WIDTH),
        state_shift[0], state_wkv[0], lw, rel_bias_table)
    mem_shape = (1, bp, N_MEM, MEM_HEADS, MEM_HEAD_DIM)
    return (yp, ys, pk[None], pv[None], mk.reshape(mem_shape), mv.reshape(mem_shape), psh[None], pS[None],
            sk[None], sv[None], ssh[None], sS[None])
```

```python
import functools
import math

import numpy as np
import jax
import jax.numpy as jnp
from jax import lax
from jax.experimental import pallas as pl
from jax.experimental.pallas import tpu as pltpu

F32 = jnp.float32
BF16 = jnp.bfloat16

D_MODEL = 1024
CHUNK = 64
WINDOW = 128
HEAD_DIM = 64
ATTN_WIDTH = 512
ATTN_HEADS = 8
KV_HEADS = 2
GROUP = 4
KV_WIDTH = 128
RWKV_WIDTH = 512
RWKV_HEADS = 8
DECAY_LORA = 64
AAA_LORA = 64
GATE_LORA = 128
RWKV_PROJ = 1792
IN_PROJ = 2560
N_MEM = 256
MEM_HEADS = 4
MEM_HEAD_DIM = 128
MEM_WIDTH = 512
D_FF = 4096
REL_BUCKETS = 32
REL_MAX_DIST = 128
NORM_EPS = 1e-6
GN_EPS = 64e-5
LOG2_E = math.log2(math.e)

V7X_VMEM_LIMIT_BYTES = 52 * 1024 * 1024
MEMORY_KV_ROW_TILE = 512
IN_PROJ_ROW_TILE = 512
PROJ_TILE = 256
TAIL_ROW_TILE = 512
MLP_TILE = 1024
WEIGHT_CAST_STEPS = 8
RWKV_SUB_CHUNKS = 4
RWKV_LOCAL_IN_FLIGHT = 3


def _params(*sem):
    return pltpu.CompilerParams(dimension_semantics=sem, vmem_limit_bytes=V7X_VMEM_LIMIT_BYTES)


def _const_spec(shape):
    nd = len(shape)
    return pl.BlockSpec(shape, lambda *_: (0,) * nd)


_NN = ((1,), (0,))
_NT = ((1,), (1,))
_TN = ((0,), (0,))


def _dg(a, b, dims):
    return lax.dot_general(a, b, (dims, ((), ())), preferred_element_type=F32)


def _dot(a, b):
    return _dg(a.astype(BF16), b.astype(BF16), _NN)


def _cat_rows(*xs):
    return jnp.concatenate(xs, axis=0)


def _cat_lanes(*xs):
    return jnp.concatenate(xs, axis=1)


def _run_interleaved(chains):
    active = list(chains)
    while active:
        still = []
        for ch in active:
            try:
                next(ch)
                still.append(ch)
            except StopIteration:
                pass
        active = still


def _run_tasks(tasks):
    finished = set()
    running = {}
    waiting = dict(tasks)
    while waiting or running:
        for name in [n for n, (_, deps) in waiting.items() if all(d in finished or d not in tasks for d in deps)]:
            running[name] = list(waiting.pop(name)[0]())
        for name in list(running):
            alive = []
            for ch in running[name]:
                try:
                    next(ch)
                    alive.append(ch)
                except StopIteration:
                    pass
            if alive:
                running[name] = alive
            else:
                del running[name]
                finished.add(name)


def _rms(x, g):
    return x * lax.rsqrt(jnp.mean(x * x, axis=-1, keepdims=True) + NORM_EPS) * g


def _cast_kernel(*refs):
    n = len(refs) // 2
    for src, dst in zip(refs[:n], refs[n:]):
        dst[...] = src[...].astype(BF16)


def _to_bf16(*weights):
    specs = [pl.BlockSpec((w.shape[0] // WEIGHT_CAST_STEPS, w.shape[1]), lambda i: (i, 0)) for w in weights]
    return pl.pallas_call(
        _cast_kernel,
        grid=(WEIGHT_CAST_STEPS,),
        in_specs=specs,
        out_specs=specs,
        out_shape=[jax.ShapeDtypeStruct(w.shape, BF16) for w in weights],
        compiler_params=_params("parallel"),
        name="cast_weights",
    )(*weights)


def _inproj_kernel(x_ref, g_ref, w_ref, q_ref, k_ref, v_ref, zr_ref):
    h = _rms(x_ref[...], g_ref[...]).astype(BF16)
    q = jnp.dot(h, w_ref[:, :ATTN_WIDTH], preferred_element_type=F32)
    q_ref[...] = (q * (HEAD_DIM ** -0.5 * LOG2_E)).astype(BF16)
    k_ref[...] = jnp.dot(h, w_ref[:, ATTN_WIDTH:ATTN_WIDTH + KV_WIDTH], preferred_element_type=F32)
    v_ref[...] = jnp.dot(h, w_ref[:, ATTN_WIDTH + KV_WIDTH:ATTN_WIDTH + 2 * KV_WIDTH],
                         preferred_element_type=F32)
    zr_ref[...] = jnp.dot(h, w_ref[:, ATTN_WIDTH + 2 * KV_WIDTH:], preferred_element_type=F32)


def _in_proj(x2d, g, w_bf16):
    n = x2d.shape[0]
    tm = min(IN_PROJ_ROW_TILE, n)
    row = lambda w: pl.BlockSpec((tm, w), lambda i: (i, 0))
    return pl.pallas_call(
        _inproj_kernel,
        grid=(n // tm,),
        in_specs=[row(D_MODEL), _const_spec((1, D_MODEL)), _const_spec((D_MODEL, IN_PROJ))],
        out_specs=[row(ATTN_WIDTH), row(KV_WIDTH), row(KV_WIDTH), row(RWKV_PROJ)],
        out_shape=[jax.ShapeDtypeStruct((n, ATTN_WIDTH), BF16)]
                  + [jax.ShapeDtypeStruct((n, w), F32) for w in (KV_WIDTH, KV_WIDTH, RWKV_PROJ)],
        compiler_params=_params("parallel"),
        name="in_proj",
    )(x2d, g, w_bf16)


def _t5_bucket(rel):
    half = REL_BUCKETS // 2
    max_exact = half // 2
    assert REL_MAX_DIST == max_exact * 2 ** 4 and half - max_exact == 2 * 4
    n = np.abs(rel)
    large = max_exact + sum((n * n >= max_exact * max_exact * 2 ** t).astype(np.int64)
                            for t in range(1, half - max_exact))
    return (np.where(rel > 0, half, 0) + np.where(n < max_exact, n, large)).astype(np.int32)


def _bias_kernel(table_ref, bucket_ref, out_ref):
    bucket = bucket_ref[...]
    hits = [bucket == b for b in range(REL_BUCKETS)]
    for h in range(ATTN_HEADS):
        acc = jnp.zeros(bucket.shape, F32)
        for b in range(REL_BUCKETS):
            acc = jnp.where(hits[b], table_ref[b, h], acc)
        out_ref[h] = acc * LOG2_E


def _rel_bias(table, n_q, n_k):
    rel = np.arange(n_k)[None, :] - WINDOW - np.arange(n_q)[:, None]
    bucket = jnp.asarray(_t5_bucket(rel))
    bias = pl.pallas_call(
        _bias_kernel,
        in_specs=[pl.BlockSpec(memory_space=pltpu.SMEM), pl.BlockSpec(memory_space=pltpu.VMEM)],
        out_specs=pl.BlockSpec(memory_space=pltpu.VMEM),
        out_shape=jax.ShapeDtypeStruct((ATTN_HEADS, n_q, n_k), F32),
        name="rel_bias",
    )(table, bucket)
    return bias.reshape(KV_HEADS, GROUP * n_q, n_k)


def _group_sinks(sink_ref, n_q):
    row_group = lax.broadcasted_iota(jnp.int32, (GROUP * n_q, 1), 0) // n_q
    sinks = []
    for kvh in range(KV_HEADS):
        sink = jnp.zeros((GROUP * n_q, 1), F32)
        for g in range(GROUP):
            sink = jnp.where(row_group == g, sink_ref[kvh * GROUP + g] * LOG2_E, sink)
        sinks.append(sink)
    return sinks


def _attn_chain(q, keys, vals, bias, sink, valid, o_ref, rows, kvh):
    n_q = q.shape[0]
    qh = _cat_rows(*[q[:, (kvh * GROUP + g) * HEAD_DIM:(kvh * GROUP + g + 1) * HEAD_DIM]
                     for g in range(GROUP)])
    s = _dg(qh, keys, _NT) + bias
    if valid is not None:
        s = jnp.where(valid, s, -jnp.inf)
    yield
    m = jnp.maximum(jnp.max(s, axis=-1, keepdims=True), sink)
    p = jnp.exp2(s - m)
    den = jnp.sum(p, axis=-1, keepdims=True) + jnp.exp2(sink - m)
    yield
    o = _dg(p.astype(BF16), vals, _NN) * (1.0 / den)
    for g in range(GROUP):
        head = kvh * GROUP + g
        o_ref[rows, head * HEAD_DIM:(head + 1) * HEAD_DIM] = o[g * n_q:(g + 1) * n_q].astype(o_ref.dtype)
    yield


def _sample_attn_kernel(seq, sink_ref, q_ref, kp_ref, kn_ref, vp_ref, vn_ref, bias_ref, o_ref):
    batch = q_ref.shape[0] // seq
    sinks = _group_sinks(sink_ref, seq)
    chains = []
    for b in range(batch):
        rows = slice(b * seq, (b + 1) * seq)
        past = slice(b * WINDOW, (b + 1) * WINDOW)
        k_all = _cat_rows(kp_ref[past, :], kn_ref[rows, :]).astype(BF16)
        v_all = _cat_rows(vp_ref[past, :], vn_ref[rows, :]).astype(BF16)
        for kvh in range(KV_HEADS):
            lanes = slice(kvh * HEAD_DIM, (kvh + 1) * HEAD_DIM)
            chains.append(_attn_chain(q_ref[rows, :], k_all[:, lanes], v_all[:, lanes], bias_ref[kvh], sinks[kvh],
                                      None, o_ref, rows, kvh))
    _run_interleaved(chains)


def _sample_attention(q, k, v, k_past, v_past, sink, bias, seq):
    vmem = pl.BlockSpec(memory_space=pltpu.VMEM)
    return pl.pallas_call(
        functools.partial(_sample_attn_kernel, seq),
        in_specs=[pl.BlockSpec(memory_space=pltpu.SMEM)] + [vmem] * 6,
        out_specs=vmem,
        out_shape=jax.ShapeDtypeStruct(q.shape, BF16),
        name="sample_attention",
    )(sink, q, k_past, k, v_past, v, bias)


def _split2(x):
    hi = x.astype(BF16)
    lo = (x - hi.astype(F32)).astype(BF16)
    return hi, lo


def _softplus(x):
    return jnp.maximum(x, 0.0) + jnp.log(1.0 + jnp.exp(-jnp.abs(x)))


def _sigmoid(x):
    return 1.0 / (1.0 + jnp.exp(-x))


def _rwkv_kernel(valid_rows, fused, steps, *refs):
    if fused:
        (xn_ref, x0_ref, gmix_ref, win_ref, shift_ref, s0_ref, mu_ref, w0_ref, w2_ref, a0_ref, a2_ref, g2_ref, kk_ref,
         ka_ref, rk_ref, lnw_ref, lnb_ref, seg_ref, tri_ref, sink_ref, bias_ref) = refs[:21]
        (out_ref, s_ref, attn_ref, ktail_ref, vtail_ref, shiftout_ref, carry_ref, y_ref, sbd_ref, zr_scr, q_scr, k_scr,
         v_scr, kp_ref, vp_ref) = refs[21:]
        g = pl.program_id(0)
        c = lax.rem(g, steps)
        cur = lax.rem(g, 2)
        zr_ref, q_ref, kc_ref, vc_ref = zr_scr.at[cur], q_scr.at[cur], k_scr.at[cur], v_scr.at[cur]
    else:
        (zr_ref, shift_ref, s0_ref, mu_ref, w0_ref, w2_ref, a0_ref, a2_ref, g2_ref, kk_ref, ka_ref, rk_ref, lnw_ref,
         lnb_ref, seg_ref, tri_ref) = refs[:16]
        out_ref, s_ref, y_ref = refs[16:]
    C = CHUNK
    R = out_ref.shape[0]
    n_sub = R // C
    pairs = range(RWKV_HEADS // 2)
    PAIR = 2 * HEAD_DIM
    W = RWKV_WIDTH

    def block_diagonal(s0, p):
        zero = jnp.zeros((HEAD_DIM, HEAD_DIM), F32)
        return _cat_rows(_cat_lanes(s0[2 * p], zero), _cat_lanes(zero, s0[2 * p + 1]))

    if fused:
        @pl.when(c == 0)
        def _():
            carry_ref[0:1, :] = shift_ref[0]
            for p in pairs:
                sbd_ref[p] = block_diagonal(s0_ref.at[0], p)
            kp_ref[...] = jnp.zeros(kp_ref.shape, F32)
            vp_ref[...] = jnp.zeros(vp_ref.shape, F32)

    def proj_chains(x_ref, slot):
        h = _rms(x_ref[...], gmix_ref[...]).astype(BF16)
        q_end, k_end, v_end = ATTN_WIDTH, ATTN_WIDTH + KV_WIDTH, ATTN_WIDTH + 2 * KV_WIDTH

        def tile(lo, hi):
            z = jnp.dot(h, win_ref[:, lo:hi], preferred_element_type=F32)
            if hi <= q_end:
                q_scr[slot, :, lo:hi] = (z * (HEAD_DIM ** -0.5 * LOG2_E)).astype(BF16)
            elif lo == q_end:
                k_scr[slot] = z[:, :KV_WIDTH]
                v_scr[slot] = z[:, KV_WIDTH:]
            else:
                zr_scr[slot, :, lo - v_end:hi - v_end] = z
            yield

        assert q_end % PROJ_TILE == 0 and v_end - q_end == PROJ_TILE
        return [tile(lo, lo + PROJ_TILE) for lo in range(0, IN_PROJ, PROJ_TILE)]

    if fused:
        @pl.when(g == 0)
        def _():
            _run_interleaved(proj_chains(x0_ref, 0))

    seg = seg_ref[...]
    seg2 = _cat_rows(seg, seg)

    def head_sum(x):
        hi, lo = _split2(x)
        tiles = [_dg(_cat_lanes(hi[:, t * PAIR:(t + 1) * PAIR], lo[:, t * PAIR:(t + 1) * PAIR]), seg2, _NN)
                 for t in range(W // PAIR)]
        return _cat_lanes(*tiles)

    first_row = lax.broadcasted_iota(jnp.int32, (C, 1), 0) == 0
    tri3 = tri_ref[...]

    lane = lax.broadcasted_iota(jnp.int32, (C, PAIR), 1)
    trow = lax.broadcasted_iota(jnp.int32, (C, PAIR), 0)
    even = lane < HEAD_DIM
    tcol = jnp.where(even, lane, lane - HEAD_DIM)
    strict = tcol < trow
    incl = tcol <= trow
    eye = jnp.where(tcol == trow, 1.0, 0.0).astype(F32)
    brow = lax.broadcasted_iota(jnp.int32, (PAIR, PAIR), 0) < HEAD_DIM
    bcol = lax.broadcasted_iota(jnp.int32, (PAIR, PAIR), 1) < HEAD_DIM
    on_diag = brow == bcol

    def bd(x):
        zero = jnp.zeros_like(x)
        return _cat_rows(jnp.where(even, x, zero), jnp.where(even, zero, x))

    def bd2(pair):
        return bd(pair[0]), bd(pair[1])

    def mm(a_pair, w_pair, dims=_NN):
        if dims == _NT:
            w_pair = (w_pair[0].T, w_pair[1].T)
        first = _dg(_cat_lanes(a_pair[0], a_pair[1]), _cat_rows(w_pair[0], w_pair[0]), _NN)
        return first + _dg(a_pair[0], w_pair[1], _NN)

    prepped = {}
    ready = {}
    if fused:
        state = {p: sbd_ref[p] for p in pairs}
        skey = lambda j, p: p
    else:
        state = {(j, p): block_diagonal(s0_ref.at[j], p) for j in range(n_sub) for p in pairs}
        skey = lambda j, p: (j, p)

    def prep_chain(j):
        rows = slice(j * C, (j + 1) * C)
        zr = zr_ref[rows, :]
        if not fused:
            before = shift_ref[j]
        else:
            before = carry_ref[0:1, :] if j == 0 else zr_ref[j * C - 1:j * C, :]
        z_prev = jnp.where(first_row, before, pltpu.roll(zr, 1, axis=0))
        zs = zr + (z_prev - zr) * mu_ref[...]
        r = zs[:, :W]
        k = zs[:, W:2 * W]
        v = zs[:, 2 * W:3 * W]
        wd = zs[:, 3 * W:3 * W + DECAY_LORA]
        ad = zs[:, 3 * W + DECAY_LORA:3 * W + DECAY_LORA + AAA_LORA]
        gd = zs[:, 3 * W + DECAY_LORA + AAA_LORA:]
        w_log = -_softplus(-(w0_ref[...] + _dot(jnp.tanh(wd), w2_ref[...]))) - 0.5
        lw = -jnp.exp(w_log)
        a = _sigmoid(a0_ref[...] + _dot(ad, a2_ref[...]))
        gate = _dot(_sigmoid(gd), g2_ref[...])
        kk = k * kk_ref[...]
        kk = kk * lax.rsqrt(jnp.maximum(head_sum(kk * kk), 1e-24))
        k2 = k * (1.0 + (a - 1.0) * ka_ref[...])
        if valid_rows < C:
            live = lax.broadcasted_iota(jnp.int32, (C, 1), 0) < valid_rows
            lw = jnp.where(live, lw, 0.0)
            kk = jnp.where(live, kk, 0.0)
            k2 = jnp.where(live, k2, 0.0)
        bvec = kk * a
        yield
        l1 = lw.astype(BF16)
        rem = lw - l1.astype(F32)
        l2 = rem.astype(BF16)
        l3 = (rem - l2.astype(F32)).astype(BF16)
        sums = _dg(tri3, _cat_rows(l1, l2, l3), _NN)
        li = sums[:C]
        lrev = sums[C:]
        yield
        inv_p = jnp.exp(-li)
        to_end = jnp.exp(lrev)
        prepped[j] = dict(
            at=_split2(-kk * jnp.exp(li - lw)), rt=_split2(r * jnp.exp(li)), bt=_split2(bvec * inv_p),
            kt=_split2(k2 * inv_p), bh=_split2(bvec * to_end), kh=_split2(k2 * to_end), v=_split2(v),
            p_end=jnp.exp(li[C - 1:C, :]), bonus=head_sum(r * k2 * rk_ref[...]) * v, gate=gate)
        yield

    def local_chain(j, p):
        d = prepped[j]
        lanes = slice(p * PAIR, (p + 1) * PAIR)
        cut = lambda pair: (pair[0][:, lanes], pair[1][:, lanes])
        at_p, rt_p, bt_p, kt_p, bh_p, kh_p, v_p = map(cut, (d['at'], d['rt'], d['bt'], d['kt'], d['bh'], d['kh'],
                                                            d['v']))
        left = (_cat_rows(at_p[0], rt_p[0]), _cat_rows(at_p[1], rt_p[1]))
        right = (_cat_rows(bd(bt_p[0]), bd(kt_p[0])), _cat_rows(bd(bt_p[1]), bd(kt_p[1])))
        aa = mm(left, right, _NT)
        yield
        a_ab = jnp.where(strict, aa[:C, :PAIR], 0.0)
        a_ak = jnp.where(strict, aa[:C, PAIR:], 0.0)
        a_rb = jnp.where(incl, aa[C:, :PAIR], 0.0)
        a_rk = jnp.where(incl, aa[C:, PAIR:], 0.0)
        inv = eye + a_ab
        ps = _split2(a_ab)
        power = mm(ps, bd2(ps))
        span = 2
        yield
        while span < C:
            ps = _split2(power)
            pw = bd2(ps)
            ih = _split2(inv)
            if span * 2 < C:
                both = mm((_cat_rows(ih[0], ps[0]), _cat_rows(ih[1], ps[1])), pw)
                inv = inv + both[:C]
                power = both[C:]
            else:
                inv = inv + mm(ih, pw)
            span *= 2
            yield
        ready[(j, p)] = dict(inv=_split2(inv), akrk=_split2(_cat_rows(a_ak, a_rk)), rb=_split2(a_rb), left=left,
                             bhkh=(_cat_rows(bh_p[0], kh_p[0]), _cat_rows(bh_p[1], kh_p[1])), v=v_p,
                             p_end=d['p_end'][:, lanes])

    def state_chain(j, p):
        d = ready.pop((j, p))
        s_prev = state[skey(j, p)]
        v_hi, v_lo = d['v']
        s_hi, s_lo = _split2(s_prev)
        lhs = (_cat_lanes(d['left'][0], d['akrk'][0]), _cat_lanes(d['left'][1], d['akrk'][1]))
        both = mm(lhs, (_cat_rows(s_hi.T, bd(v_hi)), _cat_rows(s_lo.T, bd(v_lo))))
        rhs = both[:C]
        y0 = both[C:]
        yield
        u_pair = _split2(mm(d['inv'], bd2(_split2(rhs))))
        yield
        y_ref[j * C:(j + 1) * C, p * PAIR:(p + 1) * PAIR] = y0 + mm(d['rb'], bd2(u_pair))
        t_hi = _cat_rows(u_pair[0], v_hi)
        t_lo = _cat_rows(u_pair[1], v_lo)
        w_hi, w_lo = d['bhkh']
        upd = _dg(_cat_rows(t_hi, t_lo), _cat_rows(w_hi, w_hi), _TN) + _dg(t_hi, w_lo, _TN)
        state[skey(j, p)] = s_prev * d['p_end'] + jnp.where(on_diag, upd, 0.0)
        yield

    def post_chain(j):
        rows = slice(j * C, (j + 1) * C)
        d = prepped.pop(j)
        y = y_ref[rows, :]
        mean = head_sum(y) * (1.0 / HEAD_DIM)
        dev = y - mean
        yield
        var = head_sum(dev * dev) * (1.0 / HEAD_DIM)
        yn = dev * lax.rsqrt(var + GN_EPS) * lnw_ref[...] + lnb_ref[...]
        out_ref[rows, :] = ((yn + d['bonus']) * d['gate']).astype(out_ref.dtype)
        yield

    tasks = {}
    for j in range(n_sub):
        tasks[('prep', j)] = (lambda j=j: [prep_chain(j)], [('prep', j - 1), ('local', j - RWKV_LOCAL_IN_FLIGHT)])
        tasks[('local', j)] = (lambda j=j: [local_chain(j, p) for p in pairs],
                               [('prep', j), ('local', j - RWKV_LOCAL_IN_FLIGHT)])
        tasks[('state', j)] = (lambda j=j: [state_chain(j, p) for p in pairs],
                               [('local', j)] + ([('state', j - 1)] if fused else []))
        tasks[('post', j)] = (lambda j=j: [post_chain(j)], [('state', j)])
    if fused:
        n_k = WINDOW + CHUNK
        k_all = _cat_rows(kp_ref[...], kc_ref[...]).astype(BF16)
        v_all = _cat_rows(vp_ref[...], vc_ref[...]).astype(BF16)
        first_valid = jnp.where(c == 0, WINDOW, 0)
        kcol = lax.broadcasted_iota(jnp.int32, (1, n_k), 1)
        sinks = _group_sinks(sink_ref, CHUNK)

        def attn_chains(j):
            rows = slice(j * C, (j + 1) * C)
            keys = slice(j * C, j * C + n_k)
            valid = kcol + j * C >= first_valid if j * C < WINDOW else None
            return [_attn_chain(q_ref[rows, :], k_all[keys, kvh * HEAD_DIM:(kvh + 1) * HEAD_DIM],
                                v_all[keys, kvh * HEAD_DIM:(kvh + 1) * HEAD_DIM], bias_ref[kvh], sinks[kvh], valid,
                                attn_ref, rows, kvh) for kvh in range(KV_HEADS)]

        for j in range(n_sub):
            tasks[('attn', j)] = (lambda j=j: attn_chains(j), [('attn', j - 1)])
        for t, chain in enumerate(proj_chains(xn_ref, 1 - cur)):
            tasks[('proj', t)] = (lambda chain=chain: [chain], [('proj', t - 1), ('prep', min(t, n_sub - 1))])
    _run_tasks(tasks)

    def store_state(dst, s_bd, p):
        dst[2 * p] = s_bd[:HEAD_DIM, :HEAD_DIM]
        dst[2 * p + 1] = s_bd[HEAD_DIM:, HEAD_DIM:]

    if not fused:
        for j in range(n_sub):
            for p in pairs:
                store_state(s_ref.at[j], state[(j, p)], p)
        return
    carry_ref[0:1, :] = zr_ref[R - 1:R, :]
    for p in pairs:
        sbd_ref[p] = state[p]
    kp_ref[...] = kc_ref[R - WINDOW:R, :]
    vp_ref[...] = vc_ref[R - WINDOW:R, :]

    @pl.when(c == steps - 1)
    def _():
        for p in pairs:
            store_state(s_ref.at[0], state[p], p)
        ktail_ref[0] = kc_ref[R - WINDOW:R, :]
        vtail_ref[0] = vc_ref[R - WINDOW:R, :]
        shiftout_ref[0] = zr_ref[R - 1:R, :]


def _rwkv_operands(lw):
    seg = jnp.asarray(np.kron(np.eye(2), np.ones((HEAD_DIM, HEAD_DIM))), BF16)
    ones = np.ones((CHUNK, CHUNK))
    tri3 = jnp.asarray(np.concatenate([np.tile(np.tril(ones), (1, 3)), np.tile(np.triu(ones, 1), (1, 3))]), BF16)
    row = lambda name: lw[name].reshape(1, -1)
    return [row('rwkv_mu'), row('rwkv_w0'), lw['rwkv_w2'].astype(BF16), row('rwkv_a0'),
            lw['rwkv_a2'].astype(BF16), lw['rwkv_g2'].astype(BF16), row('rwkv_k_k'), row('rwkv_k_a'),
            row('rwkv_r_k'), row('rwkv_ln_w'), row('rwkv_ln_b'), seg, tri3]


def _rwkv_mixer(zr, shift_prev, state0, lw, batch, seq):
    assert seq <= CHUNK
    if seq < CHUNK:
        zr = jnp.pad(zr.reshape(batch, seq, RWKV_PROJ), ((0, 0), (0, CHUNK - seq), (0, 0))).reshape(-1, RWKV_PROJ)
    params = _rwkv_operands(lw)
    vmem = pl.BlockSpec(memory_space=pltpu.VMEM)
    out, state = pl.pallas_call(
        functools.partial(_rwkv_kernel, seq, False, 1),
        in_specs=[vmem] * (3 + len(params)),
        out_specs=[vmem, vmem],
        out_shape=[jax.ShapeDtypeStruct((batch * CHUNK, RWKV_WIDTH), BF16), jax.ShapeDtypeStruct(state0.shape, F32)],
        scratch_shapes=[pltpu.VMEM((batch * CHUNK, RWKV_WIDTH), F32)],
        compiler_params=pltpu.CompilerParams(vmem_limit_bytes=V7X_VMEM_LIMIT_BYTES),
        name="rwkv_mixer",
    )(zr, shift_prev, state0, *params)
    if seq < CHUNK:
        out = out.reshape(batch, CHUNK, RWKV_WIDTH)[:, :seq].reshape(batch * seq, RWKV_WIDTH)
    return out, state


def _prompt_mixer(x2d, shift_prev, state0, lw, bias, batch, seq):
    rows = CHUNK * RWKV_SUB_CHUNKS
    steps = seq // rows
    total = batch * steps
    params = _rwkv_operands(lw)
    seq_block = lambda shape: pl.BlockSpec((1,) + shape, lambda g: (g // steps,) + (0,) * len(shape))
    row_spec = lambda w: pl.BlockSpec((rows, w), lambda g: (g, 0))

    def resident(shape):
        nd = len(shape)
        return pl.BlockSpec(shape, lambda g: (0,) * nd, pipeline_mode=pl.Buffered(1))

    in_specs = [pl.BlockSpec((rows, D_MODEL), lambda g: (jnp.minimum(g + 1, total - 1), 0)),
                pl.BlockSpec((rows, D_MODEL), lambda g: (0, 0), pipeline_mode=pl.Buffered(1)),
                resident((1, D_MODEL)), resident((D_MODEL, IN_PROJ)),
                seq_block((1, RWKV_PROJ)), seq_block((RWKV_HEADS, HEAD_DIM, HEAD_DIM))]
    in_specs += [resident(p.shape) for p in params]
    in_specs += [pl.BlockSpec(memory_space=pltpu.SMEM), resident(bias.shape)]
    out_specs = [row_spec(RWKV_WIDTH), seq_block((RWKV_HEADS, HEAD_DIM, HEAD_DIM)), row_spec(ATTN_WIDTH),
                 seq_block((WINDOW, KV_WIDTH)), seq_block((WINDOW, KV_WIDTH)), seq_block((1, RWKV_PROJ))]
    n = batch * seq
    out_shape = [jax.ShapeDtypeStruct((n, RWKV_WIDTH), BF16), jax.ShapeDtypeStruct(state0.shape, F32),
                 jax.ShapeDtypeStruct((n, ATTN_WIDTH), BF16), jax.ShapeDtypeStruct((batch, WINDOW, KV_WIDTH), F32),
                 jax.ShapeDtypeStruct((batch, WINDOW, KV_WIDTH), F32),
                 jax.ShapeDtypeStruct((batch, 1, RWKV_PROJ), F32)]
    scratch = [pltpu.VMEM((8, RWKV_PROJ), F32), pltpu.VMEM((rows, RWKV_WIDTH), F32),
               pltpu.VMEM((RWKV_HEADS // 2, 2 * HEAD_DIM, 2 * HEAD_DIM), F32),
               pltpu.VMEM((2, rows, RWKV_PROJ), F32), pltpu.VMEM((2, rows, ATTN_WIDTH), BF16),
                pltpu.VMEM((2, rows, KV_WIDTH), F32), pltpu.VMEM((2, rows, KV_WIDTH), F32),
                pltpu.VMEM((WINDOW, KV_WIDTH), F32), pltpu.VMEM((WINDOW, KV_WIDTH), F32)]
    r_out, state, a_out, k_tail, v_tail, shift_new = pl.pallas_call(
        functools.partial(_rwkv_kernel, rows, True, steps),
        grid=(total,),
        in_specs=in_specs,
        out_specs=out_specs,
        out_shape=out_shape,
        scratch_shapes=scratch,
        compiler_params=_params("arbitrary"),
        name="prompt_mixer",
    )(x2d, x2d, lw['norm_mix_g'], lw['w_in'], shift_prev, state0, *params, lw['attn_sink'], bias)
    return a_out, r_out, state, k_tail, v_tail, shift_new


def _memkv_kernel(m_ref, g_ref, wk_ref, wv_ref, k_ref, v_ref):
    mn = _rms(m_ref[...], g_ref[...]).astype(BF16)
    k_ref[...] = jnp.dot(mn, wk_ref[...], preferred_element_type=F32)
    v_ref[...] = jnp.dot(mn, wv_ref[...], preferred_element_type=F32)


def _memory_kv(mem2d, g, w_mk, w_mv):
    n = mem2d.shape[0]
    tm = min(MEMORY_KV_ROW_TILE, n)
    row = lambda w: pl.BlockSpec((tm, w), lambda i: (i, 0))
    return pl.pallas_call(
        _memkv_kernel,
        grid=(n // tm,),
        in_specs=[row(D_MODEL), _const_spec((1, D_MODEL)), _const_spec(w_mk.shape), _const_spec(w_mv.shape)],
        out_specs=[row(MEM_WIDTH), row(MEM_WIDTH)],
        out_shape=[jax.ShapeDtypeStruct((n, MEM_WIDTH), F32)] * 2,
        compiler_params=_params("parallel"),
        name="memory_kv",
    )(mem2d, g, w_mk, w_mv)


def _tail_kernel(x_ref, a_ref, r_ref, mk_ref, mv_ref, wo_ref, gc_ref, wq_ref, wco_ref, gm_ref, wu_ref, wd_ref, gf_ref,
                 y_ref):
    n_seq = mk_ref.shape[0]
    rows_per_seq = x_ref.shape[0] // n_seq
    x1 = x_ref[...] + _dot(a_ref[...], wo_ref[:ATTN_WIDTH, :]) + _dot(r_ref[...], wo_ref[ATTN_WIDTH:, :])
    q = _dot(_rms(x1, gc_ref[...]), wq_ref[...]).astype(BF16)
    outs = {}

    def cross_chain(b, h):
        rows = slice(b * rows_per_seq, (b + 1) * rows_per_seq)
        sl = slice(h * MEM_HEAD_DIM, (h + 1) * MEM_HEAD_DIM)
        s = _dg(q[rows, sl], mk_ref[b, :, sl].astype(BF16), _NT) * (MEM_HEAD_DIM ** -0.5)
        yield
        p = jnp.exp(s - jnp.max(s, axis=-1, keepdims=True))
        den = jnp.sum(p, axis=-1, keepdims=True)
        yield
        outs[(b, h)] = _dg(p.astype(BF16), mv_ref[b, :, sl].astype(BF16), _NN) * (1.0 / den)
        yield

    _run_interleaved([cross_chain(b, h) for b in range(n_seq) for h in range(MEM_HEADS)])
    o = _cat_rows(*[_cat_lanes(*[outs[(b, h)] for h in range(MEM_HEADS)]) for b in range(n_seq)])
    x2 = x1 + _dot(o, wco_ref[...])
    hm = _rms(x2, gm_ref[...]).astype(BF16)
    acc = x2
    for lo in range(0, D_FF, MLP_TILE):
        up = jnp.dot(hm, wu_ref[:, lo:lo + MLP_TILE], preferred_element_type=F32)
        act = jnp.square(jnp.maximum(up, 0.0)).astype(BF16)
        acc = acc + jnp.dot(act, wd_ref[lo:lo + MLP_TILE, :], preferred_element_type=F32)
    y_ref[...] = _rms(acc, gf_ref[...])


def _tail(x2d, a_out, r_out, mk, mv, lw, batch, seq):
    n = batch * seq
    tq = min(TAIL_ROW_TILE, n)
    if seq >= tq:
        assert seq % tq == 0
        seq_per_tile, tiles_per_seq = 1, seq // tq
        mem_spec = pl.BlockSpec((1, N_MEM, MEM_WIDTH), lambda i: (i // tiles_per_seq, 0, 0))
    else:
        assert tq % seq == 0
        seq_per_tile = tq // seq
        mem_spec = pl.BlockSpec((seq_per_tile, N_MEM, MEM_WIDTH), lambda i: (i, 0, 0))
    row = lambda w: pl.BlockSpec((tq, w), lambda i: (i, 0))

    def resident(shape):
        nd = len(shape)
        return pl.BlockSpec(shape, lambda *_: (0,) * nd, pipeline_mode=pl.Buffered(1))

    weights = [lw['w_out'], lw['norm_cross_g'], lw['w_cq'], lw['w_co'], lw['norm_mlp_g'], lw['w_up'], lw['w_down'],
               lw['norm_final_g']]
    return pl.pallas_call(
        _tail_kernel,
        grid=(n // tq,),
        in_specs=[row(D_MODEL), row(ATTN_WIDTH), row(RWKV_WIDTH), mem_spec, mem_spec]
                 + [resident(w.shape) for w in weights],
        out_specs=row(D_MODEL),
        out_shape=jax.ShapeDtypeStruct(x2d.shape, F32),
        compiler_params=_params("parallel"),
        name="tail",
    )(x2d, a_out, r_out, mk, mv, *weights)


def _trunk(x, mk, mv, k_past, v_past, shift_prev, state0, lw, table):
    batch, seq = x.shape[0], x.shape[1]
    x2d = x.reshape(batch * seq, D_MODEL)
    if k_past is None:
        bias = _rel_bias(table, CHUNK, WINDOW + CHUNK)
        a_out, r_out, state, k_buf, v_buf, shift_new = _prompt_mixer(x2d, shift_prev, state0, lw, bias, batch, seq)
    else:
        q, k, v, zr = _in_proj(x2d, lw['norm_mix_g'], lw['w_in'])
        bias = _rel_bias(table, seq, WINDOW + seq)
        a_out = _sample_attention(q, k, v, k_past.reshape(batch * WINDOW, KV_WIDTH),
                                  v_past.reshape(batch * WINDOW, KV_WIDTH), lw['attn_sink'], bias, seq)
        k_buf = jnp.concatenate([k_past, k.reshape(batch, seq, KV_WIDTH)], axis=1)[:, -WINDOW:]
        v_buf = jnp.concatenate([v_past, v.reshape(batch, seq, KV_WIDTH)], axis=1)[:, -WINDOW:]
        r_out, state = _rwkv_mixer(zr, shift_prev, state0, lw, batch, seq)
        shift_new = zr.reshape(batch, seq, RWKV_PROJ)[:, -1:]
    y = _tail(x2d, a_out, r_out, mk, mv, lw, batch, seq)
    kv_shape = (batch, WINDOW, KV_HEADS, HEAD_DIM)
    return y.reshape(x.shape), k_buf.reshape(kv_shape), v_buf.reshape(kv_shape), shift_new, state


def kernel(x_prompt, x_sample, mem_prompt, cache_attn_k, cache_attn_v, cache_mem_k, cache_mem_v, state_shift,
           state_wkv, norm_mix_g, w_in, attn_sink, rel_bias_table, rwkv_mu, rwkv_w0, rwkv_w2, rwkv_a0, rwkv_a2,
           rwkv_g2, rwkv_k_k, rwkv_k_a, rwkv_r_k, rwkv_ln_w, rwkv_ln_b, w_out, norm_cross_g, norm_mem_g, w_cq,
           w_mk, w_mv, w_co, norm_mlp_g, w_up, w_down, norm_final_g):
    assert norm_mix_g.shape[0] == 1, "single-layer trunk"
    bp, dec_b = x_prompt.shape[0], x_sample.shape[0]
    vec = lambda p: p[0].reshape(1, -1)
    w_in_b, w_out_b, w_cq_b, w_co_b, w_up_b, w_down_b, w_mk_b, w_mv_b = _to_bf16(
        w_in[0], w_out[0], w_cq[0], w_co[0], w_up[0], w_down[0], w_mk[0], w_mv[0])
    lw = {
        'norm_mix_g': vec(norm_mix_g), 'w_in': w_in_b, 'attn_sink': attn_sink[0],
        'rwkv_mu': rwkv_mu[0], 'rwkv_w0': rwkv_w0[0], 'rwkv_w2': rwkv_w2[0], 'rwkv_a0': rwkv_a0[0],
        'rwkv_a2': rwkv_a2[0], 'rwkv_g2': rwkv_g2[0], 'rwkv_k_k': rwkv_k_k[0], 'rwkv_k_a': rwkv_k_a[0],
        'rwkv_r_k': rwkv_r_k[0], 'rwkv_ln_w': rwkv_ln_w[0], 'rwkv_ln_b': rwkv_ln_b[0],
        'w_out': w_out_b, 'norm_cross_g': vec(norm_cross_g), 'w_cq': w_cq_b,
        'w_co': w_co_b, 'norm_mlp_g': vec(norm_mlp_g), 'w_up': w_up_b,
        'w_down': w_down_b, 'norm_final_g': norm_final_g.reshape(1, -1),
    }
    mk, mv = _memory_kv(mem_prompt.reshape(bp * N_MEM, D_MODEL), vec(norm_mem_g), w_mk_b, w_mv_b)
    mk = mk.reshape(bp, N_MEM, MEM_WIDTH)
    mv = mv.reshape(bp, N_MEM, MEM_WIDTH)
    shift0 = jnp.zeros((bp, 1, RWKV_PROJ), F32)
    wkv0 = jnp.zeros((bp, RWKV_HEADS, HEAD_DIM, HEAD_DIM), F32)
    yp, pk, pv, psh, pS = _trunk(x_prompt, mk, mv, None, None, shift0, wkv0, lw, rel_bias_table)
    ys, sk, sv, ssh, sS = _trunk(
        x_sample, cache_mem_k[0].reshape(dec_b, N_MEM, MEM_WIDTH), cache_mem_v[0].reshape(dec_b, N_MEM, MEM_WIDTH),
        cache_attn_k[0].reshape(dec_b, WINDOW, KV_WIDTH), cache_attn_v[0].reshape(dec_b, WINDOW, KV_WIDTH),
        state_shift[0], state_wkv[0], lw, rel_bias_table)
    mem_shape = (1, bp, N_MEM, MEM_HEADS, MEM_HEAD_DIM)
    return (yp, ys, pk[None], pv[None], mk.reshape(mem_shape), mv.reshape(mem_shape), psh[None], pS[None],
            sk[None], sv[None], ssh[None], sS[None])
```

```python
import functools
import math

import numpy as np
import jax
import jax.numpy as jnp
from jax import lax
from jax.experimental import pallas as pl
from jax.experimental.pallas import tpu as pltpu

F32 = jnp.float32
BF16 = jnp.bfloat16

D_MODEL = 1024
CHUNK = 64
WINDOW = 128
HEAD_DIM = 64
ATTN_WIDTH = 512
ATTN_HEADS = 8
KV_HEADS = 2
GROUP = 4
KV_WIDTH = 128
RWKV_WIDTH = 512
RWKV_HEADS = 8
DECAY_LORA = 64
AAA_LORA = 64
GATE_LORA = 128
RWKV_PROJ = 1792
IN_PROJ = 2560
N_MEM = 256
MEM_HEADS = 4
MEM_HEAD_DIM = 128
MEM_WIDTH = 512
D_FF = 4096
REL_BUCKETS = 32
REL_MAX_DIST = 128
NORM_EPS = 1e-6
GN_EPS = 64e-5
LOG2_E = math.log2(math.e)

V7X_VMEM_LIMIT_BYTES = 52 * 1024 * 1024
MEMORY_KV_ROW_TILE = 512
IN_PROJ_ROW_TILE = 512
PROJ_TILE = 256
TAIL_ROW_TILE = 512
WEIGHT_CAST_STEPS = 8
RWKV_SUB_CHUNKS = 4
RWKV_LOCAL_IN_FLIGHT = 3


def _params(*sem):
    return pltpu.CompilerParams(dimension_semantics=sem, vmem_limit_bytes=V7X_VMEM_LIMIT_BYTES)


def _const_spec(shape):
    nd = len(shape)
    return pl.BlockSpec(shape, lambda *_: (0,) * nd)


_NN = ((1,), (0,))
_NT = ((1,), (1,))
_TN = ((0,), (0,))


def _dg(a, b, dims):
    return lax.dot_general(a, b, (dims, ((), ())), preferred_element_type=F32)


def _dot(a, b):
    return _dg(a.astype(BF16), b.astype(BF16), _NN)


def _cat_rows(*xs):
    return jnp.concatenate(xs, axis=0)


def _cat_lanes(*xs):
    return jnp.concatenate(xs, axis=1)


def _run_interleaved(chains):
    active = list(chains)
    while active:
        still = []
        for ch in active:
            try:
                next(ch)
                still.append(ch)
            except StopIteration:
                pass
        active = still


def _run_tasks(tasks):
    finished = set()
    running = {}
    waiting = dict(tasks)
    while waiting or running:
        for name in [n for n, (_, deps) in waiting.items() if all(d in finished or d not in tasks for d in deps)]:
            running[name] = list(waiting.pop(name)[0]())
        for name in list(running):
            alive = []
            for ch in running[name]:
                try:
                    next(ch)
                    alive.append(ch)
                except StopIteration:
                    pass
            if alive:
                running[name] = alive
            else:
                del running[name]
                finished.add(name)


def _rms(x, g):
    return x * lax.rsqrt(jnp.mean(x * x, axis=-1, keepdims=True) + NORM_EPS) * g


def _cast_kernel(*refs):
    n = len(refs) // 2
    for src, dst in zip(refs[:n], refs[n:]):
        dst[...] = src[...].astype(BF16)


def _to_bf16(*weights):
    specs = [pl.BlockSpec((w.shape[0] // WEIGHT_CAST_STEPS, w.shape[1]), lambda i: (i, 0)) for w in weights]
    return pl.pallas_call(
        _cast_kernel,
        grid=(WEIGHT_CAST_STEPS,),
        in_specs=specs,
        out_specs=specs,
        out_shape=[jax.ShapeDtypeStruct(w.shape, BF16) for w in weights],
        compiler_params=_params("parallel"),
        name="cast_weights",
    )(*weights)


def _inproj_kernel(x_ref, g_ref, w_ref, q_ref, k_ref, v_ref, zr_ref):
    h = _rms(x_ref[...], g_ref[...]).astype(BF16)
    q = jnp.dot(h, w_ref[:, :ATTN_WIDTH], preferred_element_type=F32)
    q_ref[...] = (q * (HEAD_DIM ** -0.5 * LOG2_E)).astype(BF16)
    k_ref[...] = jnp.dot(h, w_ref[:, ATTN_WIDTH:ATTN_WIDTH + KV_WIDTH], preferred_element_type=F32)
    v_ref[...] = jnp.dot(h, w_ref[:, ATTN_WIDTH + KV_WIDTH:ATTN_WIDTH + 2 * KV_WIDTH],
                         preferred_element_type=F32)
    zr_ref[...] = jnp.dot(h, w_ref[:, ATTN_WIDTH + 2 * KV_WIDTH:], preferred_element_type=F32)


def _in_proj(x2d, g, w_bf16):
    n = x2d.shape[0]
    tm = min(IN_PROJ_ROW_TILE, n)
    row = lambda w: pl.BlockSpec((tm, w), lambda i: (i, 0))
    return pl.pallas_call(
        _inproj_kernel,
        grid=(n // tm,),
        in_specs=[row(D_MODEL), _const_spec((1, D_MODEL)), _const_spec((D_MODEL, IN_PROJ))],
        out_specs=[row(ATTN_WIDTH), row(KV_WIDTH), row(KV_WIDTH), row(RWKV_PROJ)],
        out_shape=[jax.ShapeDtypeStruct((n, ATTN_WIDTH), BF16)]
                  + [jax.ShapeDtypeStruct((n, w), F32) for w in (KV_WIDTH, KV_WIDTH, RWKV_PROJ)],
        compiler_params=_params("parallel"),
        name="in_proj",
    )(x2d, g, w_bf16)


def _t5_bucket(rel):
    half = REL_BUCKETS // 2
    max_exact = half // 2
    assert REL_MAX_DIST == max_exact * 2 ** 4 and half - max_exact == 2 * 4
    n = np.abs(rel)
    large = max_exact + sum((n * n >= max_exact * max_exact * 2 ** t).astype(np.int64)
                            for t in range(1, half - max_exact))
    return (np.where(rel > 0, half, 0) + np.where(n < max_exact, n, large)).astype(np.int32)


def _bias_kernel(table_ref, bucket_ref, out_ref):
    bucket = bucket_ref[...]
    hits = [bucket == b for b in range(REL_BUCKETS)]
    for h in range(ATTN_HEADS):
        acc = jnp.zeros(bucket.shape, F32)
        for b in range(REL_BUCKETS):
            acc = jnp.where(hits[b], table_ref[b, h], acc)
        out_ref[h] = acc * LOG2_E


def _rel_bias(table, n_q, n_k):
    rel = np.arange(n_k)[None, :] - WINDOW - np.arange(n_q)[:, None]
    bucket = jnp.asarray(_t5_bucket(rel))
    bias = pl.pallas_call(
        _bias_kernel,
        in_specs=[pl.BlockSpec(memory_space=pltpu.SMEM), pl.BlockSpec(memory_space=pltpu.VMEM)],
        out_specs=pl.BlockSpec(memory_space=pltpu.VMEM),
        out_shape=jax.ShapeDtypeStruct((ATTN_HEADS, n_q, n_k), F32),
        name="rel_bias",
    )(table, bucket)
    return bias.reshape(KV_HEADS, GROUP * n_q, n_k)


def _group_sinks(sink_ref, n_q):
    row_group = lax.broadcasted_iota(jnp.int32, (GROUP * n_q, 1), 0) // n_q
    sinks = []
    for kvh in range(KV_HEADS):
        sink = jnp.zeros((GROUP * n_q, 1), F32)
        for g in range(GROUP):
            sink = jnp.where(row_group == g, sink_ref[kvh * GROUP + g] * LOG2_E, sink)
        sinks.append(sink)
    return sinks


def _attn_chain(q, keys, vals, bias, sink, valid, o_ref, rows, kvh):
    n_q = q.shape[0]
    qh = _cat_rows(*[q[:, (kvh * GROUP + g) * HEAD_DIM:(kvh * GROUP + g + 1) * HEAD_DIM]
                     for g in range(GROUP)])
    s = _dg(qh, keys, _NT) + bias
    if valid is not None:
        s = jnp.where(valid, s, -jnp.inf)
    yield
    m = jnp.maximum(jnp.max(s, axis=-1, keepdims=True), sink)
    p = jnp.exp2(s - m)
    den = jnp.sum(p, axis=-1, keepdims=True) + jnp.exp2(sink - m)
    yield
    o = _dg(p.astype(BF16), vals, _NN) * (1.0 / den)
    for g in range(GROUP):
        head = kvh * GROUP + g
        o_ref[rows, head * HEAD_DIM:(head + 1) * HEAD_DIM] = o[g * n_q:(g + 1) * n_q].astype(o_ref.dtype)
    yield


def _sample_attn_kernel(seq, sink_ref, q_ref, kp_ref, kn_ref, vp_ref, vn_ref, bias_ref, o_ref):
    batch = q_ref.shape[0] // seq
    sinks = _group_sinks(sink_ref, seq)
    chains = []
    for b in range(batch):
        rows = slice(b * seq, (b + 1) * seq)
        past = slice(b * WINDOW, (b + 1) * WINDOW)
        k_all = _cat_rows(kp_ref[past, :], kn_ref[rows, :]).astype(BF16)
        v_all = _cat_rows(vp_ref[past, :], vn_ref[rows, :]).astype(BF16)
        for kvh in range(KV_HEADS):
            lanes = slice(kvh * HEAD_DIM, (kvh + 1) * HEAD_DIM)
            chains.append(_attn_chain(q_ref[rows, :], k_all[:, lanes], v_all[:, lanes], bias_ref[kvh], sinks[kvh],
                                      None, o_ref, rows, kvh))
    _run_interleaved(chains)


def _sample_attention(q, k, v, k_past, v_past, sink, bias, seq):
    vmem = pl.BlockSpec(memory_space=pltpu.VMEM)
    return pl.pallas_call(
        functools.partial(_sample_attn_kernel, seq),
        in_specs=[pl.BlockSpec(memory_space=pltpu.SMEM)] + [vmem] * 6,
        out_specs=vmem,
        out_shape=jax.ShapeDtypeStruct(q.shape, BF16),
        name="sample_attention",
    )(sink, q, k_past, k, v_past, v, bias)


def _split2(x):
    hi = x.astype(BF16)
    lo = (x - hi.astype(F32)).astype(BF16)
    return hi, lo


def _softplus(x):
    return jnp.maximum(x, 0.0) + jnp.log(1.0 + jnp.exp(-jnp.abs(x)))


def _sigmoid(x):
    return 1.0 / (1.0 + jnp.exp(-x))


def _rwkv_kernel(valid_rows, fused, steps, *refs):
    if fused:
        (xn_ref, x0_ref, gmix_ref, win_ref, shift_ref, s0_ref, mu_ref, w0_ref, w2_ref, a0_ref, a2_ref, g2_ref, kk_ref,
         ka_ref, rk_ref, lnw_ref, lnb_ref, seg_ref, tri_ref, sink_ref, bias_ref) = refs[:21]
        (out_ref, s_ref, attn_ref, ktail_ref, vtail_ref, shiftout_ref, carry_ref, y_ref, sbd_ref, zr_scr, q_scr, k_scr,
         v_scr, kp_ref, vp_ref) = refs[21:]
        g = pl.program_id(0)
        c = lax.rem(g, steps)
        cur = lax.rem(g, 2)
        zr_ref, q_ref, kc_ref, vc_ref = zr_scr.at[cur], q_scr.at[cur], k_scr.at[cur], v_scr.at[cur]
    else:
        (zr_ref, shift_ref, s0_ref, mu_ref, w0_ref, w2_ref, a0_ref, a2_ref, g2_ref, kk_ref, ka_ref, rk_ref, lnw_ref,
         lnb_ref, seg_ref, tri_ref) = refs[:16]
        out_ref, s_ref, y_ref = refs[16:]
    C = CHUNK
    R = out_ref.shape[0]
    n_sub = R // C
    pairs = range(RWKV_HEADS // 2)
    PAIR = 2 * HEAD_DIM
    W = RWKV_WIDTH

    def block_diagonal(s0, p):
        zero = jnp.zeros((HEAD_DIM, HEAD_DIM), F32)
        return _cat_rows(_cat_lanes(s0[2 * p], zero), _cat_lanes(zero, s0[2 * p + 1]))

    if fused:
        @pl.when(c == 0)
        def _():
            carry_ref[0:1, :] = shift_ref[0]
            for p in pairs:
                sbd_ref[p] = block_diagonal(s0_ref.at[0], p)
            kp_ref[...] = jnp.zeros(kp_ref.shape, F32)
            vp_ref[...] = jnp.zeros(vp_ref.shape, F32)

    def proj_chains(x_ref, slot):
        h = _rms(x_ref[...], gmix_ref[...]).astype(BF16)
        q_end, k_end, v_end = ATTN_WIDTH, ATTN_WIDTH + KV_WIDTH, ATTN_WIDTH + 2 * KV_WIDTH

        def tile(lo, hi):
            z = jnp.dot(h, win_ref[:, lo:hi], preferred_element_type=F32)
            if hi <= q_end:
                q_scr[slot, :, lo:hi] = (z * (HEAD_DIM ** -0.5 * LOG2_E)).astype(BF16)
            elif lo == q_end:
                k_scr[slot] = z[:, :KV_WIDTH]
                v_scr[slot] = z[:, KV_WIDTH:]
            else:
                zr_scr[slot, :, lo - v_end:hi - v_end] = z
            yield

        assert q_end % PROJ_TILE == 0 and v_end - q_end == PROJ_TILE
        return [tile(lo, lo + PROJ_TILE) for lo in range(0, IN_PROJ, PROJ_TILE)]

    if fused:
        @pl.when(g == 0)
        def _():
            _run_interleaved(proj_chains(x0_ref, 0))

    seg = seg_ref[...]
    seg2 = _cat_rows(seg, seg)

    def head_sum(x):
        hi, lo = _split2(x)
        tiles = [_dg(_cat_lanes(hi[:, t * PAIR:(t + 1) * PAIR], lo[:, t * PAIR:(t + 1) * PAIR]), seg2, _NN)
                 for t in range(W // PAIR)]
        return _cat_lanes(*tiles)

    first_row = lax.broadcasted_iota(jnp.int32, (C, 1), 0) == 0
    tri3 = tri_ref[...]

    lane = lax.broadcasted_iota(jnp.int32, (C, PAIR), 1)
    trow = lax.broadcasted_iota(jnp.int32, (C, PAIR), 0)
    even = lane < HEAD_DIM
    tcol = jnp.where(even, lane, lane - HEAD_DIM)
    strict = tcol < trow
    incl = tcol <= trow
    eye = jnp.where(tcol == trow, 1.0, 0.0).astype(F32)
    brow = lax.broadcasted_iota(jnp.int32, (PAIR, PAIR), 0) < HEAD_DIM
    bcol = lax.broadcasted_iota(jnp.int32, (PAIR, PAIR), 1) < HEAD_DIM
    on_diag = brow == bcol

    def bd(x):
        zero = jnp.zeros_like(x)
        return _cat_rows(jnp.where(even, x, zero), jnp.where(even, zero, x))

    def bd2(pair):
        return bd(pair[0]), bd(pair[1])

    def mm(a_pair, w_pair, dims=_NN, exact_rows=None):
        if dims == _NT:
            w_pair = (w_pair[0].T, w_pair[1].T)
        first = _dg(_cat_lanes(a_pair[0], a_pair[1]), _cat_rows(w_pair[0], w_pair[0]), _NN)
        if exact_rows is None:
            return first + _dg(a_pair[0], w_pair[1], _NN)
        return _cat_rows(first[:exact_rows] + _dg(a_pair[0][:exact_rows], w_pair[1], _NN), first[exact_rows:])

    prepped = {}
    ready = {}
    if fused:
        state = {p: sbd_ref[p] for p in pairs}
        skey = lambda j, p: p
    else:
        state = {(j, p): block_diagonal(s0_ref.at[j], p) for j in range(n_sub) for p in pairs}
        skey = lambda j, p: (j, p)

    def prep_chain(j):
        rows = slice(j * C, (j + 1) * C)
        zr = zr_ref[rows, :]
        if not fused:
            before = shift_ref[j]
        else:
            before = carry_ref[0:1, :] if j == 0 else zr_ref[j * C - 1:j * C, :]
        z_prev = jnp.where(first_row, before, pltpu.roll(zr, 1, axis=0))
        zs = zr + (z_prev - zr) * mu_ref[...]
        r = zs[:, :W]
        k = zs[:, W:2 * W]
        v = zs[:, 2 * W:3 * W]
        wd = zs[:, 3 * W:3 * W + DECAY_LORA]
        ad = zs[:, 3 * W + DECAY_LORA:3 * W + DECAY_LORA + AAA_LORA]
        gd = zs[:, 3 * W + DECAY_LORA + AAA_LORA:]
        w_log = -_softplus(-(w0_ref[...] + _dot(jnp.tanh(wd), w2_ref[...]))) - 0.5
        lw = -jnp.exp(w_log)
        a = _sigmoid(a0_ref[...] + _dot(ad, a2_ref[...]))
        gate = _dot(_sigmoid(gd), g2_ref[...])
        kk = k * kk_ref[...]
        kk = kk * lax.rsqrt(jnp.maximum(head_sum(kk * kk), 1e-24))
        k2 = k * (1.0 + (a - 1.0) * ka_ref[...])
        if valid_rows < C:
            live = lax.broadcasted_iota(jnp.int32, (C, 1), 0) < valid_rows
            lw = jnp.where(live, lw, 0.0)
            kk = jnp.where(live, kk, 0.0)
            k2 = jnp.where(live, k2, 0.0)
        bvec = kk * a
        yield
        l1 = lw.astype(BF16)
        rem = lw - l1.astype(F32)
        l2 = rem.astype(BF16)
        l3 = (rem - l2.astype(F32)).astype(BF16)
        sums = _dg(tri3, _cat_rows(l1, l2, l3), _NN)
        li = sums[:C]
        lrev = sums[C:]
        yield
        inv_p = jnp.exp(-li)
        to_end = jnp.exp(lrev)
        prepped[j] = dict(
            at=_split2(-kk * jnp.exp(li - lw)), rt=_split2(r * jnp.exp(li)), bt=_split2(bvec * inv_p),
            kt=_split2(k2 * inv_p), bh=_split2(bvec * to_end), kh=_split2(k2 * to_end), v=_split2(v),
            p_end=jnp.exp(li[C - 1:C, :]), bonus=head_sum(r * k2 * rk_ref[...]) * v, gate=gate)
        yield

    def local_chain(j, p):
        d = prepped[j]
        lanes = slice(p * PAIR, (p + 1) * PAIR)
        cut = lambda pair: (pair[0][:, lanes], pair[1][:, lanes])
        at_p, rt_p, bt_p, kt_p, bh_p, kh_p, v_p = map(cut, (d['at'], d['rt'], d['bt'], d['kt'], d['bh'], d['kh'],
                                                            d['v']))
        left = (_cat_rows(at_p[0], rt_p[0]), _cat_rows(at_p[1], rt_p[1]))
        right = (_cat_rows(bd(bt_p[0]), bd(kt_p[0])), _cat_rows(bd(bt_p[1]), bd(kt_p[1])))
        aa = mm(left, right, _NT, exact_rows=C)
        yield
        a_ab = jnp.where(strict, aa[:C, :PAIR], 0.0)
        a_ak = jnp.where(strict, aa[:C, PAIR:], 0.0)
        a_rb = jnp.where(incl, aa[C:, :PAIR], 0.0)
        a_rk = jnp.where(incl, aa[C:, PAIR:], 0.0)
        inv = eye + a_ab
        ps = _split2(a_ab)
        power = mm(ps, bd2(ps))
        span = 2
        yield
        while span < C:
            ps = _split2(power)
            pw = bd2(ps)
            ih = _split2(inv)
            if span * 2 < C:
                both = mm((_cat_rows(ih[0], ps[0]), _cat_rows(ih[1], ps[1])), pw)
                inv = inv + both[:C]
                power = both[C:]
            else:
                inv = inv + mm(ih, pw)
            span *= 2
            yield
        ready[(j, p)] = dict(inv=_split2(inv), akrk=_split2(_cat_rows(a_ak, a_rk)), rb=a_rb.astype(BF16), left=left,
                             bhkh=(_cat_rows(bh_p[0], kh_p[0]), _cat_rows(bh_p[1], kh_p[1])), v=v_p,
                             p_end=d['p_end'][:, lanes])

    def state_chain(j, p):
        d = ready.pop((j, p))
        s_prev = state[skey(j, p)]
        v_hi, v_lo = d['v']
        s_hi, s_lo = _split2(s_prev)
        lhs = (_cat_lanes(d['left'][0], d['akrk'][0]), _cat_lanes(d['left'][1], d['akrk'][1]))
        both = mm(lhs, (_cat_rows(s_hi.T, bd(v_hi)), _cat_rows(s_lo.T, bd(v_lo))), exact_rows=C)
        rhs = both[:C]
        y0 = both[C:]
        yield
        u_pair = _split2(mm(d['inv'], bd2(_split2(rhs))))
        yield
        y_ref[j * C:(j + 1) * C, p * PAIR:(p + 1) * PAIR] = y0 + _dg(d['rb'], bd(u_pair[0]), _NN)
        t_hi = _cat_rows(u_pair[0], v_hi)
        t_lo = _cat_rows(u_pair[1], v_lo)
        w_hi, w_lo = d['bhkh']
        upd = _dg(_cat_rows(t_hi, t_lo), _cat_rows(w_hi, w_hi), _TN) + _dg(t_hi, w_lo, _TN)
        state[skey(j, p)] = s_prev * d['p_end'] + jnp.where(on_diag, upd, 0.0)
        yield

    def post_chain(j):
        rows = slice(j * C, (j + 1) * C)
        d = prepped.pop(j)
        y = y_ref[rows, :]
        mean = head_sum(y) * (1.0 / HEAD_DIM)
        dev = y - mean
        yield
        var = head_sum(dev * dev) * (1.0 / HEAD_DIM)
        yn = dev * lax.rsqrt(var + GN_EPS) * lnw_ref[...] + lnb_ref[...]
        out_ref[rows, :] = ((yn + d['bonus']) * d['gate']).astype(out_ref.dtype)
        yield

    tasks = {}
    for j in range(n_sub):
        tasks[('prep', j)] = (lambda j=j: [prep_chain(j)], [('prep', j - 1), ('local', j - RWKV_LOCAL_IN_FLIGHT)])
        tasks[('local', j)] = (lambda j=j: [local_chain(j, p) for p in pairs],
                               [('prep', j), ('local', j - RWKV_LOCAL_IN_FLIGHT)])
        tasks[('state', j)] = (lambda j=j: [state_chain(j, p) for p in pairs],
                               [('local', j)] + ([('state', j - 1)] if fused else []))
        tasks[('post', j)] = (lambda j=j: [post_chain(j)], [('state', j)])
    if fused:
        n_k = WINDOW + CHUNK
        k_all = _cat_rows(kp_ref[...], kc_ref[...]).astype(BF16)
        v_all = _cat_rows(vp_ref[...], vc_ref[...]).astype(BF16)
        first_valid = jnp.where(c == 0, WINDOW, 0)
        kcol = lax.broadcasted_iota(jnp.int32, (1, n_k), 1)
        sinks = _group_sinks(sink_ref, CHUNK)

        def attn_chains(j):
            rows = slice(j * C, (j + 1) * C)
            keys = slice(j * C, j * C + n_k)
            valid = kcol + j * C >= first_valid if j * C < WINDOW else None
            return [_attn_chain(q_ref[rows, :], k_all[keys, kvh * HEAD_DIM:(kvh + 1) * HEAD_DIM],
                                v_all[keys, kvh * HEAD_DIM:(kvh + 1) * HEAD_DIM], bias_ref[kvh], sinks[kvh], valid,
                                attn_ref, rows, kvh) for kvh in range(KV_HEADS)]

        for j in range(n_sub):
            tasks[('attn', j)] = (lambda j=j: attn_chains(j), [('attn', j - 1)])
        for t, chain in enumerate(proj_chains(xn_ref, 1 - cur)):
            tasks[('proj', t)] = (lambda chain=chain: [chain], [('proj', t - 1), ('prep', min(t, n_sub - 1))])
    _run_tasks(tasks)

    def store_state(dst, s_bd, p):
        dst[2 * p] = s_bd[:HEAD_DIM, :HEAD_DIM]
        dst[2 * p + 1] = s_bd[HEAD_DIM:, HEAD_DIM:]

    if not fused:
        for j in range(n_sub):
            for p in pairs:
                store_state(s_ref.at[j], state[(j, p)], p)
        return
    carry_ref[0:1, :] = zr_ref[R - 1:R, :]
    for p in pairs:
        sbd_ref[p] = state[p]
    kp_ref[...] = kc_ref[R - WINDOW:R, :]
    vp_ref[...] = vc_ref[R - WINDOW:R, :]

    @pl.when(c == steps - 1)
    def _():
        for p in pairs:
            store_state(s_ref.at[0], state[p], p)
        ktail_ref[0] = kc_ref[R - WINDOW:R, :]
        vtail_ref[0] = vc_ref[R - WINDOW:R, :]
        shiftout_ref[0] = zr_ref[R - 1:R, :]


def _rwkv_operands(lw):
    seg = jnp.asarray(np.kron(np.eye(2), np.ones((HEAD_DIM, HEAD_DIM))), BF16)
    ones = np.ones((CHUNK, CHUNK))
    tri3 = jnp.asarray(np.concatenate([np.tile(np.tril(ones), (1, 3)), np.tile(np.triu(ones, 1), (1, 3))]), BF16)
    row = lambda name: lw[name].reshape(1, -1)
    return [row('rwkv_mu'), row('rwkv_w0'), lw['rwkv_w2'].astype(BF16), row('rwkv_a0'),
            lw['rwkv_a2'].astype(BF16), lw['rwkv_g2'].astype(BF16), row('rwkv_k_k'), row('rwkv_k_a'),
            row('rwkv_r_k'), row('rwkv_ln_w'), row('rwkv_ln_b'), seg, tri3]


def _rwkv_mixer(zr, shift_prev, state0, lw, batch, seq):
    assert seq <= CHUNK
    if seq < CHUNK:
        zr = jnp.pad(zr.reshape(batch, seq, RWKV_PROJ), ((0, 0), (0, CHUNK - seq), (0, 0))).reshape(-1, RWKV_PROJ)
    params = _rwkv_operands(lw)
    vmem = pl.BlockSpec(memory_space=pltpu.VMEM)
    out, state = pl.pallas_call(
        functools.partial(_rwkv_kernel, seq, False, 1),
        in_specs=[vmem] * (3 + len(params)),
        out_specs=[vmem, vmem],
        out_shape=[jax.ShapeDtypeStruct((batch * CHUNK, RWKV_WIDTH), BF16), jax.ShapeDtypeStruct(state0.shape, F32)],
        scratch_shapes=[pltpu.VMEM((batch * CHUNK, RWKV_WIDTH), F32)],
        compiler_params=pltpu.CompilerParams(vmem_limit_bytes=V7X_VMEM_LIMIT_BYTES),
        name="rwkv_mixer",
    )(zr, shift_prev, state0, *params)
    if seq < CHUNK:
        out = out.reshape(batch, CHUNK, RWKV_WIDTH)[:, :seq].reshape(batch * seq, RWKV_WIDTH)
    return out, state


def _prompt_mixer(x2d, shift_prev, state0, lw, bias, batch, seq):
    rows = CHUNK * RWKV_SUB_CHUNKS
    steps = seq // rows
    total = batch * steps
    params = _rwkv_operands(lw)
    seq_block = lambda shape: pl.BlockSpec((1,) + shape, lambda g: (g // steps,) + (0,) * len(shape))
    row_spec = lambda w: pl.BlockSpec((rows, w), lambda g: (g, 0))

    def resident(shape):
        nd = len(shape)
        return pl.BlockSpec(shape, lambda g: (0,) * nd, pipeline_mode=pl.Buffered(1))

    in_specs = [pl.BlockSpec((rows, D_MODEL), lambda g: (jnp.minimum(g + 1, total - 1), 0)),
                pl.BlockSpec((rows, D_MODEL), lambda g: (0, 0), pipeline_mode=pl.Buffered(1)),
                resident((1, D_MODEL)), resident((D_MODEL, IN_PROJ)),
                seq_block((1, RWKV_PROJ)), seq_block((RWKV_HEADS, HEAD_DIM, HEAD_DIM))]
    in_specs += [resident(p.shape) for p in params]
    in_specs += [pl.BlockSpec(memory_space=pltpu.SMEM), resident(bias.shape)]
    out_specs = [row_spec(RWKV_WIDTH), seq_block((RWKV_HEADS, HEAD_DIM, HEAD_DIM)), row_spec(ATTN_WIDTH),
                 seq_block((WINDOW, KV_WIDTH)), seq_block((WINDOW, KV_WIDTH)), seq_block((1, RWKV_PROJ))]
    n = batch * seq
    out_shape = [jax.ShapeDtypeStruct((n, RWKV_WIDTH), BF16), jax.ShapeDtypeStruct(state0.shape, F32),
                 jax.ShapeDtypeStruct((n, ATTN_WIDTH), BF16), jax.ShapeDtypeStruct((batch, WINDOW, KV_WIDTH), F32),
                 jax.ShapeDtypeStruct((batch, WINDOW, KV_WIDTH), F32),
                 jax.ShapeDtypeStruct((batch, 1, RWKV_PROJ), F32)]
    scratch = [pltpu.VMEM((8, RWKV_PROJ), F32), pltpu.VMEM((rows, RWKV_WIDTH), F32),
               pltpu.VMEM((RWKV_HEADS // 2, 2 * HEAD_DIM, 2 * HEAD_DIM), F32),
               pltpu.VMEM((2, rows, RWKV_PROJ), F32), pltpu.VMEM((2, rows, ATTN_WIDTH), BF16),
                pltpu.VMEM((2, rows, KV_WIDTH), F32), pltpu.VMEM((2, rows, KV_WIDTH), F32),
                pltpu.VMEM((WINDOW, KV_WIDTH), F32), pltpu.VMEM((WINDOW, KV_WIDTH), F32)]
    r_out, state, a_out, k_tail, v_tail, shift_new = pl.pallas_call(
        functools.partial(_rwkv_kernel, rows, True, steps),
        grid=(total,),
        in_specs=in_specs,
        out_specs=out_specs,
        out_shape=out_shape,
        scratch_shapes=scratch,
        compiler_params=_params("arbitrary"),
        name="prompt_mixer",
    )(x2d, x2d, lw['norm_mix_g'], lw['w_in'], shift_prev, state0, *params, lw['attn_sink'], bias)
    return a_out, r_out, state, k_tail, v_tail, shift_new


def _memkv_kernel(m_ref, g_ref, wk_ref, wv_ref, k_ref, v_ref):
    mn = _rms(m_ref[...], g_ref[...]).astype(BF16)
    k_ref[...] = jnp.dot(mn, wk_ref[...], preferred_element_type=F32)
    v_ref[...] = jnp.dot(mn, wv_ref[...], preferred_element_type=F32)


def _memory_kv(mem2d, g, w_mk, w_mv):
    n = mem2d.shape[0]
    tm = min(MEMORY_KV_ROW_TILE, n)
    row = lambda w: pl.BlockSpec((tm, w), lambda i: (i, 0))
    return pl.pallas_call(
        _memkv_kernel,
        grid=(n // tm,),
        in_specs=[row(D_MODEL), _const_spec((1, D_MODEL)), _const_spec(w_mk.shape), _const_spec(w_mv.shape)],
        out_specs=[row(MEM_WIDTH), row(MEM_WIDTH)],
        out_shape=[jax.ShapeDtypeStruct((n, MEM_WIDTH), F32)] * 2,
        compiler_params=_params("parallel"),
        name="memory_kv",
    )(mem2d, g, w_mk, w_mv)


def _tail_kernel(x_ref, a_ref, r_ref, mk_ref, mv_ref, wo_ref, gc_ref, wq_ref, wco_ref, gm_ref, wu_ref, wd_ref, gf_ref,
                 y_ref):
    n_seq = mk_ref.shape[0]
    rows_per_seq = x_ref.shape[0] // n_seq
    x1 = x_ref[...] + _dot(a_ref[...], wo_ref[:ATTN_WIDTH, :]) + _dot(r_ref[...], wo_ref[ATTN_WIDTH:, :])
    q = _dot(_rms(x1, gc_ref[...]), wq_ref[...]).astype(BF16)
    outs = {}

    def cross_chain(b, h):
        rows = slice(b * rows_per_seq, (b + 1) * rows_per_seq)
        sl = slice(h * MEM_HEAD_DIM, (h + 1) * MEM_HEAD_DIM)
        s = _dg(q[rows, sl], mk_ref[b, :, sl].astype(BF16), _NT) * (MEM_HEAD_DIM ** -0.5)
        yield
        p = jnp.exp(s - jnp.max(s, axis=-1, keepdims=True))
        den = jnp.sum(p, axis=-1, keepdims=True)
        yield
        outs[(b, h)] = _dg(p.astype(BF16), mv_ref[b, :, sl].astype(BF16), _NN) * (1.0 / den)
        yield

    _run_interleaved([cross_chain(b, h) for b in range(n_seq) for h in range(MEM_HEADS)])
    o = _cat_rows(*[_cat_lanes(*[outs[(b, h)] for h in range(MEM_HEADS)]) for b in range(n_seq)])
    x2 = x1 + _dot(o, wco_ref[...])
    up = _dot(_rms(x2, gm_ref[...]), wu_ref[...])
    act = jnp.square(jnp.maximum(up, 0.0))
    y_ref[...] = _rms(x2 + _dot(act, wd_ref[...]), gf_ref[...])


def _tail(x2d, a_out, r_out, mk, mv, lw, batch, seq):
    n = batch * seq
    tq = min(TAIL_ROW_TILE, n)
    if seq >= tq:
        assert seq % tq == 0
        seq_per_tile, tiles_per_seq = 1, seq // tq
        mem_spec = pl.BlockSpec((1, N_MEM, MEM_WIDTH), lambda i: (i // tiles_per_seq, 0, 0))
    else:
        assert tq % seq == 0
        seq_per_tile = tq // seq
        mem_spec = pl.BlockSpec((seq_per_tile, N_MEM, MEM_WIDTH), lambda i: (i, 0, 0))
    row = lambda w: pl.BlockSpec((tq, w), lambda i: (i, 0))

    def resident(shape):
        nd = len(shape)
        return pl.BlockSpec(shape, lambda *_: (0,) * nd, pipeline_mode=pl.Buffered(1))

    weights = [lw['w_out'], lw['norm_cross_g'], lw['w_cq'], lw['w_co'], lw['norm_mlp_g'], lw['w_up'], lw['w_down'],
               lw['norm_final_g']]
    return pl.pallas_call(
        _tail_kernel,
        grid=(n // tq,),
        in_specs=[row(D_MODEL), row(ATTN_WIDTH), row(RWKV_WIDTH), mem_spec, mem_spec]
                 + [resident(w.shape) for w in weights],
        out_specs=row(D_MODEL),
        out_shape=jax.ShapeDtypeStruct(x2d.shape, F32),
        compiler_params=_params("parallel"),
        name="tail",
    )(x2d, a_out, r_out, mk, mv, *weights)


def _trunk(x, mk, mv, k_past, v_past, shift_prev, state0, lw, table):
    batch, seq = x.shape[0], x.shape[1]
    x2d = x.reshape(batch * seq, D_MODEL)
    if k_past is None:
        bias = _rel_bias(table, CHUNK, WINDOW + CHUNK)
        a_out, r_out, state, k_buf, v_buf, shift_new = _prompt_mixer(x2d, shift_prev, state0, lw, bias, batch, seq)
    else:
        q, k, v, zr = _in_proj(x2d, lw['norm_mix_g'], lw['w_in'])
        bias = _rel_bias(table, seq, WINDOW + seq)
        a_out = _sample_attention(q, k, v, k_past.reshape(batch * WINDOW, KV_WIDTH),
                                  v_past.reshape(batch * WINDOW, KV_WIDTH), lw['attn_sink'], bias, seq)
        k_buf = jnp.concatenate([k_past, k.reshape(batch, seq, KV_WIDTH)], axis=1)[:, -WINDOW:]
        v_buf = jnp.concatenate([v_past, v.reshape(batch, seq, KV_WIDTH)], axis=1)[:, -WINDOW:]
        r_out, state = _rwkv_mixer(zr, shift_prev, state0, lw, batch, seq)
        shift_new = zr.reshape(batch, seq, RWKV_PROJ)[:, -1:]
    y = _tail(x2d, a_out, r_out, mk, mv, lw, batch, seq)
    kv_shape = (batch, WINDOW, KV_HEADS, HEAD_DIM)
    return y.reshape(x.shape), k_buf.reshape(kv_shape), v_buf.reshape(kv_shape), shift_new, state


def kernel(x_prompt, x_sample, mem_prompt, cache_attn_k, cache_attn_v, cache_mem_k, cache_mem_v, state_shift,
           state_wkv, norm_mix_g, w_in, attn_sink, rel_bias_table, rwkv_mu, rwkv_w0, rwkv_w2, rwkv_a0, rwkv_a2,
           rwkv_g2, rwkv_k_k, rwkv_k_a, rwkv_r_k, rwkv_ln_w, rwkv_ln_b, w_out, norm_cross_g, norm_mem_g, w_cq,
           w_mk, w_mv, w_co, norm_mlp_g, w_up, w_down, norm_final_g):
    assert norm_mix_g.shape[0] == 1, "single-layer trunk"
    bp, dec_b = x_prompt.shape[0], x_sample.shape[0]
    vec = lambda p: p[0].reshape(1, -1)
    w_in_b, w_out_b, w_cq_b, w_co_b, w_up_b, w_down_b, w_mk_b, w_mv_b = _to_bf16(
        w_in[0], w_out[0], w_cq[0], w_co[0], w_up[0], w_down[0], w_mk[0], w_mv[0])
    lw = {
        'norm_mix_g': vec(norm_mix_g), 'w_in': w_in_b, 'attn_sink': attn_sink[0],
        'rwkv_mu': rwkv_mu[0], 'rwkv_w0': rwkv_w0[0], 'rwkv_w2': rwkv_w2[0], 'rwkv_a0': rwkv_a0[0],
        'rwkv_a2': rwkv_a2[0], 'rwkv_g2': rwkv_g2[0], 'rwkv_k_k': rwkv_k_k[0], 'rwkv_k_a': rwkv_k_a[0],
        'rwkv_r_k': rwkv_r_k[0], 'rwkv_ln_w': rwkv_ln_w[0], 'rwkv_ln_b': rwkv_ln_b[0],
        'w_out': w_out_b, 'norm_cross_g': vec(norm_cross_g), 'w_cq': w_cq_b,
        'w_co': w_co_b, 'norm_mlp_g': vec(norm_mlp_g), 'w_up': w_up_b,
        'w_down': w_down_b, 'norm_final_g': norm_final_g.reshape(1, -1),
    }
    mk, mv = _memory_kv(mem_prompt.reshape(bp * N_MEM, D_MODEL), vec(norm_mem_g), w_mk_b, w_mv_b)
    mk = mk.reshape(bp, N_MEM, MEM_WIDTH)
    mv = mv.reshape(bp, N_MEM, MEM_WIDTH)
    shift0 = jnp.zeros((bp, 1, RWKV_PROJ), F32)
    wkv0 = jnp.zeros((bp, RWKV_HEADS, HEAD_DIM, HEAD_DIM), F32)
    yp, pk, pv, psh, pS = _trunk(x_prompt, mk, mv, None, None, shift0, wkv0, lw, rel_bias_table)
    ys, sk, sv, ssh, sS = _trunk(
        x_sample, cache_mem_k[0].reshape(dec_b, N_MEM, MEM_WIDTH), cache_mem_v[0].reshape(dec_b, N_MEM, MEM_WIDTH),
        cache_attn_k[0].reshape(dec_b, WINDOW, KV_WIDTH), cache_attn_v[0].reshape(dec_b, WINDOW, KV_WIDTH),
        state_shift[0], state_wkv[0], lw, rel_bias_table)
    mem_shape = (1, bp, N_MEM, MEM_HEADS, MEM_HEAD_DIM)
    return (yp, ys, pk[None], pv[None], mk.reshape(mem_shape), mv.reshape(mem_shape), psh[None], pS[None],
            sk[None], sv[None], ssh[None], sS[None])
```

```python
import functools
import math

import numpy as np
import jax
import jax.numpy as jnp
from jax import lax
from jax.experimental import pallas as pl
from jax.experimental.pallas import tpu as pltpu

F32 = jnp.float32
BF16 = jnp.bfloat16

D_MODEL = 1024
CHUNK = 64
WINDOW = 128
HEAD_DIM = 64
ATTN_WIDTH = 512
ATTN_HEADS = 8
KV_HEADS = 2
GROUP = 4
KV_WIDTH = 128
RWKV_WIDTH = 512
RWKV_HEADS = 8
DECAY_LORA = 64
AAA_LORA = 64
GATE_LORA = 128
RWKV_PROJ = 1792
IN_PROJ = 2560
N_MEM = 256
MEM_HEADS = 4
MEM_HEAD_DIM = 128
MEM_WIDTH = 512
D_FF = 4096
REL_BUCKETS = 32
REL_MAX_DIST = 128
NORM_EPS = 1e-6
GN_EPS = 64e-5
LOG2_E = math.log2(math.e)

V7X_VMEM_LIMIT_BYTES = 52 * 1024 * 1024
MEMORY_KV_ROW_TILE = 512
IN_PROJ_ROW_TILE = 512
PROJ_TILE = 256
TAIL_ROW_TILE = 512
WEIGHT_CAST_STEPS = 8
RWKV_SUB_CHUNKS = 4
RWKV_LOCAL_IN_FLIGHT = 3


def _params(*sem):
    return pltpu.CompilerParams(dimension_semantics=sem, vmem_limit_bytes=V7X_VMEM_LIMIT_BYTES)


def _const_spec(shape):
    nd = len(shape)
    return pl.BlockSpec(shape, lambda *_: (0,) * nd)


_NN = ((1,), (0,))
_NT = ((1,), (1,))
_TN = ((0,), (0,))


def _dg(a, b, dims):
    return lax.dot_general(a, b, (dims, ((), ())), preferred_element_type=F32)


def _dot(a, b):
    return _dg(a.astype(BF16), b.astype(BF16), _NN)


def _cat_rows(*xs):
    return jnp.concatenate(xs, axis=0)


def _cat_lanes(*xs):
    return jnp.concatenate(xs, axis=1)


def _run_interleaved(chains):
    active = list(chains)
    while active:
        still = []
        for ch in active:
            try:
                next(ch)
                still.append(ch)
            except StopIteration:
                pass
        active = still


def _run_tasks(tasks):
    finished = set()
    running = {}
    waiting = dict(tasks)
    while waiting or running:
        for name in [n for n, (_, deps) in waiting.items() if all(d in finished or d not in tasks for d in deps)]:
            running[name] = list(waiting.pop(name)[0]())
        for name in list(running):
            alive = []
            for ch in running[name]:
                try:
                    next(ch)
                    alive.append(ch)
                except StopIteration:
                    pass
            if alive:
                running[name] = alive
            else:
                del running[name]
                finished.add(name)


def _rms(x, g):
    return x * lax.rsqrt(jnp.mean(x * x, axis=-1, keepdims=True) + NORM_EPS) * g


def _cast_kernel(*refs):
    n = len(refs) // 2
    for src, dst in zip(refs[:n], refs[n:]):
        dst[...] = src[...].astype(BF16)


def _to_bf16(*weights):
    specs = [pl.BlockSpec((w.shape[0] // WEIGHT_CAST_STEPS, w.shape[1]), lambda i: (i, 0)) for w in weights]
    return pl.pallas_call(
        _cast_kernel,
        grid=(WEIGHT_CAST_STEPS,),
        in_specs=specs,
        out_specs=specs,
        out_shape=[jax.ShapeDtypeStruct(w.shape, BF16) for w in weights],
        compiler_params=_params("parallel"),
        name="cast_weights",
    )(*weights)


def _inproj_kernel(x_ref, g_ref, w_ref, q_ref, k_ref, v_ref, zr_ref):
    h = _rms(x_ref[...], g_ref[...]).astype(BF16)
    q = jnp.dot(h, w_ref[:, :ATTN_WIDTH], preferred_element_type=F32)
    q_ref[...] = (q * (HEAD_DIM ** -0.5 * LOG2_E)).astype(BF16)
    k_ref[...] = jnp.dot(h, w_ref[:, ATTN_WIDTH:ATTN_WIDTH + KV_WIDTH], preferred_element_type=F32)
    v_ref[...] = jnp.dot(h, w_ref[:, ATTN_WIDTH + KV_WIDTH:ATTN_WIDTH + 2 * KV_WIDTH],
                         preferred_element_type=F32)
    zr_ref[...] = jnp.dot(h, w_ref[:, ATTN_WIDTH + 2 * KV_WIDTH:], preferred_element_type=F32)


def _in_proj(x2d, g, w_bf16):
    n = x2d.shape[0]
    tm = min(IN_PROJ_ROW_TILE, n)
    row = lambda w: pl.BlockSpec((tm, w), lambda i: (i, 0))
    return pl.pallas_call(
        _inproj_kernel,
        grid=(n // tm,),
        in_specs=[row(D_MODEL), _const_spec((1, D_MODEL)), _const_spec((D_MODEL, IN_PROJ))],
        out_specs=[row(ATTN_WIDTH), row(KV_WIDTH), row(KV_WIDTH), row(RWKV_PROJ)],
        out_shape=[jax.ShapeDtypeStruct((n, ATTN_WIDTH), BF16)]
                  + [jax.ShapeDtypeStruct((n, w), F32) for w in (KV_WIDTH, KV_WIDTH, RWKV_PROJ)],
        compiler_params=_params("parallel"),
        name="in_proj",
    )(x2d, g, w_bf16)


def _t5_bucket(rel):
    half = REL_BUCKETS // 2
    max_exact = half // 2
    assert REL_MAX_DIST == max_exact * 2 ** 4 and half - max_exact == 2 * 4
    n = np.abs(rel)
    large = max_exact + sum((n * n >= max_exact * max_exact * 2 ** t).astype(np.int64)
                            for t in range(1, half - max_exact))
    return (np.where(rel > 0, half, 0) + np.where(n < max_exact, n, large)).astype(np.int32)


def _bias_kernel(table_ref, bucket_ref, out_ref):
    bucket = bucket_ref[...]
    hits = [bucket == b for b in range(REL_BUCKETS)]
    for h in range(ATTN_HEADS):
        acc = jnp.zeros(bucket.shape, F32)
        for b in range(REL_BUCKETS):
            acc = jnp.where(hits[b], table_ref[b, h], acc)
        out_ref[h] = acc * LOG2_E


def _rel_bias(table, n_q, n_k):
    rel = np.arange(n_k)[None, :] - WINDOW - np.arange(n_q)[:, None]
    bucket = jnp.asarray(_t5_bucket(rel))
    bias = pl.pallas_call(
        _bias_kernel,
        in_specs=[pl.BlockSpec(memory_space=pltpu.SMEM), pl.BlockSpec(memory_space=pltpu.VMEM)],
        out_specs=pl.BlockSpec(memory_space=pltpu.VMEM),
        out_shape=jax.ShapeDtypeStruct((ATTN_HEADS, n_q, n_k), F32),
        name="rel_bias",
    )(table, bucket)
    return bias.reshape(KV_HEADS, GROUP * n_q, n_k)


def _group_sinks(sink_ref, n_q):
    row_group = lax.broadcasted_iota(jnp.int32, (GROUP * n_q, 1), 0) // n_q
    sinks = []
    for kvh in range(KV_HEADS):
        sink = jnp.zeros((GROUP * n_q, 1), F32)
        for g in range(GROUP):
            sink = jnp.where(row_group == g, sink_ref[kvh * GROUP + g] * LOG2_E, sink)
        sinks.append(sink)
    return sinks


def _attn_chain(q, keys, vals, bias, sink, valid, o_ref, rows, kvh):
    n_q = q.shape[0]
    qh = _cat_rows(*[q[:, (kvh * GROUP + g) * HEAD_DIM:(kvh * GROUP + g + 1) * HEAD_DIM]
                     for g in range(GROUP)])
    s = _dg(qh, keys, _NT) + bias
    if valid is not None:
        s = jnp.where(valid, s, -jnp.inf)
    yield
    m = jnp.maximum(jnp.max(s, axis=-1, keepdims=True), sink)
    p = jnp.exp2(s - m)
    den = jnp.sum(p, axis=-1, keepdims=True) + jnp.exp2(sink - m)
    yield
    o = _dg(p.astype(BF16), vals, _NN) * (1.0 / den)
    for g in range(GROUP):
        head = kvh * GROUP + g
        o_ref[rows, head * HEAD_DIM:(head + 1) * HEAD_DIM] = o[g * n_q:(g + 1) * n_q].astype(o_ref.dtype)
    yield


def _sample_attn_kernel(seq, sink_ref, q_ref, kp_ref, kn_ref, vp_ref, vn_ref, bias_ref, o_ref):
    batch = q_ref.shape[0] // seq
    sinks = _group_sinks(sink_ref, seq)
    chains = []
    for b in range(batch):
        rows = slice(b * seq, (b + 1) * seq)
        past = slice(b * WINDOW, (b + 1) * WINDOW)
        k_all = _cat_rows(kp_ref[past, :], kn_ref[rows, :]).astype(BF16)
        v_all = _cat_rows(vp_ref[past, :], vn_ref[rows, :]).astype(BF16)
        for kvh in range(KV_HEADS):
            lanes = slice(kvh * HEAD_DIM, (kvh + 1) * HEAD_DIM)
            chains.append(_attn_chain(q_ref[rows, :], k_all[:, lanes], v_all[:, lanes], bias_ref[kvh], sinks[kvh],
                                      None, o_ref, rows, kvh))
    _run_interleaved(chains)


def _sample_attention(q, k, v, k_past, v_past, sink, bias, seq):
    vmem = pl.BlockSpec(memory_space=pltpu.VMEM)
    return pl.pallas_call(
        functools.partial(_sample_attn_kernel, seq),
        in_specs=[pl.BlockSpec(memory_space=pltpu.SMEM)] + [vmem] * 6,
        out_specs=vmem,
        out_shape=jax.ShapeDtypeStruct(q.shape, BF16),
        name="sample_attention",
    )(sink, q, k_past, k, v_past, v, bias)


def _split2(x):
    hi = x.astype(BF16)
    lo = (x - hi.astype(F32)).astype(BF16)
    return hi, lo


def _softplus(x):
    return jnp.maximum(x, 0.0) + jnp.log(1.0 + jnp.exp(-jnp.abs(x)))


def _sigmoid(x):
    return 1.0 / (1.0 + jnp.exp(-x))


def _rwkv_kernel(valid_rows, fused, steps, *refs):
    if fused:
        (xn_ref, x0_ref, gmix_ref, win_ref, shift_ref, s0_ref, mu_ref, w0_ref, w2_ref, a0_ref, a2_ref, g2_ref, kk_ref,
         ka_ref, rk_ref, lnw_ref, lnb_ref, seg_ref, tri_ref, sink_ref, bias_ref) = refs[:21]
        (out_ref, s_ref, attn_ref, ktail_ref, vtail_ref, shiftout_ref, carry_ref, y_ref, sbd_ref, zr_scr, q_scr, k_scr,
         v_scr, kp_ref, vp_ref) = refs[21:]
        g = pl.program_id(0)
        c = lax.rem(g, steps)
        cur = lax.rem(g, 2)
        zr_ref, q_ref, kc_ref, vc_ref = zr_scr.at[cur], q_scr.at[cur], k_scr.at[cur], v_scr.at[cur]
    else:
        (zr_ref, shift_ref, s0_ref, mu_ref, w0_ref, w2_ref, a0_ref, a2_ref, g2_ref, kk_ref, ka_ref, rk_ref, lnw_ref,
         lnb_ref, seg_ref, tri_ref) = refs[:16]
        out_ref, s_ref, y_ref = refs[16:]
    C = CHUNK
    R = out_ref.shape[0]
    n_sub = R // C
    pairs = range(RWKV_HEADS // 2)
    PAIR = 2 * HEAD_DIM
    W = RWKV_WIDTH

    def block_diagonal(s0, p):
        zero = jnp.zeros((HEAD_DIM, HEAD_DIM), F32)
        return _cat_rows(_cat_lanes(s0[2 * p], zero), _cat_lanes(zero, s0[2 * p + 1]))

    if fused:
        @pl.when(c == 0)
        def _():
            carry_ref[0:1, :] = shift_ref[0]
            for p in pairs:
                sbd_ref[p] = block_diagonal(s0_ref.at[0], p)
            kp_ref[...] = jnp.zeros(kp_ref.shape, F32)
            vp_ref[...] = jnp.zeros(vp_ref.shape, F32)

    def proj_chains(x_ref, slot):
        h = _rms(x_ref[...], gmix_ref[...]).astype(BF16)
        q_end, k_end, v_end = ATTN_WIDTH, ATTN_WIDTH + KV_WIDTH, ATTN_WIDTH + 2 * KV_WIDTH

        def tile(lo, hi):
            z = jnp.dot(h, win_ref[:, lo:hi], preferred_element_type=F32)
            if hi <= q_end:
                q_scr[slot, :, lo:hi] = (z * (HEAD_DIM ** -0.5 * LOG2_E)).astype(BF16)
            elif lo == q_end:
                k_scr[slot] = z[:, :KV_WIDTH]
                v_scr[slot] = z[:, KV_WIDTH:]
            else:
                zr_scr[slot, :, lo - v_end:hi - v_end] = z
            yield

        assert q_end % PROJ_TILE == 0 and v_end - q_end == PROJ_TILE
        return [tile(lo, lo + PROJ_TILE) for lo in range(0, IN_PROJ, PROJ_TILE)]

    if fused:
        @pl.when(g == 0)
        def _():
            _run_interleaved(proj_chains(x0_ref, 0))

    seg = seg_ref[...]
    seg2 = _cat_rows(seg, seg)

    def head_sum(x):
        hi, lo = _split2(x)
        n_tiles = W // PAIR
        stacked = _cat_rows(*[_cat_lanes(hi[:, t * PAIR:(t + 1) * PAIR], lo[:, t * PAIR:(t + 1) * PAIR])
                              for t in range(n_tiles)])
        sums = _dg(stacked, seg2, _NN)
        return _cat_lanes(*[sums[t * C:(t + 1) * C] for t in range(n_tiles)])

    first_row = lax.broadcasted_iota(jnp.int32, (C, 1), 0) == 0
    tri3 = tri_ref[...]

    lane = lax.broadcasted_iota(jnp.int32, (C, PAIR), 1)
    trow = lax.broadcasted_iota(jnp.int32, (C, PAIR), 0)
    even = lane < HEAD_DIM
    tcol = jnp.where(even, lane, lane - HEAD_DIM)
    strict = tcol < trow
    incl = tcol <= trow
    eye = jnp.where(tcol == trow, 1.0, 0.0).astype(F32)
    brow = lax.broadcasted_iota(jnp.int32, (PAIR, PAIR), 0) < HEAD_DIM
    bcol = lax.broadcasted_iota(jnp.int32, (PAIR, PAIR), 1) < HEAD_DIM
    on_diag = brow == bcol

    def bd(x):
        zero = jnp.zeros_like(x)
        return _cat_rows(jnp.where(even, x, zero), jnp.where(even, zero, x))

    def bd2(pair):
        return bd(pair[0]), bd(pair[1])

    def mm(a_pair, w_pair, dims=_NN, exact_rows=None):
        if dims == _NT:
            w_pair = (w_pair[0].T, w_pair[1].T)
        first = _dg(_cat_lanes(a_pair[0], a_pair[1]), _cat_rows(w_pair[0], w_pair[0]), _NN)
        if exact_rows is None:
            return first + _dg(a_pair[0], w_pair[1], _NN)
        return _cat_rows(first[:exact_rows] + _dg(a_pair[0][:exact_rows], w_pair[1], _NN), first[exact_rows:])

    prepped = {}
    ready = {}
    if fused:
        state = {p: sbd_ref[p] for p in pairs}
        skey = lambda j, p: p
    else:
        state = {(j, p): block_diagonal(s0_ref.at[j], p) for j in range(n_sub) for p in pairs}
        skey = lambda j, p: (j, p)

    def prep_chain(j):
        rows = slice(j * C, (j + 1) * C)
        zr = zr_ref[rows, :]
        if not fused:
            before = shift_ref[j]
        else:
            before = carry_ref[0:1, :] if j == 0 else zr_ref[j * C - 1:j * C, :]
        z_prev = jnp.where(first_row, before, pltpu.roll(zr, 1, axis=0))
        zs = zr + (z_prev - zr) * mu_ref[...]
        r = zs[:, :W]
        k = zs[:, W:2 * W]
        v = zs[:, 2 * W:3 * W]
        wd = zs[:, 3 * W:3 * W + DECAY_LORA]
        ad = zs[:, 3 * W + DECAY_LORA:3 * W + DECAY_LORA + AAA_LORA]
        gd = zs[:, 3 * W + DECAY_LORA + AAA_LORA:]
        w_log = -_softplus(-(w0_ref[...] + _dot(jnp.tanh(wd), w2_ref[...]))) - 0.5
        lw = -jnp.exp(w_log)
        a = _sigmoid(a0_ref[...] + _dot(ad, a2_ref[...]))
        gate = _dot(_sigmoid(gd), g2_ref[...])
        kk = k * kk_ref[...]
        kk = kk * lax.rsqrt(jnp.maximum(head_sum(kk * kk), 1e-24))
        k2 = k * (1.0 + (a - 1.0) * ka_ref[...])
        if valid_rows < C:
            live = lax.broadcasted_iota(jnp.int32, (C, 1), 0) < valid_rows
            lw = jnp.where(live, lw, 0.0)
            kk = jnp.where(live, kk, 0.0)
            k2 = jnp.where(live, k2, 0.0)
        bvec = kk * a
        yield
        l1 = lw.astype(BF16)
        rem = lw - l1.astype(F32)
        l2 = rem.astype(BF16)
        l3 = (rem - l2.astype(F32)).astype(BF16)
        sums = _dg(tri3, _cat_rows(l1, l2, l3), _NN)
        li = sums[:C]
        lrev = sums[C:]
        yield
        inv_p = jnp.exp(-li)
        to_end = jnp.exp(lrev)
        prepped[j] = dict(
            at=_split2(-kk * jnp.exp(li - lw)), rt=_split2(r * jnp.exp(li)), bt=_split2(bvec * inv_p),
            kt=_split2(k2 * inv_p), bh=_split2(bvec * to_end), kh=_split2(k2 * to_end), v=_split2(v),
            p_end=jnp.exp(li[C - 1:C, :]), bonus=head_sum(r * k2 * rk_ref[...]) * v, gate=gate)
        yield

    def local_chain(j, p):
        d = prepped[j]
        lanes = slice(p * PAIR, (p + 1) * PAIR)
        cut = lambda pair: (pair[0][:, lanes], pair[1][:, lanes])
        at_p, rt_p, bt_p, kt_p, bh_p, kh_p, v_p = map(cut, (d['at'], d['rt'], d['bt'], d['kt'], d['bh'], d['kh'],
                                                            d['v']))
        left = (_cat_rows(at_p[0], rt_p[0]), _cat_rows(at_p[1], rt_p[1]))
        right = (_cat_rows(bd(bt_p[0]), bd(kt_p[0])), _cat_rows(bd(bt_p[1]), bd(kt_p[1])))
        aa = mm(left, right, _NT, exact_rows=C)
        yield
        a_ab = jnp.where(strict, aa[:C, :PAIR], 0.0)
        a_ak = jnp.where(strict, aa[:C, PAIR:], 0.0)
        a_rb = jnp.where(incl, aa[C:, :PAIR], 0.0)
        a_rk = jnp.where(incl, aa[C:, PAIR:], 0.0)
        inv = eye + a_ab
        ps = _split2(a_ab)
        power = mm(ps, bd2(ps))
        span = 2
        yield
        while span < C:
            ps = _split2(power)
            pw = bd2(ps)
            ih = _split2(inv)
            if span * 2 < C:
                both = mm((_cat_rows(ih[0], ps[0]), _cat_rows(ih[1], ps[1])), pw)
                inv = inv + both[:C]
                power = both[C:]
            else:
                inv = inv + mm(ih, pw)
            span *= 2
            yield
        ready[(j, p)] = dict(inv=_split2(inv), akrk=_split2(_cat_rows(a_ak, a_rk)), rb=a_rb.astype(BF16), left=left,
                             bhkh=(_cat_rows(bh_p[0], kh_p[0]), _cat_rows(bh_p[1], kh_p[1])), v=v_p,
                             p_end=d['p_end'][:, lanes])

    def state_chain(j, p):
        d = ready.pop((j, p))
        s_prev = state[skey(j, p)]
        v_hi, v_lo = d['v']
        s_hi, s_lo = _split2(s_prev)
        lhs = (_cat_lanes(d['left'][0], d['akrk'][0]), _cat_lanes(d['left'][1], d['akrk'][1]))
        both = mm(lhs, (_cat_rows(s_hi.T, bd(v_hi)), _cat_rows(s_lo.T, bd(v_lo))), exact_rows=C)
        rhs = both[:C]
        y0 = both[C:]
        yield
        u_pair = _split2(mm(d['inv'], bd2(_split2(rhs))))
        yield
        y_ref[j * C:(j + 1) * C, p * PAIR:(p + 1) * PAIR] = y0 + _dg(d['rb'], bd(u_pair[0]), _NN)
        t_hi = _cat_rows(u_pair[0], v_hi)
        t_lo = _cat_rows(u_pair[1], v_lo)
        w_hi, w_lo = d['bhkh']
        upd = _dg(_cat_rows(t_hi, t_lo), _cat_rows(w_hi, w_hi), _TN) + _dg(t_hi, w_lo, _TN)
        state[skey(j, p)] = s_prev * d['p_end'] + jnp.where(on_diag, upd, 0.0)
        yield

    def post_chain(j):
        rows = slice(j * C, (j + 1) * C)
        d = prepped.pop(j)
        y = y_ref[rows, :]
        mean = head_sum(y) * (1.0 / HEAD_DIM)
        dev = y - mean
        yield
        var = head_sum(dev * dev) * (1.0 / HEAD_DIM)
        yn = dev * lax.rsqrt(var + GN_EPS) * lnw_ref[...] + lnb_ref[...]
        out_ref[rows, :] = ((yn + d['bonus']) * d['gate']).astype(out_ref.dtype)
        yield

    tasks = {}
    for j in range(n_sub):
        tasks[('prep', j)] = (lambda j=j: [prep_chain(j)], [('prep', j - 1), ('local', j - RWKV_LOCAL_IN_FLIGHT)])
        tasks[('local', j)] = (lambda j=j: [local_chain(j, p) for p in pairs],
                               [('prep', j), ('local', j - RWKV_LOCAL_IN_FLIGHT)])
        tasks[('state', j)] = (lambda j=j: [state_chain(j, p) for p in pairs],
                               [('local', j)] + ([('state', j - 1)] if fused else []))
        tasks[('post', j)] = (lambda j=j: [post_chain(j)], [('state', j)])
    if fused:
        n_k = WINDOW + CHUNK
        k_all = _cat_rows(kp_ref[...], kc_ref[...]).astype(BF16)
        v_all = _cat_rows(vp_ref[...], vc_ref[...]).astype(BF16)
        first_valid = jnp.where(c == 0, WINDOW, 0)
        kcol = lax.broadcasted_iota(jnp.int32, (1, n_k), 1)
        sinks = _group_sinks(sink_ref, CHUNK)

        def attn_chains(j):
            rows = slice(j * C, (j + 1) * C)
            keys = slice(j * C, j * C + n_k)
            valid = kcol + j * C >= first_valid if j * C < WINDOW else None
            return [_attn_chain(q_ref[rows, :], k_all[keys, kvh * HEAD_DIM:(kvh + 1) * HEAD_DIM],
                                v_all[keys, kvh * HEAD_DIM:(kvh + 1) * HEAD_DIM], bias_ref[kvh], sinks[kvh], valid,
                                attn_ref, rows, kvh) for kvh in range(KV_HEADS)]

        for j in range(n_sub):
            tasks[('attn', j)] = (lambda j=j: attn_chains(j), [('attn', j - 1)])
        for t, chain in enumerate(proj_chains(xn_ref, 1 - cur)):
            tasks[('proj', t)] = (lambda chain=chain: [chain], [('proj', t - 1), ('prep', min(t, n_sub - 1))])
    _run_tasks(tasks)

    def store_state(dst, s_bd, p):
        dst[2 * p] = s_bd[:HEAD_DIM, :HEAD_DIM]
        dst[2 * p + 1] = s_bd[HEAD_DIM:, HEAD_DIM:]

    if not fused:
        for j in range(n_sub):
            for p in pairs:
                store_state(s_ref.at[j], state[(j, p)], p)
        return
    carry_ref[0:1, :] = zr_ref[R - 1:R, :]
    for p in pairs:
        sbd_ref[p] = state[p]
    kp_ref[...] = kc_ref[R - WINDOW:R, :]
    vp_ref[...] = vc_ref[R - WINDOW:R, :]

    @pl.when(c == steps - 1)
    def _():
        for p in pairs:
            store_state(s_ref.at[0], state[p], p)
        ktail_ref[0] = kc_ref[R - WINDOW:R, :]
        vtail_ref[0] = vc_ref[R - WINDOW:R, :]
        shiftout_ref[0] = zr_ref[R - 1:R, :]


def _rwkv_operands(lw):
    seg = jnp.asarray(np.kron(np.eye(2), np.ones((HEAD_DIM, HEAD_DIM))), BF16)
    ones = np.ones((CHUNK, CHUNK))
    tri3 = jnp.asarray(np.concatenate([np.tile(np.tril(ones), (1, 3)), np.tile(np.triu(ones, 1), (1, 3))]), BF16)
    row = lambda name: lw[name].reshape(1, -1)
    return [row('rwkv_mu'), row('rwkv_w0'), lw['rwkv_w2'].astype(BF16), row('rwkv_a0'),
            lw['rwkv_a2'].astype(BF16), lw['rwkv_g2'].astype(BF16), row('rwkv_k_k'), row('rwkv_k_a'),
            row('rwkv_r_k'), row('rwkv_ln_w'), row('rwkv_ln_b'), seg, tri3]


def _rwkv_mixer(zr, shift_prev, state0, lw, batch, seq):
    assert seq <= CHUNK
    if seq < CHUNK:
        zr = jnp.pad(zr.reshape(batch, seq, RWKV_PROJ), ((0, 0), (0, CHUNK - seq), (0, 0))).reshape(-1, RWKV_PROJ)
    params = _rwkv_operands(lw)
    vmem = pl.BlockSpec(memory_space=pltpu.VMEM)
    out, state = pl.pallas_call(
        functools.partial(_rwkv_kernel, seq, False, 1),
        in_specs=[vmem] * (3 + len(params)),
        out_specs=[vmem, vmem],
        out_shape=[jax.ShapeDtypeStruct((batch * CHUNK, RWKV_WIDTH), BF16), jax.ShapeDtypeStruct(state0.shape, F32)],
        scratch_shapes=[pltpu.VMEM((batch * CHUNK, RWKV_WIDTH), F32)],
        compiler_params=pltpu.CompilerParams(vmem_limit_bytes=V7X_VMEM_LIMIT_BYTES),
        name="rwkv_mixer",
    )(zr, shift_prev, state0, *params)
    if seq < CHUNK:
        out = out.reshape(batch, CHUNK, RWKV_WIDTH)[:, :seq].reshape(batch * seq, RWKV_WIDTH)
    return out, state


def _prompt_mixer(x2d, shift_prev, state0, lw, bias, batch, seq):
    rows = CHUNK * RWKV_SUB_CHUNKS
    steps = seq // rows
    total = batch * steps
    params = _rwkv_operands(lw)
    seq_block = lambda shape: pl.BlockSpec((1,) + shape, lambda g: (g // steps,) + (0,) * len(shape))
    row_spec = lambda w: pl.BlockSpec((rows, w), lambda g: (g, 0))

    def resident(shape):
        nd = len(shape)
        return pl.BlockSpec(shape, lambda g: (0,) * nd, pipeline_mode=pl.Buffered(1))

    in_specs = [pl.BlockSpec((rows, D_MODEL), lambda g: (jnp.minimum(g + 1, total - 1), 0)),
                pl.BlockSpec((rows, D_MODEL), lambda g: (0, 0), pipeline_mode=pl.Buffered(1)),
                resident((1, D_MODEL)), resident((D_MODEL, IN_PROJ)),
                seq_block((1, RWKV_PROJ)), seq_block((RWKV_HEADS, HEAD_DIM, HEAD_DIM))]
    in_specs += [resident(p.shape) for p in params]
    in_specs += [pl.BlockSpec(memory_space=pltpu.SMEM), resident(bias.shape)]
    out_specs = [row_spec(RWKV_WIDTH), seq_block((RWKV_HEADS, HEAD_DIM, HEAD_DIM)), row_spec(ATTN_WIDTH),
                 seq_block((WINDOW, KV_WIDTH)), seq_block((WINDOW, KV_WIDTH)), seq_block((1, RWKV_PROJ))]
    n = batch * seq
    out_shape = [jax.ShapeDtypeStruct((n, RWKV_WIDTH), BF16), jax.ShapeDtypeStruct(state0.shape, F32),
                 jax.ShapeDtypeStruct((n, ATTN_WIDTH), BF16), jax.ShapeDtypeStruct((batch, WINDOW, KV_WIDTH), F32),
                 jax.ShapeDtypeStruct((batch, WINDOW, KV_WIDTH), F32),
                 jax.ShapeDtypeStruct((batch, 1, RWKV_PROJ), F32)]
    scratch = [pltpu.VMEM((8, RWKV_PROJ), F32), pltpu.VMEM((rows, RWKV_WIDTH), F32),
               pltpu.VMEM((RWKV_HEADS // 2, 2 * HEAD_DIM, 2 * HEAD_DIM), F32),
               pltpu.VMEM((2, rows, RWKV_PROJ), F32), pltpu.VMEM((2, rows, ATTN_WIDTH), BF16),
                pltpu.VMEM((2, rows, KV_WIDTH), F32), pltpu.VMEM((2, rows, KV_WIDTH), F32),
                pltpu.VMEM((WINDOW, KV_WIDTH), F32), pltpu.VMEM((WINDOW, KV_WIDTH), F32)]
    r_out, state, a_out, k_tail, v_tail, shift_new = pl.pallas_call(
        functools.partial(_rwkv_kernel, rows, True, steps),
        grid=(total,),
        in_specs=in_specs,
        out_specs=out_specs,
        out_shape=out_shape,
        scratch_shapes=scratch,
        compiler_params=_params("arbitrary"),
        name="prompt_mixer",
    )(x2d, x2d, lw['norm_mix_g'], lw['w_in'], shift_prev, state0, *params, lw['attn_sink'], bias)
    return a_out, r_out, state, k_tail, v_tail, shift_new


def _memkv_kernel(m_ref, g_ref, wk_ref, wv_ref, k_ref, v_ref):
    mn = _rms(m_ref[...], g_ref[...]).astype(BF16)
    k_ref[...] = jnp.dot(mn, wk_ref[...], preferred_element_type=F32)
    v_ref[...] = jnp.dot(mn, wv_ref[...], preferred_element_type=F32)


def _memory_kv(mem2d, g, w_mk, w_mv):
    n = mem2d.shape[0]
    tm = min(MEMORY_KV_ROW_TILE, n)
    row = lambda w: pl.BlockSpec((tm, w), lambda i: (i, 0))
    return pl.pallas_call(
        _memkv_kernel,
        grid=(n // tm,),
        in_specs=[row(D_MODEL), _const_spec((1, D_MODEL)), _const_spec(w_mk.shape), _const_spec(w_mv.shape)],
        out_specs=[row(MEM_WIDTH), row(MEM_WIDTH)],
        out_shape=[jax.ShapeDtypeStruct((n, MEM_WIDTH), F32)] * 2,
        compiler_params=_params("parallel"),
        name="memory_kv",
    )(mem2d, g, w_mk, w_mv)


def _tail_kernel(x_ref, a_ref, r_ref, mk_ref, mv_ref, wo_ref, gc_ref, wq_ref, wco_ref, gm_ref, wu_ref, wd_ref, gf_ref,
                 y_ref):
    n_seq = mk_ref.shape[0]
    rows_per_seq = x_ref.shape[0] // n_seq
    x1 = x_ref[...] + _dot(a_ref[...], wo_ref[:ATTN_WIDTH, :]) + _dot(r_ref[...], wo_ref[ATTN_WIDTH:, :])
    q = _dot(_rms(x1, gc_ref[...]), wq_ref[...]).astype(BF16)
    outs = {}

    def cross_chain(b, h):
        rows = slice(b * rows_per_seq, (b + 1) * rows_per_seq)
        sl = slice(h * MEM_HEAD_DIM, (h + 1) * MEM_HEAD_DIM)
        s = _dg(q[rows, sl], mk_ref[b, :, sl].astype(BF16), _NT) * (MEM_HEAD_DIM ** -0.5)
        yield
        p = jnp.exp(s - jnp.max(s, axis=-1, keepdims=True))
        den = jnp.sum(p, axis=-1, keepdims=True)
        yield
        outs[(b, h)] = _dg(p.astype(BF16), mv_ref[b, :, sl].astype(BF16), _NN) * (1.0 / den)
        yield

    _run_interleaved([cross_chain(b, h) for b in range(n_seq) for h in range(MEM_HEADS)])
    o = _cat_rows(*[_cat_lanes(*[outs[(b, h)] for h in range(MEM_HEADS)]) for b in range(n_seq)])
    x2 = x1 + _dot(o, wco_ref[...])
    up = _dot(_rms(x2, gm_ref[...]), wu_ref[...])
    act = jnp.square(jnp.maximum(up, 0.0))
    y_ref[...] = _rms(x2 + _dot(act, wd_ref[...]), gf_ref[...])


def _tail(x2d, a_out, r_out, mk, mv, lw, batch, seq):
    n = batch * seq
    tq = min(TAIL_ROW_TILE, n)
    if seq >= tq:
        assert seq % tq == 0
        seq_per_tile, tiles_per_seq = 1, seq // tq
        mem_spec = pl.BlockSpec((1, N_MEM, MEM_WIDTH), lambda i: (i // tiles_per_seq, 0, 0))
    else:
        assert tq % seq == 0
        seq_per_tile = tq // seq
        mem_spec = pl.BlockSpec((seq_per_tile, N_MEM, MEM_WIDTH), lambda i: (i, 0, 0))
    row = lambda w: pl.BlockSpec((tq, w), lambda i: (i, 0))

    def resident(shape):
        nd = len(shape)
        return pl.BlockSpec(shape, lambda *_: (0,) * nd, pipeline_mode=pl.Buffered(1))

    weights = [lw['w_out'], lw['norm_cross_g'], lw['w_cq'], lw['w_co'], lw['norm_mlp_g'], lw['w_up'], lw['w_down'],
               lw['norm_final_g']]
    return pl.pallas_call(
        _tail_kernel,
        grid=(n // tq,),
        in_specs=[row(D_MODEL), row(ATTN_WIDTH), row(RWKV_WIDTH), mem_spec, mem_spec]
                 + [resident(w.shape) for w in weights],
        out_specs=row(D_MODEL),
        out_shape=jax.ShapeDtypeStruct(x2d.shape, F32),
        compiler_params=_params("parallel"),
        name="tail",
    )(x2d, a_out, r_out, mk, mv, *weights)


def _trunk(x, mk, mv, k_past, v_past, shift_prev, state0, lw, table):
    batch, seq = x.shape[0], x.shape[1]
    x2d = x.reshape(batch * seq, D_MODEL)
    if k_past is None:
        bias = _rel_bias(table, CHUNK, WINDOW + CHUNK)
        a_out, r_out, state, k_buf, v_buf, shift_new = _prompt_mixer(x2d, shift_prev, state0, lw, bias, batch, seq)
    else:
        q, k, v, zr = _in_proj(x2d, lw['norm_mix_g'], lw['w_in'])
        bias = _rel_bias(table, seq, WINDOW + seq)
        a_out = _sample_attention(q, k, v, k_past.reshape(batch * WINDOW, KV_WIDTH),
                                  v_past.reshape(batch * WINDOW, KV_WIDTH), lw['attn_sink'], bias, seq)
        k_buf = jnp.concatenate([k_past, k.reshape(batch, seq, KV_WIDTH)], axis=1)[:, -WINDOW:]
        v_buf = jnp.concatenate([v_past, v.reshape(batch, seq, KV_WIDTH)], axis=1)[:, -WINDOW:]
        r_out, state = _rwkv_mixer(zr, shift_prev, state0, lw, batch, seq)
        shift_new = zr.reshape(batch, seq, RWKV_PROJ)[:, -1:]
    y = _tail(x2d, a_out, r_out, mk, mv, lw, batch, seq)
    kv_shape = (batch, WINDOW, KV_HEADS, HEAD_DIM)
    return y.reshape(x.shape), k_buf.reshape(kv_shape), v_buf.reshape(kv_shape), shift_new, state


def kernel(x_prompt, x_sample, mem_prompt, cache_attn_k, cache_attn_v, cache_mem_k, cache_mem_v, state_shift,
           state_wkv, norm_mix_g, w_in, attn_sink, rel_bias_table, rwkv_mu, rwkv_w0, rwkv_w2, rwkv_a0, rwkv_a2,
           rwkv_g2, rwkv_k_k, rwkv_k_a, rwkv_r_k, rwkv_ln_w, rwkv_ln_b, w_out, norm_cross_g, norm_mem_g, w_cq,
           w_mk, w_mv, w_co, norm_mlp_g, w_up, w_down, norm_final_g):
    assert norm_mix_g.shape[0] == 1, "single-layer trunk"
    bp, dec_b = x_prompt.shape[0], x_sample.shape[0]
    vec = lambda p: p[0].reshape(1, -1)
    w_in_b, w_out_b, w_cq_b, w_co_b, w_up_b, w_down_b, w_mk_b, w_mv_b = _to_bf16(
        w_in[0], w_out[0], w_cq[0], w_co[0], w_up[0], w_down[0], w_mk[0], w_mv[0])
    lw = {
        'norm_mix_g': vec(norm_mix_g), 'w_in': w_in_b, 'attn_sink': attn_sink[0],
        'rwkv_mu': rwkv_mu[0], 'rwkv_w0': rwkv_w0[0], 'rwkv_w2': rwkv_w2[0], 'rwkv_a0': rwkv_a0[0],
        'rwkv_a2': rwkv_a2[0], 'rwkv_g2': rwkv_g2[0], 'rwkv_k_k': rwkv_k_k[0], 'rwkv_k_a': rwkv_k_a[0],
        'rwkv_r_k': rwkv_r_k[0], 'rwkv_ln_w': rwkv_ln_w[0], 'rwkv_ln_b': rwkv_ln_b[0],
        'w_out': w_out_b, 'norm_cross_g': vec(norm_cross_g), 'w_cq': w_cq_b,
        'w_co': w_co_b, 'norm_mlp_g': vec(norm_mlp_g), 'w_up': w_up_b,
        'w_down': w_down_b, 'norm_final_g': norm_final_g.reshape(1, -1),
    }
    mk, mv = _memory_kv(mem_prompt.reshape(bp * N_MEM, D_MODEL), vec(norm_mem_g), w_mk_b, w_mv_b)
    mk = mk.reshape(bp, N_MEM, MEM_WIDTH)
    mv = mv.reshape(bp, N_MEM, MEM_WIDTH)
    shift0 = jnp.zeros((bp, 1, RWKV_PROJ), F32)
    wkv0 = jnp.zeros((bp, RWKV_HEADS, HEAD_DIM, HEAD_DIM), F32)
    yp, pk, pv, psh, pS = _trunk(x_prompt, mk, mv, None, None, shift0, wkv0, lw, rel_bias_table)
    ys, sk, sv, ssh, sS = _trunk(
        x_sample, cache_mem_k[0].reshape(dec_b, N_MEM, MEM_WIDTH), cache_mem_v[0].reshape(dec_b, N_MEM, MEM_WIDTH),
        cache_attn_k[0].reshape(dec_b, WINDOW, KV_WIDTH), cache_attn_v[0].reshape(dec_b, WINDOW, KV_WIDTH),
        state_shift[0], state_wkv[0], lw, rel_bias_table)
    mem_shape = (1, bp, N_MEM, MEM_HEADS, MEM_HEAD_DIM)
    return (yp, ys, pk[None], pv[None], mk.reshape(mem_shape), mv.reshape(mem_shape), psh[None], pS[None],
            sk[None], sv[None], ssh[None], sS[None])
```

```python
import functools
import math

import numpy as np
import jax
import jax.numpy as jnp
from jax import lax
from jax.experimental import pallas as pl
from jax.experimental.pallas import tpu as pltpu

F32 = jnp.float32
BF16 = jnp.bfloat16

D_MODEL = 1024
CHUNK = 64
WINDOW = 128
HEAD_DIM = 64
ATTN_WIDTH = 512
ATTN_HEADS = 8
KV_HEADS = 2
GROUP = 4
KV_WIDTH = 128
RWKV_WIDTH = 512
RWKV_HEADS = 8
DECAY_LORA = 64
AAA_LORA = 64
GATE_LORA = 128
RWKV_PROJ = 1792
IN_PROJ = 2560
N_MEM = 256
MEM_HEADS = 4
MEM_HEAD_DIM = 128
MEM_WIDTH = 512
D_FF = 4096
REL_BUCKETS = 32
REL_MAX_DIST = 128
NORM_EPS = 1e-6
GN_EPS = 64e-5
LOG2_E = math.log2(math.e)

V7X_VMEM_LIMIT_BYTES = 52 * 1024 * 1024
MEMORY_KV_ROW_TILE = 512
IN_PROJ_ROW_TILE = 512
PROJ_TILE = 256
TAIL_ROW_TILE = 512
WEIGHT_CAST_STEPS = 8
RWKV_SUB_CHUNKS = 8
RWKV_LOCAL_IN_FLIGHT = 3


def _params(*sem):
    return pltpu.CompilerParams(dimension_semantics=sem, vmem_limit_bytes=V7X_VMEM_LIMIT_BYTES)


def _const_spec(shape):
    nd = len(shape)
    return pl.BlockSpec(shape, lambda *_: (0,) * nd)


_NN = ((1,), (0,))
_NT = ((1,), (1,))
_TN = ((0,), (0,))


def _dg(a, b, dims):
    return lax.dot_general(a, b, (dims, ((), ())), preferred_element_type=F32)


def _dot(a, b):
    return _dg(a.astype(BF16), b.astype(BF16), _NN)


def _cat_rows(*xs):
    return jnp.concatenate(xs, axis=0)


def _cat_lanes(*xs):
    return jnp.concatenate(xs, axis=1)


def _run_interleaved(chains):
    active = list(chains)
    while active:
        still = []
        for ch in active:
            try:
                next(ch)
                still.append(ch)
            except StopIteration:
                pass
        active = still


def _run_tasks(tasks):
    finished = set()
    running = {}
    waiting = dict(tasks)
    while waiting or running:
        for name in [n for n, (_, deps) in waiting.items() if all(d in finished or d not in tasks for d in deps)]:
            running[name] = list(waiting.pop(name)[0]())
        for name in list(running):
            alive = []
            for ch in running[name]:
                try:
                    next(ch)
                    alive.append(ch)
                except StopIteration:
                    pass
            if alive:
                running[name] = alive
            else:
                del running[name]
                finished.add(name)


def _rms(x, g):
    return x * lax.rsqrt(jnp.mean(x * x, axis=-1, keepdims=True) + NORM_EPS) * g


def _cast_kernel(*refs):
    n = len(refs) // 2
    for src, dst in zip(refs[:n], refs[n:]):
        dst[...] = src[...].astype(BF16)


def _to_bf16(*weights):
    specs = [pl.BlockSpec((w.shape[0] // WEIGHT_CAST_STEPS, w.shape[1]), lambda i: (i, 0)) for w in weights]
    return pl.pallas_call(
        _cast_kernel,
        grid=(WEIGHT_CAST_STEPS,),
        in_specs=specs,
        out_specs=specs,
        out_shape=[jax.ShapeDtypeStruct(w.shape, BF16) for w in weights],
        compiler_params=_params("parallel"),
        name="cast_weights",
    )(*weights)


def _inproj_kernel(x_ref, g_ref, w_ref, q_ref, k_ref, v_ref, zr_ref):
    h = _rms(x_ref[...], g_ref[...]).astype(BF16)
    q = jnp.dot(h, w_ref[:, :ATTN_WIDTH], preferred_element_type=F32)
    q_ref[...] = (q * (HEAD_DIM ** -0.5 * LOG2_E)).astype(BF16)
    k_ref[...] = jnp.dot(h, w_ref[:, ATTN_WIDTH:ATTN_WIDTH + KV_WIDTH], preferred_element_type=F32)
    v_ref[...] = jnp.dot(h, w_ref[:, ATTN_WIDTH + KV_WIDTH:ATTN_WIDTH + 2 * KV_WIDTH],
                         preferred_element_type=F32)
    zr_ref[...] = jnp.dot(h, w_ref[:, ATTN_WIDTH + 2 * KV_WIDTH:], preferred_element_type=F32)


def _in_proj(x2d, g, w_bf16):
    n = x2d.shape[0]
    tm = min(IN_PROJ_ROW_TILE, n)
    row = lambda w: pl.BlockSpec((tm, w), lambda i: (i, 0))
    return pl.pallas_call(
        _inproj_kernel,
        grid=(n // tm,),
        in_specs=[row(D_MODEL), _const_spec((1, D_MODEL)), _const_spec((D_MODEL, IN_PROJ))],
        out_specs=[row(ATTN_WIDTH), row(KV_WIDTH), row(KV_WIDTH), row(RWKV_PROJ)],
        out_shape=[jax.ShapeDtypeStruct((n, ATTN_WIDTH), BF16)]
                  + [jax.ShapeDtypeStruct((n, w), F32) for w in (KV_WIDTH, KV_WIDTH, RWKV_PROJ)],
        compiler_params=_params("parallel"),
        name="in_proj",
    )(x2d, g, w_bf16)


def _t5_bucket(rel):
    half = REL_BUCKETS // 2
    max_exact = half // 2
    assert REL_MAX_DIST == max_exact * 2 ** 4 and half - max_exact == 2 * 4
    n = np.abs(rel)
    large = max_exact + sum((n * n >= max_exact * max_exact * 2 ** t).astype(np.int64)
                            for t in range(1, half - max_exact))
    return (np.where(rel > 0, half, 0) + np.where(n < max_exact, n, large)).astype(np.int32)


def _bias_kernel(table_ref, bucket_ref, out_ref):
    bucket = bucket_ref[...]
    hits = [bucket == b for b in range(REL_BUCKETS)]
    for h in range(ATTN_HEADS):
        acc = jnp.zeros(bucket.shape, F32)
        for b in range(REL_BUCKETS):
            acc = jnp.where(hits[b], table_ref[b, h], acc)
        out_ref[h] = acc * LOG2_E


def _rel_bias(table, n_q, n_k):
    rel = np.arange(n_k)[None, :] - WINDOW - np.arange(n_q)[:, None]
    bucket = jnp.asarray(_t5_bucket(rel))
    bias = pl.pallas_call(
        _bias_kernel,
        in_specs=[pl.BlockSpec(memory_space=pltpu.SMEM), pl.BlockSpec(memory_space=pltpu.VMEM)],
        out_specs=pl.BlockSpec(memory_space=pltpu.VMEM),
        out_shape=jax.ShapeDtypeStruct((ATTN_HEADS, n_q, n_k), F32),
        name="rel_bias",
    )(table, bucket)
    return bias.reshape(KV_HEADS, GROUP * n_q, n_k)


def _group_sinks(sink_ref, n_q):
    row_group = lax.broadcasted_iota(jnp.int32, (GROUP * n_q, 1), 0) // n_q
    sinks = []
    for kvh in range(KV_HEADS):
        sink = jnp.zeros((GROUP * n_q, 1), F32)
        for g in range(GROUP):
            sink = jnp.where(row_group == g, sink_ref[kvh * GROUP + g] * LOG2_E, sink)
        sinks.append(sink)
    return sinks


def _attn_chain(q, keys, vals, bias, sink, valid, o_ref, rows, kvh):
    n_q = q.shape[0]
    qh = _cat_rows(*[q[:, (kvh * GROUP + g) * HEAD_DIM:(kvh * GROUP + g + 1) * HEAD_DIM]
                     for g in range(GROUP)])
    s = _dg(qh, keys, _NT) + bias
    if valid is not None:
        s = jnp.where(valid, s, -jnp.inf)
    yield
    m = jnp.maximum(jnp.max(s, axis=-1, keepdims=True), sink)
    p = jnp.exp2(s - m)
    den = jnp.sum(p, axis=-1, keepdims=True) + jnp.exp2(sink - m)
    yield
    o = _dg(p.astype(BF16), vals, _NN) * (1.0 / den)
    for g in range(GROUP):
        head = kvh * GROUP + g
        o_ref[rows, head * HEAD_DIM:(head + 1) * HEAD_DIM] = o[g * n_q:(g + 1) * n_q].astype(o_ref.dtype)
    yield


def _sample_attn_kernel(seq, sink_ref, q_ref, kp_ref, kn_ref, vp_ref, vn_ref, bias_ref, o_ref):
    batch = q_ref.shape[0] // seq
    sinks = _group_sinks(sink_ref, seq)
    chains = []
    for b in range(batch):
        rows = slice(b * seq, (b + 1) * seq)
        past = slice(b * WINDOW, (b + 1) * WINDOW)
        k_all = _cat_rows(kp_ref[past, :], kn_ref[rows, :]).astype(BF16)
        v_all = _cat_rows(vp_ref[past, :], vn_ref[rows, :]).astype(BF16)
        for kvh in range(KV_HEADS):
            lanes = slice(kvh * HEAD_DIM, (kvh + 1) * HEAD_DIM)
            chains.append(_attn_chain(q_ref[rows, :], k_all[:, lanes], v_all[:, lanes], bias_ref[kvh], sinks[kvh],
                                      None, o_ref, rows, kvh))
    _run_interleaved(chains)


def _sample_attention(q, k, v, k_past, v_past, sink, bias, seq):
    vmem = pl.BlockSpec(memory_space=pltpu.VMEM)
    return pl.pallas_call(
        functools.partial(_sample_attn_kernel, seq),
        in_specs=[pl.BlockSpec(memory_space=pltpu.SMEM)] + [vmem] * 6,
        out_specs=vmem,
        out_shape=jax.ShapeDtypeStruct(q.shape, BF16),
        name="sample_attention",
    )(sink, q, k_past, k, v_past, v, bias)


def _split2(x):
    hi = x.astype(BF16)
    lo = (x - hi.astype(F32)).astype(BF16)
    return hi, lo


def _softplus(x):
    return jnp.maximum(x, 0.0) + jnp.log(1.0 + jnp.exp(-jnp.abs(x)))


def _sigmoid(x):
    return 1.0 / (1.0 + jnp.exp(-x))


def _rwkv_kernel(valid_rows, fused, steps, *refs):
    if fused:
        (xn_ref, x0_ref, gmix_ref, win_ref, shift_ref, s0_ref, mu_ref, w0_ref, w2_ref, a0_ref, a2_ref, g2_ref, kk_ref,
         ka_ref, rk_ref, lnw_ref, lnb_ref, seg_ref, tri_ref, sink_ref, bias_ref) = refs[:21]
        (out_ref, s_ref, attn_ref, ktail_ref, vtail_ref, shiftout_ref, carry_ref, y_ref, sbd_ref, zr_scr, q_scr, k_scr,
         v_scr, kp_ref, vp_ref) = refs[21:]
        g = pl.program_id(0)
        c = lax.rem(g, steps)
        cur = lax.rem(g, 2)
        zr_ref, q_ref, kc_ref, vc_ref = zr_scr.at[cur], q_scr.at[cur], k_scr.at[cur], v_scr.at[cur]
    else:
        (zr_ref, shift_ref, s0_ref, mu_ref, w0_ref, w2_ref, a0_ref, a2_ref, g2_ref, kk_ref, ka_ref, rk_ref, lnw_ref,
         lnb_ref, seg_ref, tri_ref) = refs[:16]
        out_ref, s_ref, y_ref = refs[16:]
    C = CHUNK
    R = out_ref.shape[0]
    n_sub = R // C
    pairs = range(RWKV_HEADS // 2)
    PAIR = 2 * HEAD_DIM
    W = RWKV_WIDTH

    def block_diagonal(s0, p):
        zero = jnp.zeros((HEAD_DIM, HEAD_DIM), F32)
        return _cat_rows(_cat_lanes(s0[2 * p], zero), _cat_lanes(zero, s0[2 * p + 1]))

    if fused:
        @pl.when(c == 0)
        def _():
            carry_ref[0:1, :] = shift_ref[0]
            for p in pairs:
                sbd_ref[p] = block_diagonal(s0_ref.at[0], p)
            kp_ref[...] = jnp.zeros(kp_ref.shape, F32)
            vp_ref[...] = jnp.zeros(vp_ref.shape, F32)

    def proj_chains(x_ref, slot):
        h = _rms(x_ref[...], gmix_ref[...]).astype(BF16)
        q_end, k_end, v_end = ATTN_WIDTH, ATTN_WIDTH + KV_WIDTH, ATTN_WIDTH + 2 * KV_WIDTH

        def tile(lo, hi):
            z = jnp.dot(h, win_ref[:, lo:hi], preferred_element_type=F32)
            if hi <= q_end:
                q_scr[slot, :, lo:hi] = (z * (HEAD_DIM ** -0.5 * LOG2_E)).astype(BF16)
            elif lo == q_end:
                k_scr[slot] = z[:, :KV_WIDTH]
                v_scr[slot] = z[:, KV_WIDTH:]
            else:
                zr_scr[slot, :, lo - v_end:hi - v_end] = z
            yield

        assert q_end % PROJ_TILE == 0 and v_end - q_end == PROJ_TILE
        return [tile(lo, lo + PROJ_TILE) for lo in range(0, IN_PROJ, PROJ_TILE)]

    if fused:
        @pl.when(g == 0)
        def _():
            _run_interleaved(proj_chains(x0_ref, 0))

    seg = seg_ref[...]
    seg2 = _cat_rows(seg, seg)

    def head_sum(x):
        hi, lo = _split2(x)
        n_tiles = W // PAIR
        stacked = _cat_rows(*[_cat_lanes(hi[:, t * PAIR:(t + 1) * PAIR], lo[:, t * PAIR:(t + 1) * PAIR])
                              for t in range(n_tiles)])
        sums = _dg(stacked, seg2, _NN)
        return _cat_lanes(*[sums[t * C:(t + 1) * C] for t in range(n_tiles)])

    first_row = lax.broadcasted_iota(jnp.int32, (C, 1), 0) == 0
    tri3 = tri_ref[...]

    lane = lax.broadcasted_iota(jnp.int32, (C, PAIR), 1)
    trow = lax.broadcasted_iota(jnp.int32, (C, PAIR), 0)
    even = lane < HEAD_DIM
    tcol = jnp.where(even, lane, lane - HEAD_DIM)
    strict = tcol < trow
    incl = tcol <= trow
    eye = jnp.where(tcol == trow, 1.0, 0.0).astype(F32)
    brow = lax.broadcasted_iota(jnp.int32, (PAIR, PAIR), 0) < HEAD_DIM
    bcol = lax.broadcasted_iota(jnp.int32, (PAIR, PAIR), 1) < HEAD_DIM
    on_diag = brow == bcol

    def bd(x):
        zero = jnp.zeros_like(x)
        return _cat_rows(jnp.where(even, x, zero), jnp.where(even, zero, x))

    def bd2(pair):
        return bd(pair[0]), bd(pair[1])

    def mm(a_pair, w_pair, dims=_NN, exact_rows=None):
        if dims == _NT:
            w_pair = (w_pair[0].T, w_pair[1].T)
        first = _dg(_cat_lanes(a_pair[0], a_pair[1]), _cat_rows(w_pair[0], w_pair[0]), _NN)
        if exact_rows is None:
            return first + _dg(a_pair[0], w_pair[1], _NN)
        return _cat_rows(first[:exact_rows] + _dg(a_pair[0][:exact_rows], w_pair[1], _NN), first[exact_rows:])

    prepped = {}
    ready = {}
    if fused:
        state = {p: sbd_ref[p] for p in pairs}
        skey = lambda j, p: p
    else:
        state = {(j, p): block_diagonal(s0_ref.at[j], p) for j in range(n_sub) for p in pairs}
        skey = lambda j, p: (j, p)

    def prep_chain(j):
        rows = slice(j * C, (j + 1) * C)
        zr = zr_ref[rows, :]
        if not fused:
            before = shift_ref[j]
        else:
            before = carry_ref[0:1, :] if j == 0 else zr_ref[j * C - 1:j * C, :]
        z_prev = jnp.where(first_row, before, pltpu.roll(zr, 1, axis=0))
        zs = zr + (z_prev - zr) * mu_ref[...]
        r = zs[:, :W]
        k = zs[:, W:2 * W]
        v = zs[:, 2 * W:3 * W]
        wd = zs[:, 3 * W:3 * W + DECAY_LORA]
        ad = zs[:, 3 * W + DECAY_LORA:3 * W + DECAY_LORA + AAA_LORA]
        gd = zs[:, 3 * W + DECAY_LORA + AAA_LORA:]
        w_log = -_softplus(-(w0_ref[...] + _dot(jnp.tanh(wd), w2_ref[...]))) - 0.5
        lw = -jnp.exp(w_log)
        a = _sigmoid(a0_ref[...] + _dot(ad, a2_ref[...]))
        gate = _dot(_sigmoid(gd), g2_ref[...])
        kk = k * kk_ref[...]
        kk = kk * lax.rsqrt(jnp.maximum(head_sum(kk * kk), 1e-24))
        k2 = k * (1.0 + (a - 1.0) * ka_ref[...])
        if valid_rows < C:
            live = lax.broadcasted_iota(jnp.int32, (C, 1), 0) < valid_rows
            lw = jnp.where(live, lw, 0.0)
            kk = jnp.where(live, kk, 0.0)
            k2 = jnp.where(live, k2, 0.0)
        bvec = kk * a
        yield
        l1 = lw.astype(BF16)
        rem = lw - l1.astype(F32)
        l2 = rem.astype(BF16)
        l3 = (rem - l2.astype(F32)).astype(BF16)
        sums = _dg(tri3, _cat_rows(l1, l2, l3), _NN)
        li = sums[:C]
        lrev = sums[C:]
        yield
        inv_p = jnp.exp(-li)
        to_end = jnp.exp(lrev)
        prepped[j] = dict(
            at=_split2(-kk * jnp.exp(li - lw)), rt=_split2(r * jnp.exp(li)), bt=_split2(bvec * inv_p),
            kt=_split2(k2 * inv_p), bh=_split2(bvec * to_end), kh=_split2(k2 * to_end), v=_split2(v),
            p_end=jnp.exp(li[C - 1:C, :]), bonus=head_sum(r * k2 * rk_ref[...]) * v, gate=gate)
        yield

    def local_chain(j, p):
        d = prepped[j]
        lanes = slice(p * PAIR, (p + 1) * PAIR)
        cut = lambda pair: (pair[0][:, lanes], pair[1][:, lanes])
        at_p, rt_p, bt_p, kt_p, bh_p, kh_p, v_p = map(cut, (d['at'], d['rt'], d['bt'], d['kt'], d['bh'], d['kh'],
                                                            d['v']))
        left = (_cat_rows(at_p[0], rt_p[0]), _cat_rows(at_p[1], rt_p[1]))
        right = (_cat_rows(bd(bt_p[0]), bd(kt_p[0])), _cat_rows(bd(bt_p[1]), bd(kt_p[1])))
        aa = mm(left, right, _NT, exact_rows=C)
        yield
        a_ab = jnp.where(strict, aa[:C, :PAIR], 0.0)
        a_ak = jnp.where(strict, aa[:C, PAIR:], 0.0)
        a_rb = jnp.where(incl, aa[C:, :PAIR], 0.0)
        a_rk = jnp.where(incl, aa[C:, PAIR:], 0.0)
        inv = eye + a_ab
        ps = _split2(a_ab)
        power = mm(ps, bd2(ps))
        span = 2
        yield
        while span < C:
            ps = _split2(power)
            pw = bd2(ps)
            ih = _split2(inv)
            if span * 2 < C:
                both = mm((_cat_rows(ih[0], ps[0]), _cat_rows(ih[1], ps[1])), pw)
                inv = inv + both[:C]
                power = both[C:]
            else:
                inv = inv + mm(ih, pw)
            span *= 2
            yield
        ready[(j, p)] = dict(inv=_split2(inv), akrk=_split2(_cat_rows(a_ak, a_rk)), rb=a_rb.astype(BF16), left=left,
                             bhkh=(_cat_rows(bh_p[0], kh_p[0]), _cat_rows(bh_p[1], kh_p[1])), v=v_p,
                             p_end=d['p_end'][:, lanes])

    def state_chain(j, p):
        d = ready.pop((j, p))
        s_prev = state[skey(j, p)]
        v_hi, v_lo = d['v']
        s_hi, s_lo = _split2(s_prev)
        lhs = (_cat_lanes(d['left'][0], d['akrk'][0]), _cat_lanes(d['left'][1], d['akrk'][1]))
        both = mm(lhs, (_cat_rows(s_hi.T, bd(v_hi)), _cat_rows(s_lo.T, bd(v_lo))), exact_rows=C)
        rhs = both[:C]
        y0 = both[C:]
        yield
        u_pair = _split2(mm(d['inv'], bd2(_split2(rhs))))
        yield
        y_ref[j * C:(j + 1) * C, p * PAIR:(p + 1) * PAIR] = y0 + _dg(d['rb'], bd(u_pair[0]), _NN)
        t_hi = _cat_rows(u_pair[0], v_hi)
        t_lo = _cat_rows(u_pair[1], v_lo)
        w_hi, w_lo = d['bhkh']
        upd = _dg(_cat_rows(t_hi, t_lo), _cat_rows(w_hi, w_hi), _TN) + _dg(t_hi, w_lo, _TN)
        state[skey(j, p)] = s_prev * d['p_end'] + jnp.where(on_diag, upd, 0.0)
        yield

    def post_chain(j):
        rows = slice(j * C, (j + 1) * C)
        d = prepped.pop(j)
        y = y_ref[rows, :]
        mean = head_sum(y) * (1.0 / HEAD_DIM)
        dev = y - mean
        yield
        var = head_sum(dev * dev) * (1.0 / HEAD_DIM)
        yn = dev * lax.rsqrt(var + GN_EPS) * lnw_ref[...] + lnb_ref[...]
        out_ref[rows, :] = ((yn + d['bonus']) * d['gate']).astype(out_ref.dtype)
        yield

    tasks = {}
    for j in range(n_sub):
        tasks[('prep', j)] = (lambda j=j: [prep_chain(j)], [('prep', j - 1), ('local', j - RWKV_LOCAL_IN_FLIGHT)])
        tasks[('local', j)] = (lambda j=j: [local_chain(j, p) for p in pairs],
                               [('prep', j), ('local', j - RWKV_LOCAL_IN_FLIGHT)])
        tasks[('state', j)] = (lambda j=j: [state_chain(j, p) for p in pairs],
                               [('local', j)] + ([('state', j - 1)] if fused else []))
        tasks[('post', j)] = (lambda j=j: [post_chain(j)], [('state', j)])
    if fused:
        n_k = WINDOW + CHUNK
        k_all = _cat_rows(kp_ref[...], kc_ref[...]).astype(BF16)
        v_all = _cat_rows(vp_ref[...], vc_ref[...]).astype(BF16)
        first_valid = jnp.where(c == 0, WINDOW, 0)
        kcol = lax.broadcasted_iota(jnp.int32, (1, n_k), 1)
        sinks = _group_sinks(sink_ref, CHUNK)

        def attn_chains(j):
            rows = slice(j * C, (j + 1) * C)
            keys = slice(j * C, j * C + n_k)
            valid = kcol + j * C >= first_valid if j * C < WINDOW else None
            return [_attn_chain(q_ref[rows, :], k_all[keys, kvh * HEAD_DIM:(kvh + 1) * HEAD_DIM],
                                v_all[keys, kvh * HEAD_DIM:(kvh + 1) * HEAD_DIM], bias_ref[kvh], sinks[kvh], valid,
                                attn_ref, rows, kvh) for kvh in range(KV_HEADS)]

        for j in range(n_sub):
            tasks[('attn', j)] = (lambda j=j: attn_chains(j), [('attn', j - 1)])
        for t, chain in enumerate(proj_chains(xn_ref, 1 - cur)):
            tasks[('proj', t)] = (lambda chain=chain: [chain], [('proj', t - 1), ('prep', min(t, n_sub - 1))])
    _run_tasks(tasks)

    def store_state(dst, s_bd, p):
        dst[2 * p] = s_bd[:HEAD_DIM, :HEAD_DIM]
        dst[2 * p + 1] = s_bd[HEAD_DIM:, HEAD_DIM:]

    if not fused:
        for j in range(n_sub):
            for p in pairs:
                store_state(s_ref.at[j], state[(j, p)], p)
        return
    carry_ref[0:1, :] = zr_ref[R - 1:R, :]
    for p in pairs:
        sbd_ref[p] = state[p]
    kp_ref[...] = kc_ref[R - WINDOW:R, :]
    vp_ref[...] = vc_ref[R - WINDOW:R, :]

    @pl.when(c == steps - 1)
    def _():
        for p in pairs:
            store_state(s_ref.at[0], state[p], p)
        ktail_ref[0] = kc_ref[R - WINDOW:R, :]
        vtail_ref[0] = vc_ref[R - WINDOW:R, :]
        shiftout_ref[0] = zr_ref[R - 1:R, :]


def _rwkv_operands(lw):
    seg = jnp.asarray(np.kron(np.eye(2), np.ones((HEAD_DIM, HEAD_DIM))), BF16)
    ones = np.ones((CHUNK, CHUNK))
    tri3 = jnp.asarray(np.concatenate([np.tile(np.tril(ones), (1, 3)), np.tile(np.triu(ones, 1), (1, 3))]), BF16)
    row = lambda name: lw[name].reshape(1, -1)
    return [row('rwkv_mu'), row('rwkv_w0'), lw['rwkv_w2'].astype(BF16), row('rwkv_a0'),
            lw['rwkv_a2'].astype(BF16), lw['rwkv_g2'].astype(BF16), row('rwkv_k_k'), row('rwkv_k_a'),
            row('rwkv_r_k'), row('rwkv_ln_w'), row('rwkv_ln_b'), seg, tri3]


def _rwkv_mixer(zr, shift_prev, state0, lw, batch, seq):
    assert seq <= CHUNK
    if seq < CHUNK:
        zr = jnp.pad(zr.reshape(batch, seq, RWKV_PROJ), ((0, 0), (0, CHUNK - seq), (0, 0))).reshape(-1, RWKV_PROJ)
    params = _rwkv_operands(lw)
    vmem = pl.BlockSpec(memory_space=pltpu.VMEM)
    out, state = pl.pallas_call(
        functools.partial(_rwkv_kernel, seq, False, 1),
        in_specs=[vmem] * (3 + len(params)),
        out_specs=[vmem, vmem],
        out_shape=[jax.ShapeDtypeStruct((batch * CHUNK, RWKV_WIDTH), BF16), jax.ShapeDtypeStruct(state0.shape, F32)],
        scratch_shapes=[pltpu.VMEM((batch * CHUNK, RWKV_WIDTH), F32)],
        compiler_params=pltpu.CompilerParams(vmem_limit_bytes=V7X_VMEM_LIMIT_BYTES),
        name="rwkv_mixer",
    )(zr, shift_prev, state0, *params)
    if seq < CHUNK:
        out = out.reshape(batch, CHUNK, RWKV_WIDTH)[:, :seq].reshape(batch * seq, RWKV_WIDTH)
    return out, state


def _prompt_mixer(x2d, shift_prev, state0, lw, bias, batch, seq):
    rows = CHUNK * RWKV_SUB_CHUNKS
    steps = seq // rows
    total = batch * steps
    params = _rwkv_operands(lw)
    seq_block = lambda shape: pl.BlockSpec((1,) + shape, lambda g: (g // steps,) + (0,) * len(shape))
    row_spec = lambda w: pl.BlockSpec((rows, w), lambda g: (g, 0))

    def resident(shape):
        nd = len(shape)
        return pl.BlockSpec(shape, lambda g: (0,) * nd, pipeline_mode=pl.Buffered(1))

    in_specs = [pl.BlockSpec((rows, D_MODEL), lambda g: (jnp.minimum(g + 1, total - 1), 0)),
                pl.BlockSpec((rows, D_MODEL), lambda g: (0, 0), pipeline_mode=pl.Buffered(1)),
                resident((1, D_MODEL)), resident((D_MODEL, IN_PROJ)),
                seq_block((1, RWKV_PROJ)), seq_block((RWKV_HEADS, HEAD_DIM, HEAD_DIM))]
    in_specs += [resident(p.shape) for p in params]
    in_specs += [pl.BlockSpec(memory_space=pltpu.SMEM), resident(bias.shape)]
    out_specs = [row_spec(RWKV_WIDTH), seq_block((RWKV_HEADS, HEAD_DIM, HEAD_DIM)), row_spec(ATTN_WIDTH),
                 seq_block((WINDOW, KV_WIDTH)), seq_block((WINDOW, KV_WIDTH)), seq_block((1, RWKV_PROJ))]
    n = batch * seq
    out_shape = [jax.ShapeDtypeStruct((n, RWKV_WIDTH), BF16), jax.ShapeDtypeStruct(state0.shape, F32),
                 jax.ShapeDtypeStruct((n, ATTN_WIDTH), BF16), jax.ShapeDtypeStruct((batch, WINDOW, KV_WIDTH), F32),
                 jax.ShapeDtypeStruct((batch, WINDOW, KV_WIDTH), F32),
                 jax.ShapeDtypeStruct((batch, 1, RWKV_PROJ), F32)]
    scratch = [pltpu.VMEM((8, RWKV_PROJ), F32), pltpu.VMEM((rows, RWKV_WIDTH), F32),
               pltpu.VMEM((RWKV_HEADS // 2, 2 * HEAD_DIM, 2 * HEAD_DIM), F32),
               pltpu.VMEM((2, rows, RWKV_PROJ), F32), pltpu.VMEM((2, rows, ATTN_WIDTH), BF16),
                pltpu.VMEM((2, rows, KV_WIDTH), F32), pltpu.VMEM((2, rows, KV_WIDTH), F32),
                pltpu.VMEM((WINDOW, KV_WIDTH), F32), pltpu.VMEM((WINDOW, KV_WIDTH), F32)]
    r_out, state, a_out, k_tail, v_tail, shift_new = pl.pallas_call(
        functools.partial(_rwkv_kernel, rows, True, steps),
        grid=(total,),
        in_specs=in_specs,
        out_specs=out_specs,
        out_shape=out_shape,
        scratch_shapes=scratch,
        compiler_params=_params("arbitrary"),
        name="prompt_mixer",
    )(x2d, x2d, lw['norm_mix_g'], lw['w_in'], shift_prev, state0, *params, lw['attn_sink'], bias)
    return a_out, r_out, state, k_tail, v_tail, shift_new


def _memkv_kernel(m_ref, g_ref, wk_ref, wv_ref, k_ref, v_ref):
    mn = _rms(m_ref[...], g_ref[...]).astype(BF16)
    k_ref[...] = jnp.dot(mn, wk_ref[...], preferred_element_type=F32)
    v_ref[...] = jnp.dot(mn, wv_ref[...], preferred_element_type=F32)


def _memory_kv(mem2d, g, w_mk, w_mv):
    n = mem2d.shape[0]
    tm = min(MEMORY_KV_ROW_TILE, n)
    row = lambda w: pl.BlockSpec((tm, w), lambda i: (i, 0))
    return pl.pallas_call(
        _memkv_kernel,
        grid=(n // tm,),
        in_specs=[row(D_MODEL), _const_spec((1, D_MODEL)), _const_spec(w_mk.shape), _const_spec(w_mv.shape)],
        out_specs=[row(MEM_WIDTH), row(MEM_WIDTH)],
        out_shape=[jax.ShapeDtypeStruct((n, MEM_WIDTH), F32)] * 2,
        compiler_params=_params("parallel"),
        name="memory_kv",
    )(mem2d, g, w_mk, w_mv)


def _tail_kernel(x_ref, a_ref, r_ref, mk_ref, mv_ref, wo_ref, gc_ref, wq_ref, wco_ref, gm_ref, wu_ref, wd_ref, gf_ref,
                 y_ref):
    n_seq = mk_ref.shape[0]
    rows_per_seq = x_ref.shape[0] // n_seq
    x1 = x_ref[...] + _dot(a_ref[...], wo_ref[:ATTN_WIDTH, :]) + _dot(r_ref[...], wo_ref[ATTN_WIDTH:, :])
    q = _dot(_rms(x1, gc_ref[...]), wq_ref[...]).astype(BF16)
    outs = {}

    def cross_chain(b, h):
        rows = slice(b * rows_per_seq, (b + 1) * rows_per_seq)
        sl = slice(h * MEM_HEAD_DIM, (h + 1) * MEM_HEAD_DIM)
        s = _dg(q[rows, sl], mk_ref[b, :, sl].astype(BF16), _NT) * (MEM_HEAD_DIM ** -0.5)
        yield
        p = jnp.exp(s - jnp.max(s, axis=-1, keepdims=True))
        den = jnp.sum(p, axis=-1, keepdims=True)
        yield
        outs[(b, h)] = _dg(p.astype(BF16), mv_ref[b, :, sl].astype(BF16), _NN) * (1.0 / den)
        yield

    _run_interleaved([cross_chain(b, h) for b in range(n_seq) for h in range(MEM_HEADS)])
    o = _cat_rows(*[_cat_lanes(*[outs[(b, h)] for h in range(MEM_HEADS)]) for b in range(n_seq)])
    x2 = x1 + _dot(o, wco_ref[...])
    up = _dot(_rms(x2, gm_ref[...]), wu_ref[...])
    act = jnp.square(jnp.maximum(up, 0.0))
    y_ref[...] = _rms(x2 + _dot(act, wd_ref[...]), gf_ref[...])


def _tail(x2d, a_out, r_out, mk, mv, lw, batch, seq):
    n = batch * seq
    tq = min(TAIL_ROW_TILE, n)
    if seq >= tq:
        assert seq % tq == 0
        seq_per_tile, tiles_per_seq = 1, seq // tq
        mem_spec = pl.BlockSpec((1, N_MEM, MEM_WIDTH), lambda i: (i // tiles_per_seq, 0, 0))
    else:
        assert tq % seq == 0
        seq_per_tile = tq // seq
        mem_spec = pl.BlockSpec((seq_per_tile, N_MEM, MEM_WIDTH), lambda i: (i, 0, 0))
    row = lambda w: pl.BlockSpec((tq, w), lambda i: (i, 0))

    def resident(shape):
        nd = len(shape)
        return pl.BlockSpec(shape, lambda *_: (0,) * nd, pipeline_mode=pl.Buffered(1))

    weights = [lw['w_out'], lw['norm_cross_g'], lw['w_cq'], lw['w_co'], lw['norm_mlp_g'], lw['w_up'], lw['w_down'],
               lw['norm_final_g']]
    return pl.pallas_call(
        _tail_kernel,
        grid=(n // tq,),
        in_specs=[row(D_MODEL), row(ATTN_WIDTH), row(RWKV_WIDTH), mem_spec, mem_spec]
                 + [resident(w.shape) for w in weights],
        out_specs=row(D_MODEL),
        out_shape=jax.ShapeDtypeStruct(x2d.shape, F32),
        compiler_params=_params("parallel"),
        name="tail",
    )(x2d, a_out, r_out, mk, mv, *weights)


def _trunk(x, mk, mv, k_past, v_past, shift_prev, state0, lw, table):
    batch, seq = x.shape[0], x.shape[1]
    x2d = x.reshape(batch * seq, D_MODEL)
    if k_past is None:
        bias = _rel_bias(table, CHUNK, WINDOW + CHUNK)
        a_out, r_out, state, k_buf, v_buf, shift_new = _prompt_mixer(x2d, shift_prev, state0, lw, bias, batch, seq)
    else:
        q, k, v, zr = _in_proj(x2d, lw['norm_mix_g'], lw['w_in'])
        bias = _rel_bias(table, seq, WINDOW + seq)
        a_out = _sample_attention(q, k, v, k_past.reshape(batch * WINDOW, KV_WIDTH),
                                  v_past.reshape(batch * WINDOW, KV_WIDTH), lw['attn_sink'], bias, seq)
        k_buf = jnp.concatenate([k_past, k.reshape(batch, seq, KV_WIDTH)], axis=1)[:, -WINDOW:]
        v_buf = jnp.concatenate([v_past, v.reshape(batch, seq, KV_WIDTH)], axis=1)[:, -WINDOW:]
        r_out, state = _rwkv_mixer(zr, shift_prev, state0, lw, batch, seq)
        shift_new = zr.reshape(batch, seq, RWKV_PROJ)[:, -1:]
    y = _tail(x2d, a_out, r_out, mk, mv, lw, batch, seq)
    kv_shape = (batch, WINDOW, KV_HEADS, HEAD_DIM)
    return y.reshape(x.shape), k_buf.reshape(kv_shape), v_buf.reshape(kv_shape), shift_new, state


def kernel(x_prompt, x_sample, mem_prompt, cache_attn_k, cache_attn_v, cache_mem_k, cache_mem_v, state_shift,
           state_wkv, norm_mix_g, w_in, attn_sink, rel_bias_table, rwkv_mu, rwkv_w0, rwkv_w2, rwkv_a0, rwkv_a2,
           rwkv_g2, rwkv_k_k, rwkv_k_a, rwkv_r_k, rwkv_ln_w, rwkv_ln_b, w_out, norm_cross_g, norm_mem_g, w_cq,
           w_mk, w_mv, w_co, norm_mlp_g, w_up, w_down, norm_final_g):
    assert norm_mix_g.shape[0] == 1, "single-layer trunk"
    bp, dec_b = x_prompt.shape[0], x_sample.shape[0]
    vec = lambda p: p[0].reshape(1, -1)
    w_in_b, w_out_b, w_cq_b, w_co_b, w_up_b, w_down_b, w_mk_b, w_mv_b = _to_bf16(
        w_in[0], w_out[0], w_cq[0], w_co[0], w_up[0], w_down[0], w_mk[0], w_mv[0])
    lw = {
        'norm_mix_g': vec(norm_mix_g), 'w_in': w_in_b, 'attn_sink': attn_sink[0],
        'rwkv_mu': rwkv_mu[0], 'rwkv_w0': rwkv_w0[0], 'rwkv_w2': rwkv_w2[0], 'rwkv_a0': rwkv_a0[0],
        'rwkv_a2': rwkv_a2[0], 'rwkv_g2': rwkv_g2[0], 'rwkv_k_k': rwkv_k_k[0], 'rwkv_k_a': rwkv_k_a[0],
        'rwkv_r_k': rwkv_r_k[0], 'rwkv_ln_w': rwkv_ln_w[0], 'rwkv_ln_b': rwkv_ln_b[0],
        'w_out': w_out_b, 'norm_cross_g': vec(norm_cross_g), 'w_cq': w_cq_b,
        'w_co': w_co_b, 'norm_mlp_g': vec(norm_mlp_g), 'w_up': w_up_b,
        'w_down': w_down_b, 'norm_final_g': norm_final_g.reshape(1, -1),
    }
    mk, mv = _memory_kv(mem_prompt.reshape(bp * N_MEM, D_MODEL), vec(norm_mem_g), w_mk_b, w_mv_b)
    mk = mk.reshape(bp, N_MEM, MEM_WIDTH)
    mv = mv.reshape(bp, N_MEM, MEM_WIDTH)
    shift0 = jnp.zeros((bp, 1, RWKV_PROJ), F32)
    wkv0 = jnp.zeros((bp, RWKV_HEADS, HEAD_DIM, HEAD_DIM), F32)
    yp, pk, pv, psh, pS = _trunk(x_prompt, mk, mv, None, None, shift0, wkv0, lw, rel_bias_table)
    ys, sk, sv, ssh, sS = _trunk(
        x_sample, cache_mem_k[0].reshape(dec_b, N_MEM, MEM_WIDTH), cache_mem_v[0].reshape(dec_b, N_MEM, MEM_WIDTH),
        cache_attn_k[0].reshape(dec_b, WINDOW, KV_WIDTH), cache_attn_v[0].reshape(dec_b, WINDOW, KV_WIDTH),
        state_shift[0], state_wkv[0], lw, rel_bias_table)
    mem_shape = (1, bp, N_MEM, MEM_HEADS, MEM_HEAD_DIM)
    return (yp, ys, pk[None], pv[None], mk.reshape(mem_shape), mv.reshape(mem_shape), psh[None], pS[None],
            sk[None], sv[None], ssh[None], sS[None])
```

```python
import functools
import math

import numpy as np
import jax
import jax.numpy as jnp
from jax import lax
from jax.experimental import pallas as pl
from jax.experimental.pallas import tpu as pltpu

F32 = jnp.float32
BF16 = jnp.bfloat16

D_MODEL = 1024
CHUNK = 64
WINDOW = 128
HEAD_DIM = 64
ATTN_WIDTH = 512
ATTN_HEADS = 8
KV_HEADS = 2
GROUP = 4
KV_WIDTH = 128
RWKV_WIDTH = 512
RWKV_HEADS = 8
DECAY_LORA = 64
AAA_LORA = 64
GATE_LORA = 128
RWKV_PROJ = 1792
IN_PROJ = 2560
N_MEM = 256
MEM_HEADS = 4
MEM_HEAD_DIM = 128
MEM_WIDTH = 512
D_FF = 4096
REL_BUCKETS = 32
REL_MAX_DIST = 128
NORM_EPS = 1e-6
GN_EPS = 64e-5
LOG2_E = math.log2(math.e)

V7X_VMEM_LIMIT_BYTES = 52 * 1024 * 1024
MEMORY_KV_ROW_TILE = 512
IN_PROJ_ROW_TILE = 512
PROJ_TILE = 256
TAIL_ROW_TILE = 512
WEIGHT_CAST_STEPS = 8
RWKV_SUB_CHUNKS = 8
RWKV_LOCAL_IN_FLIGHT = 3


def _params(*sem):
    return pltpu.CompilerParams(dimension_semantics=sem, vmem_limit_bytes=V7X_VMEM_LIMIT_BYTES)


def _const_spec(shape):
    nd = len(shape)
    return pl.BlockSpec(shape, lambda *_: (0,) * nd)


_NN = ((1,), (0,))
_NT = ((1,), (1,))
_TN = ((0,), (0,))


def _dg(a, b, dims):
    return lax.dot_general(a, b, (dims, ((), ())), preferred_element_type=F32)


def _dot(a, b):
    return _dg(a.astype(BF16), b.astype(BF16), _NN)


def _cat_rows(*xs):
    return jnp.concatenate(xs, axis=0)


def _cat_lanes(*xs):
    return jnp.concatenate(xs, axis=1)


def _run_interleaved(chains):
    active = list(chains)
    while active:
        still = []
        for ch in active:
            try:
                next(ch)
                still.append(ch)
            except StopIteration:
                pass
        active = still


def _run_tasks(tasks):
    finished = set()
    running = {}
    waiting = dict(tasks)
    while waiting or running:
        for name in [n for n, (_, deps) in waiting.items() if all(d in finished or d not in tasks for d in deps)]:
            running[name] = list(waiting.pop(name)[0]())
        for name in list(running):
            alive = []
            for ch in running[name]:
                try:
                    next(ch)
                    alive.append(ch)
                except StopIteration:
                    pass
            if alive:
                running[name] = alive
            else:
                del running[name]
                finished.add(name)


def _rms(x, g):
    return x * lax.rsqrt(jnp.mean(x * x, axis=-1, keepdims=True) + NORM_EPS) * g


def _cast_kernel(*refs):
    n = len(refs) // 2
    for src, dst in zip(refs[:n], refs[n:]):
        dst[...] = src[...].astype(BF16)


def _to_bf16(*weights):
    specs = [pl.BlockSpec((w.shape[0] // WEIGHT_CAST_STEPS, w.shape[1]), lambda i: (i, 0)) for w in weights]
    return pl.pallas_call(
        _cast_kernel,
        grid=(WEIGHT_CAST_STEPS,),
        in_specs=specs,
        out_specs=specs,
        out_shape=[jax.ShapeDtypeStruct(w.shape, BF16) for w in weights],
        compiler_params=_params("parallel"),
        name="cast_weights",
    )(*weights)


def _inproj_kernel(x_ref, g_ref, w_ref, q_ref, k_ref, v_ref, zr_ref):
    h = _rms(x_ref[...], g_ref[...]).astype(BF16)
    q = jnp.dot(h, w_ref[:, :ATTN_WIDTH], preferred_element_type=F32)
    q_ref[...] = (q * (HEAD_DIM ** -0.5 * LOG2_E)).astype(BF16)
    k_ref[...] = jnp.dot(h, w_ref[:, ATTN_WIDTH:ATTN_WIDTH + KV_WIDTH], preferred_element_type=F32)
    v_ref[...] = jnp.dot(h, w_ref[:, ATTN_WIDTH + KV_WIDTH:ATTN_WIDTH + 2 * KV_WIDTH],
                         preferred_element_type=F32)
    zr_ref[...] = jnp.dot(h, w_ref[:, ATTN_WIDTH + 2 * KV_WIDTH:], preferred_element_type=F32)


def _in_proj(x2d, g, w_bf16):
    n = x2d.shape[0]
    tm = min(IN_PROJ_ROW_TILE, n)
    row = lambda w: pl.BlockSpec((tm, w), lambda i: (i, 0))
    return pl.pallas_call(
        _inproj_kernel,
        grid=(n // tm,),
        in_specs=[row(D_MODEL), _const_spec((1, D_MODEL)), _const_spec((D_MODEL, IN_PROJ))],
        out_specs=[row(ATTN_WIDTH), row(KV_WIDTH), row(KV_WIDTH), row(RWKV_PROJ)],
        out_shape=[jax.ShapeDtypeStruct((n, ATTN_WIDTH), BF16)]
                  + [jax.ShapeDtypeStruct((n, w), F32) for w in (KV_WIDTH, KV_WIDTH, RWKV_PROJ)],
        compiler_params=_params("parallel"),
        name="in_proj",
    )(x2d, g, w_bf16)


def _t5_bucket(rel):
    half = REL_BUCKETS // 2
    max_exact = half // 2
    assert REL_MAX_DIST == max_exact * 2 ** 4 and half - max_exact == 2 * 4
    n = np.abs(rel)
    large = max_exact + sum((n * n >= max_exact * max_exact * 2 ** t).astype(np.int64)
                            for t in range(1, half - max_exact))
    return (np.where(rel > 0, half, 0) + np.where(n < max_exact, n, large)).astype(np.int32)


def _bias_kernel(table_ref, bucket_ref, out_ref):
    bucket = bucket_ref[...]
    hits = [bucket == b for b in range(REL_BUCKETS)]
    for h in range(ATTN_HEADS):
        acc = jnp.zeros(bucket.shape, F32)
        for b in range(REL_BUCKETS):
            acc = jnp.where(hits[b], table_ref[b, h], acc)
        out_ref[h] = acc * LOG2_E


def _rel_bias(table, n_q, n_k):
    rel = np.arange(n_k)[None, :] - WINDOW - np.arange(n_q)[:, None]
    bucket = jnp.asarray(_t5_bucket(rel))
    bias = pl.pallas_call(
        _bias_kernel,
        in_specs=[pl.BlockSpec(memory_space=pltpu.SMEM), pl.BlockSpec(memory_space=pltpu.VMEM)],
        out_specs=pl.BlockSpec(memory_space=pltpu.VMEM),
        out_shape=jax.ShapeDtypeStruct((ATTN_HEADS, n_q, n_k), F32),
        name="rel_bias",
    )(table, bucket)
    return bias.reshape(KV_HEADS, GROUP * n_q, n_k)


def _group_sinks(sink_ref, n_q):
    row_group = lax.broadcasted_iota(jnp.int32, (GROUP * n_q, 1), 0) // n_q
    sinks = []
    for kvh in range(KV_HEADS):
        sink = jnp.zeros((GROUP * n_q, 1), F32)
        for g in range(GROUP):
            sink = jnp.where(row_group == g, sink_ref[kvh * GROUP + g] * LOG2_E, sink)
        sinks.append(sink)
    return sinks


def _attn_chain(q, keys, vals, bias, sink, valid, o_ref, rows, kvh):
    n_q = q.shape[0]
    qh = _cat_rows(*[q[:, (kvh * GROUP + g) * HEAD_DIM:(kvh * GROUP + g + 1) * HEAD_DIM]
                     for g in range(GROUP)])
    s = _dg(qh, keys, _NT) + bias
    if valid is not None:
        s = jnp.where(valid, s, -jnp.inf)
    yield
    m = jnp.maximum(jnp.max(s, axis=-1, keepdims=True), sink)
    p = jnp.exp2(s - m)
    den = jnp.sum(p, axis=-1, keepdims=True) + jnp.exp2(sink - m)
    yield
    o = _dg(p.astype(BF16), vals, _NN) * (1.0 / den)
    for g in range(GROUP):
        head = kvh * GROUP + g
        o_ref[rows, head * HEAD_DIM:(head + 1) * HEAD_DIM] = o[g * n_q:(g + 1) * n_q].astype(o_ref.dtype)
    yield


def _sample_attn_kernel(seq, sink_ref, q_ref, kp_ref, kn_ref, vp_ref, vn_ref, bias_ref, o_ref):
    batch = q_ref.shape[0] // seq
    sinks = _group_sinks(sink_ref, seq)
    chains = []
    for b in range(batch):
        rows = slice(b * seq, (b + 1) * seq)
        past = slice(b * WINDOW, (b + 1) * WINDOW)
        k_all = _cat_rows(kp_ref[past, :], kn_ref[rows, :]).astype(BF16)
        v_all = _cat_rows(vp_ref[past, :], vn_ref[rows, :]).astype(BF16)
        for kvh in range(KV_HEADS):
            lanes = slice(kvh * HEAD_DIM, (kvh + 1) * HEAD_DIM)
            chains.append(_attn_chain(q_ref[rows, :], k_all[:, lanes], v_all[:, lanes], bias_ref[kvh], sinks[kvh],
                                      None, o_ref, rows, kvh))
    _run_interleaved(chains)


def _sample_attention(q, k, v, k_past, v_past, sink, bias, seq):
    vmem = pl.BlockSpec(memory_space=pltpu.VMEM)
    return pl.pallas_call(
        functools.partial(_sample_attn_kernel, seq),
        in_specs=[pl.BlockSpec(memory_space=pltpu.SMEM)] + [vmem] * 6,
        out_specs=vmem,
        out_shape=jax.ShapeDtypeStruct(q.shape, BF16),
        name="sample_attention",
    )(sink, q, k_past, k, v_past, v, bias)


def _split2(x):
    hi = x.astype(BF16)
    lo = (x - hi.astype(F32)).astype(BF16)
    return hi, lo


def _softplus(x):
    return jnp.maximum(x, 0.0) + jnp.log(1.0 + jnp.exp(-jnp.abs(x)))


def _sigmoid(x):
    return 1.0 / (1.0 + jnp.exp(-x))


def _rwkv_kernel(valid_rows, fused, steps, *refs):
    if fused:
        (xn_ref, x0_ref, gmix_ref, win_ref, shift_ref, s0_ref, mu_ref, w0_ref, w2_ref, a0_ref, a2_ref, g2_ref, kk_ref,
         ka_ref, rk_ref, lnw_ref, lnb_ref, seg_ref, tri_ref, sink_ref, bias_ref) = refs[:21]
        (out_ref, s_ref, attn_ref, ktail_ref, vtail_ref, shiftout_ref, carry_ref, y_ref, sbd_ref, zr_scr, q_scr, k_scr,
         v_scr, kp_ref, vp_ref) = refs[21:]
        g = pl.program_id(0)
        c = lax.rem(g, steps)
        cur = lax.rem(g, 2)
        zr_ref, q_ref, kc_ref, vc_ref = zr_scr.at[cur], q_scr.at[cur], k_scr.at[cur], v_scr.at[cur]
    else:
        (zr_ref, shift_ref, s0_ref, mu_ref, w0_ref, w2_ref, a0_ref, a2_ref, g2_ref, kk_ref, ka_ref, rk_ref, lnw_ref,
         lnb_ref, seg_ref, tri_ref) = refs[:16]
        out_ref, s_ref, y_ref = refs[16:]
    C = CHUNK
    R = out_ref.shape[0]
    n_sub = R // C
    pairs = range(RWKV_HEADS // 2)
    PAIR = 2 * HEAD_DIM
    W = RWKV_WIDTH

    def block_diagonal(s0, p):
        zero = jnp.zeros((HEAD_DIM, HEAD_DIM), F32)
        return _cat_rows(_cat_lanes(s0[2 * p], zero), _cat_lanes(zero, s0[2 * p + 1]))

    if fused:
        @pl.when(c == 0)
        def _():
            carry_ref[0:1, :] = shift_ref[0]
            for p in pairs:
                sbd_ref[p] = block_diagonal(s0_ref.at[0], p)
            kp_ref[...] = jnp.zeros(kp_ref.shape, F32)
            vp_ref[...] = jnp.zeros(vp_ref.shape, F32)

    def proj_chains(x_ref, slot):
        h = _rms(x_ref[...], gmix_ref[...]).astype(BF16)
        q_end, k_end, v_end = ATTN_WIDTH, ATTN_WIDTH + KV_WIDTH, ATTN_WIDTH + 2 * KV_WIDTH

        def tile(lo, hi):
            z = jnp.dot(h, win_ref[:, lo:hi], preferred_element_type=F32)
            if hi <= q_end:
                q_scr[slot, :, lo:hi] = (z * (HEAD_DIM ** -0.5 * LOG2_E)).astype(BF16)
            elif lo == q_end:
                k_scr[slot] = z[:, :KV_WIDTH]
                v_scr[slot] = z[:, KV_WIDTH:]
            else:
                zr_scr[slot, :, lo - v_end:hi - v_end] = z
            yield

        assert q_end % PROJ_TILE == 0 and v_end - q_end == PROJ_TILE
        return [tile(lo, lo + PROJ_TILE) for lo in range(0, IN_PROJ, PROJ_TILE)]

    if fused:
        @pl.when(g == 0)
        def _():
            _run_interleaved(proj_chains(x0_ref, 0))

    seg = seg_ref[...]
    seg2 = _cat_rows(seg, seg)

    def head_sum(x):
        hi, lo = _split2(x)
        n_tiles = W // PAIR
        stacked = _cat_rows(*[_cat_lanes(hi[:, t * PAIR:(t + 1) * PAIR], lo[:, t * PAIR:(t + 1) * PAIR])
                              for t in range(n_tiles)])
        sums = _dg(stacked, seg2, _NN)
        return _cat_lanes(*[sums[t * C:(t + 1) * C] for t in range(n_tiles)])

    first_row = lax.broadcasted_iota(jnp.int32, (C, 1), 0) == 0
    tri3 = tri_ref[...]

    lane = lax.broadcasted_iota(jnp.int32, (C, PAIR), 1)
    trow = lax.broadcasted_iota(jnp.int32, (C, PAIR), 0)
    even = lane < HEAD_DIM
    tcol = jnp.where(even, lane, lane - HEAD_DIM)
    strict = tcol < trow
    incl = tcol <= trow
    eye = jnp.where(tcol == trow, 1.0, 0.0).astype(F32)
    brow = lax.broadcasted_iota(jnp.int32, (PAIR, PAIR), 0) < HEAD_DIM
    bcol = lax.broadcasted_iota(jnp.int32, (PAIR, PAIR), 1) < HEAD_DIM
    on_diag = brow == bcol

    def bd(x):
        zero = jnp.zeros_like(x)
        return _cat_rows(jnp.where(even, x, zero), jnp.where(even, zero, x))

    def bd2(pair):
        return bd(pair[0]), bd(pair[1])

    def mm(a_pair, w_pair, dims=_NN, exact_rows=None):
        if dims == _NT:
            w_pair = (w_pair[0].T, w_pair[1].T)
        first = _dg(_cat_lanes(a_pair[0], a_pair[1]), _cat_rows(w_pair[0], w_pair[0]), _NN)
        if exact_rows is None:
            return first + _dg(a_pair[0], w_pair[1], _NN)
        return _cat_rows(first[:exact_rows] + _dg(a_pair[0][:exact_rows], w_pair[1], _NN), first[exact_rows:])

    prepped = {}
    ready = {}
    if fused:
        state = {p: sbd_ref[p] for p in pairs}
        skey = lambda j, p: p
    else:
        state = {(j, p): block_diagonal(s0_ref.at[j], p) for j in range(n_sub) for p in pairs}
        skey = lambda j, p: (j, p)

    lora_cols = slice(3 * W, RWKV_PROJ)
    z_tail = zr_ref[:, lora_cols]
    row_id = lax.broadcasted_iota(jnp.int32, (R, 1), 0)
    tail_prev = pltpu.roll(z_tail, 1, axis=0)
    if fused:
        tail_prev = jnp.where(row_id == 0, carry_ref[0:1, lora_cols], tail_prev)
    else:
        for j in range(n_sub):
            tail_prev = jnp.where(row_id == j * C, shift_ref[j][:, lora_cols], tail_prev)
    zs_tail = z_tail + (tail_prev - z_tail) * mu_ref[:, lora_cols]
    lora_w = _dot(jnp.tanh(zs_tail[:, :DECAY_LORA]), w2_ref[...])
    lora_a = _dot(zs_tail[:, DECAY_LORA:DECAY_LORA + AAA_LORA], a2_ref[...])
    gate_all = _dot(_sigmoid(zs_tail[:, DECAY_LORA + AAA_LORA:]), g2_ref[...])

    def prep_chain(j):
        rows = slice(j * C, (j + 1) * C)
        zr = zr_ref[rows, :3 * W]
        if not fused:
            before = shift_ref[j][:, :3 * W]
        else:
            before = carry_ref[0:1, :3 * W] if j == 0 else zr_ref[j * C - 1:j * C, :3 * W]
        z_prev = jnp.where(first_row, before, pltpu.roll(zr, 1, axis=0))
        zs = zr + (z_prev - zr) * mu_ref[:, :3 * W]
        r = zs[:, :W]
        k = zs[:, W:2 * W]
        v = zs[:, 2 * W:3 * W]
        w_log = -_softplus(-(w0_ref[...] + lora_w[rows])) - 0.5
        lw = -jnp.exp(w_log)
        a = _sigmoid(a0_ref[...] + lora_a[rows])
        gate = gate_all[rows]
        kk = k * kk_ref[...]
        kk = kk * lax.rsqrt(jnp.maximum(head_sum(kk * kk), 1e-24))
        k2 = k * (1.0 + (a - 1.0) * ka_ref[...])
        if valid_rows < C:
            live = lax.broadcasted_iota(jnp.int32, (C, 1), 0) < valid_rows
            lw = jnp.where(live, lw, 0.0)
            kk = jnp.where(live, kk, 0.0)
            k2 = jnp.where(live, k2, 0.0)
        bvec = kk * a
        yield
        l1 = lw.astype(BF16)
        rem = lw - l1.astype(F32)
        l2 = rem.astype(BF16)
        l3 = (rem - l2.astype(F32)).astype(BF16)
        sums = _dg(tri3, _cat_rows(l1, l2, l3), _NN)
        li = sums[:C]
        lrev = sums[C:]
        yield
        inv_p = jnp.exp(-li)
        to_end = jnp.exp(lrev)
        prepped[j] = dict(
            at=_split2(-kk * jnp.exp(li - lw)), rt=_split2(r * jnp.exp(li)), bt=_split2(bvec * inv_p),
            kt=_split2(k2 * inv_p), bh=_split2(bvec * to_end), kh=_split2(k2 * to_end), v=_split2(v),
            p_end=jnp.exp(li[C - 1:C, :]), bonus=head_sum(r * k2 * rk_ref[...]) * v, gate=gate)
        yield

    def local_chain(j, p):
        d = prepped[j]
        lanes = slice(p * PAIR, (p + 1) * PAIR)
        cut = lambda pair: (pair[0][:, lanes], pair[1][:, lanes])
        at_p, rt_p, bt_p, kt_p, bh_p, kh_p, v_p = map(cut, (d['at'], d['rt'], d['bt'], d['kt'], d['bh'], d['kh'],
                                                            d['v']))
        left = (_cat_rows(at_p[0], rt_p[0]), _cat_rows(at_p[1], rt_p[1]))
        right = (_cat_rows(bd(bt_p[0]), bd(kt_p[0])), _cat_rows(bd(bt_p[1]), bd(kt_p[1])))
        aa = mm(left, right, _NT, exact_rows=C)
        yield
        a_ab = jnp.where(strict, aa[:C, :PAIR], 0.0)
        a_ak = jnp.where(strict, aa[:C, PAIR:], 0.0)
        a_rb = jnp.where(incl, aa[C:, :PAIR], 0.0)
        a_rk = jnp.where(incl, aa[C:, PAIR:], 0.0)
        inv = eye + a_ab
        ps = _split2(a_ab)
        power = mm(ps, bd2(ps))
        span = 2
        yield
        while span < C:
            ps = _split2(power)
            pw = bd2(ps)
            ih = _split2(inv)
            if span * 2 < C:
                both = mm((_cat_rows(ih[0], ps[0]), _cat_rows(ih[1], ps[1])), pw)
                inv = inv + both[:C]
                power = both[C:]
            else:
                inv = inv + mm(ih, pw)
            span *= 2
            yield
        ready[(j, p)] = dict(inv=_split2(inv), akrk=_split2(_cat_rows(a_ak, a_rk)), rb=a_rb.astype(BF16), left=left,
                             bhkh=(_cat_rows(bh_p[0], kh_p[0]), _cat_rows(bh_p[1], kh_p[1])), v=v_p,
                             p_end=d['p_end'][:, lanes])

    def state_chain(j, p):
        d = ready.pop((j, p))
        s_prev = state[skey(j, p)]
        v_hi, v_lo = d['v']
        s_hi, s_lo = _split2(s_prev)
        lhs = (_cat_lanes(d['left'][0], d['akrk'][0]), _cat_lanes(d['left'][1], d['akrk'][1]))
        both = mm(lhs, (_cat_rows(s_hi.T, bd(v_hi)), _cat_rows(s_lo.T, bd(v_lo))), exact_rows=C)
        rhs = both[:C]
        y0 = both[C:]
        yield
        u_pair = _split2(mm(d['inv'], bd2(_split2(rhs))))
        yield
        y_ref[j * C:(j + 1) * C, p * PAIR:(p + 1) * PAIR] = y0 + _dg(d['rb'], bd(u_pair[0]), _NN)
        t_hi = _cat_rows(u_pair[0], v_hi)
        t_lo = _cat_rows(u_pair[1], v_lo)
        w_hi, w_lo = d['bhkh']
        upd = _dg(_cat_rows(t_hi, t_lo), _cat_rows(w_hi, w_hi), _TN) + _dg(t_hi, w_lo, _TN)
        state[skey(j, p)] = s_prev * d['p_end'] + jnp.where(on_diag, upd, 0.0)
        yield

    def post_chain(j):
        rows = slice(j * C, (j + 1) * C)
        d = prepped.pop(j)
        y = y_ref[rows, :]
        mean = head_sum(y) * (1.0 / HEAD_DIM)
        dev = y - mean
        yield
        var = head_sum(dev * dev) * (1.0 / HEAD_DIM)
        yn = dev * lax.rsqrt(var + GN_EPS) * lnw_ref[...] + lnb_ref[...]
        out_ref[rows, :] = ((yn + d['bonus']) * d['gate']).astype(out_ref.dtype)
        yield

    tasks = {}
    for j in range(n_sub):
        tasks[('prep', j)] = (lambda j=j: [prep_chain(j)], [('prep', j - 1), ('local', j - RWKV_LOCAL_IN_FLIGHT)])
        tasks[('local', j)] = (lambda j=j: [local_chain(j, p) for p in pairs],
                               [('prep', j), ('local', j - RWKV_LOCAL_IN_FLIGHT)])
        tasks[('state', j)] = (lambda j=j: [state_chain(j, p) for p in pairs],
                               [('local', j)] + ([('state', j - 1)] if fused else []))
        tasks[('post', j)] = (lambda j=j: [post_chain(j)], [('state', j)])
    if fused:
        n_k = WINDOW + CHUNK
        k_all = _cat_rows(kp_ref[...], kc_ref[...]).astype(BF16)
        v_all = _cat_rows(vp_ref[...], vc_ref[...]).astype(BF16)
        first_valid = jnp.where(c == 0, WINDOW, 0)
        kcol = lax.broadcasted_iota(jnp.int32, (1, n_k), 1)
        sinks = _group_sinks(sink_ref, CHUNK)

        def attn_chains(j):
            rows = slice(j * C, (j + 1) * C)
            keys = slice(j * C, j * C + n_k)
            valid = kcol + j * C >= first_valid if j * C < WINDOW else None
            return [_attn_chain(q_ref[rows, :], k_all[keys, kvh * HEAD_DIM:(kvh + 1) * HEAD_DIM],
                                v_all[keys, kvh * HEAD_DIM:(kvh + 1) * HEAD_DIM], bias_ref[kvh], sinks[kvh], valid,
                                attn_ref, rows, kvh) for kvh in range(KV_HEADS)]

        for j in range(n_sub):
            tasks[('attn', j)] = (lambda j=j: attn_chains(j), [('attn', j - 1)])
        for t, chain in enumerate(proj_chains(xn_ref, 1 - cur)):
            tasks[('proj', t)] = (lambda chain=chain: [chain], [('proj', t - 1), ('prep', min(t, n_sub - 1))])
    _run_tasks(tasks)

    def store_state(dst, s_bd, p):
        dst[2 * p] = s_bd[:HEAD_DIM, :HEAD_DIM]
        dst[2 * p + 1] = s_bd[HEAD_DIM:, HEAD_DIM:]

    if not fused:
        for j in range(n_sub):
            for p in pairs:
                store_state(s_ref.at[j], state[(j, p)], p)
        return
    carry_ref[0:1, :] = zr_ref[R - 1:R, :]
    for p in pairs:
        sbd_ref[p] = state[p]
    kp_ref[...] = kc_ref[R - WINDOW:R, :]
    vp_ref[...] = vc_ref[R - WINDOW:R, :]

    @pl.when(c == steps - 1)
    def _():
        for p in pairs:
            store_state(s_ref.at[0], state[p], p)
        ktail_ref[0] = kc_ref[R - WINDOW:R, :]
        vtail_ref[0] = vc_ref[R - WINDOW:R, :]
        shiftout_ref[0] = zr_ref[R - 1:R, :]


def _rwkv_operands(lw):
    seg = jnp.asarray(np.kron(np.eye(2), np.ones((HEAD_DIM, HEAD_DIM))), BF16)
    ones = np.ones((CHUNK, CHUNK))
    tri3 = jnp.asarray(np.concatenate([np.tile(np.tril(ones), (1, 3)), np.tile(np.triu(ones, 1), (1, 3))]), BF16)
    row = lambda name: lw[name].reshape(1, -1)
    return [row('rwkv_mu'), row('rwkv_w0'), lw['rwkv_w2'].astype(BF16), row('rwkv_a0'),
            lw['rwkv_a2'].astype(BF16), lw['rwkv_g2'].astype(BF16), row('rwkv_k_k'), row('rwkv_k_a'),
            row('rwkv_r_k'), row('rwkv_ln_w'), row('rwkv_ln_b'), seg, tri3]


def _rwkv_mixer(zr, shift_prev, state0, lw, batch, seq):
    assert seq <= CHUNK
    if seq < CHUNK:
        zr = jnp.pad(zr.reshape(batch, seq, RWKV_PROJ), ((0, 0), (0, CHUNK - seq), (0, 0))).reshape(-1, RWKV_PROJ)
    params = _rwkv_operands(lw)
    vmem = pl.BlockSpec(memory_space=pltpu.VMEM)
    out, state = pl.pallas_call(
        functools.partial(_rwkv_kernel, seq, False, 1),
        in_specs=[vmem] * (3 + len(params)),
        out_specs=[vmem, vmem],
        out_shape=[jax.ShapeDtypeStruct((batch * CHUNK, RWKV_WIDTH), BF16), jax.ShapeDtypeStruct(state0.shape, F32)],
        scratch_shapes=[pltpu.VMEM((batch * CHUNK, RWKV_WIDTH), F32)],
        compiler_params=pltpu.CompilerParams(vmem_limit_bytes=V7X_VMEM_LIMIT_BYTES),
        name="rwkv_mixer",
    )(zr, shift_prev, state0, *params)
    if seq < CHUNK:
        out = out.reshape(batch, CHUNK, RWKV_WIDTH)[:, :seq].reshape(batch * seq, RWKV_WIDTH)
    return out, state


def _prompt_mixer(x2d, shift_prev, state0, lw, bias, batch, seq):
    rows = CHUNK * RWKV_SUB_CHUNKS
    steps = seq // rows
    total = batch * steps
    params = _rwkv_operands(lw)
    seq_block = lambda shape: pl.BlockSpec((1,) + shape, lambda g: (g // steps,) + (0,) * len(shape))
    row_spec = lambda w: pl.BlockSpec((rows, w), lambda g: (g, 0))

    def resident(shape):
        nd = len(shape)
        return pl.BlockSpec(shape, lambda g: (0,) * nd, pipeline_mode=pl.Buffered(1))

    in_specs = [pl.BlockSpec((rows, D_MODEL), lambda g: (jnp.minimum(g + 1, total - 1), 0)),
                pl.BlockSpec((rows, D_MODEL), lambda g: (0, 0), pipeline_mode=pl.Buffered(1)),
                resident((1, D_MODEL)), resident((D_MODEL, IN_PROJ)),
                seq_block((1, RWKV_PROJ)), seq_block((RWKV_HEADS, HEAD_DIM, HEAD_DIM))]
    in_specs += [resident(p.shape) for p in params]
    in_specs += [pl.BlockSpec(memory_space=pltpu.SMEM), resident(bias.shape)]
    out_specs = [row_spec(RWKV_WIDTH), seq_block((RWKV_HEADS, HEAD_DIM, HEAD_DIM)), row_spec(ATTN_WIDTH),
                 seq_block((WINDOW, KV_WIDTH)), seq_block((WINDOW, KV_WIDTH)), seq_block((1, RWKV_PROJ))]
    n = batch * seq
    out_shape = [jax.ShapeDtypeStruct((n, RWKV_WIDTH), BF16), jax.ShapeDtypeStruct(state0.shape, F32),
                 jax.ShapeDtypeStruct((n, ATTN_WIDTH), BF16), jax.ShapeDtypeStruct((batch, WINDOW, KV_WIDTH), F32),
                 jax.ShapeDtypeStruct((batch, WINDOW, KV_WIDTH), F32),
                 jax.ShapeDtypeStruct((batch, 1, RWKV_PROJ), F32)]
    scratch = [pltpu.VMEM((8, RWKV_PROJ), F32), pltpu.VMEM((rows, RWKV_WIDTH), F32),
               pltpu.VMEM((RWKV_HEADS // 2, 2 * HEAD_DIM, 2 * HEAD_DIM), F32),
               pltpu.VMEM((2, rows, RWKV_PROJ), F32), pltpu.VMEM((2, rows, ATTN_WIDTH), BF16),
                pltpu.VMEM((2, rows, KV_WIDTH), F32), pltpu.VMEM((2, rows, KV_WIDTH), F32),
                pltpu.VMEM((WINDOW, KV_WIDTH), F32), pltpu.VMEM((WINDOW, KV_WIDTH), F32)]
    r_out, state, a_out, k_tail, v_tail, shift_new = pl.pallas_call(
        functools.partial(_rwkv_kernel, rows, True, steps),
        grid=(total,),
        in_specs=in_specs,
        out_specs=out_specs,
        out_shape=out_shape,
        scratch_shapes=scratch,
        compiler_params=_params("arbitrary"),
        name="prompt_mixer",
    )(x2d, x2d, lw['norm_mix_g'], lw['w_in'], shift_prev, state0, *params, lw['attn_sink'], bias)
    return a_out, r_out, state, k_tail, v_tail, shift_new


def _memkv_kernel(m_ref, g_ref, wk_ref, wv_ref, k_ref, v_ref):
    mn = _rms(m_ref[...], g_ref[...]).astype(BF16)
    k_ref[...] = jnp.dot(mn, wk_ref[...], preferred_element_type=F32)
    v_ref[...] = jnp.dot(mn, wv_ref[...], preferred_element_type=F32)


def _memory_kv(mem2d, g, w_mk, w_mv):
    n = mem2d.shape[0]
    tm = min(MEMORY_KV_ROW_TILE, n)
    row = lambda w: pl.BlockSpec((tm, w), lambda i: (i, 0))
    return pl.pallas_call(
        _memkv_kernel,
        grid=(n // tm,),
        in_specs=[row(D_MODEL), _const_spec((1, D_MODEL)), _const_spec(w_mk.shape), _const_spec(w_mv.shape)],
        out_specs=[row(MEM_WIDTH), row(MEM_WIDTH)],
        out_shape=[jax.ShapeDtypeStruct((n, MEM_WIDTH), F32)] * 2,
        compiler_params=_params("parallel"),
        name="memory_kv",
    )(mem2d, g, w_mk, w_mv)


def _tail_kernel(x_ref, a_ref, r_ref, mk_ref, mv_ref, wo_ref, gc_ref, wq_ref, wco_ref, gm_ref, wu_ref, wd_ref, gf_ref,
                 y_ref):
    n_seq = mk_ref.shape[0]
    rows_per_seq = x_ref.shape[0] // n_seq
    x1 = x_ref[...] + _dot(a_ref[...], wo_ref[:ATTN_WIDTH, :]) + _dot(r_ref[...], wo_ref[ATTN_WIDTH:, :])
    q = _dot(_rms(x1, gc_ref[...]), wq_ref[...]).astype(BF16)
    outs = {}

    def cross_chain(b, h):
        rows = slice(b * rows_per_seq, (b + 1) * rows_per_seq)
        sl = slice(h * MEM_HEAD_DIM, (h + 1) * MEM_HEAD_DIM)
        s = _dg(q[rows, sl], mk_ref[b, :, sl].astype(BF16), _NT) * (MEM_HEAD_DIM ** -0.5)
        yield
        p = jnp.exp(s - jnp.max(s, axis=-1, keepdims=True))
        den = jnp.sum(p, axis=-1, keepdims=True)
        yield
        outs[(b, h)] = _dg(p.astype(BF16), mv_ref[b, :, sl].astype(BF16), _NN) * (1.0 / den)
        yield

    _run_interleaved([cross_chain(b, h) for b in range(n_seq) for h in range(MEM_HEADS)])
    o = _cat_rows(*[_cat_lanes(*[outs[(b, h)] for h in range(MEM_HEADS)]) for b in range(n_seq)])
    x2 = x1 + _dot(o, wco_ref[...])
    up = _dot(_rms(x2, gm_ref[...]), wu_ref[...])
    act = jnp.square(jnp.maximum(up, 0.0))
    y_ref[...] = _rms(x2 + _dot(act, wd_ref[...]), gf_ref[...])


def _tail(x2d, a_out, r_out, mk, mv, lw, batch, seq):
    n = batch * seq
    tq = min(TAIL_ROW_TILE, n)
    if seq >= tq:
        assert seq % tq == 0
        seq_per_tile, tiles_per_seq = 1, seq // tq
        mem_spec = pl.BlockSpec((1, N_MEM, MEM_WIDTH), lambda i: (i // tiles_per_seq, 0, 0))
    else:
        assert tq % seq == 0
        seq_per_tile = tq // seq
        mem_spec = pl.BlockSpec((seq_per_tile, N_MEM, MEM_WIDTH), lambda i: (i, 0, 0))
    row = lambda w: pl.BlockSpec((tq, w), lambda i: (i, 0))

    def resident(shape):
        nd = len(shape)
        return pl.BlockSpec(shape, lambda *_: (0,) * nd, pipeline_mode=pl.Buffered(1))

    weights = [lw['w_out'], lw['norm_cross_g'], lw['w_cq'], lw['w_co'], lw['norm_mlp_g'], lw['w_up'], lw['w_down'],
               lw['norm_final_g']]
    return pl.pallas_call(
        _tail_kernel,
        grid=(n // tq,),
        in_specs=[row(D_MODEL), row(ATTN_WIDTH), row(RWKV_WIDTH), mem_spec, mem_spec]
                 + [resident(w.shape) for w in weights],
        out_specs=row(D_MODEL),
        out_shape=jax.ShapeDtypeStruct(x2d.shape, F32),
        compiler_params=_params("parallel"),
        name="tail",
    )(x2d, a_out, r_out, mk, mv, *weights)


def _trunk(x, mk, mv, k_past, v_past, shift_prev, state0, lw, table):
    batch, seq = x.shape[0], x.shape[1]
    x2d = x.reshape(batch * seq, D_MODEL)
    if k_past is None:
        bias = _rel_bias(table, CHUNK, WINDOW + CHUNK)
        a_out, r_out, state, k_buf, v_buf, shift_new = _prompt_mixer(x2d, shift_prev, state0, lw, bias, batch, seq)
    else:
        q, k, v, zr = _in_proj(x2d, lw['norm_mix_g'], lw['w_in'])
        bias = _rel_bias(table, seq, WINDOW + seq)
        a_out = _sample_attention(q, k, v, k_past.reshape(batch * WINDOW, KV_WIDTH),
                                  v_past.reshape(batch * WINDOW, KV_WIDTH), lw['attn_sink'], bias, seq)
        k_buf = jnp.concatenate([k_past, k.reshape(batch, seq, KV_WIDTH)], axis=1)[:, -WINDOW:]
        v_buf = jnp.concatenate([v_past, v.reshape(batch, seq, KV_WIDTH)], axis=1)[:, -WINDOW:]
        r_out, state = _rwkv_mixer(zr, shift_prev, state0, lw, batch, seq)
        shift_new = zr.reshape(batch, seq, RWKV_PROJ)[:, -1:]
    y = _tail(x2d, a_out, r_out, mk, mv, lw, batch, seq)
    kv_shape = (batch, WINDOW, KV_HEADS, HEAD_DIM)
    return y.reshape(x.shape), k_buf.reshape(kv_shape), v_buf.reshape(kv_shape), shift_new, state


def kernel(x_prompt, x_sample, mem_prompt, cache_attn_k, cache_attn_v, cache_mem_k, cache_mem_v, state_shift,
           state_wkv, norm_mix_g, w_in, attn_sink, rel_bias_table, rwkv_mu, rwkv_w0, rwkv_w2, rwkv_a0, rwkv_a2,
           rwkv_g2, rwkv_k_k, rwkv_k_a, rwkv_r_k, rwkv_ln_w, rwkv_ln_b, w_out, norm_cross_g, norm_mem_g, w_cq,
           w_mk, w_mv, w_co, norm_mlp_g, w_up, w_down, norm_final_g):
    assert norm_mix_g.shape[0] == 1, "single-layer trunk"
    bp, dec_b = x_prompt.shape[0], x_sample.shape[0]
    vec = lambda p: p[0].reshape(1, -1)
    w_in_b, w_out_b, w_cq_b, w_co_b, w_up_b, w_down_b, w_mk_b, w_mv_b = _to_bf16(
        w_in[0], w_out[0], w_cq[0], w_co[0], w_up[0], w_down[0], w_mk[0], w_mv[0])
    lw = {
        'norm_mix_g': vec(norm_mix_g), 'w_in': w_in_b, 'attn_sink': attn_sink[0],
        'rwkv_mu': rwkv_mu[0], 'rwkv_w0': rwkv_w0[0], 'rwkv_w2': rwkv_w2[0], 'rwkv_a0': rwkv_a0[0],
        'rwkv_a2': rwkv_a2[0], 'rwkv_g2': rwkv_g2[0], 'rwkv_k_k': rwkv_k_k[0], 'rwkv_k_a': rwkv_k_a[0],
        'rwkv_r_k': rwkv_r_k[0], 'rwkv_ln_w': rwkv_ln_w[0], 'rwkv_ln_b': rwkv_ln_b[0],
        'w_out': w_out_b, 'norm_cross_g': vec(norm_cross_g), 'w_cq': w_cq_b,
        'w_co': w_co_b, 'norm_mlp_g': vec(norm_mlp_g), 'w_up': w_up_b,
        'w_down': w_down_b, 'norm_final_g': norm_final_g.reshape(1, -1),
    }
    mk, mv = _memory_kv(mem_prompt.reshape(bp * N_MEM, D_MODEL), vec(norm_mem_g), w_mk_b, w_mv_b)
    mk = mk.reshape(bp, N_MEM, MEM_WIDTH)
    mv = mv.reshape(bp, N_MEM, MEM_WIDTH)
    shift0 = jnp.zeros((bp, 1, RWKV_PROJ), F32)
    wkv0 = jnp.zeros((bp, RWKV_HEADS, HEAD_DIM, HEAD_DIM), F32)
    yp, pk, pv, psh, pS = _trunk(x_prompt, mk, mv, None, None, shift0, wkv0, lw, rel_bias_table)
    ys, sk, sv, ssh, sS = _trunk(
        x_sample, cache_mem_k[0].reshape(dec_b, N_MEM, MEM_WIDTH), cache_mem_v[0].reshape(dec_b, N_MEM, MEM_WIDTH),
        cache_attn_k[0].reshape(dec_b, WINDOW, KV_WIDTH), cache_attn_v[0].reshape(dec_b, WINDOW, KV_WIDTH),
        state_shift[0], state_wkv[0], lw, rel_bias_table)
    mem_shape = (1, bp, N_MEM, MEM_HEADS, MEM_HEAD_DIM)
    return (yp, ys, pk[None], pv[None], mk.reshape(mem_shape), mv.reshape(mem_shape), psh[None], pS[None],
            sk[None], sv[None], ssh[None], sS[None])
```
